```python
import jax, jax.numpy as jnp
from jax import lax
import numpy as np

D_MODEL = 1024
BATCH = 4
SEQ = 8192
DEPTH = 2

HEAD_DIM = 64
ROT_DIM = HEAD_DIM // 4
ROPE_THETA = 500000.0
QBLK = 128

DILATED_GROUPS = ((128, 1), (512, 4), (2048, 16))
A_HEADS_PER_GROUP = 2
A_HEADS = A_HEADS_PER_GROUP * len(DILATED_GROUPS)
B_HEADS = 4
IDX_HEADS = 4
IDX_DIM = 64
DSA_TOPK = 256
C_HEADS = 4
D_HEADS = 4

N_BRANCH = 4
D_FF = 2816
LN_EPS = 1e-5
DEEPNORM_ALPHA = (2 * DEPTH) ** 0.25
DEEPNORM_BETA = (8 * DEPTH) ** -0.25

A_QKV_W = 3 * A_HEADS * HEAD_DIM
B_QKV_W = 3 * B_HEADS * HEAD_DIM
IDX_Q_W = IDX_HEADS * IDX_DIM
IDX_K_W = IDX_DIM
IDX_W_W = IDX_HEADS
C_QKV_W = 3 * C_HEADS * HEAD_DIM
D_QKV_W = 3 * D_HEADS * HEAD_DIM
FG_W = D_HEADS
GATE_W = N_BRANCH * D_MODEL
OFF_B = A_QKV_W
OFF_IQ = OFF_B + B_QKV_W
OFF_IK = OFF_IQ + IDX_Q_W
OFF_IW = OFF_IK + IDX_K_W
OFF_C = OFF_IW + IDX_W_W
OFF_D = OFF_C + C_QKV_W
OFF_FG = OFF_D + D_QKV_W
OFF_GATE = OFF_FG + FG_W
N_IN = OFF_GATE + GATE_W
SPLIT_POINTS = [OFF_B, OFF_IQ, OFF_IK, OFF_IW, OFF_C, OFF_D, OFF_FG, OFF_GATE]

BRANCH_WIDTHS = (A_HEADS_PER_GROUP * HEAD_DIM, B_HEADS * HEAD_DIM, C_HEADS * HEAD_DIM, D_HEADS * HEAD_DIM)
BRANCH_OFFSETS = (0, BRANCH_WIDTHS[0], BRANCH_WIDTHS[0] + BRANCH_WIDTHS[1],
                  BRANCH_WIDTHS[0] + BRANCH_WIDTHS[1] + BRANCH_WIDTHS[2],
                  BRANCH_WIDTHS[0] + BRANCH_WIDTHS[1] + BRANCH_WIDTHS[2] + BRANCH_WIDTHS[3])

kernel_name = 'hybrid_gated_dilated_dsa_stickbreak_fox_macaron_deepnorm'

F32 = jnp.float32


def _layer_norm(x, g, b):
    xf = x.astype(F32)
    mu = jnp.mean(xf, axis=-1, keepdims=True)
    var = jnp.mean(jnp.square(xf - mu), axis=-1, keepdims=True)
    return ((xf - mu) * lax.rsqrt(var + LN_EPS) * g.astype(F32) + b.astype(F32)).astype(x.dtype)


def _modulate(h, shift, scale):
    return h * (1 + scale[:, None, :]) + shift[:, None, :]


def _swiglu(u, w_in, w_out):
    g, up = jnp.split(u @ w_in, 2, axis=-1)
    return (jax.nn.silu(g) * up) @ w_out


def _rope_partial(x, pos):
    half = ROT_DIM // 2
    inv_freq = ROPE_THETA ** (-(jnp.arange(half, dtype=F32) * (2.0 / ROT_DIM)))
    ang = pos.astype(F32)[:, None] * inv_freq[None, :]
    cos = jnp.cos(ang)[:, None, :]
    sin = jnp.sin(ang)[:, None, :]
    xr = x[..., :ROT_DIM].astype(F32)
    x1, x2 = xr[..., :half], xr[..., half:]
    rot = jnp.concatenate([x1 * cos - x2 * sin, x2 * cos + x1 * sin], axis=-1)
    return jnp.concatenate([rot.astype(x.dtype), x[..., ROT_DIM:]], axis=-1)


def _to_blocks(a):
    b, t = a.shape[:2]
    return a.reshape(b, t // QBLK, QBLK, *a.shape[2:]).swapaxes(0, 1)


def _from_blocks(a):
    nb, b, q = a.shape[:3]
    return a.swapaxes(0, 1).reshape(b, nb * q, *a.shape[3:])


def _dilated_window_attention(q, k, v, window, dilation):
    bsz, t, h, dh = q.shape
    span = window // dilation
    unit = dilation * span
    t_pad = -(-t // unit) * unit
    nb = t_pad // unit

    def prep(a):
        a = jnp.pad(a, ((0, 0), (0, t_pad - t), (0, 0), (0, 0)))
        a = a.reshape(bsz, t_pad // dilation, dilation, h, dh).transpose(0, 2, 3, 1, 4)
        return a.reshape(bsz, dilation, h, nb, span, dh)

    def with_prev(a):
        prev = jnp.pad(a, ((0, 0), (0, 0), (0, 0), (1, 0), (0, 0), (0, 0)))[:, :, :, :-1]
        return jnp.concatenate([prev, a], axis=4)

    qb = prep(q)
    kk = with_prev(prep(k))
    vv = with_prev(prep(v))
    s = jnp.einsum('brhnqe,brhnke->brhnqk', qb, kk, preferred_element_type=F32) * (dh ** -0.5)
    qi = jnp.arange(span)[:, None]
    kj = jnp.arange(2 * span)[None, :]
    dist = span + qi - kj
    band = (dist >= 0) & (dist <= span)
    before_start = (jnp.arange(nb) == 0)[:, None, None] & (kj < span)[None]
    mask = band[None] & ~before_start
    s = jnp.where(mask, s, -jnp.inf)
    lse = jax.nn.logsumexp(s, axis=-1)
    p = jnp.exp(s - lse[..., None])
    o = jnp.einsum('brhnqk,brhnke->brhnqe', p.astype(v.dtype), vv)
    o = o.reshape(bsz, dilation, h, t_pad // dilation, dh).transpose(0, 3, 1, 2, 4).reshape(bsz, t_pad, h, dh)[:, :t]
    lse = lse.reshape(bsz, dilation, h, t_pad // dilation).transpose(0, 3, 1, 2).reshape(bsz, t_pad, h)[:, :t]
    return o, lse


def _dsa_attention(q, k, v, q_idx, k_idx, w_idx):
    bsz, t, h, dh = q.shape
    topk = min(DSA_TOPK, t // 4)
    key_pos = jnp.arange(t)
    gather = jax.vmap(lambda arr, idx: arr[idx])

    def one_block(args):
        qb, qib, wb, t0 = args
        qpos = t0 + jnp.arange(QBLK)
        rel = jnp.maximum(jnp.einsum('bqhe,bse->bqhs', qib, k_idx, preferred_element_type=F32), 0.0)
        score = jnp.einsum('bqh,bqhs->bqs', wb.astype(F32), rel)
        causal = key_pos[None, :] <= qpos[:, None]
        score = jnp.where(causal[None], score, -jnp.inf)
        _, sel = lax.top_k(score, topk)
        valid = sel <= qpos[None, :, None]
        k_sel = gather(k, sel)
        v_sel = gather(v, sel)
        s = jnp.einsum('bqhe,bqkhe->bhqk', qb, k_sel, preferred_element_type=F32) * (dh ** -0.5)
        s = jnp.where(valid[:, None], s, -jnp.inf)
        p = jax.nn.softmax(s, axis=-1)
        return jnp.einsum('bhqk,bqkhe->bqhe', p.astype(v.dtype), v_sel)

    t0s = jnp.arange(t // QBLK) * QBLK
    out = lax.map(one_block, (_to_blocks(q), _to_blocks(q_idx), _to_blocks(w_idx), t0s))
    return _from_blocks(out)


def _stick_breaking_attention(q, k, v):
    bsz, t, h, dh = q.shape
    key_pos = jnp.arange(t)

    def one_block(args):
        qb, t0 = args
        qpos = t0 + jnp.arange(QBLK)
        z = jnp.einsum('bqhe,bshe->bhqs', qb, k, preferred_element_type=F32) * (dh ** -0.5)
        before = key_pos[None, :] < qpos[:, None]
        log_beta = jax.nn.log_sigmoid(z)
        log_keep = jnp.where(before, jax.nn.log_sigmoid(-z), 0.0)
        later = lax.cumsum(log_keep, axis=3, reverse=True) - log_keep
        a = jnp.where(before, jnp.exp(log_beta + later), 0.0)
        return jnp.einsum('bhqs,bshe->bqhe', a.astype(v.dtype), v)

    t0s = jnp.arange(t // QBLK) * QBLK
    return _from_blocks(lax.map(one_block, (_to_blocks(q), t0s)))


def _forgetting_attention(q, k, v, log_f):
    bsz, t, h, dh = q.shape
    cum = lax.cumsum(log_f, axis=1)
    cum_keys = cum.transpose(0, 2, 1)
    key_pos = jnp.arange(t)

    def one_block(args):
        qb, cq, t0 = args
        qpos = t0 + jnp.arange(QBLK)
        s = jnp.einsum('bqhe,bshe->bhqs', qb, k, preferred_element_type=F32) * (dh ** -0.5)
        s = s + cq.transpose(0, 2, 1)[..., None] - cum_keys[:, :, None, :]
        causal = key_pos[None, :] <= qpos[:, None]
        s = jnp.where(causal, s, -jnp.inf)
        p = jax.nn.softmax(s, axis=-1)
        return jnp.einsum('bhqs,bshe->bqhe', p.astype(v.dtype), v)

    t0s = jnp.arange(t // QBLK) * QBLK
    return _from_blocks(lax.map(one_block, (_to_blocks(q), _to_blocks(cum), t0s)))


def _hybrid_mixer(u, w_in, b_gate, b_forget, w_branch, w_out):
    bsz, t, _ = u.shape
    pos = jnp.arange(t)
    proj = u @ w_in
    a_qkv, b_qkv, i_q, i_k, i_w, c_qkv, d_qkv, f_logit, g_logit = jnp.split(proj, SPLIT_POINTS, axis=-1)

    def qkv(a, h):
        a = a.reshape(bsz, t, 3, h, HEAD_DIM)
        return a[:, :, 0], a[:, :, 1], a[:, :, 2]

    qa, ka, va = qkv(a_qkv, A_HEADS)
    qa, ka = _rope_partial(qa, pos), _rope_partial(ka, pos)
    outs, lses = [], []
    for g, (window, dilation) in enumerate(DILATED_GROUPS):
        hs = slice(g * A_HEADS_PER_GROUP, (g + 1) * A_HEADS_PER_GROUP)
        o, l = _dilated_window_attention(qa[:, :, hs], ka[:, :, hs], va[:, :, hs], window, dilation)
        outs.append(o)
        lses.append(l)
    wts = jax.nn.softmax(jnp.stack(lses), axis=0)
    y_a = jnp.sum(wts[..., None].astype(va.dtype) * jnp.stack(outs), axis=0).reshape(bsz, t, -1)

    qb, kb, vb = qkv(b_qkv, B_HEADS)
    qb, kb = _rope_partial(qb, pos), _rope_partial(kb, pos)
    q_idx = _rope_partial(i_q.reshape(bsz, t, IDX_HEADS, IDX_DIM), pos)
    k_idx = _rope_partial(i_k[:, :, None, :], pos)[:, :, 0]
    y_b = _dsa_attention(qb, kb, vb, q_idx, k_idx, i_w).reshape(bsz, t, -1)

    qc, kc, vc = qkv(c_qkv, C_HEADS)
    y_c = _stick_breaking_attention(qc, kc, vc).reshape(bsz, t, -1)

    qd, kd, vd = qkv(d_qkv, D_HEADS)
    log_f = jax.nn.log_sigmoid((f_logit + b_forget).astype(F32))
    y_d = _forgetting_attention(qd, kd, vd, log_f).reshape(bsz, t, -1)

    gates = jax.nn.sigmoid((g_logit + b_gate).astype(F32)).astype(u.dtype).reshape(bsz, t, N_BRANCH, D_MODEL)
    branches = (y_a, y_b, y_c, y_d)
    merged = gates[:, :, 0] * (y_a @ w_branch[BRANCH_OFFSETS[0]:BRANCH_OFFSETS[1]])
    for i in range(1, N_BRANCH):
        merged = merged + gates[:, :, i] * (branches[i] @ w_branch[BRANCH_OFFSETS[i]:BRANCH_OFFSETS[i + 1]])
    return merged @ w_out


def setup_inputs(seed: int = 0) -> dict:
    key = jax.random.key(seed)
    ks = jax.random.split(key, 16)
    D = D_MODEL

    def normal(k, shape, std):
        return jax.random.normal(k, shape, F32) * std

    std = D ** -0.5
    x = normal(ks[0], (BATCH, SEQ, D), 1.0)
    c = normal(ks[1], (BATCH, D), 1.0)
    ada_w = normal(ks[2], (DEPTH, D, 9 * D), 0.1 * std)
    ada_b = normal(ks[3], (DEPTH, 9 * D), 0.02)
    ln_g = 1.0 + normal(ks[4], (DEPTH, 3, D), 0.02)
    ln_b = normal(ks[5], (DEPTH, 3, D), 0.02)
    ffn_w_in = normal(ks[6], (DEPTH, 2, D, 2 * D_FF), std)
    ffn_w_out = normal(ks[7], (DEPTH, 2, D_FF, D), (D_FF ** -0.5) * DEEPNORM_BETA)

    pk = jax.random.split(ks[8], 13)

    def qkv_cols(k1, k2, h):
        return [normal(k1, (DEPTH, D, 2 * h * HEAD_DIM), std),
                normal(k2, (DEPTH, D, h * HEAD_DIM), std * DEEPNORM_BETA)]

    pieces = (qkv_cols(pk[0], pk[1], A_HEADS)
              + qkv_cols(pk[2], pk[3], B_HEADS)
              + [normal(pk[4], (DEPTH, D, IDX_Q_W), std),
                 normal(pk[5], (DEPTH, D, IDX_K_W), std),
                 normal(pk[6], (DEPTH, D, IDX_W_W), std)]
              + qkv_cols(pk[7], pk[8], C_HEADS)
              + qkv_cols(pk[9], pk[10], D_HEADS)
              + [normal(pk[11], (DEPTH, D, FG_W), std),
                 normal(pk[12], (DEPTH, D, GATE_W), std)])
    mix_w_in = jnp.concatenate(pieces, axis=-1)
    mix_b_gate = normal(ks[9], (DEPTH, GATE_W), 0.02)
    mix_b_forget = jax.random.uniform(ks[10], (DEPTH, D_HEADS), F32, 1.0, 4.0)
    bk = jax.random.split(ks[11], N_BRANCH)
    mix_w_branch = jnp.concatenate(
        [normal(bk[i], (DEPTH, BRANCH_WIDTHS[i], D), BRANCH_WIDTHS[i] ** -0.5) for i in range(N_BRANCH)], axis=1)
    mix_w_out = normal(ks[12], (DEPTH, D, D), std * DEEPNORM_BETA)
    return {'x': x, 'c': c, 'ada_w': ada_w, 'ada_b': ada_b, 'ln_g': ln_g, 'ln_b': ln_b,
            'ffn_w_in': ffn_w_in, 'ffn_w_out': ffn_w_out, 'mix_w_in': mix_w_in,
            'mix_b_gate': mix_b_gate, 'mix_b_forget': mix_b_forget,
            'mix_w_branch': mix_w_branch, 'mix_w_out': mix_w_out}


def reference(x, c, ada_w, ada_b, ln_g, ln_b, ffn_w_in, ffn_w_out, mix_w_in,
              mix_b_gate, mix_b_forget, mix_w_branch, mix_w_out):
    cond = jax.nn.silu(c)
    for l in range(DEPTH):
        mod = (cond @ ada_w[l] + ada_b[l]).reshape(-1, 3, 3, D_MODEL)

        h = 0.5 * _swiglu(_modulate(x, mod[:, 0, 0], mod[:, 0, 1]), ffn_w_in[l, 0], ffn_w_out[l, 0])
        x = _layer_norm(DEEPNORM_ALPHA * x + (1 + mod[:, 0, 2])[:, None, :] * h, ln_g[l, 0], ln_b[l, 0])

        h = _hybrid_mixer(_modulate(x, mod[:, 1, 0], mod[:, 1, 1]), mix_w_in[l], mix_b_gate[l],
                          mix_b_forget[l], mix_w_branch[l], mix_w_out[l])
        x = _layer_norm(DEEPNORM_ALPHA * x + (1 + mod[:, 1, 2])[:, None, :] * h, ln_g[l, 1], ln_b[l, 1])

        h = 0.5 * _swiglu(_modulate(x, mod[:, 2, 0], mod[:, 2, 1]), ffn_w_in[l, 1], ffn_w_out[l, 1])
        x = _layer_norm(DEEPNORM_ALPHA * x + (1 + mod[:, 2, 2])[:, None, :] * h, ln_g[l, 2], ln_b[l, 2])
    return x
```

```python
import functools

import jax
import jax.numpy as jnp
from jax import lax
from jax.experimental import pallas as pl
from jax.experimental.pallas import tpu as pltpu

F32 = jnp.float32
BF16 = jnp.bfloat16
I32 = jnp.int32

LANE = 128
HEAD_DIM = 64
ROT_DIM = HEAD_DIM // 4
ROPE_THETA = 500000.0
DILATED_GROUPS = ((128, 1), (512, 4), (2048, 16))
SPAN = 128
A_HEADS, B_HEADS, C_HEADS, D_HEADS = 6, 4, 4, 4
IDX_HEADS, IDX_DIM = 4, 64
DSA_TOPK = 256
N_BRANCH = 4
LN_EPS = 1e-5
BRANCH_WIDTHS = (128, 256, 256, 256)

OFF_A = 0
OFF_B = OFF_A + 3 * A_HEADS * HEAD_DIM
OFF_IQ = OFF_B + 3 * B_HEADS * HEAD_DIM
OFF_IK = OFF_IQ + IDX_HEADS * IDX_DIM
OFF_IW = OFF_IK + IDX_DIM
OFF_C = OFF_IW + IDX_HEADS
OFF_D = OFF_C + 3 * C_HEADS * HEAD_DIM
OFF_FG = OFF_D + 3 * D_HEADS * HEAD_DIM
OFF_GATE = OFF_FG + D_HEADS

BLK_AQ, BLK_AK, BLK_BQ, BLK_BK = 0, 3, 6, 8
N_ROPE_BLKS_16 = 10
BLK_AV, BLK_BV = 10, 13
BLK_CQ, BLK_CK, BLK_CV = 15, 17, 19
BLK_DQ, BLK_DK, BLK_DV = 21, 23, 25
N_BLKS_16 = 28
BLK_IQ, BLK_IX = 0, 2
N_BLKS_32 = 4
LANE_IW = 80
LANE_FG = 84

INT_MIN = -2147483648
NEG_BIG = -1e30
VMEM_LIMIT = 56 * 1024 * 1024


def _nt(a, b):
    return lax.dot_general(a, b, (((1,), (1,)), ((), ())), preferred_element_type=F32)


def _dot(a, b):
    return jnp.dot(a, b, preferred_element_type=F32)


def _split3(x):
    hi = x.astype(BF16)
    r1 = x - hi.astype(F32)
    mid = r1.astype(BF16)
    lo = (r1 - mid.astype(F32)).astype(BF16)
    return hi, mid, lo


def _split2(x):
    hi = x.astype(BF16)
    return hi, (x - hi.astype(F32)).astype(BF16)


def _log_sigmoid(z):
    return -(jnp.maximum(-z, 0.0) + jnp.log(1.0 + jnp.exp(-jnp.abs(z))))


def _params(sem):
    return pltpu.CompilerParams(dimension_semantics=sem, vmem_limit_bytes=VMEM_LIMIT)


def _const_spec(shape):
    n = len(shape)
    return pl.BlockSpec(shape, lambda *_: (0,) * n)


def _pick(n, prefs):
    for p in prefs:
        if n % p == 0:
            return p
    return n


def _ada_kernel(c_ref, w_ref, b_ref, o_ref):
    c = c_ref[...]
    cond = c * jax.nn.sigmoid(c)
    ch, cm, _ = _split3(cond)
    wh, wm, _ = _split3(w_ref[0])
    o_ref[0] = _dot(ch, wh) + _dot(ch, wm) + _dot(cm, wh) + b_ref[0]


def _ada(c8, ada_w, ada_b):
    depth, d, n = ada_w.shape
    tn = _pick(n, (1152, 1024, 512, 256, 128))
    return pl.pallas_call(
        _ada_kernel,
        grid=(depth, n // tn),
        in_specs=[pl.BlockSpec((8, d), lambda l, j: (0, 0)),
                  pl.BlockSpec((1, d, tn), lambda l, j: (l, 0, j)),
                  pl.BlockSpec((1, 1, tn), lambda l, j: (l, 0, j))],
        out_specs=pl.BlockSpec((1, 8, tn), lambda l, j: (l, 0, j)),
        out_shape=jax.ShapeDtypeStruct((depth, 8, n), F32),
        compiler_params=_params(("parallel", "parallel")),
        name="ada_mod",
    )(c8, ada_w, ada_b.reshape(depth, 1, n))


def _deepnorm_ln(x, h, gate, g, b, alpha):
    y = alpha * x + (1.0 + gate) * h
    mu = jnp.mean(y, axis=-1, keepdims=True)
    yc = y - mu
    var = jnp.mean(yc * yc, axis=-1, keepdims=True)
    return yc * lax.rsqrt(var + LN_EPS) * g + b


def _ffn_kernel(x_ref, sh_ref, sc_ref, gt_ref, wg_ref, wu_ref, wo_ref, lg_ref, lb_ref, o_ref, *, tf, alpha):
    x = x_ref[0]
    u = (x * (1.0 + sc_ref[0]) + sh_ref[0]).astype(BF16)
    acc = jnp.zeros(x.shape, F32)
    for j in range(wg_ref.shape[1] // tf):
        g = _dot(u, wg_ref[:, j * tf:(j + 1) * tf])
        up = _dot(u, wu_ref[:, j * tf:(j + 1) * tf])
        h = (g * jax.nn.sigmoid(g) * up).astype(BF16)
        acc = acc + _dot(h, wo_ref[j * tf:(j + 1) * tf, :])
    o_ref[0] = _deepnorm_ln(x, 0.5 * acc, gt_ref[0], lg_ref[...], lb_ref[...], alpha)


def _ffn(x, shift, scale, gate, wg, wu, wo, ln_g, ln_b, alpha):
    bsz, t, d = x.shape
    f = wg.shape[1]
    tm = _pick(t, (512, 256, 128))
    tf = _pick(f, (256, 128))
    row = lambda b, i: (b, 0, 0)
    wspec = lambda shape: pl.BlockSpec(shape, lambda b, i: (0, 0), pipeline_mode=pl.Buffered(1))
    return pl.pallas_call(
        functools.partial(_ffn_kernel, tf=tf, alpha=alpha),
        grid=(bsz, t // tm),
        in_specs=[pl.BlockSpec((1, tm, d), lambda b, i: (b, i, 0)),
                  pl.BlockSpec((1, 1, d), row), pl.BlockSpec((1, 1, d), row), pl.BlockSpec((1, 1, d), row),
                  wspec((d, f)), wspec((d, f)), wspec((f, d)),
                  pl.BlockSpec((1, d), lambda b, i: (0, 0)), pl.BlockSpec((1, d), lambda b, i: (0, 0))],
        out_specs=pl.BlockSpec((1, tm, d), lambda b, i: (b, i, 0)),
        out_shape=jax.ShapeDtypeStruct((bsz, t, d), F32),
        compiler_params=_params(("parallel", "parallel")),
        name="ffn",
    )(x, shift, scale, gate, wg, wu, wo, ln_g, ln_b)


def _proj_kernel(x_ref, sh_ref, sc_ref, w_ref, cos_ref, s1_ref, s2_ref, o_ref, *, tn, n_rope_blks):
    u = (x_ref[0] * (1.0 + sc_ref[0]) + sh_ref[0]).astype(BF16)
    n = w_ref.shape[1]
    per = tn // LANE
    for j in range(n // tn):
        y = _dot(u, w_ref[:, j * tn:(j + 1) * tn])
        for i in range(per):
            blk = j * per + i
            yi = y[:, i * LANE:(i + 1) * LANE]
            if blk < n_rope_blks:
                yi = (yi * cos_ref[...] + pltpu.roll(yi, LANE - ROT_DIM // 2, 1) * s1_ref[...]
                      + pltpu.roll(yi, ROT_DIM // 2, 1) * s2_ref[...])
            o_ref[0, :, blk * LANE:(blk + 1) * LANE] = yi.astype(o_ref.dtype)


def _proj(x, shift, scale, w, tables, n_rope_blks, out_dtype, name):
    bsz, t, d = x.shape
    n = w.shape[1]
    tm = _pick(t, (512, 256, 128))
    tn = _pick(n, (256, 128))
    row = lambda b, i: (b, 0, 0)
    tab = pl.BlockSpec((tm, LANE), lambda b, i: (i, 0))
    return pl.pallas_call(
        functools.partial(_proj_kernel, tn=tn, n_rope_blks=n_rope_blks),
        grid=(bsz, t // tm),
        in_specs=[pl.BlockSpec((1, tm, d), lambda b, i: (b, i, 0)),
                  pl.BlockSpec((1, 1, d), row), pl.BlockSpec((1, 1, d), row),
                  pl.BlockSpec((d, n), lambda b, i: (0, 0), pipeline_mode=pl.Buffered(1)),
                  tab, tab, tab],
        out_specs=pl.BlockSpec((1, tm, n), lambda b, i: (b, i, 0)),
        out_shape=jax.ShapeDtypeStruct((bsz, t, n), out_dtype),
        compiler_params=_params(("parallel", "parallel")),
        name=name,
    )(x, shift, scale, w, *tables)


def _cum_kernel(p_ref, bias_ref, col_ref, row_ref, carry_ref):
    @pl.when(pl.program_id(1) == 0)
    def _():
        carry_ref[...] = jnp.zeros_like(carry_ref)

    tc = p_ref.shape[1]
    lane = lax.broadcasted_iota(I32, (tc, LANE), 1)
    lf = _log_sigmoid(p_ref[0] + bias_ref[...])
    lf = jnp.where(lane >= LANE_FG, jnp.where(lane < LANE_FG + D_HEADS, lf, 0.0), 0.0)
    ri = lax.broadcasted_iota(I32, (tc, tc), 0)
    ci = lax.broadcasted_iota(I32, (tc, tc), 1)
    tri = jnp.where(ri >= ci, 1.0, 0.0).astype(BF16)
    hi, mid, lo = _split3(lf)
    cum = _dot(tri, hi) + _dot(tri, mid) + _dot(tri, lo) + carry_ref[...]
    col_ref[0] = cum
    carry_ref[...] = cum[tc - 1:tc, :]
    sr = lax.broadcasted_iota(I32, (8, LANE), 0)
    sl = lax.broadcasted_iota(I32, (8, LANE), 1)
    sel = jnp.where(sl == sr + LANE_FG, jnp.where(sr < D_HEADS, 1.0, 0.0), 0.0).astype(BF16)
    ch, cm, cl = _split3(cum)
    row_ref[0, 0] = _nt(sel, ch) + _nt(sel, cm) + _nt(sel, cl)


def _cum(p32, bias_row, tc):
    bsz, t, _ = p32.shape
    return pl.pallas_call(
        _cum_kernel,
        grid=(bsz, t // tc),
        in_specs=[pl.BlockSpec((1, tc, LANE), lambda b, i: (b, i, BLK_IX)),
                  pl.BlockSpec((1, LANE), lambda b, i: (0, 0))],
        out_specs=[pl.BlockSpec((1, tc, LANE), lambda b, i: (b, i, 0)),
                   pl.BlockSpec((1, 1, 8, tc), lambda b, i: (b, i, 0, 0))],
        out_shape=[jax.ShapeDtypeStruct((bsz, t, LANE), F32),
                   jax.ShapeDtypeStruct((bsz, t // tc, 8, tc), F32)],
        scratch_shapes=[pltpu.VMEM((1, LANE), F32)],
        compiler_params=_params(("parallel", "arbitrary")),
        name="forget_cumsum",
    )(p32, bias_row)


def _head_halves(q):
    lane = lax.broadcasted_iota(I32, q.shape, 1)
    zero = jnp.zeros_like(q)
    return jnp.where(lane < HEAD_DIM, q, zero), jnp.where(lane >= HEAD_DIM, q, zero)


def _lane_pick(a, b):
    lane = lax.broadcasted_iota(I32, a.shape, 1)
    return jnp.where(lane < HEAD_DIM, a, b)


def _dil_kernel(q_ref, k_ref, v_ref, o_ref, lse_ref, *, kw):
    nb = q_ref.shape[1] // SPAN

    def body(i, carry):
        qs = pl.multiple_of(i * SPAN, SPAN)
        ks = pl.multiple_of(jnp.maximum(i * SPAN + SPAN - kw, 0), SPAN)
        q = q_ref[0, pl.ds(qs, SPAN), :] * 0.125
        k = k_ref[0, pl.ds(ks, kw), :]
        v = v_ref[0, pl.ds(ks, kw), :]
        qpos = qs + lax.broadcasted_iota(I32, (SPAN, kw), 0)
        kpos = ks + lax.broadcasted_iota(I32, (SPAN, kw), 1)
        dist = qpos - kpos
        outs, lses = [], []
        for qh in _head_halves(q):
            s = _nt(qh, k)
            s = jnp.where(dist >= 0, jnp.where(dist <= SPAN, s, NEG_BIG), NEG_BIG)
            m = jnp.max(s, axis=-1, keepdims=True)
            p = jnp.exp(s - m)
            l = jnp.sum(p, axis=-1, keepdims=True)
            outs.append(_dot(p.astype(BF16), v) / l)
            lses.append(jnp.broadcast_to(m + jnp.log(l), (SPAN, LANE)))
        o_ref[0, pl.ds(qs, SPAN), :] = _lane_pick(outs[0], outs[1]).astype(o_ref.dtype)
        lse_ref[0, pl.ds(qs, SPAN), :] = _lane_pick(lses[0], lses[1])
        return carry

    lax.fori_loop(0, nb, body, 0)


def _dilated(q, k, v):
    ns, ln, _ = q.shape
    kw = min(2 * SPAN, ln)
    spec = pl.BlockSpec((1, ln, LANE), lambda s: (s, 0, 0))
    return pl.pallas_call(
        functools.partial(_dil_kernel, kw=kw),
        grid=(ns,),
        in_specs=[spec, spec, spec],
        out_specs=[spec, spec],
        out_shape=[jax.ShapeDtypeStruct((ns, ln, LANE), BF16), jax.ShapeDtypeStruct((ns, ln, LANE), F32)],
        compiler_params=_params(("parallel",)),
        name="dilated_window",
    )(q, k, v)


def _fox_kernel(q_ref, k_ref, v_ref, ccol_ref, crow_ref, o_ref, *, tq):
    hp = pl.program_id(1)
    qi = pl.program_id(2)
    q0, q1 = _head_halves(q_ref[0] * 0.125)
    ccol = ccol_ref[0]
    lane = lax.broadcasted_iota(I32, (tq, LANE), 1)
    sub = lax.broadcasted_iota(I32, (8, tq), 0)
    cq = [jnp.sum(jnp.where(lane == LANE_FG + 2 * hp + h, ccol, 0.0), axis=1, keepdims=True) for h in range(2)]
    qpos = qi * tq + lax.broadcasted_iota(I32, (tq, tq), 0)
    col = lax.broadcasted_iota(I32, (tq, tq), 1)

    def body(j, carry):
        ms, ls, acc = carry
        ks = pl.multiple_of(j * tq, tq)
        k = k_ref[0, pl.ds(ks, tq), :]
        v = v_ref[0, pl.ds(ks, tq), :]
        cr = crow_ref[0, j]
        causal = (ks + col) <= qpos
        new_m, new_l, pv, alphas = [], [], [], []
        for h, qh in enumerate((q0, q1)):
            ck = jnp.sum(jnp.where(sub == 2 * hp + h, cr, 0.0), axis=0, keepdims=True)
            s = _nt(qh, k) + (cq[h] - ck)
            s = jnp.where(causal, s, NEG_BIG)
            m = jnp.maximum(ms[h], jnp.max(s, axis=-1, keepdims=True))
            alpha = jnp.exp(ms[h] - m)
            p = jnp.exp(s - m)
            new_m.append(m)
            new_l.append(alpha * ls[h] + jnp.sum(p, axis=-1, keepdims=True))
            pv.append(_dot(p.astype(BF16), v))
            alphas.append(jnp.broadcast_to(alpha, (tq, LANE)))
        acc = acc * _lane_pick(alphas[0], alphas[1]) + _lane_pick(pv[0], pv[1])
        return tuple(new_m), tuple(new_l), acc

    init_m = (jnp.full((tq, 1), NEG_BIG, F32),) * 2
    init_l = (jnp.zeros((tq, 1), F32),) * 2
    _, ls, acc = lax.fori_loop(0, qi + 1, body, (init_m, init_l, jnp.zeros((tq, LANE), F32)))
    inv = _lane_pick(jnp.broadcast_to(1.0 / ls[0], (tq, LANE)), jnp.broadcast_to(1.0 / ls[1], (tq, LANE)))
    o_ref[0] = (acc * inv).astype(o_ref.dtype)


def _fox(p16, ccol, crow, tq):
    bsz, t, _ = p16.shape
    npair = D_HEADS // 2
    nblk = t // tq
    return pl.pallas_call(
        functools.partial(_fox_kernel, tq=tq),
        grid=(bsz, npair, nblk),
        in_specs=[pl.BlockSpec((1, tq, LANE), lambda b, h, i: (b, i, BLK_DQ + h)),
                  pl.BlockSpec((1, t, LANE), lambda b, h, i: (b, 0, BLK_DK + h)),
                  pl.BlockSpec((1, t, LANE), lambda b, h, i: (b, 0, BLK_DV + h)),
                  pl.BlockSpec((1, tq, LANE), lambda b, h, i: (b, i, 0)),
                  pl.BlockSpec((1, nblk, 8, tq), lambda b, h, i: (b, 0, 0, 0))],
        out_specs=pl.BlockSpec((1, tq, LANE), lambda b, h, i: (b, i, h)),
        out_shape=jax.ShapeDtypeStruct((bsz, t, npair * LANE), BF16),
        compiler_params=_params(("parallel", "parallel", "arbitrary")),
        name="forgetting_attention",
    )(p16, p16, p16, ccol, crow)


def _sb_kernel(q_ref, k_ref, v_ref, o_ref, *, tq):
    qi = pl.program_id(2)
    q0, q1 = _head_halves(q_ref[0] * 0.125)
    qpos = qi * tq + lax.broadcasted_iota(I32, (tq, tq), 0)
    col = lax.broadcasted_iota(I32, (tq, tq), 1)
    row = lax.broadcasted_iota(I32, (tq, tq), 0)
    later = jnp.where(row > col, 1.0, 0.0).astype(BF16)

    def body(it, carry):
        rs, acc = carry
        j = qi - it
        ks = pl.multiple_of(j * tq, tq)
        k = k_ref[0, pl.ds(ks, tq), :]
        v = v_ref[0, pl.ds(ks, tq), :]
        before = (ks + col) < qpos
        new_r, av = [], []
        for h, qh in enumerate((q0, q1)):
            z = _nt(qh, k)
            lk_all = _log_sigmoid(-z)
            lk = jnp.where(before, lk_all, 0.0)
            hi = lk.astype(BF16)
            lo = (lk - hi.astype(F32)).astype(BF16)
            suffix = _dot(hi, later) + _dot(lo, later)
            a = jnp.where(before, jnp.exp(lk_all + z + suffix + rs[h]), 0.0)
            av.append(_dot(a.astype(BF16), v))
            new_r.append(rs[h] + jnp.sum(lk, axis=-1, keepdims=True))
        return tuple(new_r), acc + _lane_pick(av[0], av[1])

    init_r = (jnp.zeros((tq, 1), F32),) * 2
    _, acc = lax.fori_loop(0, qi + 1, body, (init_r, jnp.zeros((tq, LANE), F32)))
    o_ref[0] = acc.astype(o_ref.dtype)


def _stick_breaking(p16, tq):
    bsz, t, _ = p16.shape
    npair = C_HEADS // 2
    return pl.pallas_call(
        functools.partial(_sb_kernel, tq=tq),
        grid=(bsz, npair, t // tq),
        in_specs=[pl.BlockSpec((1, tq, LANE), lambda b, h, i: (b, i, BLK_CQ + h)),
                  pl.BlockSpec((1, t, LANE), lambda b, h, i: (b, 0, BLK_CK + h)),
                  pl.BlockSpec((1, t, LANE), lambda b, h, i: (b, 0, BLK_CV + h))],
        out_specs=pl.BlockSpec((1, tq, LANE), lambda b, h, i: (b, i, h)),
        out_shape=jax.ShapeDtypeStruct((bsz, t, npair * LANE), BF16),
        compiler_params=_params(("parallel", "parallel", "arbitrary")),
        name="stick_breaking",
    )(p16, p16, p16)


def _dsa_kernel(iq0_ref, iq1_ref, ixq_ref, ixk_ref, q0_ref, q1_ref, k0_ref, k1_ref, v0_ref, v1_ref,
                o_ref, key_scr, *, tq, tk, topk, idx_bits):
    qi = pl.program_id(1)
    nc = (qi * tq + tq + tk - 1) // tk
    lane = lax.broadcasted_iota(I32, (tq, LANE), 1)
    per = tk // LANE

    iq = []
    for ref in (iq0_ref, iq1_ref):
        blk = ref[0]
        iq.append(_split2(jnp.where(lane < IDX_DIM, blk, 0.0)))
        iq.append(_split2(jnp.where(lane < IDX_DIM, pltpu.roll(blk, IDX_DIM, 1), 0.0)))
    wblk = ixq_ref[0]
    wts = [jnp.sum(jnp.where(lane == LANE_IW + h, wblk, 0.0), axis=1, keepdims=True) for h in range(IDX_HEADS)]
    qpos = qi * tq + lax.broadcasted_iota(I32, (tq, tk), 0)
    col = lax.broadcasted_iota(I32, (tq, tk), 1)

    def score_chunk(c, carry):
        ks = pl.multiple_of(c * tk, tk)
        kh, kl = _split2(ixk_ref[0, pl.ds(ks, tk), :])
        sc = jnp.zeros((tq, tk), F32)
        for h in range(IDX_HEADS):
            qh_hi, qh_lo = iq[h]
            dots = _nt(qh_hi, kh) + _nt(qh_hi, kl) + _nt(qh_lo, kh)
            sc = sc + wts[h] * jnp.maximum(dots, 0.0)
        sc = jnp.where(sc == 0.0, 0.0, sc)
        bits = lax.bitcast_convert_type(sc, I32)
        key = jnp.where(bits < 0, bits ^ jnp.int32(0x7FFFFFFF), bits)
        key_scr[c] = jnp.where((ks + col) <= qpos, key, INT_MIN)
        return carry

    lax.fori_loop(0, nc, score_chunk, 0)

    def count(pred):
        def body(c, acc):
            blk = key_scr[c]
            for i in range(per):
                idx = c * tk + i * LANE + lane
                acc = acc + pred(blk[:, i * LANE:(i + 1) * LANE], idx)
            return acc
        acc = lax.fori_loop(0, nc, body, jnp.zeros((tq, LANE), F32))
        return jnp.sum(acc, axis=1, keepdims=True)

    def count_ge(cand):
        cb = jnp.broadcast_to(cand, (tq, LANE))
        return count(lambda kb, idx: jnp.where(kb >= cb, 1.0, 0.0))

    kf = jnp.float32(topk)
    th = jnp.where(count_ge(jnp.zeros((tq, 1), I32)) >= kf, 0, INT_MIN).astype(I32)

    def bisect(i, th):
        cand = th | lax.shift_left(jnp.int32(1), jnp.int32(30) - i)
        return jnp.where(count_ge(cand) >= kf, cand, th)

    th = lax.fori_loop(0, 31, bisect, th)
    thb = jnp.broadcast_to(th, (tq, LANE))
    need = kf - count(lambda kb, idx: jnp.where(kb > thb, 1.0, 0.0))

    def tie_bisect(i, m):
        cand = m | lax.shift_left(jnp.int32(1), jnp.int32(idx_bits - 1) - i)
        cb = jnp.broadcast_to(cand, (tq, LANE))
        n_lt = count(lambda kb, idx: jnp.where(kb == thb, jnp.where(idx < cb, 1.0, 0.0), 0.0))
        return jnp.where(n_lt < need, cand, m)

    cut = lax.fori_loop(0, idx_bits, tie_bisect, jnp.zeros((tq, 1), I32))

    qh = _head_halves(q0_ref[0] * 0.125) + _head_halves(q1_ref[0] * 0.125)
    k_refs = (k0_ref, k0_ref, k1_ref, k1_ref)
    v_refs = (v0_ref, v0_ref, v1_ref, v1_ref)

    def attend(c, carry):
        ms, ls, accs = carry
        ks = pl.multiple_of(c * tk, tk)
        kb = key_scr[c]
        kpos = ks + col
        sel = (kpos <= qpos) & ((kb > th) | ((kb == th) & (kpos <= cut)))
        new_m, new_l, pv, alphas = [], [], [], []
        for h in range(B_HEADS):
            k = k_refs[h][0, pl.ds(ks, tk), :]
            v = v_refs[h][0, pl.ds(ks, tk), :]
            s = jnp.where(sel, _nt(qh[h], k), NEG_BIG)
            m = jnp.maximum(ms[h], jnp.max(s, axis=-1, keepdims=True))
            alpha = jnp.exp(ms[h] - m)
            p = jnp.where(sel, jnp.exp(s - m), 0.0)
            new_m.append(m)
            new_l.append(alpha * ls[h] + jnp.sum(p, axis=-1, keepdims=True))
            pv.append(_dot(p.astype(BF16), v))
            alphas.append(jnp.broadcast_to(alpha, (tq, LANE)))
        new_acc = tuple(accs[g] * _lane_pick(alphas[2 * g], alphas[2 * g + 1])
                        + _lane_pick(pv[2 * g], pv[2 * g + 1]) for g in range(2))
        return tuple(new_m), tuple(new_l), new_acc

    init = ((jnp.full((tq, 1), NEG_BIG, F32),) * B_HEADS, (jnp.zeros((tq, 1), F32),) * B_HEADS,
            (jnp.zeros((tq, LANE), F32),) * 2)
    _, ls, accs = lax.fori_loop(0, nc, attend, init)
    for g in range(2):
        inv = _lane_pick(jnp.broadcast_to(1.0 / ls[2 * g], (tq, LANE)),
                         jnp.broadcast_to(1.0 / ls[2 * g + 1], (tq, LANE)))
        o_ref[0, :, g * LANE:(g + 1) * LANE] = (accs[g] * inv).astype(o_ref.dtype)


def _dsa(p16, p32, tq, tk):
    bsz, t, _ = p16.shape
    topk = min(DSA_TOPK, t // 4)
    idx_bits = max(1, (t - 1).bit_length())
    qspec = lambda arr_blk: pl.BlockSpec((1, tq, LANE), lambda b, i: (b, i, arr_blk))
    kspec = lambda arr_blk: pl.BlockSpec((1, t, LANE), lambda b, i: (b, 0, arr_blk))
    return pl.pallas_call(
        functools.partial(_dsa_kernel, tq=tq, tk=tk, topk=topk, idx_bits=idx_bits),
        grid=(bsz, t // tq),
        in_specs=[qspec(BLK_IQ), qspec(BLK_IQ + 1), qspec(BLK_IX), kspec(BLK_IX),
                  qspec(BLK_BQ), qspec(BLK_BQ + 1), kspec(BLK_BK), kspec(BLK_BK + 1),
                  kspec(BLK_BV), kspec(BLK_BV + 1)],
        out_specs=pl.BlockSpec((1, tq, 2 * LANE), lambda b, i: (b, i, 0)),
        out_shape=jax.ShapeDtypeStruct((bsz, t, B_HEADS * HEAD_DIM), BF16),
        scratch_shapes=[pltpu.VMEM((t // tk, tq, tk), I32)],
        compiler_params=_params(("parallel", "arbitrary")),
        name="dsa_topk_attention",
    )(p32, p32, p32, p32, p16, p16, p16, p16, p16, p16)


def _merge_kernel(x_ref, sh_ref, sc_ref, gt_ref, oa0_ref, oa1_ref, oa2_ref, la0_ref, la1_ref, la2_ref,
                  yb_ref, yc_ref, yd_ref, wg_ref, bg_ref, wb_ref, wo_ref, lg_ref, lb_ref, o_ref, *, alpha):
    x = x_ref[0]
    d = x.shape[1]
    u = (x * (1.0 + sc_ref[0]) + sh_ref[0]).astype(BF16)
    lses = [r[0] for r in (la0_ref, la1_ref, la2_ref)]
    top = jnp.maximum(jnp.maximum(lses[0], lses[1]), lses[2])
    es = [jnp.exp(l - top) for l in lses]
    den = es[0] + es[1] + es[2]
    ya = sum((e / den) * r[0].astype(F32) for e, r in zip(es, (oa0_ref, oa1_ref, oa2_ref)))
    branches = (ya.astype(BF16), yb_ref[0], yc_ref[0], yd_ref[0])
    merged = jnp.zeros(x.shape, F32)
    off = 0
    for i, br in enumerate(branches):
        gate = jax.nn.sigmoid(_dot(u, wg_ref[:, i * d:(i + 1) * d]) + bg_ref[:, i * d:(i + 1) * d])
        merged = merged + gate * _dot(br, wb_ref[off:off + BRANCH_WIDTHS[i], :])
        off += BRANCH_WIDTHS[i]
    h = _dot(merged.astype(BF16), wo_ref[...])
    o_ref[0] = _deepnorm_ln(x, h, gt_ref[0], lg_ref[...], lb_ref[...], alpha)


def _merge(x, shift, scale, gate, oas, las, yb, yc, yd, wg, bg, wb, wo, ln_g, ln_b, alpha):
    bsz, t, d = x.shape
    tm = _pick(t, (256, 128))
    row = lambda b, i: (b, 0, 0)
    tok = lambda w: pl.BlockSpec((1, tm, w), lambda b, i: (b, i, 0))
    wspec = lambda shape: pl.BlockSpec(shape, lambda b, i: (0, 0), pipeline_mode=pl.Buffered(1))
    return pl.pallas_call(
        functools.partial(_merge_kernel, alpha=alpha),
        grid=(bsz, t // tm),
        in_specs=[tok(d), pl.BlockSpec((1, 1, d), row), pl.BlockSpec((1, 1, d), row), pl.BlockSpec((1, 1, d), row),
                  tok(LANE), tok(LANE), tok(LANE), tok(LANE), tok(LANE), tok(LANE),
                  tok(2 * LANE), tok(2 * LANE), tok(2 * LANE),
                  wspec(wg.shape), wspec(bg.shape), wspec(wb.shape), wspec(wo.shape),
                  pl.BlockSpec((1, d), lambda b, i: (0, 0)), pl.BlockSpec((1, d), lambda b, i: (0, 0))],
        out_specs=tok(d),
        out_shape=jax.ShapeDtypeStruct((bsz, t, d), F32),
        compiler_params=_params(("parallel", "parallel")),
        name="gated_merge",
    )(x, shift, scale, gate, *oas, *las, yb, yc, yd, wg, bg, wb, wo, ln_g, ln_b)


def _rope_tables(t):
    half = ROT_DIM // 2
    inv_freq = ROPE_THETA ** (-(jnp.arange(half, dtype=F32) * (2.0 / ROT_DIM)))
    ang = jnp.arange(t, dtype=F32)[:, None] * inv_freq[None, :]
    cos, sin = jnp.cos(ang), jnp.sin(ang)
    ones = jnp.ones((t, HEAD_DIM - ROT_DIM), F32)
    zeros = jnp.zeros((t, HEAD_DIM - half), F32)
    c64 = jnp.concatenate([cos, cos, ones], axis=1)
    s1 = jnp.concatenate([-sin, zeros], axis=1)
    s2 = jnp.concatenate([jnp.zeros((t, half), F32), sin, zeros[:, half:]], axis=1)
    return tuple(jnp.concatenate([a, a], axis=1) for a in (c64, s1, s2))


def _mixer_weights(w_in):
    d = w_in.shape[0]

    def qkv(off, heads):
        w = heads * HEAD_DIM
        return w_in[:, off:off + w], w_in[:, off + w:off + 2 * w], w_in[:, off + 2 * w:off + 3 * w]

    aq, ak, av = qkv(OFF_A, A_HEADS)
    bq, bk, bv = qkv(OFF_B, B_HEADS)
    cq, ck, cv = qkv(OFF_C, C_HEADS)
    dq, dk, dv = qkv(OFF_D, D_HEADS)
    w16 = jnp.concatenate([aq, ak, bq, bk, av, bv, cq, ck, cv, dq, dk, dv, jnp.zeros((d, LANE), F32)], axis=1)
    ix = jnp.concatenate([w_in[:, OFF_IK:OFF_IK + IDX_DIM], jnp.zeros((d, LANE_IW - IDX_DIM), F32),
                          w_in[:, OFF_IW:OFF_IW + IDX_HEADS], w_in[:, OFF_FG:OFF_FG + D_HEADS],
                          jnp.zeros((d, LANE - LANE_FG - D_HEADS), F32)], axis=1)
    w32 = jnp.concatenate([w_in[:, OFF_IQ:OFF_IQ + IDX_HEADS * IDX_DIM], ix, jnp.zeros((d, LANE), F32)], axis=1)
    return w16.astype(BF16), w32.astype(BF16), w_in[:, OFF_GATE:].astype(BF16)


def _to_residues(a, dil):
    bsz, t, w = a.shape
    if dil == 1:
        return a
    return a.reshape(bsz, t // dil, dil, w).transpose(0, 2, 1, 3).reshape(bsz * dil, t // dil, w)


def _from_residues(a, dil, bsz):
    if dil == 1:
        return a
    _, ln, w = a.shape
    return a.reshape(bsz, dil, ln, w).transpose(0, 2, 1, 3).reshape(bsz, ln * dil, w)


def _mixer(x, shift, scale, gate, w_in, b_gate, b_forget, w_branch, w_out, ln_g, ln_b, tables, alpha):
    bsz, t, d = x.shape
    w16, w32, wg = _mixer_weights(w_in)
    p16 = _proj(x, shift, scale, w16, tables, N_ROPE_BLKS_16, BF16, "mixer_proj_bf16")
    p32 = _proj(x, shift, scale, w32, tables, N_BLKS_32, F32, "mixer_proj_f32")

    oas, las = [], []
    for g, (_, dil) in enumerate(DILATED_GROUPS):
        streams = [_to_residues(p16[:, :, (blk + g) * LANE:(blk + g + 1) * LANE], dil)
                   for blk in (BLK_AQ, BLK_AK, BLK_AV)]
        o, lse = _dilated(*streams)
        oas.append(_from_residues(o, dil, bsz))
        las.append(_from_residues(lse, dil, bsz))

    tq = _pick(t, (256, 128))
    bias_row = jnp.zeros((1, LANE), F32).at[0, LANE_FG:LANE_FG + D_HEADS].set(b_forget)
    ccol, crow = _cum(p32, bias_row, tq)
    yd = _fox(p16, ccol, crow, tq)
    yc = _stick_breaking(p16, tq)
    yb = _dsa(p16, p32, _pick(t, (128,)), _pick(t, (512, 256, 128)))
    return _merge(x, shift, scale, gate, oas, las, yb, yc, yd, wg, b_gate.reshape(1, -1),
                  w_branch.astype(BF16), w_out.astype(BF16), ln_g, ln_b, alpha)


def kernel(x, c, ada_w, ada_b, ln_g, ln_b, ffn_w_in, ffn_w_out, mix_w_in, mix_b_gate, mix_b_forget,
           mix_w_branch, mix_w_out):
    bsz, t, d = x.shape
    depth = ada_w.shape[0]
    f = ffn_w_out.shape[2]
    alpha = float((2 * depth) ** 0.25)
    assert t % (DILATED_GROUPS[-1][0]) == 0 and d % LANE == 0 and bsz <= 8

    c8 = jnp.zeros((8, d), F32).at[:bsz].set(c)
    mod = _ada(c8, ada_w, ada_b)[:, :bsz].reshape(depth, bsz, 3, 3, 1, d)
    tables = _rope_tables(t)

    for l in range(depth):
        m = lambda sub, kind: mod[l, :, sub, kind]
        lng = lambda sub: ln_g[l, sub].reshape(1, d)
        lnb = lambda sub: ln_b[l, sub].reshape(1, d)

        def ffn(x, sub, which):
            w_in = ffn_w_in[l, which].astype(BF16)
            return _ffn(x, m(sub, 0), m(sub, 1), m(sub, 2), w_in[:, :f], w_in[:, f:],
                        ffn_w_out[l, which].astype(BF16), lng(sub), lnb(sub), alpha)

        x = ffn(x, 0, 0)
        x = _mixer(x, m(1, 0), m(1, 1), m(1, 2), mix_w_in[l], mix_b_gate[l], mix_b_forget[l],
                   mix_w_branch[l], mix_w_out[l], lng(1), lnb(1), tables, alpha)
        x = ffn(x, 2, 1)
    return x
```

```python
import functools

import jax
import jax.numpy as jnp
from jax import lax
from jax.experimental import pallas as pl
from jax.experimental.pallas import tpu as pltpu

F32 = jnp.float32
BF16 = jnp.bfloat16
I32 = jnp.int32

LANE = 128
HEAD_DIM = 64
ROT_DIM = HEAD_DIM // 4
ROPE_THETA = 500000.0
DILATED_GROUPS = ((128, 1), (512, 4), (2048, 16))
SPAN = 128
A_HEADS, B_HEADS, C_HEADS, D_HEADS = 6, 4, 4, 4
IDX_HEADS, IDX_DIM = 4, 64
DSA_TOPK = 256
N_BRANCH = 4
LN_EPS = 1e-5
BRANCH_WIDTHS = (128, 256, 256, 256)

OFF_A = 0
OFF_B = OFF_A + 3 * A_HEADS * HEAD_DIM
OFF_IQ = OFF_B + 3 * B_HEADS * HEAD_DIM
OFF_IK = OFF_IQ + IDX_HEADS * IDX_DIM
OFF_IW = OFF_IK + IDX_DIM
OFF_C = OFF_IW + IDX_HEADS
OFF_D = OFF_C + 3 * C_HEADS * HEAD_DIM
OFF_FG = OFF_D + 3 * D_HEADS * HEAD_DIM
OFF_GATE = OFF_FG + D_HEADS

BLK_AQ, BLK_AK, BLK_BQ, BLK_BK = 0, 3, 6, 8
N_ROPE_BLKS_16 = 10
BLK_AV, BLK_BV = 10, 13
BLK_CQ, BLK_CK, BLK_CV = 15, 17, 19
BLK_DQ, BLK_DK, BLK_DV = 21, 23, 25
N_BLKS_16 = 28
BLK_IQ, BLK_IX = 0, 2
N_BLKS_32 = 4
LANE_IW = 80
LANE_FG = 84

INT_MIN = -2147483648
NEG_BIG = -1e30
EXP_DEAD = -104.0
STRIP = 32
VMEM_LIMIT = 56 * 1024 * 1024


def _nt(a, b):
    return lax.dot_general(a, b, (((1,), (1,)), ((), ())), preferred_element_type=F32)


def _dot(a, b):
    return jnp.dot(a, b, preferred_element_type=F32)


def _split3(x):
    hi = x.astype(BF16)
    r1 = x - hi.astype(F32)
    mid = r1.astype(BF16)
    lo = (r1 - mid.astype(F32)).astype(BF16)
    return hi, mid, lo


def _split2(x):
    hi = x.astype(BF16)
    return hi, (x - hi.astype(F32)).astype(BF16)


def _log_sigmoid(z):
    return -(jnp.maximum(-z, 0.0) + jnp.log(1.0 + jnp.exp(-jnp.abs(z))))


def _params(sem):
    return pltpu.CompilerParams(dimension_semantics=sem, vmem_limit_bytes=VMEM_LIMIT)


def _const_spec(shape):
    n = len(shape)
    return pl.BlockSpec(shape, lambda *_: (0,) * n)


def _pick(n, prefs):
    for p in prefs:
        if n % p == 0:
            return p
    return n


def _ada_kernel(c_ref, w_ref, b_ref, o_ref):
    c = c_ref[...]
    cond = c * jax.nn.sigmoid(c)
    ch, cm, _ = _split3(cond)
    wh, wm, _ = _split3(w_ref[0])
    o_ref[0] = _dot(ch, wh) + _dot(ch, wm) + _dot(cm, wh) + b_ref[0]


def _ada(c8, ada_w, ada_b):
    depth, d, n = ada_w.shape
    tn = _pick(n, (1152, 1024, 512, 256, 128))
    return pl.pallas_call(
        _ada_kernel,
        grid=(depth, n // tn),
        in_specs=[pl.BlockSpec((8, d), lambda l, j: (0, 0)),
                  pl.BlockSpec((1, d, tn), lambda l, j: (l, 0, j)),
                  pl.BlockSpec((1, 1, tn), lambda l, j: (l, 0, j))],
        out_specs=pl.BlockSpec((1, 8, tn), lambda l, j: (l, 0, j)),
        out_shape=jax.ShapeDtypeStruct((depth, 8, n), F32),
        compiler_params=_params(("parallel", "parallel")),
        name="ada_mod",
    )(c8, ada_w, ada_b.reshape(depth, 1, n))


def _deepnorm_ln(x, h, gate, g, b, alpha):
    y = alpha * x + (1.0 + gate) * h
    mu = jnp.mean(y, axis=-1, keepdims=True)
    yc = y - mu
    var = jnp.mean(yc * yc, axis=-1, keepdims=True)
    return yc * lax.rsqrt(var + LN_EPS) * g + b


def _ffn_kernel(x_ref, sh_ref, sc_ref, gt_ref, wg_ref, wu_ref, wo_ref, lg_ref, lb_ref, o_ref, *, tf, alpha):
    x = x_ref[0]
    u = (x * (1.0 + sc_ref[0]) + sh_ref[0]).astype(BF16)
    acc = jnp.zeros(x.shape, F32)
    for j in range(wg_ref.shape[1] // tf):
        g = _dot(u, wg_ref[:, j * tf:(j + 1) * tf])
        up = _dot(u, wu_ref[:, j * tf:(j + 1) * tf])
        h = (g * jax.nn.sigmoid(g) * up).astype(BF16)
        acc = acc + _dot(h, wo_ref[j * tf:(j + 1) * tf, :])
    o_ref[0] = _deepnorm_ln(x, 0.5 * acc, gt_ref[0], lg_ref[...], lb_ref[...], alpha)


def _ffn(x, shift, scale, gate, wg, wu, wo, ln_g, ln_b, alpha):
    bsz, t, d = x.shape
    f = wg.shape[1]
    tm = _pick(t, (512, 256, 128))
    tf = _pick(f, (256, 128))
    row = lambda b, i: (b, 0, 0)
    wspec = lambda shape: pl.BlockSpec(shape, lambda b, i: (0, 0), pipeline_mode=pl.Buffered(1))
    return pl.pallas_call(
        functools.partial(_ffn_kernel, tf=tf, alpha=alpha),
        grid=(bsz, t // tm),
        in_specs=[pl.BlockSpec((1, tm, d), lambda b, i: (b, i, 0)),
                  pl.BlockSpec((1, 1, d), row), pl.BlockSpec((1, 1, d), row), pl.BlockSpec((1, 1, d), row),
                  wspec((d, f)), wspec((d, f)), wspec((f, d)),
                  pl.BlockSpec((1, d), lambda b, i: (0, 0)), pl.BlockSpec((1, d), lambda b, i: (0, 0))],
        out_specs=pl.BlockSpec((1, tm, d), lambda b, i: (b, i, 0)),
        out_shape=jax.ShapeDtypeStruct((bsz, t, d), F32),
        compiler_params=_params(("parallel", "parallel")),
        name="ffn",
    )(x, shift, scale, gate, wg, wu, wo, ln_g, ln_b)


def _proj_kernel(x_ref, sh_ref, sc_ref, w_ref, cos_ref, s1_ref, s2_ref, o_ref, *, tn, n_rope_blks):
    u = (x_ref[0] * (1.0 + sc_ref[0]) + sh_ref[0]).astype(BF16)
    n = w_ref.shape[1]
    per = tn // LANE
    for j in range(n // tn):
        y = _dot(u, w_ref[:, j * tn:(j + 1) * tn])
        for i in range(per):
            blk = j * per + i
            yi = y[:, i * LANE:(i + 1) * LANE]
            if blk < n_rope_blks:
                yi = (yi * cos_ref[...] + pltpu.roll(yi, LANE - ROT_DIM // 2, 1) * s1_ref[...]
                      + pltpu.roll(yi, ROT_DIM // 2, 1) * s2_ref[...])
            o_ref[0, :, blk * LANE:(blk + 1) * LANE] = yi.astype(o_ref.dtype)


def _proj(x, shift, scale, w, tables, n_rope_blks, out_dtype, name):
    bsz, t, d = x.shape
    n = w.shape[1]
    tm = _pick(t, (512, 256, 128))
    tn = _pick(n, (256, 128))
    row = lambda b, i: (b, 0, 0)
    tab = pl.BlockSpec((tm, LANE), lambda b, i: (i, 0))
    return pl.pallas_call(
        functools.partial(_proj_kernel, tn=tn, n_rope_blks=n_rope_blks),
        grid=(bsz, t // tm),
        in_specs=[pl.BlockSpec((1, tm, d), lambda b, i: (b, i, 0)),
                  pl.BlockSpec((1, 1, d), row), pl.BlockSpec((1, 1, d), row),
                  pl.BlockSpec((d, n), lambda b, i: (0, 0), pipeline_mode=pl.Buffered(1)),
                  tab, tab, tab],
        out_specs=pl.BlockSpec((1, tm, n), lambda b, i: (b, i, 0)),
        out_shape=jax.ShapeDtypeStruct((bsz, t, n), out_dtype),
        compiler_params=_params(("parallel", "parallel")),
        name=name,
    )(x, shift, scale, w, *tables)


def _cum_kernel(p_ref, bias_ref, out_ref, carry_ref):
    @pl.when(pl.program_id(1) == 0)
    def _():
        carry_ref[...] = jnp.zeros_like(carry_ref)

    tc = p_ref.shape[1]
    lane = lax.broadcasted_iota(I32, (tc, LANE), 1)
    lf = _log_sigmoid(p_ref[0] + bias_ref[...])
    lf = jnp.where(lane >= LANE_FG, jnp.where(lane < LANE_FG + D_HEADS, lf, 0.0), 0.0)
    ri = lax.broadcasted_iota(I32, (tc, tc), 0)
    ci = lax.broadcasted_iota(I32, (tc, tc), 1)
    tri = jnp.where(ri >= ci, 1.0, 0.0).astype(BF16)
    hi, mid, lo = _split3(lf)
    cum = _dot(tri, hi) + _dot(tri, mid) + _dot(tri, lo) + carry_ref[...]
    carry_ref[...] = cum[tc - 1:tc, :]
    er = lax.broadcasted_iota(I32, (LANE, LANE), 0)
    ch, cm, cl = _split3(-cum)
    for h in range(D_HEADS):
        onehot = jnp.where(er == LANE_FG + h, 1.0, 0.0).astype(BF16)
        out_ref[0, h] = _dot(ch, onehot) + _dot(cm, onehot) + _dot(cl, onehot)


def _cum(p32, bias_row, tc):
    bsz, t, _ = p32.shape
    return pl.pallas_call(
        _cum_kernel,
        grid=(bsz, t // tc),
        in_specs=[pl.BlockSpec((1, tc, LANE), lambda b, i: (b, i, BLK_IX)),
                  pl.BlockSpec((1, LANE), lambda b, i: (0, 0))],
        out_specs=pl.BlockSpec((1, D_HEADS, tc, LANE), lambda b, i: (b, 0, i, 0)),
        out_shape=jax.ShapeDtypeStruct((bsz, D_HEADS, t, LANE), F32),
        scratch_shapes=[pltpu.VMEM((1, LANE), F32)],
        compiler_params=_params(("parallel", "arbitrary")),
        name="forget_cumsum",
    )(p32, bias_row)


def _head_halves(q):
    lane = lax.broadcasted_iota(I32, q.shape, 1)
    zero = jnp.zeros_like(q)
    return jnp.where(lane < HEAD_DIM, q, zero), jnp.where(lane >= HEAD_DIM, q, zero)


def _static_loop(n, body, carry):
    for i in range(n):
        carry = body(i, carry)
    return carry


def _fold8(x, op):
    out = x[0:8]
    for r in range(8, x.shape[0], 8):
        out = op(out, x[r:r + 8])
    return out


def _transposed_values(p16, blk, npair, tq):
    bsz, t, _ = p16.shape
    v = p16[:, :, blk * LANE:(blk + npair) * LANE].reshape(bsz, t // tq, tq, npair, LANE)
    return v.transpose(0, 3, 1, 4, 2)


def _token_major(yt):
    bsz, npair, _, t = yt.shape
    return yt.transpose(0, 3, 1, 2).reshape(bsz, t, npair * LANE)


def _lane_pick(a, b):
    lane = lax.broadcasted_iota(I32, a.shape, 1)
    return jnp.where(lane < HEAD_DIM, a, b)


def _dil_kernel(q_ref, k_ref, v_ref, o_ref, lse_ref, *, kw):
    nb = q_ref.shape[1] // SPAN

    def body(i, carry):
        qs = pl.multiple_of(i * SPAN, SPAN)
        ks = pl.multiple_of(jnp.maximum(i * SPAN + SPAN - kw, 0), SPAN)
        q = q_ref[0, pl.ds(qs, SPAN), :] * 0.125
        k = k_ref[0, pl.ds(ks, kw), :]
        v = v_ref[0, pl.ds(ks, kw), :]
        qpos = qs + lax.broadcasted_iota(I32, (SPAN, kw), 0)
        kpos = ks + lax.broadcasted_iota(I32, (SPAN, kw), 1)
        dist = qpos - kpos
        outs, lses = [], []
        for qh in _head_halves(q):
            s = _nt(qh, k)
            s = jnp.where(dist >= 0, jnp.where(dist <= SPAN, s, NEG_BIG), NEG_BIG)
            m = jnp.max(s, axis=-1, keepdims=True)
            p = jnp.exp(s - m)
            l = jnp.sum(p, axis=-1, keepdims=True)
            outs.append(_dot(p.astype(BF16), v) / l)
            lses.append(jnp.broadcast_to(m + jnp.log(l), (SPAN, LANE)))
        o_ref[0, pl.ds(qs, SPAN), :] = _lane_pick(outs[0], outs[1]).astype(o_ref.dtype)
        lse_ref[0, pl.ds(qs, SPAN), :] = _lane_pick(lses[0], lses[1])
        return carry

    lax.fori_loop(0, nb, body, 0)


def _dilated(q, k, v):
    ns, ln, _ = q.shape
    kw = min(2 * SPAN, ln)
    spec = pl.BlockSpec((1, ln, LANE), lambda s: (s, 0, 0))
    return pl.pallas_call(
        functools.partial(_dil_kernel, kw=kw),
        grid=(ns,),
        in_specs=[spec, spec, spec],
        out_specs=[spec, spec],
        out_shape=[jax.ShapeDtypeStruct((ns, ln, LANE), BF16), jax.ShapeDtypeStruct((ns, ln, LANE), F32)],
        compiler_params=_params(("parallel",)),
        name="dilated_window",
    )(q, k, v)


def _fox_kernel(q_ref, k_ref, vt_ref, bias_ref, o_ref, s_scr, p_scr, acc_scr, m_scr, l_scr, *, tq):
    qi = pl.program_id(2)
    rep = tq // LANE
    nstrip = tq // STRIP
    qh = _head_halves(q_ref[0] * 0.125)
    m_scr[...] = jnp.full(m_scr.shape, NEG_BIG, F32)
    l_scr[...] = jnp.zeros(l_scr.shape, F32)
    acc_scr[...] = jnp.zeros(acc_scr.shape, F32)
    kk = lax.broadcasted_iota(I32, (STRIP, tq), 0)
    qq = lax.broadcasted_iota(I32, (STRIP, tq), 1)

    def block(j, diag):
        ks = pl.multiple_of(j * tq, tq)
        kblk = k_ref[0, pl.ds(ks, tq), :]
        for h in range(2):
            s_scr[h] = _nt(kblk, qh[h])
        for h in range(2):
            def pass1(i, mx):
                r0 = i * STRIP
                s = s_scr[h, pl.ds(r0, STRIP), :] + jnp.tile(
                    bias_ref[0, h, pl.ds(pl.multiple_of(ks + r0, STRIP), STRIP), :], (1, rep))
                if diag:
                    s = jnp.where(kk + r0 <= qq, s, NEG_BIG)
                s_scr[h, pl.ds(r0, STRIP), :] = s
                return jnp.maximum(mx, _fold8(s, jnp.maximum))

            mx = _static_loop(nstrip, pass1, jnp.full((8, tq), NEG_BIG, F32))
            m_old = m_scr[h]
            m_new = jnp.maximum(m_old, jnp.max(mx, axis=0, keepdims=True))
            alpha = jnp.exp(m_old - m_new)
            m_scr[h] = m_new

            def pass2(i, ls):
                r0 = i * STRIP
                p = jnp.exp(s_scr[h, pl.ds(r0, STRIP), :] - m_new)
                p_scr[h, pl.ds(r0, STRIP), :] = p.astype(BF16)
                return ls + _fold8(p, jnp.add)

            ls = _static_loop(nstrip, pass2, jnp.zeros((8, tq), F32))
            l_scr[h] = alpha * l_scr[h] + jnp.sum(ls, axis=0, keepdims=True)
            pv = _dot(vt_ref[0, 0, j, h * HEAD_DIM:(h + 1) * HEAD_DIM, :], p_scr[h])
            acc_scr[h] = acc_scr[h] * alpha + pv

    def body(j, carry):
        block(j, False)
        return carry

    lax.fori_loop(0, qi, body, 0)
    block(qi, True)
    for h in range(2):
        o_ref[0, 0, h * HEAD_DIM:(h + 1) * HEAD_DIM, :] = (acc_scr[h] * (1.0 / l_scr[h])).astype(o_ref.dtype)


def _fox(p16, vt, bias, tq):
    bsz, t, _ = p16.shape
    npair = D_HEADS // 2
    nblk = t // tq
    return pl.pallas_call(
        functools.partial(_fox_kernel, tq=tq),
        grid=(bsz, npair, nblk),
        in_specs=[pl.BlockSpec((1, tq, LANE), lambda b, h, i: (b, i, BLK_DQ + h)),
                  pl.BlockSpec((1, t, LANE), lambda b, h, i: (b, 0, BLK_DK + h)),
                  pl.BlockSpec((1, 1, nblk, LANE, tq), lambda b, h, i: (b, h, 0, 0, 0)),
                  pl.BlockSpec((1, 2, t, LANE), lambda b, h, i: (b, h, 0, 0))],
        out_specs=pl.BlockSpec((1, 1, LANE, tq), lambda b, h, i: (b, h, 0, i)),
        out_shape=jax.ShapeDtypeStruct((bsz, npair, LANE, t), BF16),
        scratch_shapes=[pltpu.VMEM((2, tq, tq), F32), pltpu.VMEM((2, tq, tq), BF16),
                        pltpu.VMEM((2, HEAD_DIM, tq), F32), pltpu.VMEM((2, 1, tq), F32),
                        pltpu.VMEM((2, 1, tq), F32)],
        compiler_params=_params(("parallel", "parallel", "arbitrary")),
        name="forgetting_attention",
    )(p16, p16, vt, bias)


def _sb_kernel(q_ref, k_ref, vt_ref, o_ref, z_scr, sfx_scr, hi_scr, lo_scr, a_scr, acc_scr, r_scr, *, tq):
    qi = pl.program_id(2)
    nstrip = tq // STRIP
    qh = _head_halves(q_ref[0] * 0.125)
    acc_scr[...] = jnp.zeros(acc_scr.shape, F32)
    r_scr[...] = jnp.zeros(r_scr.shape, F32)
    kk = lax.broadcasted_iota(I32, (STRIP, tq), 0)
    qq = lax.broadcasted_iota(I32, (STRIP, tq), 1)
    row = lax.broadcasted_iota(I32, (tq, tq), 0)
    col = lax.broadcasted_iota(I32, (tq, tq), 1)
    later = jnp.where(col > row, 1.0, 0.0).astype(BF16)

    def block(j, diag):
        ks = pl.multiple_of(j * tq, tq)
        kblk = k_ref[0, pl.ds(ks, tq), :]
        for h in range(2):
            z_scr[h] = _nt(kblk, qh[h])
        for h in range(2):
            def pass1(i, rsum):
                r0 = i * STRIP
                z = z_scr[h, pl.ds(r0, STRIP), :]
                lk = _log_sigmoid(-z)
                z_scr[h, pl.ds(r0, STRIP), :] = lk + z
                if diag:
                    lk = jnp.where(kk + r0 < qq, lk, 0.0)
                hi = lk.astype(BF16)
                hi_scr[h, pl.ds(r0, STRIP), :] = hi
                lo_scr[h, pl.ds(r0, STRIP), :] = (lk - hi.astype(F32)).astype(BF16)
                return rsum + _fold8(lk, jnp.add)

            rsum = _static_loop(nstrip, pass1, jnp.zeros((8, tq), F32))
            sfx_scr[h] = _dot(later, hi_scr[h]) + _dot(later, lo_scr[h])
            r_old = r_scr[h]

            def pass2(i, carry):
                r0 = i * STRIP
                a = jnp.exp(z_scr[h, pl.ds(r0, STRIP), :] + sfx_scr[h, pl.ds(r0, STRIP), :] + r_old)
                if diag:
                    a = jnp.where(kk + r0 < qq, a, 0.0)
                a_scr[h, pl.ds(r0, STRIP), :] = a.astype(BF16)
                return carry

            _static_loop(nstrip, pass2, 0)
            acc_scr[h] = acc_scr[h] + _dot(vt_ref[0, 0, j, h * HEAD_DIM:(h + 1) * HEAD_DIM, :], a_scr[h])
            r_scr[h] = r_old + jnp.sum(rsum, axis=0, keepdims=True)

    block(qi, True)

    def live():
        return jnp.max(jnp.maximum(r_scr[0], r_scr[1])) > EXP_DEAD

    def cond(c):
        return jnp.logical_and(c[0] >= 0, c[1])

    def body(c):
        block(c[0], False)
        return c[0] - 1, live()

    lax.while_loop(cond, body, (qi - 1, live()))
    for h in range(2):
        o_ref[0, 0, h * HEAD_DIM:(h + 1) * HEAD_DIM, :] = acc_scr[h].astype(o_ref.dtype)


def _stick_breaking(p16, vt, tq):
    bsz, t, _ = p16.shape
    npair = C_HEADS // 2
    nblk = t // tq
    return pl.pallas_call(
        functools.partial(_sb_kernel, tq=tq),
        grid=(bsz, npair, nblk),
        in_specs=[pl.BlockSpec((1, tq, LANE), lambda b, h, i: (b, i, BLK_CQ + h)),
                  pl.BlockSpec((1, t, LANE), lambda b, h, i: (b, 0, BLK_CK + h)),
                  pl.BlockSpec((1, 1, nblk, LANE, tq), lambda b, h, i: (b, h, 0, 0, 0))],
        out_specs=pl.BlockSpec((1, 1, LANE, tq), lambda b, h, i: (b, h, 0, i)),
        out_shape=jax.ShapeDtypeStruct((bsz, npair, LANE, t), BF16),
        scratch_shapes=[pltpu.VMEM((2, tq, tq), F32), pltpu.VMEM((2, tq, tq), F32),
                        pltpu.VMEM((2, tq, tq), BF16), pltpu.VMEM((2, tq, tq), BF16),
                        pltpu.VMEM((2, tq, tq), BF16), pltpu.VMEM((2, HEAD_DIM, tq), F32),
                        pltpu.VMEM((2, 1, tq), F32)],
        compiler_params=_params(("parallel", "parallel", "arbitrary")),
        name="stick_breaking",
    )(p16, p16, vt)


def _dsa_kernel(iq0_ref, iq1_ref, ixq_ref, ixk_ref, q0_ref, q1_ref, k0_ref, k1_ref, v0_ref, v1_ref,
                o_ref, key_scr, *, tq, tk, topk, idx_bits):
    qi = pl.program_id(1)
    nc = (qi * tq + tq + tk - 1) // tk
    lane = lax.broadcasted_iota(I32, (tq, LANE), 1)
    per = tk // LANE

    iq = []
    for ref in (iq0_ref, iq1_ref):
        blk = ref[0]
        iq.append(_split2(jnp.where(lane < IDX_DIM, blk, 0.0)))
        iq.append(_split2(jnp.where(lane < IDX_DIM, pltpu.roll(blk, IDX_DIM, 1), 0.0)))
    wblk = ixq_ref[0]
    wts = [jnp.sum(jnp.where(lane == LANE_IW + h, wblk, 0.0), axis=1, keepdims=True) for h in range(IDX_HEADS)]
    qpos = qi * tq + lax.broadcasted_iota(I32, (tq, tk), 0)
    col = lax.broadcasted_iota(I32, (tq, tk), 1)

    def score_chunk(c, carry):
        ks = pl.multiple_of(c * tk, tk)
        kh, kl = _split2(ixk_ref[0, pl.ds(ks, tk), :])
        sc = jnp.zeros((tq, tk), F32)
        for h in range(IDX_HEADS):
            qh_hi, qh_lo = iq[h]
            dots = _nt(qh_hi, kh) + _nt(qh_hi, kl) + _nt(qh_lo, kh)
            sc = sc + wts[h] * jnp.maximum(dots, 0.0)
        sc = jnp.where(sc == 0.0, 0.0, sc)
        bits = lax.bitcast_convert_type(sc, I32)
        key = jnp.where(bits < 0, bits ^ jnp.int32(0x7FFFFFFF), bits)
        key_scr[c] = jnp.where((ks + col) <= qpos, key, INT_MIN)
        return carry

    lax.fori_loop(0, nc, score_chunk, 0)

    def count(pred):
        def body(c, acc):
            blk = key_scr[c]
            for i in range(per):
                idx = c * tk + i * LANE + lane
                acc = acc + pred(blk[:, i * LANE:(i + 1) * LANE], idx)
            return acc
        acc = lax.fori_loop(0, nc, body, jnp.zeros((tq, LANE), F32))
        return jnp.sum(acc, axis=1, keepdims=True)

    def count_ge(cand):
        cb = jnp.broadcast_to(cand, (tq, LANE))
        return count(lambda kb, idx: jnp.where(kb >= cb, 1.0, 0.0))

    kf = jnp.float32(topk)
    th = jnp.where(count_ge(jnp.zeros((tq, 1), I32)) >= kf, 0, INT_MIN).astype(I32)

    def bisect(i, th):
        cand = th | lax.shift_left(jnp.int32(1), jnp.int32(30) - i)
        return jnp.where(count_ge(cand) >= kf, cand, th)

    th = lax.fori_loop(0, 31, bisect, th)
    thb = jnp.broadcast_to(th, (tq, LANE))
    need = kf - count(lambda kb, idx: jnp.where(kb > thb, 1.0, 0.0))

    def tie_bisect(i, m):
        cand = m | lax.shift_left(jnp.int32(1), jnp.int32(idx_bits - 1) - i)
        cb = jnp.broadcast_to(cand, (tq, LANE))
        n_lt = count(lambda kb, idx: jnp.where(kb == thb, jnp.where(idx < cb, 1.0, 0.0), 0.0))
        return jnp.where(n_lt < need, cand, m)

    cut = lax.fori_loop(0, idx_bits, tie_bisect, jnp.zeros((tq, 1), I32))

    qh = _head_halves(q0_ref[0] * 0.125) + _head_halves(q1_ref[0] * 0.125)
    k_refs = (k0_ref, k0_ref, k1_ref, k1_ref)
    v_refs = (v0_ref, v0_ref, v1_ref, v1_ref)

    def attend(c, carry):
        ms, ls, accs = carry
        ks = pl.multiple_of(c * tk, tk)
        kb = key_scr[c]
        kpos = ks + col
        sel = (kpos <= qpos) & ((kb > th) | ((kb == th) & (kpos <= cut)))
        new_m, new_l, pv, alphas = [], [], [], []
        for h in range(B_HEADS):
            k = k_refs[h][0, pl.ds(ks, tk), :]
            v = v_refs[h][0, pl.ds(ks, tk), :]
            s = jnp.where(sel, _nt(qh[h], k), NEG_BIG)
            m = jnp.maximum(ms[h], jnp.max(s, axis=-1, keepdims=True))
            alpha = jnp.exp(ms[h] - m)
            p = jnp.where(sel, jnp.exp(s - m), 0.0)
            new_m.append(m)
            new_l.append(alpha * ls[h] + jnp.sum(p, axis=-1, keepdims=True))
            pv.append(_dot(p.astype(BF16), v))
            alphas.append(jnp.broadcast_to(alpha, (tq, LANE)))
        new_acc = tuple(accs[g] * _lane_pick(alphas[2 * g], alphas[2 * g + 1])
                        + _lane_pick(pv[2 * g], pv[2 * g + 1]) for g in range(2))
        return tuple(new_m), tuple(new_l), new_acc

    init = ((jnp.full((tq, 1), NEG_BIG, F32),) * B_HEADS, (jnp.zeros((tq, 1), F32),) * B_HEADS,
            (jnp.zeros((tq, LANE), F32),) * 2)
    _, ls, accs = lax.fori_loop(0, nc, attend, init)
    for g in range(2):
        inv = _lane_pick(jnp.broadcast_to(1.0 / ls[2 * g], (tq, LANE)),
                         jnp.broadcast_to(1.0 / ls[2 * g + 1], (tq, LANE)))
        o_ref[0, :, g * LANE:(g + 1) * LANE] = (accs[g] * inv).astype(o_ref.dtype)


def _dsa(p16, p32, tq, tk):
    bsz, t, _ = p16.shape
    topk = min(DSA_TOPK, t // 4)
    idx_bits = max(1, (t - 1).bit_length())
    qspec = lambda arr_blk: pl.BlockSpec((1, tq, LANE), lambda b, i: (b, i, arr_blk))
    kspec = lambda arr_blk: pl.BlockSpec((1, t, LANE), lambda b, i: (b, 0, arr_blk))
    return pl.pallas_call(
        functools.partial(_dsa_kernel, tq=tq, tk=tk, topk=topk, idx_bits=idx_bits),
        grid=(bsz, t // tq),
        in_specs=[qspec(BLK_IQ), qspec(BLK_IQ + 1), qspec(BLK_IX), kspec(BLK_IX),
                  qspec(BLK_BQ), qspec(BLK_BQ + 1), kspec(BLK_BK), kspec(BLK_BK + 1),
                  kspec(BLK_BV), kspec(BLK_BV + 1)],
        out_specs=pl.BlockSpec((1, tq, 2 * LANE), lambda b, i: (b, i, 0)),
        out_shape=jax.ShapeDtypeStruct((bsz, t, B_HEADS * HEAD_DIM), BF16),
        scratch_shapes=[pltpu.VMEM((t // tk, tq, tk), I32)],
        compiler_params=_params(("parallel", "arbitrary")),
        name="dsa_topk_attention",
    )(p32, p32, p32, p32, p16, p16, p16, p16, p16, p16)


def _merge_kernel(x_ref, sh_ref, sc_ref, gt_ref, oa0_ref, oa1_ref, oa2_ref, la0_ref, la1_ref, la2_ref,
                  yb_ref, yc_ref, yd_ref, wg_ref, bg_ref, wb_ref, wo_ref, lg_ref, lb_ref, o_ref, *, alpha):
    x = x_ref[0]
    d = x.shape[1]
    u = (x * (1.0 + sc_ref[0]) + sh_ref[0]).astype(BF16)
    lses = [r[0] for r in (la0_ref, la1_ref, la2_ref)]
    top = jnp.maximum(jnp.maximum(lses[0], lses[1]), lses[2])
    es = [jnp.exp(l - top) for l in lses]
    den = es[0] + es[1] + es[2]
    ya = sum((e / den) * r[0].astype(F32) for e, r in zip(es, (oa0_ref, oa1_ref, oa2_ref)))
    branches = (ya.astype(BF16), yb_ref[0], yc_ref[0], yd_ref[0])
    merged = jnp.zeros(x.shape, F32)
    off = 0
    for i, br in enumerate(branches):
        gate = jax.nn.sigmoid(_dot(u, wg_ref[:, i * d:(i + 1) * d]) + bg_ref[:, i * d:(i + 1) * d])
        merged = merged + gate * _dot(br, wb_ref[off:off + BRANCH_WIDTHS[i], :])
        off += BRANCH_WIDTHS[i]
    h = _dot(merged.astype(BF16), wo_ref[...])
    o_ref[0] = _deepnorm_ln(x, h, gt_ref[0], lg_ref[...], lb_ref[...], alpha)


def _merge(x, shift, scale, gate, oas, las, yb, yc, yd, wg, bg, wb, wo, ln_g, ln_b, alpha):
    bsz, t, d = x.shape
    tm = _pick(t, (256, 128))
    row = lambda b, i: (b, 0, 0)
    tok = lambda w: pl.BlockSpec((1, tm, w), lambda b, i: (b, i, 0))
    wspec = lambda shape: pl.BlockSpec(shape, lambda b, i: (0, 0), pipeline_mode=pl.Buffered(1))
    return pl.pallas_call(
        functools.partial(_merge_kernel, alpha=alpha),
        grid=(bsz, t // tm),
        in_specs=[tok(d), pl.BlockSpec((1, 1, d), row), pl.BlockSpec((1, 1, d), row), pl.BlockSpec((1, 1, d), row),
                  tok(LANE), tok(LANE), tok(LANE), tok(LANE), tok(LANE), tok(LANE),
                  tok(2 * LANE), tok(2 * LANE), tok(2 * LANE),
                  wspec(wg.shape), wspec(bg.shape), wspec(wb.shape), wspec(wo.shape),
                  pl.BlockSpec((1, d), lambda b, i: (0, 0)), pl.BlockSpec((1, d), lambda b, i: (0, 0))],
        out_specs=tok(d),
        out_shape=jax.ShapeDtypeStruct((bsz, t, d), F32),
        compiler_params=_params(("parallel", "parallel")),
        name="gated_merge",
    )(x, shift, scale, gate, *oas, *las, yb, yc, yd, wg, bg, wb, wo, ln_g, ln_b)


def _rope_tables(t):
    half = ROT_DIM // 2
    inv_freq = ROPE_THETA ** (-(jnp.arange(half, dtype=F32) * (2.0 / ROT_DIM)))
    ang = jnp.arange(t, dtype=F32)[:, None] * inv_freq[None, :]
    cos, sin = jnp.cos(ang), jnp.sin(ang)
    ones = jnp.ones((t, HEAD_DIM - ROT_DIM), F32)
    zeros = jnp.zeros((t, HEAD_DIM - half), F32)
    c64 = jnp.concatenate([cos, cos, ones], axis=1)
    s1 = jnp.concatenate([-sin, zeros], axis=1)
    s2 = jnp.concatenate([jnp.zeros((t, half), F32), sin, zeros[:, half:]], axis=1)
    return tuple(jnp.concatenate([a, a], axis=1) for a in (c64, s1, s2))


def _mixer_weights(w_in):
    d = w_in.shape[0]

    def qkv(off, heads):
        w = heads * HEAD_DIM
        return w_in[:, off:off + w], w_in[:, off + w:off + 2 * w], w_in[:, off + 2 * w:off + 3 * w]

    aq, ak, av = qkv(OFF_A, A_HEADS)
    bq, bk, bv = qkv(OFF_B, B_HEADS)
    cq, ck, cv = qkv(OFF_C, C_HEADS)
    dq, dk, dv = qkv(OFF_D, D_HEADS)
    w16 = jnp.concatenate([aq, ak, bq, bk, av, bv, cq, ck, cv, dq, dk, dv, jnp.zeros((d, LANE), F32)], axis=1)
    ix = jnp.concatenate([w_in[:, OFF_IK:OFF_IK + IDX_DIM], jnp.zeros((d, LANE_IW - IDX_DIM), F32),
                          w_in[:, OFF_IW:OFF_IW + IDX_HEADS], w_in[:, OFF_FG:OFF_FG + D_HEADS],
                          jnp.zeros((d, LANE - LANE_FG - D_HEADS), F32)], axis=1)
    w32 = jnp.concatenate([w_in[:, OFF_IQ:OFF_IQ + IDX_HEADS * IDX_DIM], ix, jnp.zeros((d, LANE), F32)], axis=1)
    return w16.astype(BF16), w32.astype(BF16), w_in[:, OFF_GATE:].astype(BF16)


def _to_residues(a, dil):
    bsz, t, w = a.shape
    if dil == 1:
        return a
    return a.reshape(bsz, t // dil, dil, w).transpose(0, 2, 1, 3).reshape(bsz * dil, t // dil, w)


def _from_residues(a, dil, bsz):
    if dil == 1:
        return a
    _, ln, w = a.shape
    return a.reshape(bsz, dil, ln, w).transpose(0, 2, 1, 3).reshape(bsz, ln * dil, w)


def _mixer(x, shift, scale, gate, w_in, b_gate, b_forget, w_branch, w_out, ln_g, ln_b, tables, alpha):
    bsz, t, d = x.shape
    w16, w32, wg = _mixer_weights(w_in)
    p16 = _proj(x, shift, scale, w16, tables, N_ROPE_BLKS_16, BF16, "mixer_proj_bf16")
    p32 = _proj(x, shift, scale, w32, tables, N_BLKS_32, F32, "mixer_proj_f32")

    oas, las = [], []
    for g, (_, dil) in enumerate(DILATED_GROUPS):
        streams = [_to_residues(p16[:, :, (blk + g) * LANE:(blk + g + 1) * LANE], dil)
                   for blk in (BLK_AQ, BLK_AK, BLK_AV)]
        o, lse = _dilated(*streams)
        oas.append(_from_residues(o, dil, bsz))
        las.append(_from_residues(lse, dil, bsz))

    tq = _pick(t, (256, 128))
    bias_row = jnp.zeros((1, LANE), F32).at[0, LANE_FG:LANE_FG + D_HEADS].set(b_forget)
    fbias = _cum(p32, bias_row, tq)
    yd = _token_major(_fox(p16, _transposed_values(p16, BLK_DV, D_HEADS // 2, tq), fbias, tq))
    yc = _token_major(_stick_breaking(p16, _transposed_values(p16, BLK_CV, C_HEADS // 2, tq), tq))
    yb = _dsa(p16, p32, _pick(t, (128,)), _pick(t, (512, 256, 128)))
    return _merge(x, shift, scale, gate, oas, las, yb, yc, yd, wg, b_gate.reshape(1, -1),
                  w_branch.astype(BF16), w_out.astype(BF16), ln_g, ln_b, alpha)


def kernel(x, c, ada_w, ada_b, ln_g, ln_b, ffn_w_in, ffn_w_out, mix_w_in, mix_b_gate, mix_b_forget,
           mix_w_branch, mix_w_out):
    bsz, t, d = x.shape
    depth = ada_w.shape[0]
    f = ffn_w_out.shape[2]
    alpha = float((2 * depth) ** 0.25)
    assert t % (DILATED_GROUPS[-1][0]) == 0 and d % LANE == 0 and bsz <= 8

    c8 = jnp.zeros((8, d), F32).at[:bsz].set(c)
    mod = _ada(c8, ada_w, ada_b)[:, :bsz].reshape(depth, bsz, 3, 3, 1, d)
    tables = _rope_tables(t)

    for l in range(depth):
        m = lambda sub, kind: mod[l, :, sub, kind]
        lng = lambda sub: ln_g[l, sub].reshape(1, d)
        lnb = lambda sub: ln_b[l, sub].reshape(1, d)

        def ffn(x, sub, which):
            w_in = ffn_w_in[l, which].astype(BF16)
            return _ffn(x, m(sub, 0), m(sub, 1), m(sub, 2), w_in[:, :f], w_in[:, f:],
                        ffn_w_out[l, which].astype(BF16), lng(sub), lnb(sub), alpha)

        x = ffn(x, 0, 0)
        x = _mixer(x, m(1, 0), m(1, 1), m(1, 2), mix_w_in[l], mix_b_gate[l], mix_b_forget[l],
                   mix_w_branch[l], mix_w_out[l], lng(1), lnb(1), tables, alpha)
        x = ffn(x, 2, 1)
    return x
```

```python
import functools

import jax
import jax.numpy as jnp
from jax import lax
from jax.experimental import pallas as pl
from jax.experimental.pallas import tpu as pltpu

F32 = jnp.float32
BF16 = jnp.bfloat16
I32 = jnp.int32

LANE = 128
HEAD_DIM = 64
ROT_DIM = HEAD_DIM // 4
ROPE_THETA = 500000.0
DILATED_GROUPS = ((128, 1), (512, 4), (2048, 16))
SPAN = 128
A_HEADS, B_HEADS, C_HEADS, D_HEADS = 6, 4, 4, 4
IDX_HEADS, IDX_DIM = 4, 64
DSA_TOPK = 256
N_BRANCH = 4
LN_EPS = 1e-5
BRANCH_WIDTHS = (128, 256, 256, 256)

OFF_A = 0
OFF_B = OFF_A + 3 * A_HEADS * HEAD_DIM
OFF_IQ = OFF_B + 3 * B_HEADS * HEAD_DIM
OFF_IK = OFF_IQ + IDX_HEADS * IDX_DIM
OFF_IW = OFF_IK + IDX_DIM
OFF_C = OFF_IW + IDX_HEADS
OFF_D = OFF_C + 3 * C_HEADS * HEAD_DIM
OFF_FG = OFF_D + 3 * D_HEADS * HEAD_DIM
OFF_GATE = OFF_FG + D_HEADS

BLK_AQ, BLK_AK, BLK_BQ, BLK_BK = 0, 3, 6, 8
N_ROPE_BLKS_16 = 10
BLK_AV, BLK_BV = 10, 13
BLK_CQ, BLK_CK, BLK_CV = 15, 17, 19
BLK_DQ, BLK_DK, BLK_DV = 21, 23, 25
N_BLKS_16 = 28
BLK_IQ, BLK_IX = 0, 2
N_BLKS_32 = 4
LANE_IW = 80
LANE_FG = 84

INT_MIN = -2147483648
NEG_BIG = -1e30
EXP_DEAD = -104.0
STRIP = 32
VMEM_LIMIT = 56 * 1024 * 1024


def _nt(a, b):
    return lax.dot_general(a, b, (((1,), (1,)), ((), ())), preferred_element_type=F32)


def _dot(a, b):
    return jnp.dot(a, b, preferred_element_type=F32)


def _split3(x):
    hi = x.astype(BF16)
    r1 = x - hi.astype(F32)
    mid = r1.astype(BF16)
    lo = (r1 - mid.astype(F32)).astype(BF16)
    return hi, mid, lo


def _split2(x):
    hi = x.astype(BF16)
    return hi, (x - hi.astype(F32)).astype(BF16)


def _log_sigmoid(z):
    return -(jnp.maximum(-z, 0.0) + jnp.log(1.0 + jnp.exp(-jnp.abs(z))))


def _params(sem):
    return pltpu.CompilerParams(dimension_semantics=sem, vmem_limit_bytes=VMEM_LIMIT)


def _const_spec(shape):
    n = len(shape)
    return pl.BlockSpec(shape, lambda *_: (0,) * n)


def _pick(n, prefs):
    for p in prefs:
        if n % p == 0:
            return p
    return n


def _ada_kernel(c_ref, w_ref, b_ref, o_ref):
    c = c_ref[...]
    cond = c * jax.nn.sigmoid(c)
    ch, cm, _ = _split3(cond)
    wh, wm, _ = _split3(w_ref[0])
    o_ref[0] = _dot(ch, wh) + _dot(ch, wm) + _dot(cm, wh) + b_ref[0]


def _ada(c8, ada_w, ada_b):
    depth, d, n = ada_w.shape
    tn = _pick(n, (1152, 1024, 512, 256, 128))
    return pl.pallas_call(
        _ada_kernel,
        grid=(depth, n // tn),
        in_specs=[pl.BlockSpec((8, d), lambda l, j: (0, 0)),
                  pl.BlockSpec((1, d, tn), lambda l, j: (l, 0, j)),
                  pl.BlockSpec((1, 1, tn), lambda l, j: (l, 0, j))],
        out_specs=pl.BlockSpec((1, 8, tn), lambda l, j: (l, 0, j)),
        out_shape=jax.ShapeDtypeStruct((depth, 8, n), F32),
        compiler_params=_params(("parallel", "parallel")),
        name="ada_mod",
    )(c8, ada_w, ada_b.reshape(depth, 1, n))


def _deepnorm_ln(x, h, gate, g, b, alpha):
    y = alpha * x + (1.0 + gate) * h
    mu = jnp.mean(y, axis=-1, keepdims=True)
    yc = y - mu
    var = jnp.mean(yc * yc, axis=-1, keepdims=True)
    return yc * lax.rsqrt(var + LN_EPS) * g + b


def _ffn_kernel(x_ref, sh_ref, sc_ref, gt_ref, wg_ref, wu_ref, wo_ref, lg_ref, lb_ref, o_ref, *, tf, alpha):
    x = x_ref[0]
    u = (x * (1.0 + sc_ref[0]) + sh_ref[0]).astype(BF16)
    acc = jnp.zeros(x.shape, F32)
    for j in range(wg_ref.shape[1] // tf):
        g = _dot(u, wg_ref[:, j * tf:(j + 1) * tf])
        up = _dot(u, wu_ref[:, j * tf:(j + 1) * tf])
        h = (g * jax.nn.sigmoid(g) * up).astype(BF16)
        acc = acc + _dot(h, wo_ref[j * tf:(j + 1) * tf, :])
    o_ref[0] = _deepnorm_ln(x, 0.5 * acc, gt_ref[0], lg_ref[...], lb_ref[...], alpha)


def _ffn(x, shift, scale, gate, wg, wu, wo, ln_g, ln_b, alpha):
    bsz, t, d = x.shape
    f = wg.shape[1]
    tm = _pick(t, (512, 256, 128))
    tf = _pick(f, (256, 128))
    row = lambda b, i: (b, 0, 0)
    wspec = lambda shape: pl.BlockSpec(shape, lambda b, i: (0, 0), pipeline_mode=pl.Buffered(1))
    return pl.pallas_call(
        functools.partial(_ffn_kernel, tf=tf, alpha=alpha),
        grid=(bsz, t // tm),
        in_specs=[pl.BlockSpec((1, tm, d), lambda b, i: (b, i, 0)),
                  pl.BlockSpec((1, 1, d), row), pl.BlockSpec((1, 1, d), row), pl.BlockSpec((1, 1, d), row),
                  wspec((d, f)), wspec((d, f)), wspec((f, d)),
                  pl.BlockSpec((1, d), lambda b, i: (0, 0)), pl.BlockSpec((1, d), lambda b, i: (0, 0))],
        out_specs=pl.BlockSpec((1, tm, d), lambda b, i: (b, i, 0)),
        out_shape=jax.ShapeDtypeStruct((bsz, t, d), F32),
        compiler_params=_params(("parallel", "parallel")),
        name="ffn",
    )(x, shift, scale, gate, wg, wu, wo, ln_g, ln_b)


def _proj_kernel(x_ref, sh_ref, sc_ref, w_ref, cos_ref, s1_ref, s2_ref, o_ref, *, tn, n_rope_blks):
    u = (x_ref[0] * (1.0 + sc_ref[0]) + sh_ref[0]).astype(BF16)
    n = w_ref.shape[1]
    per = tn // LANE
    for j in range(n // tn):
        y = _dot(u, w_ref[:, j * tn:(j + 1) * tn])
        for i in range(per):
            blk = j * per + i
            yi = y[:, i * LANE:(i + 1) * LANE]
            if blk < n_rope_blks:
                yi = (yi * cos_ref[...] + pltpu.roll(yi, LANE - ROT_DIM // 2, 1) * s1_ref[...]
                      + pltpu.roll(yi, ROT_DIM // 2, 1) * s2_ref[...])
            o_ref[0, :, blk * LANE:(blk + 1) * LANE] = yi.astype(o_ref.dtype)


def _proj(x, shift, scale, w, tables, n_rope_blks, out_dtype, name):
    bsz, t, d = x.shape
    n = w.shape[1]
    tm = _pick(t, (512, 256, 128))
    tn = _pick(n, (256, 128))
    row = lambda b, i: (b, 0, 0)
    tab = pl.BlockSpec((tm, LANE), lambda b, i: (i, 0))
    return pl.pallas_call(
        functools.partial(_proj_kernel, tn=tn, n_rope_blks=n_rope_blks),
        grid=(bsz, t // tm),
        in_specs=[pl.BlockSpec((1, tm, d), lambda b, i: (b, i, 0)),
                  pl.BlockSpec((1, 1, d), row), pl.BlockSpec((1, 1, d), row),
                  pl.BlockSpec((d, n), lambda b, i: (0, 0), pipeline_mode=pl.Buffered(1)),
                  tab, tab, tab],
        out_specs=pl.BlockSpec((1, tm, n), lambda b, i: (b, i, 0)),
        out_shape=jax.ShapeDtypeStruct((bsz, t, n), out_dtype),
        compiler_params=_params(("parallel", "parallel")),
        name=name,
    )(x, shift, scale, w, *tables)


def _cum_kernel(p_ref, bias_ref, out_ref, kcat_ref, carry_ref):
    @pl.when(pl.program_id(1) == 0)
    def _():
        carry_ref[...] = jnp.zeros_like(carry_ref)

    tc = p_ref.shape[1]
    lane = lax.broadcasted_iota(I32, (tc, LANE), 1)
    lf = _log_sigmoid(p_ref[0] + bias_ref[...])
    lf = jnp.where(lane >= LANE_FG, jnp.where(lane < LANE_FG + D_HEADS, lf, 0.0), 0.0)
    ri = lax.broadcasted_iota(I32, (tc, tc), 0)
    ci = lax.broadcasted_iota(I32, (tc, tc), 1)
    tri = jnp.where(ri >= ci, 1.0, 0.0).astype(BF16)
    hi, mid, lo = _split3(lf)
    cum = _dot(tri, hi) + _dot(tri, mid) + _dot(tri, lo) + carry_ref[...]
    carry_ref[...] = cum[tc - 1:tc, :]
    er = lax.broadcasted_iota(I32, (LANE, LANE), 0)
    ch, cm, cl = _split3(-cum)
    for h in range(D_HEADS):
        onehot = jnp.where(er == LANE_FG + h, 1.0, 0.0).astype(BF16)
        out_ref[0, h] = _dot(ch, onehot) + _dot(cm, onehot) + _dot(cl, onehot)
    x = p_ref[0]
    hi = x.astype(BF16).astype(F32)
    kcat_ref[0, :, 0:LANE] = jnp.where(lane < IDX_DIM, hi, pltpu.roll(hi, IDX_DIM, 1)).astype(BF16)
    kcat_ref[0, :, LANE:2 * LANE] = jnp.where(lane < IDX_DIM, x - hi, 0.0).astype(BF16)


def _cum(p32, bias_row, tc):
    bsz, t, _ = p32.shape
    return pl.pallas_call(
        _cum_kernel,
        grid=(bsz, t // tc),
        in_specs=[pl.BlockSpec((1, tc, LANE), lambda b, i: (b, i, BLK_IX)),
                  pl.BlockSpec((1, LANE), lambda b, i: (0, 0))],
        out_specs=[pl.BlockSpec((1, D_HEADS, tc, LANE), lambda b, i: (b, 0, i, 0)),
                   pl.BlockSpec((1, tc, 2 * LANE), lambda b, i: (b, i, 0))],
        out_shape=[jax.ShapeDtypeStruct((bsz, D_HEADS, t, LANE), F32),
                   jax.ShapeDtypeStruct((bsz, t, 2 * LANE), BF16)],
        scratch_shapes=[pltpu.VMEM((1, LANE), F32)],
        compiler_params=_params(("parallel", "arbitrary")),
        name="forget_cumsum",
    )(p32, bias_row)


def _head_halves(q):
    lane = lax.broadcasted_iota(I32, q.shape, 1)
    zero = jnp.zeros_like(q)
    return jnp.where(lane < HEAD_DIM, q, zero), jnp.where(lane >= HEAD_DIM, q, zero)


def _static_loop(n, body, carry):
    for i in range(n):
        carry = body(i, carry)
    return carry


def _fold8(x, op):
    out = x[0:8]
    for r in range(8, x.shape[0], 8):
        out = op(out, x[r:r + 8])
    return out


def _transposed_values(p16, blk, npair, tq):
    bsz, t, _ = p16.shape
    v = p16[:, :, blk * LANE:(blk + npair) * LANE].reshape(bsz, t // tq, tq, npair, LANE)
    return v.transpose(0, 3, 1, 4, 2)


def _token_major(yt):
    bsz, npair, _, t = yt.shape
    return yt.transpose(0, 3, 1, 2).reshape(bsz, t, npair * LANE)


def _lane_pick(a, b):
    lane = lax.broadcasted_iota(I32, a.shape, 1)
    return jnp.where(lane < HEAD_DIM, a, b)


def _dil_kernel(q_ref, k_ref, v_ref, o_ref, lse_ref, *, kw):
    nb = q_ref.shape[1] // SPAN

    def body(i, carry):
        qs = pl.multiple_of(i * SPAN, SPAN)
        ks = pl.multiple_of(jnp.maximum(i * SPAN + SPAN - kw, 0), SPAN)
        q = q_ref[0, pl.ds(qs, SPAN), :] * 0.125
        k = k_ref[0, pl.ds(ks, kw), :]
        v = v_ref[0, pl.ds(ks, kw), :]
        qpos = qs + lax.broadcasted_iota(I32, (SPAN, kw), 0)
        kpos = ks + lax.broadcasted_iota(I32, (SPAN, kw), 1)
        dist = qpos - kpos
        outs, lses = [], []
        for qh in _head_halves(q):
            s = _nt(qh, k)
            s = jnp.where(dist >= 0, jnp.where(dist <= SPAN, s, NEG_BIG), NEG_BIG)
            m = jnp.max(s, axis=-1, keepdims=True)
            p = jnp.exp(s - m)
            l = jnp.sum(p, axis=-1, keepdims=True)
            outs.append(_dot(p.astype(BF16), v) / l)
            lses.append(jnp.broadcast_to(m + jnp.log(l), (SPAN, LANE)))
        o_ref[0, pl.ds(qs, SPAN), :] = _lane_pick(outs[0], outs[1]).astype(o_ref.dtype)
        lse_ref[0, pl.ds(qs, SPAN), :] = _lane_pick(lses[0], lses[1])
        return carry

    lax.fori_loop(0, nb, body, 0)


def _dilated(q, k, v):
    ns, ln, _ = q.shape
    kw = min(2 * SPAN, ln)
    spec = pl.BlockSpec((1, ln, LANE), lambda s: (s, 0, 0))
    return pl.pallas_call(
        functools.partial(_dil_kernel, kw=kw),
        grid=(ns,),
        in_specs=[spec, spec, spec],
        out_specs=[spec, spec],
        out_shape=[jax.ShapeDtypeStruct((ns, ln, LANE), BF16), jax.ShapeDtypeStruct((ns, ln, LANE), F32)],
        compiler_params=_params(("parallel",)),
        name="dilated_window",
    )(q, k, v)


def _fox_kernel(q_ref, k_ref, vt_ref, bias_ref, o_ref, s_scr, p_scr, acc_scr, m_scr, l_scr, *, tq):
    qi = pl.program_id(2)
    rep = tq // LANE
    nstrip = tq // STRIP
    qh = _head_halves(q_ref[0] * 0.125)
    m_scr[...] = jnp.full(m_scr.shape, NEG_BIG, F32)
    l_scr[...] = jnp.zeros(l_scr.shape, F32)
    acc_scr[...] = jnp.zeros(acc_scr.shape, F32)
    kk = lax.broadcasted_iota(I32, (STRIP, tq), 0)
    qq = lax.broadcasted_iota(I32, (STRIP, tq), 1)

    def block(j, diag):
        ks = pl.multiple_of(j * tq, tq)
        kblk = k_ref[0, pl.ds(ks, tq), :]
        for h in range(2):
            s_scr[h] = _nt(kblk, qh[h])
        for h in range(2):
            def pass1(i, mx):
                r0 = i * STRIP
                s = s_scr[h, pl.ds(r0, STRIP), :] + jnp.tile(
                    bias_ref[0, h, pl.ds(pl.multiple_of(ks + r0, STRIP), STRIP), :], (1, rep))
                if diag:
                    s = jnp.where(kk + r0 <= qq, s, NEG_BIG)
                s_scr[h, pl.ds(r0, STRIP), :] = s
                return jnp.maximum(mx, _fold8(s, jnp.maximum))

            mx = _static_loop(nstrip, pass1, jnp.full((8, tq), NEG_BIG, F32))
            m_old = m_scr[h]
            m_new = jnp.maximum(m_old, jnp.max(mx, axis=0, keepdims=True))
            alpha = jnp.exp(m_old - m_new)
            m_scr[h] = m_new

            def pass2(i, ls):
                r0 = i * STRIP
                p = jnp.exp(s_scr[h, pl.ds(r0, STRIP), :] - m_new)
                p_scr[h, pl.ds(r0, STRIP), :] = p.astype(BF16)
                return ls + _fold8(p, jnp.add)

            ls = _static_loop(nstrip, pass2, jnp.zeros((8, tq), F32))
            l_scr[h] = alpha * l_scr[h] + jnp.sum(ls, axis=0, keepdims=True)
            pv = _dot(vt_ref[0, 0, j, h * HEAD_DIM:(h + 1) * HEAD_DIM, :], p_scr[h])
            acc_scr[h] = acc_scr[h] * alpha + pv

    def body(j, carry):
        block(j, False)
        return carry

    lax.fori_loop(0, qi, body, 0)
    block(qi, True)
    for h in range(2):
        o_ref[0, 0, h * HEAD_DIM:(h + 1) * HEAD_DIM, :] = (acc_scr[h] * (1.0 / l_scr[h])).astype(o_ref.dtype)


def _fox(p16, vt, bias, tq):
    bsz, t, _ = p16.shape
    npair = D_HEADS // 2
    nblk = t // tq
    return pl.pallas_call(
        functools.partial(_fox_kernel, tq=tq),
        grid=(bsz, npair, nblk),
        in_specs=[pl.BlockSpec((1, tq, LANE), lambda b, h, i: (b, i, BLK_DQ + h)),
                  pl.BlockSpec((1, t, LANE), lambda b, h, i: (b, 0, BLK_DK + h)),
                  pl.BlockSpec((1, 1, nblk, LANE, tq), lambda b, h, i: (b, h, 0, 0, 0)),
                  pl.BlockSpec((1, 2, t, LANE), lambda b, h, i: (b, h, 0, 0))],
        out_specs=pl.BlockSpec((1, 1, LANE, tq), lambda b, h, i: (b, h, 0, i)),
        out_shape=jax.ShapeDtypeStruct((bsz, npair, LANE, t), BF16),
        scratch_shapes=[pltpu.VMEM((2, tq, tq), F32), pltpu.VMEM((2, tq, tq), BF16),
                        pltpu.VMEM((2, HEAD_DIM, tq), F32), pltpu.VMEM((2, 1, tq), F32),
                        pltpu.VMEM((2, 1, tq), F32)],
        compiler_params=_params(("parallel", "parallel", "arbitrary")),
        name="forgetting_attention",
    )(p16, p16, vt, bias)


def _sb_kernel(q_ref, k_ref, vt_ref, o_ref, z_scr, sfx_scr, hi_scr, lo_scr, a_scr, acc_scr, r_scr, *, tq):
    qi = pl.program_id(2)
    nstrip = tq // STRIP
    qh = _head_halves(q_ref[0] * 0.125)
    acc_scr[...] = jnp.zeros(acc_scr.shape, F32)
    r_scr[...] = jnp.zeros(r_scr.shape, F32)
    kk = lax.broadcasted_iota(I32, (STRIP, tq), 0)
    qq = lax.broadcasted_iota(I32, (STRIP, tq), 1)
    row = lax.broadcasted_iota(I32, (tq, tq), 0)
    col = lax.broadcasted_iota(I32, (tq, tq), 1)
    later = jnp.where(col > row, 1.0, 0.0).astype(BF16)

    def block(j, diag):
        ks = pl.multiple_of(j * tq, tq)
        kblk = k_ref[0, pl.ds(ks, tq), :]
        for h in range(2):
            z_scr[h] = _nt(kblk, qh[h])
        for h in range(2):
            def pass1(i, rsum):
                r0 = i * STRIP
                z = z_scr[h, pl.ds(r0, STRIP), :]
                lk = _log_sigmoid(-z)
                z_scr[h, pl.ds(r0, STRIP), :] = lk + z
                if diag:
                    lk = jnp.where(kk + r0 < qq, lk, 0.0)
                hi = lk.astype(BF16)
                hi_scr[h, pl.ds(r0, STRIP), :] = hi
                lo_scr[h, pl.ds(r0, STRIP), :] = (lk - hi.astype(F32)).astype(BF16)
                return rsum + _fold8(lk, jnp.add)

            rsum = _static_loop(nstrip, pass1, jnp.zeros((8, tq), F32))
            sfx_scr[h] = _dot(later, hi_scr[h]) + _dot(later, lo_scr[h])
            r_old = r_scr[h]

            def pass2(i, carry):
                r0 = i * STRIP
                a = jnp.exp(z_scr[h, pl.ds(r0, STRIP), :] + sfx_scr[h, pl.ds(r0, STRIP), :] + r_old)
                if diag:
                    a = jnp.where(kk + r0 < qq, a, 0.0)
                a_scr[h, pl.ds(r0, STRIP), :] = a.astype(BF16)
                return carry

            _static_loop(nstrip, pass2, 0)
            acc_scr[h] = acc_scr[h] + _dot(vt_ref[0, 0, j, h * HEAD_DIM:(h + 1) * HEAD_DIM, :], a_scr[h])
            r_scr[h] = r_old + jnp.sum(rsum, axis=0, keepdims=True)

    block(qi, True)

    def live():
        return jnp.max(jnp.maximum(r_scr[0], r_scr[1])) > EXP_DEAD

    def cond(c):
        return jnp.logical_and(c[0] >= 0, c[1])

    def body(c):
        block(c[0], False)
        return c[0] - 1, live()

    lax.while_loop(cond, body, (qi - 1, live()))
    for h in range(2):
        o_ref[0, 0, h * HEAD_DIM:(h + 1) * HEAD_DIM, :] = acc_scr[h].astype(o_ref.dtype)


def _stick_breaking(p16, vt, tq):
    bsz, t, _ = p16.shape
    npair = C_HEADS // 2
    nblk = t // tq
    return pl.pallas_call(
        functools.partial(_sb_kernel, tq=tq),
        grid=(bsz, npair, nblk),
        in_specs=[pl.BlockSpec((1, tq, LANE), lambda b, h, i: (b, i, BLK_CQ + h)),
                  pl.BlockSpec((1, t, LANE), lambda b, h, i: (b, 0, BLK_CK + h)),
                  pl.BlockSpec((1, 1, nblk, LANE, tq), lambda b, h, i: (b, h, 0, 0, 0))],
        out_specs=pl.BlockSpec((1, 1, LANE, tq), lambda b, h, i: (b, h, 0, i)),
        out_shape=jax.ShapeDtypeStruct((bsz, npair, LANE, t), BF16),
        scratch_shapes=[pltpu.VMEM((2, tq, tq), F32), pltpu.VMEM((2, tq, tq), F32),
                        pltpu.VMEM((2, tq, tq), BF16), pltpu.VMEM((2, tq, tq), BF16),
                        pltpu.VMEM((2, tq, tq), BF16), pltpu.VMEM((2, HEAD_DIM, tq), F32),
                        pltpu.VMEM((2, 1, tq), F32)],
        compiler_params=_params(("parallel", "parallel", "arbitrary")),
        name="stick_breaking",
    )(p16, p16, vt)


def _dsa_kernel(iq0_ref, iq1_ref, ixq_ref, kcat_ref, q0_ref, q1_ref, k0_ref, k1_ref, vt_ref, o_ref,
                key_scr, s_scr, p_scr, eq_scr, rank_scr, bias_scr, acc_scr, m_scr, l_scr, *, tq, topk):
    qi = pl.program_id(1)
    nstrip = tq // STRIP
    lane = lax.broadcasted_iota(I32, (tq, LANE), 1)
    kk = lax.broadcasted_iota(I32, (STRIP, tq), 0)
    qq = lax.broadcasted_iota(I32, (STRIP, tq), 1)

    qcat = []
    for ref in (iq0_ref, iq1_ref):
        for half in range(2):
            x = ref[0] if half == 0 else pltpu.roll(ref[0], IDX_DIM, 1)
            hi = x.astype(BF16).astype(F32)
            first = jnp.where(lane < IDX_DIM, hi, pltpu.roll(x - hi, IDX_DIM, 1))
            second = jnp.where(lane < IDX_DIM, hi, 0.0)
            qcat.append(jnp.concatenate([first, second], axis=1).astype(BF16))
    sr = lax.broadcasted_iota(I32, (8, LANE), 0)
    sl = lax.broadcasted_iota(I32, (8, LANE), 1)
    onehot = jnp.where(sl == sr + LANE_IW, 1.0, 0.0).astype(BF16)
    wh, wm, wl = _split3(ixq_ref[0])
    wt = _nt(onehot, wh) + _nt(onehot, wm) + _nt(onehot, wl)

    def score_chunk(c, diag):
        ks = pl.multiple_of(c * tq, tq)
        kc = kcat_ref[0, pl.ds(ks, tq), :]
        for h in range(IDX_HEADS):
            s_scr[h] = _nt(kc, qcat[h])
        for i in range(nstrip):
            r0 = i * STRIP
            sc = jnp.zeros((STRIP, tq), F32)
            for h in range(IDX_HEADS):
                sc = sc + wt[h:h + 1, :] * jnp.maximum(s_scr[h, r0:r0 + STRIP, :], 0.0)
            sc = jnp.where(sc == 0.0, 0.0, sc)
            bits = lax.bitcast_convert_type(sc, I32)
            key = jnp.where(bits < 0, bits ^ jnp.int32(0x7FFFFFFF), bits)
            if diag:
                key = jnp.where(kk + r0 <= qq, key, INT_MIN)
            key_scr[c, r0:r0 + STRIP, :] = key

    def score_body(c, carry):
        score_chunk(c, False)
        return carry

    lax.fori_loop(0, qi, score_body, 0)
    score_chunk(qi, True)

    def count(pred):
        def body(c, acc):
            for i in range(nstrip):
                acc = acc + _fold8(jnp.where(pred(key_scr[c, i * STRIP:(i + 1) * STRIP, :]), 1.0, 0.0), jnp.add)
            return acc
        acc = lax.fori_loop(0, qi + 1, body, jnp.zeros((8, tq), F32))
        return jnp.sum(acc, axis=0, keepdims=True)

    kf = jnp.float32(topk)
    n_ge0 = count(lambda kb: kb >= 0)
    n_gt0 = count(lambda kb: kb > 0)
    th0 = jnp.where(n_ge0 >= kf, 0, INT_MIN).astype(I32)
    settled0 = jnp.where(n_ge0 >= kf, jnp.where(n_gt0 < kf, 1.0, 0.0), 0.0)

    def unsettled(settled):
        return jnp.min(settled) == 0.0

    def search_cond(c):
        return jnp.logical_and(c[0] >= 0, c[3])

    def search_body(c):
        bit, th, settled, _ = c
        cand = th | lax.shift_left(jnp.int32(1), bit)
        n = count(lambda kb: kb >= cand)
        th = jnp.where(n >= kf, jnp.where(settled == 0.0, cand, th), th)
        settled = jnp.maximum(settled, jnp.where(n == kf, 1.0, 0.0))
        return bit - 1, th, settled, unsettled(settled)

    _, th, _, _ = lax.while_loop(search_cond, search_body, (jnp.int32(30), th0, settled0, unsettled(settled0)))
    need = kf - count(lambda kb: kb > th)

    qh = _head_halves(q0_ref[0] * 0.125) + _head_halves(q1_ref[0] * 0.125)
    k_refs = (k0_ref, k0_ref, k1_ref, k1_ref)
    m_scr[...] = jnp.full(m_scr.shape, NEG_BIG, F32)
    l_scr[...] = jnp.zeros(l_scr.shape, F32)
    acc_scr[...] = jnp.zeros(acc_scr.shape, F32)
    row = lax.broadcasted_iota(I32, (tq, tq), 0)
    col = lax.broadcasted_iota(I32, (tq, tq), 1)
    upto = jnp.where(col <= row, 1.0, 0.0).astype(BF16)

    def attend_chunk(c, diag, seen):
        ks = pl.multiple_of(c * tq, tq)
        for i in range(nstrip):
            r0 = i * STRIP
            eq_scr[r0:r0 + STRIP, :] = jnp.where(key_scr[c, r0:r0 + STRIP, :] == th, 1.0, 0.0).astype(BF16)
        rank_scr[...] = _dot(upto, eq_scr[...])
        for i in range(nstrip):
            r0 = i * STRIP
            kb = key_scr[c, r0:r0 + STRIP, :]
            tie = jnp.where(rank_scr[r0:r0 + STRIP, :] + seen <= need, 0.0, NEG_BIG)
            bias = jnp.where(kb > th, 0.0, jnp.where(kb == th, tie, NEG_BIG))
            if diag:
                bias = jnp.where(kk + r0 <= qq, bias, NEG_BIG)
            bias_scr[r0:r0 + STRIP, :] = bias
        seen = seen + rank_scr[tq - 1:tq, :]
        for h in range(B_HEADS):
            s_scr[h] = _nt(k_refs[h][0, pl.ds(ks, tq), :], qh[h])
        for h in range(B_HEADS):
            def pass1(i, mx):
                r0 = i * STRIP
                s = s_scr[h, r0:r0 + STRIP, :] + bias_scr[r0:r0 + STRIP, :]
                s_scr[h, r0:r0 + STRIP, :] = s
                return jnp.maximum(mx, _fold8(s, jnp.maximum))

            mx = _static_loop(nstrip, pass1, jnp.full((8, tq), NEG_BIG, F32))
            m_old = m_scr[h]
            m_new = jnp.maximum(m_old, jnp.max(mx, axis=0, keepdims=True))
            alpha = jnp.exp(m_old - m_new)
            m_scr[h] = m_new

            def pass2(i, ls):
                r0 = i * STRIP
                p = jnp.exp(s_scr[h, r0:r0 + STRIP, :] - m_new)
                p_scr[h, r0:r0 + STRIP, :] = p.astype(BF16)
                return ls + _fold8(p, jnp.add)

            ls = _static_loop(nstrip, pass2, jnp.zeros((8, tq), F32))
            l_scr[h] = alpha * l_scr[h] + jnp.sum(ls, axis=0, keepdims=True)
            lo = (h % 2) * HEAD_DIM
            acc_scr[h] = acc_scr[h] * alpha + _dot(vt_ref[0, h // 2, c, lo:lo + HEAD_DIM, :], p_scr[h])
        return seen

    seen = lax.fori_loop(0, qi, lambda c, seen: attend_chunk(c, False, seen), jnp.zeros((1, tq), F32))
    attend_chunk(qi, True, seen)
    for h in range(B_HEADS):
        lo = (h % 2) * HEAD_DIM
        o_ref[0, h // 2, lo:lo + HEAD_DIM, :] = (acc_scr[h] * (1.0 / l_scr[h])).astype(o_ref.dtype)


def _dsa(p16, p32, kcat, vt, tq):
    bsz, t, _ = p16.shape
    topk = min(DSA_TOPK, t // 4)
    npair = B_HEADS // 2
    nblk = t // tq
    qspec = lambda arr_blk: pl.BlockSpec((1, tq, LANE), lambda b, i: (b, i, arr_blk))
    kspec = lambda arr_blk: pl.BlockSpec((1, t, LANE), lambda b, i: (b, 0, arr_blk))
    sq = lambda n, dt: pltpu.VMEM((n, tq, tq), dt) if n else pltpu.VMEM((tq, tq), dt)
    return pl.pallas_call(
        functools.partial(_dsa_kernel, tq=tq, topk=topk),
        grid=(bsz, nblk),
        in_specs=[qspec(BLK_IQ), qspec(BLK_IQ + 1), qspec(BLK_IX),
                  pl.BlockSpec((1, t, 2 * LANE), lambda b, i: (b, 0, 0)),
                  qspec(BLK_BQ), qspec(BLK_BQ + 1), kspec(BLK_BK), kspec(BLK_BK + 1),
                  pl.BlockSpec((1, npair, nblk, LANE, tq), lambda b, i: (b, 0, 0, 0, 0))],
        out_specs=pl.BlockSpec((1, npair, LANE, tq), lambda b, i: (b, 0, 0, i)),
        out_shape=jax.ShapeDtypeStruct((bsz, npair, LANE, t), BF16),
        scratch_shapes=[sq(nblk, I32), sq(B_HEADS, F32), sq(B_HEADS, BF16), sq(0, BF16), sq(0, F32), sq(0, F32),
                        pltpu.VMEM((B_HEADS, HEAD_DIM, tq), F32), pltpu.VMEM((B_HEADS, 1, tq), F32),
                        pltpu.VMEM((B_HEADS, 1, tq), F32)],
        compiler_params=_params(("parallel", "arbitrary")),
        name="dsa_topk_attention",
    )(p32, p32, p32, kcat, p16, p16, p16, p16, vt)


def _merge_kernel(x_ref, sh_ref, sc_ref, gt_ref, oa0_ref, oa1_ref, oa2_ref, la0_ref, la1_ref, la2_ref,
                  yb_ref, yc_ref, yd_ref, wg_ref, bg_ref, wb_ref, wo_ref, lg_ref, lb_ref, o_ref, *, alpha):
    x = x_ref[0]
    d = x.shape[1]
    u = (x * (1.0 + sc_ref[0]) + sh_ref[0]).astype(BF16)
    lses = [r[0] for r in (la0_ref, la1_ref, la2_ref)]
    top = jnp.maximum(jnp.maximum(lses[0], lses[1]), lses[2])
    es = [jnp.exp(l - top) for l in lses]
    den = es[0] + es[1] + es[2]
    ya = sum((e / den) * r[0].astype(F32) for e, r in zip(es, (oa0_ref, oa1_ref, oa2_ref)))
    branches = (ya.astype(BF16), yb_ref[0], yc_ref[0], yd_ref[0])
    merged = jnp.zeros(x.shape, F32)
    off = 0
    for i, br in enumerate(branches):
        gate = jax.nn.sigmoid(_dot(u, wg_ref[:, i * d:(i + 1) * d]) + bg_ref[:, i * d:(i + 1) * d])
        merged = merged + gate * _dot(br, wb_ref[off:off + BRANCH_WIDTHS[i], :])
        off += BRANCH_WIDTHS[i]
    h = _dot(merged.astype(BF16), wo_ref[...])
    o_ref[0] = _deepnorm_ln(x, h, gt_ref[0], lg_ref[...], lb_ref[...], alpha)


def _merge(x, shift, scale, gate, oas, las, yb, yc, yd, wg, bg, wb, wo, ln_g, ln_b, alpha):
    bsz, t, d = x.shape
    tm = _pick(t, (256, 128))
    row = lambda b, i: (b, 0, 0)
    tok = lambda w: pl.BlockSpec((1, tm, w), lambda b, i: (b, i, 0))
    wspec = lambda shape: pl.BlockSpec(shape, lambda b, i: (0, 0), pipeline_mode=pl.Buffered(1))
    return pl.pallas_call(
        functools.partial(_merge_kernel, alpha=alpha),
        grid=(bsz, t // tm),
        in_specs=[tok(d), pl.BlockSpec((1, 1, d), row), pl.BlockSpec((1, 1, d), row), pl.BlockSpec((1, 1, d), row),
                  tok(LANE), tok(LANE), tok(LANE), tok(LANE), tok(LANE), tok(LANE),
                  tok(2 * LANE), tok(2 * LANE), tok(2 * LANE),
                  wspec(wg.shape), wspec(bg.shape), wspec(wb.shape), wspec(wo.shape),
                  pl.BlockSpec((1, d), lambda b, i: (0, 0)), pl.BlockSpec((1, d), lambda b, i: (0, 0))],
        out_specs=tok(d),
        out_shape=jax.ShapeDtypeStruct((bsz, t, d), F32),
        compiler_params=_params(("parallel", "parallel")),
        name="gated_merge",
    )(x, shift, scale, gate, *oas, *las, yb, yc, yd, wg, bg, wb, wo, ln_g, ln_b)


def _rope_tables(t):
    half = ROT_DIM // 2
    inv_freq = ROPE_THETA ** (-(jnp.arange(half, dtype=F32) * (2.0 / ROT_DIM)))
    ang = jnp.arange(t, dtype=F32)[:, None] * inv_freq[None, :]
    cos, sin = jnp.cos(ang), jnp.sin(ang)
    ones = jnp.ones((t, HEAD_DIM - ROT_DIM), F32)
    zeros = jnp.zeros((t, HEAD_DIM - half), F32)
    c64 = jnp.concatenate([cos, cos, ones], axis=1)
    s1 = jnp.concatenate([-sin, zeros], axis=1)
    s2 = jnp.concatenate([jnp.zeros((t, half), F32), sin, zeros[:, half:]], axis=1)
    return tuple(jnp.concatenate([a, a], axis=1) for a in (c64, s1, s2))


def _mixer_weights(w_in):
    d = w_in.shape[0]

    def qkv(off, heads):
        w = heads * HEAD_DIM
        return w_in[:, off:off + w], w_in[:, off + w:off + 2 * w], w_in[:, off + 2 * w:off + 3 * w]

    aq, ak, av = qkv(OFF_A, A_HEADS)
    bq, bk, bv = qkv(OFF_B, B_HEADS)
    cq, ck, cv = qkv(OFF_C, C_HEADS)
    dq, dk, dv = qkv(OFF_D, D_HEADS)
    w16 = jnp.concatenate([aq, ak, bq, bk, av, bv, cq, ck, cv, dq, dk, dv, jnp.zeros((d, LANE), F32)], axis=1)
    ix = jnp.concatenate([w_in[:, OFF_IK:OFF_IK + IDX_DIM], jnp.zeros((d, LANE_IW - IDX_DIM), F32),
                          w_in[:, OFF_IW:OFF_IW + IDX_HEADS], w_in[:, OFF_FG:OFF_FG + D_HEADS],
                          jnp.zeros((d, LANE - LANE_FG - D_HEADS), F32)], axis=1)
    w32 = jnp.concatenate([w_in[:, OFF_IQ:OFF_IQ + IDX_HEADS * IDX_DIM], ix, jnp.zeros((d, LANE), F32)], axis=1)
    return w16.astype(BF16), w32.astype(BF16), w_in[:, OFF_GATE:].astype(BF16)


def _to_residues(a, dil):
    bsz, t, w = a.shape
    if dil == 1:
        return a
    return a.reshape(bsz, t // dil, dil, w).transpose(0, 2, 1, 3).reshape(bsz * dil, t // dil, w)


def _from_residues(a, dil, bsz):
    if dil == 1:
        return a
    _, ln, w = a.shape
    return a.reshape(bsz, dil, ln, w).transpose(0, 2, 1, 3).reshape(bsz, ln * dil, w)


def _mixer(x, shift, scale, gate, w_in, b_gate, b_forget, w_branch, w_out, ln_g, ln_b, tables, alpha):
    bsz, t, d = x.shape
    w16, w32, wg = _mixer_weights(w_in)
    p16 = _proj(x, shift, scale, w16, tables, N_ROPE_BLKS_16, BF16, "mixer_proj_bf16")
    p32 = _proj(x, shift, scale, w32, tables, N_BLKS_32, F32, "mixer_proj_f32")

    oas, las = [], []
    for g, (_, dil) in enumerate(DILATED_GROUPS):
        streams = [_to_residues(p16[:, :, (blk + g) * LANE:(blk + g + 1) * LANE], dil)
                   for blk in (BLK_AQ, BLK_AK, BLK_AV)]
        o, lse = _dilated(*streams)
        oas.append(_from_residues(o, dil, bsz))
        las.append(_from_residues(lse, dil, bsz))

    tq = _pick(t, (256, 128))
    bias_row = jnp.zeros((1, LANE), F32).at[0, LANE_FG:LANE_FG + D_HEADS].set(b_forget)
    fbias, kcat = _cum(p32, bias_row, tq)
    yd = _token_major(_fox(p16, _transposed_values(p16, BLK_DV, D_HEADS // 2, tq), fbias, tq))
    yc = _token_major(_stick_breaking(p16, _transposed_values(p16, BLK_CV, C_HEADS // 2, tq), tq))
    yb = _token_major(_dsa(p16, p32, kcat, _transposed_values(p16, BLK_BV, B_HEADS // 2, tq), tq))
    return _merge(x, shift, scale, gate, oas, las, yb, yc, yd, wg, b_gate.reshape(1, -1),
                  w_branch.astype(BF16), w_out.astype(BF16), ln_g, ln_b, alpha)


def kernel(x, c, ada_w, ada_b, ln_g, ln_b, ffn_w_in, ffn_w_out, mix_w_in, mix_b_gate, mix_b_forget,
           mix_w_branch, mix_w_out):
    bsz, t, d = x.shape
    depth = ada_w.shape[0]
    f = ffn_w_out.shape[2]
    alpha = float((2 * depth) ** 0.25)
    assert t % (DILATED_GROUPS[-1][0]) == 0 and d % LANE == 0 and bsz <= 8

    c8 = jnp.zeros((8, d), F32).at[:bsz].set(c)
    mod = _ada(c8, ada_w, ada_b)[:, :bsz].reshape(depth, bsz, 3, 3, 1, d)
    tables = _rope_tables(t)

    for l in range(depth):
        m = lambda sub, kind: mod[l, :, sub, kind]
        lng = lambda sub: ln_g[l, sub].reshape(1, d)
        lnb = lambda sub: ln_b[l, sub].reshape(1, d)

        def ffn(x, sub, which):
            w_in = ffn_w_in[l, which].astype(BF16)
            return _ffn(x, m(sub, 0), m(sub, 1), m(sub, 2), w_in[:, :f], w_in[:, f:],
                        ffn_w_out[l, which].astype(BF16), lng(sub), lnb(sub), alpha)

        x = ffn(x, 0, 0)
        x = _mixer(x, m(1, 0), m(1, 1), m(1, 2), mix_w_in[l], mix_b_gate[l], mix_b_forget[l],
                   mix_w_branch[l], mix_w_out[l], lng(1), lnb(1), tables, alpha)
        x = ffn(x, 2, 1)
    return x
```

```python
import functools

import jax
import jax.numpy as jnp
from jax import lax
from jax.experimental import pallas as pl
from jax.experimental.pallas import tpu as pltpu

F32 = jnp.float32
BF16 = jnp.bfloat16
I32 = jnp.int32

LANE = 128
HEAD_DIM = 64
ROT_DIM = HEAD_DIM // 4
ROPE_THETA = 500000.0
DILATED_GROUPS = ((128, 1), (512, 4), (2048, 16))
SPAN = 128
A_HEADS, B_HEADS, C_HEADS, D_HEADS = 6, 4, 4, 4
IDX_HEADS, IDX_DIM = 4, 64
DSA_TOPK = 256
N_BRANCH = 4
LN_EPS = 1e-5
BRANCH_WIDTHS = (128, 256, 256, 256)

OFF_A = 0
OFF_B = OFF_A + 3 * A_HEADS * HEAD_DIM
OFF_IQ = OFF_B + 3 * B_HEADS * HEAD_DIM
OFF_IK = OFF_IQ + IDX_HEADS * IDX_DIM
OFF_IW = OFF_IK + IDX_DIM
OFF_C = OFF_IW + IDX_HEADS
OFF_D = OFF_C + 3 * C_HEADS * HEAD_DIM
OFF_FG = OFF_D + 3 * D_HEADS * HEAD_DIM
OFF_GATE = OFF_FG + D_HEADS

BLK_AQ, BLK_AK, BLK_BQ, BLK_BK = 0, 3, 6, 8
N_ROPE_BLKS_16 = 10
BLK_AV, BLK_BV = 10, 13
BLK_CQ, BLK_CK, BLK_CV = 15, 17, 19
BLK_DQ, BLK_DK, BLK_DV = 21, 23, 25
N_BLKS_16 = 28
BLK_IQ, BLK_IX = 0, 2
N_BLKS_32 = 4
LANE_IW = 80
LANE_FG = 84

INT_MIN = -2147483648
NEG_BIG = -1e30
EXP_DEAD = -104.0
STRIP = 32
VMEM_LIMIT = 56 * 1024 * 1024


def _nt(a, b):
    return lax.dot_general(a, b, (((1,), (1,)), ((), ())), preferred_element_type=F32)


def _dot(a, b):
    return jnp.dot(a, b, preferred_element_type=F32)


def _split3(x):
    hi = x.astype(BF16)
    r1 = x - hi.astype(F32)
    mid = r1.astype(BF16)
    lo = (r1 - mid.astype(F32)).astype(BF16)
    return hi, mid, lo


def _split2(x):
    hi = x.astype(BF16)
    return hi, (x - hi.astype(F32)).astype(BF16)


def _log_sigmoid(z):
    return -(jnp.maximum(-z, 0.0) + jnp.log(1.0 + jnp.exp(-jnp.abs(z))))


def _params(sem):
    return pltpu.CompilerParams(dimension_semantics=sem, vmem_limit_bytes=VMEM_LIMIT)


def _const_spec(shape):
    n = len(shape)
    return pl.BlockSpec(shape, lambda *_: (0,) * n)


def _pick(n, prefs):
    for p in prefs:
        if n % p == 0:
            return p
    return n


def _ada_kernel(c_ref, w_ref, b_ref, o_ref):
    c = c_ref[...]
    cond = c * jax.nn.sigmoid(c)
    ch, cm, _ = _split3(cond)
    wh, wm, _ = _split3(w_ref[0])
    o_ref[0] = _dot(ch, wh) + _dot(ch, wm) + _dot(cm, wh) + b_ref[0]


def _ada(c8, ada_w, ada_b):
    depth, d, n = ada_w.shape
    tn = _pick(n, (1152, 1024, 512, 256, 128))
    return pl.pallas_call(
        _ada_kernel,
        grid=(depth, n // tn),
        in_specs=[pl.BlockSpec((8, d), lambda l, j: (0, 0)),
                  pl.BlockSpec((1, d, tn), lambda l, j: (l, 0, j)),
                  pl.BlockSpec((1, 1, tn), lambda l, j: (l, 0, j))],
        out_specs=pl.BlockSpec((1, 8, tn), lambda l, j: (l, 0, j)),
        out_shape=jax.ShapeDtypeStruct((depth, 8, n), F32),
        compiler_params=_params(("parallel", "parallel")),
        name="ada_mod",
    )(c8, ada_w, ada_b.reshape(depth, 1, n))


def _deepnorm_ln(x, h, gate, g, b, alpha):
    y = alpha * x + (1.0 + gate) * h
    mu = jnp.mean(y, axis=-1, keepdims=True)
    yc = y - mu
    var = jnp.mean(yc * yc, axis=-1, keepdims=True)
    return yc * lax.rsqrt(var + LN_EPS) * g + b


def _ffn_kernel(x_ref, sh_ref, sc_ref, gt_ref, wg_ref, wu_ref, wo_ref, lg_ref, lb_ref, o_ref, *, tf, alpha):
    x = x_ref[0]
    u = (x * (1.0 + sc_ref[0]) + sh_ref[0]).astype(BF16)
    acc = jnp.zeros(x.shape, F32)
    for j in range(wg_ref.shape[1] // tf):
        g = _dot(u, wg_ref[:, j * tf:(j + 1) * tf])
        up = _dot(u, wu_ref[:, j * tf:(j + 1) * tf])
        h = (g * jax.nn.sigmoid(g) * up).astype(BF16)
        acc = acc + _dot(h, wo_ref[j * tf:(j + 1) * tf, :])
    o_ref[0] = _deepnorm_ln(x, 0.5 * acc, gt_ref[0], lg_ref[...], lb_ref[...], alpha)


def _ffn(x, shift, scale, gate, wg, wu, wo, ln_g, ln_b, alpha):
    bsz, t, d = x.shape
    f = wg.shape[1]
    tm = _pick(t, (512, 256, 128))
    tf = _pick(f, (256, 128))
    row = lambda b, i: (b, 0, 0)
    wspec = lambda shape: pl.BlockSpec(shape, lambda b, i: (0, 0), pipeline_mode=pl.Buffered(1))
    return pl.pallas_call(
        functools.partial(_ffn_kernel, tf=tf, alpha=alpha),
        grid=(bsz, t // tm),
        in_specs=[pl.BlockSpec((1, tm, d), lambda b, i: (b, i, 0)),
                  pl.BlockSpec((1, 1, d), row), pl.BlockSpec((1, 1, d), row), pl.BlockSpec((1, 1, d), row),
                  wspec((d, f)), wspec((d, f)), wspec((f, d)),
                  pl.BlockSpec((1, d), lambda b, i: (0, 0)), pl.BlockSpec((1, d), lambda b, i: (0, 0))],
        out_specs=pl.BlockSpec((1, tm, d), lambda b, i: (b, i, 0)),
        out_shape=jax.ShapeDtypeStruct((bsz, t, d), F32),
        compiler_params=_params(("parallel", "parallel")),
        name="ffn",
    )(x, shift, scale, gate, wg, wu, wo, ln_g, ln_b)


def _proj_kernel(x_ref, sh_ref, sc_ref, w_ref, cos_ref, s1_ref, s2_ref, o_ref, *, tn, n_rope_blks):
    u = (x_ref[0] * (1.0 + sc_ref[0]) + sh_ref[0]).astype(BF16)
    n = w_ref.shape[1]
    per = tn // LANE
    for j in range(n // tn):
        y = _dot(u, w_ref[:, j * tn:(j + 1) * tn])
        for i in range(per):
            blk = j * per + i
            yi = y[:, i * LANE:(i + 1) * LANE]
            if blk < n_rope_blks:
                yi = (yi * cos_ref[...] + pltpu.roll(yi, LANE - ROT_DIM // 2, 1) * s1_ref[...]
                      + pltpu.roll(yi, ROT_DIM // 2, 1) * s2_ref[...])
            o_ref[0, :, blk * LANE:(blk + 1) * LANE] = yi.astype(o_ref.dtype)


def _proj(x, shift, scale, w, tables, n_rope_blks, out_dtype, name):
    bsz, t, d = x.shape
    n = w.shape[1]
    tm = _pick(t, (512, 256, 128))
    tn = _pick(n, (256, 128))
    row = lambda b, i: (b, 0, 0)
    tab = pl.BlockSpec((tm, LANE), lambda b, i: (i, 0))
    return pl.pallas_call(
        functools.partial(_proj_kernel, tn=tn, n_rope_blks=n_rope_blks),
        grid=(bsz, t // tm),
        in_specs=[pl.BlockSpec((1, tm, d), lambda b, i: (b, i, 0)),
                  pl.BlockSpec((1, 1, d), row), pl.BlockSpec((1, 1, d), row),
                  pl.BlockSpec((d, n), lambda b, i: (0, 0), pipeline_mode=pl.Buffered(1)),
                  tab, tab, tab],
        out_specs=pl.BlockSpec((1, tm, n), lambda b, i: (b, i, 0)),
        out_shape=jax.ShapeDtypeStruct((bsz, t, n), out_dtype),
        compiler_params=_params(("parallel", "parallel")),
        name=name,
    )(x, shift, scale, w, *tables)


def _cum_kernel(p_ref, bias_ref, out_ref, kcat_ref, carry_ref):
    @pl.when(pl.program_id(1) == 0)
    def _():
        carry_ref[...] = jnp.zeros_like(carry_ref)

    tc = p_ref.shape[1]
    lane = lax.broadcasted_iota(I32, (tc, LANE), 1)
    lf = _log_sigmoid(p_ref[0] + bias_ref[...])
    lf = jnp.where(lane >= LANE_FG, jnp.where(lane < LANE_FG + D_HEADS, lf, 0.0), 0.0)
    ri = lax.broadcasted_iota(I32, (tc, tc), 0)
    ci = lax.broadcasted_iota(I32, (tc, tc), 1)
    tri = jnp.where(ri >= ci, 1.0, 0.0).astype(BF16)
    hi, mid, lo = _split3(lf)
    cum = _dot(tri, hi) + _dot(tri, mid) + _dot(tri, lo) + carry_ref[...]
    carry_ref[...] = cum[tc - 1:tc, :]
    er = lax.broadcasted_iota(I32, (LANE, LANE), 0)
    ch, cm, cl = _split3(-cum)
    for h in range(D_HEADS):
        onehot = jnp.where(er == LANE_FG + h, 1.0, 0.0).astype(BF16)
        out_ref[0, h] = _dot(ch, onehot) + _dot(cm, onehot) + _dot(cl, onehot)
    x = p_ref[0]
    hi = x.astype(BF16).astype(F32)
    kcat_ref[0, :, 0:LANE] = jnp.where(lane < IDX_DIM, hi, pltpu.roll(hi, IDX_DIM, 1)).astype(BF16)
    kcat_ref[0, :, LANE:2 * LANE] = jnp.where(lane < IDX_DIM, x - hi, 0.0).astype(BF16)


def _cum(p32, bias_row, tc):
    bsz, t, _ = p32.shape
    return pl.pallas_call(
        _cum_kernel,
        grid=(bsz, t // tc),
        in_specs=[pl.BlockSpec((1, tc, LANE), lambda b, i: (b, i, BLK_IX)),
                  pl.BlockSpec((1, LANE), lambda b, i: (0, 0))],
        out_specs=[pl.BlockSpec((1, D_HEADS, tc, LANE), lambda b, i: (b, 0, i, 0)),
                   pl.BlockSpec((1, tc, 2 * LANE), lambda b, i: (b, i, 0))],
        out_shape=[jax.ShapeDtypeStruct((bsz, D_HEADS, t, LANE), F32),
                   jax.ShapeDtypeStruct((bsz, t, 2 * LANE), BF16)],
        scratch_shapes=[pltpu.VMEM((1, LANE), F32)],
        compiler_params=_params(("parallel", "arbitrary")),
        name="forget_cumsum",
    )(p32, bias_row)


def _head_halves(q):
    lane = lax.broadcasted_iota(I32, q.shape, 1)
    zero = jnp.zeros_like(q)
    return jnp.where(lane < HEAD_DIM, q, zero), jnp.where(lane >= HEAD_DIM, q, zero)


def _static_loop(n, body, carry):
    for i in range(n):
        carry = body(i, carry)
    return carry


def _fold8(x, op):
    out = x[0:8]
    for r in range(8, x.shape[0], 8):
        out = op(out, x[r:r + 8])
    return out


def _transposed_values(p16, blk, npair, tq):
    bsz, t, _ = p16.shape
    v = p16[:, :, blk * LANE:(blk + npair) * LANE].reshape(bsz, t // tq, tq, npair, LANE)
    return v.transpose(0, 3, 1, 4, 2)


def _token_major(yt):
    bsz, npair, _, t = yt.shape
    return yt.transpose(0, 3, 1, 2).reshape(bsz, t, npair * LANE)


def _lane_pick(a, b):
    lane = lax.broadcasted_iota(I32, a.shape, 1)
    return jnp.where(lane < HEAD_DIM, a, b)


def _dil_kernel(q_ref, k_ref, v_ref, o_ref, lse_ref, *, kw):
    nb = q_ref.shape[1] // SPAN

    def body(i, carry):
        qs = pl.multiple_of(i * SPAN, SPAN)
        ks = pl.multiple_of(jnp.maximum(i * SPAN + SPAN - kw, 0), SPAN)
        q = q_ref[0, pl.ds(qs, SPAN), :] * 0.125
        k = k_ref[0, pl.ds(ks, kw), :]
        v = v_ref[0, pl.ds(ks, kw), :]
        qpos = qs + lax.broadcasted_iota(I32, (SPAN, kw), 0)
        kpos = ks + lax.broadcasted_iota(I32, (SPAN, kw), 1)
        dist = qpos - kpos
        outs, lses = [], []
        for qh in _head_halves(q):
            s = _nt(qh, k)
            s = jnp.where(dist >= 0, jnp.where(dist <= SPAN, s, NEG_BIG), NEG_BIG)
            m = jnp.max(s, axis=-1, keepdims=True)
            p = jnp.exp(s - m)
            l = jnp.sum(p, axis=-1, keepdims=True)
            outs.append(_dot(p.astype(BF16), v) / l)
            lses.append(jnp.broadcast_to(m + jnp.log(l), (SPAN, LANE)))
        o_ref[0, pl.ds(qs, SPAN), :] = _lane_pick(outs[0], outs[1]).astype(o_ref.dtype)
        lse_ref[0, pl.ds(qs, SPAN), :] = _lane_pick(lses[0], lses[1])
        return carry

    lax.fori_loop(0, nb, body, 0)


def _dilated(q, k, v):
    ns, ln, _ = q.shape
    kw = min(2 * SPAN, ln)
    spec = pl.BlockSpec((1, ln, LANE), lambda s: (s, 0, 0))
    return pl.pallas_call(
        functools.partial(_dil_kernel, kw=kw),
        grid=(ns,),
        in_specs=[spec, spec, spec],
        out_specs=[spec, spec],
        out_shape=[jax.ShapeDtypeStruct((ns, ln, LANE), BF16), jax.ShapeDtypeStruct((ns, ln, LANE), F32)],
        compiler_params=_params(("parallel",)),
        name="dilated_window",
    )(q, k, v)


def _fox_kernel(q_ref, k_ref, vt_ref, bias_ref, o_ref, s_scr, p_scr, acc_scr, m_scr, l_scr, *, tq):
    qi = pl.program_id(2)
    rep = tq // LANE
    nstrip = tq // STRIP
    qh = _head_halves(q_ref[0] * 0.125)
    m_scr[...] = jnp.full(m_scr.shape, NEG_BIG, F32)
    l_scr[...] = jnp.zeros(l_scr.shape, F32)
    acc_scr[...] = jnp.zeros(acc_scr.shape, F32)
    kk = lax.broadcasted_iota(I32, (STRIP, tq), 0)
    qq = lax.broadcasted_iota(I32, (STRIP, tq), 1)

    def block(j, diag):
        ks = pl.multiple_of(j * tq, tq)
        kblk = k_ref[0, pl.ds(ks, tq), :]
        for h in range(2):
            s_scr[h] = _nt(kblk, qh[h])
        for h in range(2):
            def pass1(i, mx):
                r0 = i * STRIP
                s = s_scr[h, pl.ds(r0, STRIP), :] + jnp.tile(
                    bias_ref[0, h, pl.ds(pl.multiple_of(ks + r0, STRIP), STRIP), :], (1, rep))
                if diag:
                    s = jnp.where(kk + r0 <= qq, s, NEG_BIG)
                s_scr[h, pl.ds(r0, STRIP), :] = s
                return jnp.maximum(mx, _fold8(s, jnp.maximum))

            mx = _static_loop(nstrip, pass1, jnp.full((8, tq), NEG_BIG, F32))
            m_old = m_scr[h]
            m_new = jnp.maximum(m_old, jnp.max(mx, axis=0, keepdims=True))
            alpha = jnp.exp(m_old - m_new)
            m_scr[h] = m_new

            def pass2(i, ls):
                r0 = i * STRIP
                p = jnp.exp(s_scr[h, pl.ds(r0, STRIP), :] - m_new)
                p_scr[h, pl.ds(r0, STRIP), :] = p.astype(BF16)
                return ls + _fold8(p, jnp.add)

            ls = _static_loop(nstrip, pass2, jnp.zeros((8, tq), F32))
            l_scr[h] = alpha * l_scr[h] + jnp.sum(ls, axis=0, keepdims=True)
            pv = _dot(vt_ref[0, 0, j, h * HEAD_DIM:(h + 1) * HEAD_DIM, :], p_scr[h])
            acc_scr[h] = acc_scr[h] * alpha + pv

    def body(j, carry):
        block(j, False)
        return carry

    lax.fori_loop(0, qi, body, 0)
    block(qi, True)
    for h in range(2):
        o_ref[0, 0, h * HEAD_DIM:(h + 1) * HEAD_DIM, :] = (acc_scr[h] * (1.0 / l_scr[h])).astype(o_ref.dtype)


def _fox(p16, vt, bias, tq):
    bsz, t, _ = p16.shape
    npair = D_HEADS // 2
    nblk = t // tq
    return pl.pallas_call(
        functools.partial(_fox_kernel, tq=tq),
        grid=(bsz, npair, nblk),
        in_specs=[pl.BlockSpec((1, tq, LANE), lambda b, h, i: (b, i, BLK_DQ + h)),
                  pl.BlockSpec((1, t, LANE), lambda b, h, i: (b, 0, BLK_DK + h)),
                  pl.BlockSpec((1, 1, nblk, LANE, tq), lambda b, h, i: (b, h, 0, 0, 0)),
                  pl.BlockSpec((1, 2, t, LANE), lambda b, h, i: (b, h, 0, 0))],
        out_specs=pl.BlockSpec((1, 1, LANE, tq), lambda b, h, i: (b, h, 0, i)),
        out_shape=jax.ShapeDtypeStruct((bsz, npair, LANE, t), BF16),
        scratch_shapes=[pltpu.VMEM((2, tq, tq), F32), pltpu.VMEM((2, tq, tq), BF16),
                        pltpu.VMEM((2, HEAD_DIM, tq), F32), pltpu.VMEM((2, 1, tq), F32),
                        pltpu.VMEM((2, 1, tq), F32)],
        compiler_params=_params(("parallel", "parallel", "arbitrary")),
        name="forgetting_attention",
    )(p16, p16, vt, bias)


def _sb_kernel(q_ref, k_ref, vt_ref, o_ref, z_scr, sfx_scr, hi_scr, lo_scr, a_scr, acc_scr, r_scr, *, tq):
    qi = pl.program_id(2)
    nstrip = tq // STRIP
    qh = _head_halves(q_ref[0] * 0.125)
    acc_scr[...] = jnp.zeros(acc_scr.shape, F32)
    r_scr[...] = jnp.zeros(r_scr.shape, F32)
    kk = lax.broadcasted_iota(I32, (STRIP, tq), 0)
    qq = lax.broadcasted_iota(I32, (STRIP, tq), 1)
    row = lax.broadcasted_iota(I32, (tq, tq), 0)
    col = lax.broadcasted_iota(I32, (tq, tq), 1)
    later = jnp.where(col > row, 1.0, 0.0).astype(BF16)

    def block(j, diag):
        ks = pl.multiple_of(j * tq, tq)
        kblk = k_ref[0, pl.ds(ks, tq), :]
        for h in range(2):
            z_scr[h] = _nt(kblk, qh[h])
        for h in range(2):
            def pass1(i, rsum):
                r0 = i * STRIP
                z = z_scr[h, pl.ds(r0, STRIP), :]
                lk = _log_sigmoid(-z)
                z_scr[h, pl.ds(r0, STRIP), :] = lk + z
                if diag:
                    lk = jnp.where(kk + r0 < qq, lk, 0.0)
                hi = lk.astype(BF16)
                hi_scr[h, pl.ds(r0, STRIP), :] = hi
                lo_scr[h, pl.ds(r0, STRIP), :] = (lk - hi.astype(F32)).astype(BF16)
                return rsum + _fold8(lk, jnp.add)

            rsum = _static_loop(nstrip, pass1, jnp.zeros((8, tq), F32))
            sfx_scr[h] = _dot(later, hi_scr[h]) + _dot(later, lo_scr[h])
            r_old = r_scr[h]

            def pass2(i, carry):
                r0 = i * STRIP
                a = jnp.exp(z_scr[h, pl.ds(r0, STRIP), :] + sfx_scr[h, pl.ds(r0, STRIP), :] + r_old)
                if diag:
                    a = jnp.where(kk + r0 < qq, a, 0.0)
                a_scr[h, pl.ds(r0, STRIP), :] = a.astype(BF16)
                return carry

            _static_loop(nstrip, pass2, 0)
            acc_scr[h] = acc_scr[h] + _dot(vt_ref[0, 0, j, h * HEAD_DIM:(h + 1) * HEAD_DIM, :], a_scr[h])
            r_scr[h] = r_old + jnp.sum(rsum, axis=0, keepdims=True)

    block(qi, True)

    def live():
        return jnp.max(jnp.maximum(r_scr[0], r_scr[1])) > EXP_DEAD

    def cond(c):
        return jnp.logical_and(c[0] >= 0, c[1])

    def body(c):
        block(c[0], False)
        return c[0] - 1, live()

    lax.while_loop(cond, body, (qi - 1, live()))
    for h in range(2):
        o_ref[0, 0, h * HEAD_DIM:(h + 1) * HEAD_DIM, :] = acc_scr[h].astype(o_ref.dtype)


def _stick_breaking(p16, vt, tq):
    bsz, t, _ = p16.shape
    npair = C_HEADS // 2
    nblk = t // tq
    return pl.pallas_call(
        functools.partial(_sb_kernel, tq=tq),
        grid=(bsz, npair, nblk),
        in_specs=[pl.BlockSpec((1, tq, LANE), lambda b, h, i: (b, i, BLK_CQ + h)),
                  pl.BlockSpec((1, t, LANE), lambda b, h, i: (b, 0, BLK_CK + h)),
                  pl.BlockSpec((1, 1, nblk, LANE, tq), lambda b, h, i: (b, h, 0, 0, 0))],
        out_specs=pl.BlockSpec((1, 1, LANE, tq), lambda b, h, i: (b, h, 0, i)),
        out_shape=jax.ShapeDtypeStruct((bsz, npair, LANE, t), BF16),
        scratch_shapes=[pltpu.VMEM((2, tq, tq), F32), pltpu.VMEM((2, tq, tq), F32),
                        pltpu.VMEM((2, tq, tq), BF16), pltpu.VMEM((2, tq, tq), BF16),
                        pltpu.VMEM((2, tq, tq), BF16), pltpu.VMEM((2, HEAD_DIM, tq), F32),
                        pltpu.VMEM((2, 1, tq), F32)],
        compiler_params=_params(("parallel", "parallel", "arbitrary")),
        name="stick_breaking",
    )(p16, p16, vt)


def _dsa_kernel(iq0_ref, iq1_ref, ixq_ref, kcat_ref, q0_ref, q1_ref, k0_ref, k1_ref, vt_ref, o_ref,
                key_scr, s_scr, p_scr, eq_scr, rank_scr, bias_scr, acc_scr, m_scr, l_scr, *, tq, topk):
    qi = pl.program_id(1)
    nstrip = tq // STRIP
    lane = lax.broadcasted_iota(I32, (tq, LANE), 1)
    kk = lax.broadcasted_iota(I32, (STRIP, tq), 0)
    qq = lax.broadcasted_iota(I32, (STRIP, tq), 1)

    qcat = []
    for ref in (iq0_ref, iq1_ref):
        for half in range(2):
            x = ref[0] if half == 0 else pltpu.roll(ref[0], IDX_DIM, 1)
            hi = x.astype(BF16).astype(F32)
            first = jnp.where(lane < IDX_DIM, hi, pltpu.roll(x - hi, IDX_DIM, 1))
            second = jnp.where(lane < IDX_DIM, hi, 0.0)
            qcat.append(jnp.concatenate([first, second], axis=1).astype(BF16))
    sr = lax.broadcasted_iota(I32, (8, LANE), 0)
    sl = lax.broadcasted_iota(I32, (8, LANE), 1)
    onehot = jnp.where(sl == sr + LANE_IW, 1.0, 0.0).astype(BF16)
    wh, wm, wl = _split3(ixq_ref[0])
    wt = _nt(onehot, wh) + _nt(onehot, wm) + _nt(onehot, wl)

    def score_chunk(c, diag):
        ks = pl.multiple_of(c * tq, tq)
        kc = kcat_ref[0, pl.ds(ks, tq), :]
        for h in range(IDX_HEADS):
            s_scr[h] = _nt(kc, qcat[h])
        for i in range(nstrip):
            r0 = i * STRIP
            sc = jnp.zeros((STRIP, tq), F32)
            for h in range(IDX_HEADS):
                sc = sc + wt[h:h + 1, :] * jnp.maximum(s_scr[h, r0:r0 + STRIP, :], 0.0)
            sc = jnp.where(sc == 0.0, 0.0, sc)
            bits = lax.bitcast_convert_type(sc, I32)
            key = jnp.where(bits < 0, bits ^ jnp.int32(0x7FFFFFFF), bits)
            if diag:
                key = jnp.where(kk + r0 <= qq, key, INT_MIN)
            key_scr[c, r0:r0 + STRIP, :] = key

    def score_body(c, carry):
        score_chunk(c, False)
        return carry

    lax.fori_loop(0, qi, score_body, 0)
    score_chunk(qi, True)

    def count(pred):
        def body(c, acc):
            for i in range(nstrip):
                acc = acc + _fold8(jnp.where(pred(key_scr[c, i * STRIP:(i + 1) * STRIP, :]), 1.0, 0.0), jnp.add)
            return acc
        acc = lax.fori_loop(0, qi + 1, body, jnp.zeros((8, tq), F32))
        return jnp.sum(acc, axis=0, keepdims=True)

    kf = jnp.float32(topk)
    n_ge0 = count(lambda kb: kb >= 0)
    n_gt0 = count(lambda kb: kb > 0)
    th0 = jnp.where(n_ge0 >= kf, 0, INT_MIN).astype(I32)
    settled0 = jnp.where(n_ge0 >= kf, jnp.where(n_gt0 < kf, 1.0, 0.0), 0.0)

    def unsettled(settled):
        return jnp.min(settled) == 0.0

    def search_cond(c):
        return jnp.logical_and(c[0] >= 0, c[3])

    def search_body(c):
        bit, th, settled, _ = c
        cand = th | lax.shift_left(jnp.int32(1), bit)
        n = count(lambda kb: kb >= cand)
        th = jnp.where(n >= kf, jnp.where(settled == 0.0, cand, th), th)
        settled = jnp.maximum(settled, jnp.where(n == kf, 1.0, 0.0))
        return bit - 1, th, settled, unsettled(settled)

    _, th, _, _ = lax.while_loop(search_cond, search_body, (jnp.int32(30), th0, settled0, unsettled(settled0)))
    need = kf - count(lambda kb: kb > th)

    qh = _head_halves(q0_ref[0] * 0.125) + _head_halves(q1_ref[0] * 0.125)
    k_refs = (k0_ref, k0_ref, k1_ref, k1_ref)
    m_scr[...] = jnp.full(m_scr.shape, NEG_BIG, F32)
    l_scr[...] = jnp.zeros(l_scr.shape, F32)
    acc_scr[...] = jnp.zeros(acc_scr.shape, F32)
    row = lax.broadcasted_iota(I32, (tq, tq), 0)
    col = lax.broadcasted_iota(I32, (tq, tq), 1)
    upto = jnp.where(col <= row, 1.0, 0.0).astype(BF16)

    def attend_chunk(c, diag, seen):
        ks = pl.multiple_of(c * tq, tq)
        for i in range(nstrip):
            r0 = i * STRIP
            eq_scr[r0:r0 + STRIP, :] = jnp.where(key_scr[c, r0:r0 + STRIP, :] == th, 1.0, 0.0).astype(BF16)
        rank_scr[...] = _dot(upto, eq_scr[...])
        for i in range(nstrip):
            r0 = i * STRIP
            kb = key_scr[c, r0:r0 + STRIP, :]
            tie = jnp.where(rank_scr[r0:r0 + STRIP, :] + seen <= need, 0.0, NEG_BIG)
            bias = jnp.where(kb > th, 0.0, jnp.where(kb == th, tie, NEG_BIG))
            if diag:
                bias = jnp.where(kk + r0 <= qq, bias, NEG_BIG)
            bias_scr[r0:r0 + STRIP, :] = bias
        seen = seen + rank_scr[tq - 1:tq, :]
        for h in range(B_HEADS):
            s_scr[h] = _nt(k_refs[h][0, pl.ds(ks, tq), :], qh[h])
        for h in range(B_HEADS):
            def pass1(i, mx):
                r0 = i * STRIP
                s = s_scr[h, r0:r0 + STRIP, :] + bias_scr[r0:r0 + STRIP, :]
                s_scr[h, r0:r0 + STRIP, :] = s
                return jnp.maximum(mx, _fold8(s, jnp.maximum))

            mx = _static_loop(nstrip, pass1, jnp.full((8, tq), NEG_BIG, F32))
            m_old = m_scr[h]
            m_new = jnp.maximum(m_old, jnp.max(mx, axis=0, keepdims=True))
            alpha = jnp.exp(m_old - m_new)
            m_scr[h] = m_new

            def pass2(i, ls):
                r0 = i * STRIP
                p = jnp.exp(s_scr[h, r0:r0 + STRIP, :] - m_new)
                p_scr[h, r0:r0 + STRIP, :] = p.astype(BF16)
                return ls + _fold8(p, jnp.add)

            ls = _static_loop(nstrip, pass2, jnp.zeros((8, tq), F32))
            l_scr[h] = alpha * l_scr[h] + jnp.sum(ls, axis=0, keepdims=True)
            lo = (h % 2) * HEAD_DIM
            acc_scr[h] = acc_scr[h] * alpha + _dot(vt_ref[0, h // 2, c, lo:lo + HEAD_DIM, :], p_scr[h])
        return seen

    seen = lax.fori_loop(0, qi, lambda c, seen: attend_chunk(c, False, seen), jnp.zeros((1, tq), F32))
    attend_chunk(qi, True, seen)
    for h in range(B_HEADS):
        lo = (h % 2) * HEAD_DIM
        o_ref[0, h // 2, lo:lo + HEAD_DIM, :] = (acc_scr[h] * (1.0 / l_scr[h])).astype(o_ref.dtype)


def _dsa(p16, p32, kcat, vt, tq):
    bsz, t, _ = p16.shape
    topk = min(DSA_TOPK, t // 4)
    npair = B_HEADS // 2
    nblk = t // tq
    qspec = lambda arr_blk: pl.BlockSpec((1, tq, LANE), lambda b, i: (b, i, arr_blk))
    once = pl.Buffered(1)
    kspec = lambda arr_blk: pl.BlockSpec((1, t, LANE), lambda b, i: (b, 0, arr_blk), pipeline_mode=once)
    sq = lambda n, dt: pltpu.VMEM((n, tq, tq), dt) if n else pltpu.VMEM((tq, tq), dt)
    return pl.pallas_call(
        functools.partial(_dsa_kernel, tq=tq, topk=topk),
        grid=(bsz, nblk),
        in_specs=[qspec(BLK_IQ), qspec(BLK_IQ + 1), qspec(BLK_IX),
                  pl.BlockSpec((1, t, 2 * LANE), lambda b, i: (b, 0, 0), pipeline_mode=once),
                  qspec(BLK_BQ), qspec(BLK_BQ + 1), kspec(BLK_BK), kspec(BLK_BK + 1),
                  pl.BlockSpec((1, npair, nblk, LANE, tq), lambda b, i: (b, 0, 0, 0, 0), pipeline_mode=once)],
        out_specs=pl.BlockSpec((1, npair, LANE, tq), lambda b, i: (b, 0, 0, i)),
        out_shape=jax.ShapeDtypeStruct((bsz, npair, LANE, t), BF16),
        scratch_shapes=[sq(nblk, I32), sq(B_HEADS, F32), sq(B_HEADS, BF16), sq(0, BF16), sq(0, F32), sq(0, F32),
                        pltpu.VMEM((B_HEADS, HEAD_DIM, tq), F32), pltpu.VMEM((B_HEADS, 1, tq), F32),
                        pltpu.VMEM((B_HEADS, 1, tq), F32)],
        compiler_params=_params(("parallel", "arbitrary")),
        name="dsa_topk_attention",
    )(p32, p32, p32, kcat, p16, p16, p16, p16, vt)


def _merge_kernel(x_ref, sh_ref, sc_ref, gt_ref, oa0_ref, oa1_ref, oa2_ref, la0_ref, la1_ref, la2_ref,
                  yb_ref, yc_ref, yd_ref, wg_ref, bg_ref, wb_ref, wo_ref, lg_ref, lb_ref, o_ref, *, alpha):
    x = x_ref[0]
    d = x.shape[1]
    u = (x * (1.0 + sc_ref[0]) + sh_ref[0]).astype(BF16)
    lses = [r[0] for r in (la0_ref, la1_ref, la2_ref)]
    top = jnp.maximum(jnp.maximum(lses[0], lses[1]), lses[2])
    es = [jnp.exp(l - top) for l in lses]
    den = es[0] + es[1] + es[2]
    ya = sum((e / den) * r[0].astype(F32) for e, r in zip(es, (oa0_ref, oa1_ref, oa2_ref)))
    branches = (ya.astype(BF16), yb_ref[0], yc_ref[0], yd_ref[0])
    merged = jnp.zeros(x.shape, F32)
    off = 0
    for i, br in enumerate(branches):
        gate = jax.nn.sigmoid(_dot(u, wg_ref[:, i * d:(i + 1) * d]) + bg_ref[:, i * d:(i + 1) * d])
        merged = merged + gate * _dot(br, wb_ref[off:off + BRANCH_WIDTHS[i], :])
        off += BRANCH_WIDTHS[i]
    h = _dot(merged.astype(BF16), wo_ref[...])
    o_ref[0] = _deepnorm_ln(x, h, gt_ref[0], lg_ref[...], lb_ref[...], alpha)


def _merge(x, shift, scale, gate, oas, las, yb, yc, yd, wg, bg, wb, wo, ln_g, ln_b, alpha):
    bsz, t, d = x.shape
    tm = _pick(t, (256, 128))
    row = lambda b, i: (b, 0, 0)
    tok = lambda w: pl.BlockSpec((1, tm, w), lambda b, i: (b, i, 0))
    wspec = lambda shape: pl.BlockSpec(shape, lambda b, i: (0, 0), pipeline_mode=pl.Buffered(1))
    return pl.pallas_call(
        functools.partial(_merge_kernel, alpha=alpha),
        grid=(bsz, t // tm),
        in_specs=[tok(d), pl.BlockSpec((1, 1, d), row), pl.BlockSpec((1, 1, d), row), pl.BlockSpec((1, 1, d), row),
                  tok(LANE), tok(LANE), tok(LANE), tok(LANE), tok(LANE), tok(LANE),
                  tok(2 * LANE), tok(2 * LANE), tok(2 * LANE),
                  wspec(wg.shape), wspec(bg.shape), wspec(wb.shape), wspec(wo.shape),
                  pl.BlockSpec((1, d), lambda b, i: (0, 0)), pl.BlockSpec((1, d), lambda b, i: (0, 0))],
        out_specs=tok(d),
        out_shape=jax.ShapeDtypeStruct((bsz, t, d), F32),
        compiler_params=_params(("parallel", "parallel")),
        name="gated_merge",
    )(x, shift, scale, gate, *oas, *las, yb, yc, yd, wg, bg, wb, wo, ln_g, ln_b)


def _rope_tables(t):
    half = ROT_DIM // 2
    inv_freq = ROPE_THETA ** (-(jnp.arange(half, dtype=F32) * (2.0 / ROT_DIM)))
    ang = jnp.arange(t, dtype=F32)[:, None] * inv_freq[None, :]
    cos, sin = jnp.cos(ang), jnp.sin(ang)
    ones = jnp.ones((t, HEAD_DIM - ROT_DIM), F32)
    zeros = jnp.zeros((t, HEAD_DIM - half), F32)
    c64 = jnp.concatenate([cos, cos, ones], axis=1)
    s1 = jnp.concatenate([-sin, zeros], axis=1)
    s2 = jnp.concatenate([jnp.zeros((t, half), F32), sin, zeros[:, half:]], axis=1)
    return tuple(jnp.concatenate([a, a], axis=1) for a in (c64, s1, s2))


def _mixer_weights(w_in):
    d = w_in.shape[0]

    def qkv(off, heads):
        w = heads * HEAD_DIM
        return w_in[:, off:off + w], w_in[:, off + w:off + 2 * w], w_in[:, off + 2 * w:off + 3 * w]

    aq, ak, av = qkv(OFF_A, A_HEADS)
    bq, bk, bv = qkv(OFF_B, B_HEADS)
    cq, ck, cv = qkv(OFF_C, C_HEADS)
    dq, dk, dv = qkv(OFF_D, D_HEADS)
    w16 = jnp.concatenate([aq, ak, bq, bk, av, bv, cq, ck, cv, dq, dk, dv, jnp.zeros((d, LANE), F32)], axis=1)
    ix = jnp.concatenate([w_in[:, OFF_IK:OFF_IK + IDX_DIM], jnp.zeros((d, LANE_IW - IDX_DIM), F32),
                          w_in[:, OFF_IW:OFF_IW + IDX_HEADS], w_in[:, OFF_FG:OFF_FG + D_HEADS],
                          jnp.zeros((d, LANE - LANE_FG - D_HEADS), F32)], axis=1)
    w32 = jnp.concatenate([w_in[:, OFF_IQ:OFF_IQ + IDX_HEADS * IDX_DIM], ix, jnp.zeros((d, LANE), F32)], axis=1)
    return w16.astype(BF16), w32.astype(BF16), w_in[:, OFF_GATE:].astype(BF16)


def _to_residues(a, dil):
    bsz, t, w = a.shape
    if dil == 1:
        return a
    return a.reshape(bsz, t // dil, dil, w).transpose(0, 2, 1, 3).reshape(bsz * dil, t // dil, w)


def _from_residues(a, dil, bsz):
    if dil == 1:
        return a
    _, ln, w = a.shape
    return a.reshape(bsz, dil, ln, w).transpose(0, 2, 1, 3).reshape(bsz, ln * dil, w)


def _mixer(x, shift, scale, gate, w_in, b_gate, b_forget, w_branch, w_out, ln_g, ln_b, tables, alpha):
    bsz, t, d = x.shape
    w16, w32, wg = _mixer_weights(w_in)
    p16 = _proj(x, shift, scale, w16, tables, N_ROPE_BLKS_16, BF16, "mixer_proj_bf16")
    p32 = _proj(x, shift, scale, w32, tables, N_BLKS_32, F32, "mixer_proj_f32")

    oas, las = [], []
    for g, (_, dil) in enumerate(DILATED_GROUPS):
        streams = [_to_residues(p16[:, :, (blk + g) * LANE:(blk + g + 1) * LANE], dil)
                   for blk in (BLK_AQ, BLK_AK, BLK_AV)]
        o, lse = _dilated(*streams)
        oas.append(_from_residues(o, dil, bsz))
        las.append(_from_residues(lse, dil, bsz))

    tq = _pick(t, (256, 128))
    bias_row = jnp.zeros((1, LANE), F32).at[0, LANE_FG:LANE_FG + D_HEADS].set(b_forget)
    fbias, kcat = _cum(p32, bias_row, tq)
    tqf = _pick(t, (512, 256, 128))
    yd = _token_major(_fox(p16, _transposed_values(p16, BLK_DV, D_HEADS // 2, tqf), fbias, tqf))
    yc = _token_major(_stick_breaking(p16, _transposed_values(p16, BLK_CV, C_HEADS // 2, tq), tq))
    yb = _token_major(_dsa(p16, p32, kcat, _transposed_values(p16, BLK_BV, B_HEADS // 2, tqf), tqf))
    return _merge(x, shift, scale, gate, oas, las, yb, yc, yd, wg, b_gate.reshape(1, -1),
                  w_branch.astype(BF16), w_out.astype(BF16), ln_g, ln_b, alpha)


def kernel(x, c, ada_w, ada_b, ln_g, ln_b, ffn_w_in, ffn_w_out, mix_w_in, mix_b_gate, mix_b_forget,
           mix_w_branch, mix_w_out):
    bsz, t, d = x.shape
    depth = ada_w.shape[0]
    f = ffn_w_out.shape[2]
    alpha = float((2 * depth) ** 0.25)
    assert t % (DILATED_GROUPS[-1][0]) == 0 and d % LANE == 0 and bsz <= 8

    c8 = jnp.zeros((8, d), F32).at[:bsz].set(c)
    mod = _ada(c8, ada_w, ada_b)[:, :bsz].reshape(depth, bsz, 3, 3, 1, d)
    tables = _rope_tables(t)

    for l in range(depth):
        m = lambda sub, kind: mod[l, :, sub, kind]
        lng = lambda sub: ln_g[l, sub].reshape(1, d)
        lnb = lambda sub: ln_b[l, sub].reshape(1, d)

        def ffn(x, sub, which):
            w_in = ffn_w_in[l, which].astype(BF16)
            return _ffn(x, m(sub, 0), m(sub, 1), m(sub, 2), w_in[:, :f], w_in[:, f:],
                        ffn_w_out[l, which].astype(BF16), lng(sub), lnb(sub), alpha)

        x = ffn(x, 0, 0)
        x = _mixer(x, m(1, 0), m(1, 1), m(1, 2), mix_w_in[l], mix_b_gate[l], mix_b_forget[l],
                   mix_w_branch[l], mix_w_out[l], lng(1), lnb(1), tables, alpha)
        x = ffn(x, 2, 1)
    return x
```

```python
import functools

import jax
import jax.numpy as jnp
from jax import lax
from jax.experimental import pallas as pl
from jax.experimental.pallas import tpu as pltpu

F32 = jnp.float32
BF16 = jnp.bfloat16
I32 = jnp.int32

LANE = 128
HEAD_DIM = 64
ROT_DIM = HEAD_DIM // 4
ROPE_THETA = 500000.0
DILATED_GROUPS = ((128, 1), (512, 4), (2048, 16))
SPAN = 128
A_HEADS, B_HEADS, C_HEADS, D_HEADS = 6, 4, 4, 4
IDX_HEADS, IDX_DIM = 4, 64
DSA_TOPK = 256
N_BRANCH = 4
LN_EPS = 1e-5
BRANCH_WIDTHS = (128, 256, 256, 256)

OFF_A = 0
OFF_B = OFF_A + 3 * A_HEADS * HEAD_DIM
OFF_IQ = OFF_B + 3 * B_HEADS * HEAD_DIM
OFF_IK = OFF_IQ + IDX_HEADS * IDX_DIM
OFF_IW = OFF_IK + IDX_DIM
OFF_C = OFF_IW + IDX_HEADS
OFF_D = OFF_C + 3 * C_HEADS * HEAD_DIM
OFF_FG = OFF_D + 3 * D_HEADS * HEAD_DIM
OFF_GATE = OFF_FG + D_HEADS

BLK_AQ, BLK_AK, BLK_BQ, BLK_BK = 0, 3, 6, 8
N_ROPE_BLKS_16 = 10
BLK_AV, BLK_BV = 10, 13
BLK_CQ, BLK_CK, BLK_CV = 15, 17, 19
BLK_DQ, BLK_DK, BLK_DV = 21, 23, 25
N_BLKS_16 = 28
BLK_IQ, BLK_IX = 0, 2
N_BLKS_32 = 4
LANE_IW = 80
LANE_FG = 84

INT_MIN = -2147483648
NEG_BIG = -1e30
EXP_DEAD = -104.0
STRIP = 32
VMEM_LIMIT = 56 * 1024 * 1024


def _nt(a, b):
    return lax.dot_general(a, b, (((1,), (1,)), ((), ())), preferred_element_type=F32)


def _dot(a, b):
    return jnp.dot(a, b, preferred_element_type=F32)


def _split3(x):
    hi = x.astype(BF16)
    r1 = x - hi.astype(F32)
    mid = r1.astype(BF16)
    lo = (r1 - mid.astype(F32)).astype(BF16)
    return hi, mid, lo


def _split2(x):
    hi = x.astype(BF16)
    return hi, (x - hi.astype(F32)).astype(BF16)


def _log_sigmoid(z):
    return -(jnp.maximum(-z, 0.0) + jnp.log(1.0 + jnp.exp(-jnp.abs(z))))


def _params(sem):
    return pltpu.CompilerParams(dimension_semantics=sem, vmem_limit_bytes=VMEM_LIMIT)


def _const_spec(shape):
    n = len(shape)
    return pl.BlockSpec(shape, lambda *_: (0,) * n)


def _pick(n, prefs):
    for p in prefs:
        if n % p == 0:
            return p
    return n


def _ada_kernel(c_ref, w_ref, b_ref, o_ref):
    c = c_ref[...]
    cond = c * jax.nn.sigmoid(c)
    ch, cm, _ = _split3(cond)
    wh, wm, _ = _split3(w_ref[0])
    o_ref[0] = _dot(ch, wh) + _dot(ch, wm) + _dot(cm, wh) + b_ref[0]


def _ada(c8, ada_w, ada_b):
    depth, d, n = ada_w.shape
    tn = _pick(n, (1152, 1024, 512, 256, 128))
    return pl.pallas_call(
        _ada_kernel,
        grid=(depth, n // tn),
        in_specs=[pl.BlockSpec((8, d), lambda l, j: (0, 0)),
                  pl.BlockSpec((1, d, tn), lambda l, j: (l, 0, j)),
                  pl.BlockSpec((1, 1, tn), lambda l, j: (l, 0, j))],
        out_specs=pl.BlockSpec((1, 8, tn), lambda l, j: (l, 0, j)),
        out_shape=jax.ShapeDtypeStruct((depth, 8, n), F32),
        compiler_params=_params(("parallel", "parallel")),
        name="ada_mod",
    )(c8, ada_w, ada_b.reshape(depth, 1, n))


def _deepnorm_ln(x, h, gate, g, b, alpha):
    y = alpha * x + (1.0 + gate) * h
    mu = jnp.mean(y, axis=-1, keepdims=True)
    yc = y - mu
    var = jnp.mean(yc * yc, axis=-1, keepdims=True)
    return yc * lax.rsqrt(var + LN_EPS) * g + b


def _ffn_kernel(x_ref, sh_ref, sc_ref, gt_ref, wg_ref, wu_ref, wo_ref, lg_ref, lb_ref, o_ref, *, tf, alpha):
    x = x_ref[0]
    u = (x * (1.0 + sc_ref[0]) + sh_ref[0]).astype(BF16)
    acc = jnp.zeros(x.shape, F32)
    for j in range(wg_ref.shape[1] // tf):
        g = _dot(u, wg_ref[:, j * tf:(j + 1) * tf])
        up = _dot(u, wu_ref[:, j * tf:(j + 1) * tf])
        h = (g * jax.nn.sigmoid(g) * up).astype(BF16)
        acc = acc + _dot(h, wo_ref[j * tf:(j + 1) * tf, :])
    o_ref[0] = _deepnorm_ln(x, 0.5 * acc, gt_ref[0], lg_ref[...], lb_ref[...], alpha)


def _ffn(x, shift, scale, gate, wg, wu, wo, ln_g, ln_b, alpha):
    bsz, t, d = x.shape
    f = wg.shape[1]
    tm = _pick(t, (512, 256, 128))
    tf = _pick(f, (256, 128))
    row = lambda b, i: (b, 0, 0)
    wspec = lambda shape: pl.BlockSpec(shape, lambda b, i: (0, 0), pipeline_mode=pl.Buffered(1))
    return pl.pallas_call(
        functools.partial(_ffn_kernel, tf=tf, alpha=alpha),
        grid=(bsz, t // tm),
        in_specs=[pl.BlockSpec((1, tm, d), lambda b, i: (b, i, 0)),
                  pl.BlockSpec((1, 1, d), row), pl.BlockSpec((1, 1, d), row), pl.BlockSpec((1, 1, d), row),
                  wspec((d, f)), wspec((d, f)), wspec((f, d)),
                  pl.BlockSpec((1, d), lambda b, i: (0, 0)), pl.BlockSpec((1, d), lambda b, i: (0, 0))],
        out_specs=pl.BlockSpec((1, tm, d), lambda b, i: (b, i, 0)),
        out_shape=jax.ShapeDtypeStruct((bsz, t, d), F32),
        compiler_params=_params(("parallel", "parallel")),
        name="ffn",
    )(x, shift, scale, gate, wg, wu, wo, ln_g, ln_b)


def _proj_kernel(x_ref, sh_ref, sc_ref, w_ref, cos_ref, s1_ref, s2_ref, o_ref, *, tn, n_rope_blks):
    u = (x_ref[0] * (1.0 + sc_ref[0]) + sh_ref[0]).astype(BF16)
    n = w_ref.shape[1]
    per = tn // LANE
    for j in range(n // tn):
        y = _dot(u, w_ref[:, j * tn:(j + 1) * tn])
        for i in range(per):
            blk = j * per + i
            yi = y[:, i * LANE:(i + 1) * LANE]
            if blk < n_rope_blks:
                yi = (yi * cos_ref[...] + pltpu.roll(yi, LANE - ROT_DIM // 2, 1) * s1_ref[...]
                      + pltpu.roll(yi, ROT_DIM // 2, 1) * s2_ref[...])
            o_ref[0, :, blk * LANE:(blk + 1) * LANE] = yi.astype(o_ref.dtype)


def _proj(x, shift, scale, w, tables, n_rope_blks, out_dtype, name):
    bsz, t, d = x.shape
    n = w.shape[1]
    tm = _pick(t, (512, 256, 128))
    tn = _pick(n, (256, 128))
    row = lambda b, i: (b, 0, 0)
    tab = pl.BlockSpec((tm, LANE), lambda b, i: (i, 0))
    return pl.pallas_call(
        functools.partial(_proj_kernel, tn=tn, n_rope_blks=n_rope_blks),
        grid=(bsz, t // tm),
        in_specs=[pl.BlockSpec((1, tm, d), lambda b, i: (b, i, 0)),
                  pl.BlockSpec((1, 1, d), row), pl.BlockSpec((1, 1, d), row),
                  pl.BlockSpec((d, n), lambda b, i: (0, 0), pipeline_mode=pl.Buffered(1)),
                  tab, tab, tab],
        out_specs=pl.BlockSpec((1, tm, n), lambda b, i: (b, i, 0)),
        out_shape=jax.ShapeDtypeStruct((bsz, t, n), out_dtype),
        compiler_params=_params(("parallel", "parallel")),
        name=name,
    )(x, shift, scale, w, *tables)


def _cum_kernel(p_ref, bias_ref, out_ref, kcat_ref, carry_ref):
    @pl.when(pl.program_id(1) == 0)
    def _():
        carry_ref[...] = jnp.zeros_like(carry_ref)

    tc = p_ref.shape[1]
    lane = lax.broadcasted_iota(I32, (tc, LANE), 1)
    lf = _log_sigmoid(p_ref[0] + bias_ref[...])
    lf = jnp.where(lane >= LANE_FG, jnp.where(lane < LANE_FG + D_HEADS, lf, 0.0), 0.0)
    ri = lax.broadcasted_iota(I32, (tc, tc), 0)
    ci = lax.broadcasted_iota(I32, (tc, tc), 1)
    tri = jnp.where(ri >= ci, 1.0, 0.0).astype(BF16)
    hi, mid, lo = _split3(lf)
    cum = _dot(tri, hi) + _dot(tri, mid) + _dot(tri, lo) + carry_ref[...]
    carry_ref[...] = cum[tc - 1:tc, :]
    er = lax.broadcasted_iota(I32, (LANE, LANE), 0)
    ch, cm, cl = _split3(-cum)
    for h in range(D_HEADS):
        onehot = jnp.where(er == LANE_FG + h, 1.0, 0.0).astype(BF16)
        out_ref[0, h] = _dot(ch, onehot) + _dot(cm, onehot) + _dot(cl, onehot)
    x = p_ref[0]
    hi = x.astype(BF16).astype(F32)
    kcat_ref[0, :, 0:LANE] = jnp.where(lane < IDX_DIM, hi, pltpu.roll(hi, IDX_DIM, 1)).astype(BF16)
    kcat_ref[0, :, LANE:2 * LANE] = jnp.where(lane < IDX_DIM, x - hi, 0.0).astype(BF16)


def _cum(p32, bias_row, tc):
    bsz, t, _ = p32.shape
    return pl.pallas_call(
        _cum_kernel,
        grid=(bsz, t // tc),
        in_specs=[pl.BlockSpec((1, tc, LANE), lambda b, i: (b, i, BLK_IX)),
                  pl.BlockSpec((1, LANE), lambda b, i: (0, 0))],
        out_specs=[pl.BlockSpec((1, D_HEADS, tc, LANE), lambda b, i: (b, 0, i, 0)),
                   pl.BlockSpec((1, tc, 2 * LANE), lambda b, i: (b, i, 0))],
        out_shape=[jax.ShapeDtypeStruct((bsz, D_HEADS, t, LANE), F32),
                   jax.ShapeDtypeStruct((bsz, t, 2 * LANE), BF16)],
        scratch_shapes=[pltpu.VMEM((1, LANE), F32)],
        compiler_params=_params(("parallel", "arbitrary")),
        name="forget_cumsum",
    )(p32, bias_row)


def _head_halves(q):
    lane = lax.broadcasted_iota(I32, q.shape, 1)
    zero = jnp.zeros_like(q)
    return jnp.where(lane < HEAD_DIM, q, zero), jnp.where(lane >= HEAD_DIM, q, zero)


def _static_loop(n, body, carry):
    for i in range(n):
        carry = body(i, carry)
    return carry


def _fold8(x, op):
    out = x[0:8]
    for r in range(8, x.shape[0], 8):
        out = op(out, x[r:r + 8])
    return out


def _transposed_values(p16, blk, npair, tq):
    bsz, t, _ = p16.shape
    v = p16[:, :, blk * LANE:(blk + npair) * LANE].reshape(bsz, t // tq, tq, npair, LANE)
    return v.transpose(0, 3, 1, 4, 2)


def _token_major(yt):
    bsz, npair, _, t = yt.shape
    return yt.transpose(0, 3, 1, 2).reshape(bsz, t, npair * LANE)


def _lane_pick(a, b):
    lane = lax.broadcasted_iota(I32, a.shape, 1)
    return jnp.where(lane < HEAD_DIM, a, b)


def _dil_kernel(q_ref, k_ref, v_ref, o_ref, lse_ref, *, kw):
    nb = q_ref.shape[1] // SPAN

    def body(i, carry):
        qs = pl.multiple_of(i * SPAN, SPAN)
        ks = pl.multiple_of(jnp.maximum(i * SPAN + SPAN - kw, 0), SPAN)
        q = q_ref[0, pl.ds(qs, SPAN), :] * 0.125
        k = k_ref[0, pl.ds(ks, kw), :]
        v = v_ref[0, pl.ds(ks, kw), :]
        qpos = qs + lax.broadcasted_iota(I32, (SPAN, kw), 0)
        kpos = ks + lax.broadcasted_iota(I32, (SPAN, kw), 1)
        dist = qpos - kpos
        outs, lses = [], []
        for qh in _head_halves(q):
            s = _nt(qh, k)
            s = jnp.where(dist >= 0, jnp.where(dist <= SPAN, s, NEG_BIG), NEG_BIG)
            m = jnp.max(s, axis=-1, keepdims=True)
            p = jnp.exp(s - m)
            l = jnp.sum(p, axis=-1, keepdims=True)
            outs.append(_dot(p.astype(BF16), v) / l)
            lses.append(jnp.broadcast_to(m + jnp.log(l), (SPAN, LANE)))
        o_ref[0, pl.ds(qs, SPAN), :] = _lane_pick(outs[0], outs[1]).astype(o_ref.dtype)
        lse_ref[0, pl.ds(qs, SPAN), :] = _lane_pick(lses[0], lses[1])
        return carry

    lax.fori_loop(0, nb, body, 0, unroll=min(nb, 4))


def _dilated(q, k, v):
    ns, ln, _ = q.shape
    kw = min(2 * SPAN, ln)
    spec = pl.BlockSpec((1, ln, LANE), lambda s: (s, 0, 0))
    return pl.pallas_call(
        functools.partial(_dil_kernel, kw=kw),
        grid=(ns,),
        in_specs=[spec, spec, spec],
        out_specs=[spec, spec],
        out_shape=[jax.ShapeDtypeStruct((ns, ln, LANE), BF16), jax.ShapeDtypeStruct((ns, ln, LANE), F32)],
        compiler_params=_params(("parallel",)),
        name="dilated_window",
    )(q, k, v)


def _fox_kernel(q_ref, k_ref, vt_ref, bias_ref, o_ref, s_scr, p_scr, al_scr, acc_scr, m_scr, l_scr, *, tq):
    qi = pl.program_id(2)
    rep = tq // LANE
    nstrip = tq // STRIP
    qh = _head_halves(q_ref[0] * 0.125)
    m_scr[...] = jnp.full(m_scr.shape, NEG_BIG, F32)
    l_scr[...] = jnp.zeros(l_scr.shape, F32)
    acc_scr[...] = jnp.zeros(acc_scr.shape, F32)
    kk = lax.broadcasted_iota(I32, (STRIP, tq), 0)
    qq = lax.broadcasted_iota(I32, (STRIP, tq), 1)

    def issue_scores(j, s_buf):
        kblk = k_ref[0, pl.ds(pl.multiple_of(j * tq, tq), tq), :]
        for h in range(2):
            s_buf[h] = _nt(kblk, qh[h])

    def softmax(j, s_buf, p_buf, al_buf, diag):
        ks = pl.multiple_of(j * tq, tq)
        for h in range(2):
            def pass1(i, mx):
                r0 = i * STRIP
                s = s_buf[h, pl.ds(r0, STRIP), :] + jnp.tile(
                    bias_ref[0, h, pl.ds(pl.multiple_of(ks + r0, STRIP), STRIP), :], (1, rep))
                if diag:
                    s = jnp.where(kk + r0 <= qq, s, NEG_BIG)
                s_buf[h, pl.ds(r0, STRIP), :] = s
                return jnp.maximum(mx, _fold8(s, jnp.maximum))

            mx = _static_loop(nstrip, pass1, jnp.full((8, tq), NEG_BIG, F32))
            m_old = m_scr[h]
            m_new = jnp.maximum(m_old, jnp.max(mx, axis=0, keepdims=True))
            al_buf[h] = jnp.exp(m_old - m_new)
            m_scr[h] = m_new

            def pass2(i, ls):
                r0 = i * STRIP
                p = jnp.exp(s_buf[h, pl.ds(r0, STRIP), :] - m_new)
                p_buf[h, pl.ds(r0, STRIP), :] = p.astype(BF16)
                return ls + _fold8(p, jnp.add)

            ls = _static_loop(nstrip, pass2, jnp.zeros((8, tq), F32))
            l_scr[h] = al_buf[h] * l_scr[h] + jnp.sum(ls, axis=0, keepdims=True)

    def apply_values(j, p_buf, al_buf):
        for h in range(2):
            pv = _dot(vt_ref[0, 0, j, h * HEAD_DIM:(h + 1) * HEAD_DIM, :], p_buf[h])
            acc_scr[h] = acc_scr[h] * al_buf[h] + pv

    s_a, s_b = s_scr.at[0], s_scr.at[1]
    p_a, p_b = p_scr.at[0], p_scr.at[1]
    al_a, al_b = al_scr.at[0], al_scr.at[1]
    p_b[...] = jnp.zeros(p_b.shape, BF16)
    al_b[...] = jnp.ones(al_b.shape, F32)
    issue_scores(0, s_a)

    def pair(i, carry):
        a = 2 * i
        issue_scores(a + 1, s_b)
        softmax(a, s_a, p_a, al_a, False)
        apply_values(jnp.maximum(a - 1, 0), p_b, al_b)
        issue_scores(a + 2, s_a)
        softmax(a + 1, s_b, p_b, al_b, False)
        apply_values(a, p_a, al_a)
        return carry

    lax.fori_loop(0, qi // 2, pair, 0)
    last = 2 * (qi // 2)

    @pl.when(qi % 2 == 1)
    def _():
        issue_scores(qi, s_b)
        softmax(last, s_a, p_a, al_a, False)
        apply_values(jnp.maximum(last - 1, 0), p_b, al_b)
        softmax(qi, s_b, p_b, al_b, True)
        apply_values(last, p_a, al_a)
        apply_values(qi, p_b, al_b)

    @pl.when(qi % 2 == 0)
    def _():
        softmax(qi, s_a, p_a, al_a, True)
        apply_values(jnp.maximum(qi - 1, 0), p_b, al_b)
        apply_values(qi, p_a, al_a)

    for h in range(2):
        o_ref[0, 0, h * HEAD_DIM:(h + 1) * HEAD_DIM, :] = (acc_scr[h] * (1.0 / l_scr[h])).astype(o_ref.dtype)


def _fox(p16, vt, bias, tq):
    bsz, t, _ = p16.shape
    npair = D_HEADS // 2
    nblk = t // tq
    return pl.pallas_call(
        functools.partial(_fox_kernel, tq=tq),
        grid=(bsz, npair, nblk),
        in_specs=[pl.BlockSpec((1, tq, LANE), lambda b, h, i: (b, i, BLK_DQ + h)),
                  pl.BlockSpec((1, t, LANE), lambda b, h, i: (b, 0, BLK_DK + h)),
                  pl.BlockSpec((1, 1, nblk, LANE, tq), lambda b, h, i: (b, h, 0, 0, 0)),
                  pl.BlockSpec((1, 2, t, LANE), lambda b, h, i: (b, h, 0, 0))],
        out_specs=pl.BlockSpec((1, 1, LANE, tq), lambda b, h, i: (b, h, 0, i)),
        out_shape=jax.ShapeDtypeStruct((bsz, npair, LANE, t), BF16),
        scratch_shapes=[pltpu.VMEM((2, 2, tq, tq), F32), pltpu.VMEM((2, 2, tq, tq), BF16),
                        pltpu.VMEM((2, 2, 1, tq), F32),
                        pltpu.VMEM((2, HEAD_DIM, tq), F32), pltpu.VMEM((2, 1, tq), F32),
                        pltpu.VMEM((2, 1, tq), F32)],
        compiler_params=_params(("parallel", "parallel", "arbitrary")),
        name="forgetting_attention",
    )(p16, p16, vt, bias)


def _sb_kernel(q_ref, k_ref, vt_ref, o_ref, z_scr, sfx_scr, hi_scr, lo_scr, a_scr, acc_scr, r_scr, *, tq):
    qi = pl.program_id(2)
    nstrip = tq // STRIP
    qh = _head_halves(q_ref[0] * 0.125)
    acc_scr[...] = jnp.zeros(acc_scr.shape, F32)
    r_scr[...] = jnp.zeros(r_scr.shape, F32)
    kk = lax.broadcasted_iota(I32, (STRIP, tq), 0)
    qq = lax.broadcasted_iota(I32, (STRIP, tq), 1)
    row = lax.broadcasted_iota(I32, (tq, tq), 0)
    col = lax.broadcasted_iota(I32, (tq, tq), 1)
    later = jnp.where(col > row, 1.0, 0.0).astype(BF16)

    def block(j, diag):
        ks = pl.multiple_of(j * tq, tq)
        kblk = k_ref[0, pl.ds(ks, tq), :]
        for h in range(2):
            z_scr[h] = _nt(kblk, qh[h])
        for h in range(2):
            def pass1(i, rsum):
                r0 = i * STRIP
                z = z_scr[h, pl.ds(r0, STRIP), :]
                lk = _log_sigmoid(-z)
                z_scr[h, pl.ds(r0, STRIP), :] = lk + z
                if diag:
                    lk = jnp.where(kk + r0 < qq, lk, 0.0)
                hi = lk.astype(BF16)
                hi_scr[h, pl.ds(r0, STRIP), :] = hi
                lo_scr[h, pl.ds(r0, STRIP), :] = (lk - hi.astype(F32)).astype(BF16)
                return rsum + _fold8(lk, jnp.add)

            rsum = _static_loop(nstrip, pass1, jnp.zeros((8, tq), F32))
            sfx_scr[h] = _dot(later, hi_scr[h]) + _dot(later, lo_scr[h])
            r_old = r_scr[h]

            def pass2(i, carry):
                r0 = i * STRIP
                a = jnp.exp(z_scr[h, pl.ds(r0, STRIP), :] + sfx_scr[h, pl.ds(r0, STRIP), :] + r_old)
                if diag:
                    a = jnp.where(kk + r0 < qq, a, 0.0)
                a_scr[h, pl.ds(r0, STRIP), :] = a.astype(BF16)
                return carry

            _static_loop(nstrip, pass2, 0)
            acc_scr[h] = acc_scr[h] + _dot(vt_ref[0, 0, j, h * HEAD_DIM:(h + 1) * HEAD_DIM, :], a_scr[h])
            r_scr[h] = r_old + jnp.sum(rsum, axis=0, keepdims=True)

    block(qi, True)

    def live():
        return jnp.max(jnp.maximum(r_scr[0], r_scr[1])) > EXP_DEAD

    def cond(c):
        return jnp.logical_and(c[0] >= 0, c[1])

    def body(c):
        block(c[0], False)
        return c[0] - 1, live()

    lax.while_loop(cond, body, (qi - 1, live()))
    for h in range(2):
        o_ref[0, 0, h * HEAD_DIM:(h + 1) * HEAD_DIM, :] = acc_scr[h].astype(o_ref.dtype)


def _stick_breaking(p16, vt, tq):
    bsz, t, _ = p16.shape
    npair = C_HEADS // 2
    nblk = t // tq
    return pl.pallas_call(
        functools.partial(_sb_kernel, tq=tq),
        grid=(bsz, npair, nblk),
        in_specs=[pl.BlockSpec((1, tq, LANE), lambda b, h, i: (b, i, BLK_CQ + h)),
                  pl.BlockSpec((1, t, LANE), lambda b, h, i: (b, 0, BLK_CK + h)),
                  pl.BlockSpec((1, 1, nblk, LANE, tq), lambda b, h, i: (b, h, 0, 0, 0))],
        out_specs=pl.BlockSpec((1, 1, LANE, tq), lambda b, h, i: (b, h, 0, i)),
        out_shape=jax.ShapeDtypeStruct((bsz, npair, LANE, t), BF16),
        scratch_shapes=[pltpu.VMEM((2, tq, tq), F32), pltpu.VMEM((2, tq, tq), F32),
                        pltpu.VMEM((2, tq, tq), BF16), pltpu.VMEM((2, tq, tq), BF16),
                        pltpu.VMEM((2, tq, tq), BF16), pltpu.VMEM((2, HEAD_DIM, tq), F32),
                        pltpu.VMEM((2, 1, tq), F32)],
        compiler_params=_params(("parallel", "parallel", "arbitrary")),
        name="stick_breaking",
    )(p16, p16, vt)


def _dsa_kernel(iq0_ref, iq1_ref, ixq_ref, kcat_ref, q0_ref, q1_ref, k0_ref, k1_ref, vt_ref, o_ref,
                key_scr, s_scr, p_scr, eq_scr, rank_scr, bias_scr, acc_scr, m_scr, l_scr, *, tq, topk):
    qi = pl.program_id(1)
    nstrip = tq // STRIP
    lane = lax.broadcasted_iota(I32, (tq, LANE), 1)
    kk = lax.broadcasted_iota(I32, (STRIP, tq), 0)
    qq = lax.broadcasted_iota(I32, (STRIP, tq), 1)

    qcat = []
    for ref in (iq0_ref, iq1_ref):
        for half in range(2):
            x = ref[0] if half == 0 else pltpu.roll(ref[0], IDX_DIM, 1)
            hi = x.astype(BF16).astype(F32)
            first = jnp.where(lane < IDX_DIM, hi, pltpu.roll(x - hi, IDX_DIM, 1))
            second = jnp.where(lane < IDX_DIM, hi, 0.0)
            qcat.append(jnp.concatenate([first, second], axis=1).astype(BF16))
    sr = lax.broadcasted_iota(I32, (8, LANE), 0)
    sl = lax.broadcasted_iota(I32, (8, LANE), 1)
    onehot = jnp.where(sl == sr + LANE_IW, 1.0, 0.0).astype(BF16)
    wh, wm, wl = _split3(ixq_ref[0])
    wt = _nt(onehot, wh) + _nt(onehot, wm) + _nt(onehot, wl)

    def score_chunk(c, diag):
        ks = pl.multiple_of(c * tq, tq)
        kc = kcat_ref[0, pl.ds(ks, tq), :]
        for h in range(IDX_HEADS):
            s_scr[h] = _nt(kc, qcat[h])
        for i in range(nstrip):
            r0 = i * STRIP
            sc = jnp.zeros((STRIP, tq), F32)
            for h in range(IDX_HEADS):
                sc = sc + wt[h:h + 1, :] * jnp.maximum(s_scr[h, r0:r0 + STRIP, :], 0.0)
            sc = jnp.where(sc == 0.0, 0.0, sc)
            bits = lax.bitcast_convert_type(sc, I32)
            key = jnp.where(bits < 0, bits ^ jnp.int32(0x7FFFFFFF), bits)
            if diag:
                key = jnp.where(kk + r0 <= qq, key, INT_MIN)
            key_scr[c, r0:r0 + STRIP, :] = key

    def score_body(c, carry):
        score_chunk(c, False)
        return carry

    lax.fori_loop(0, qi, score_body, 0)
    score_chunk(qi, True)

    def count(pred):
        def body(c, acc):
            for i in range(nstrip):
                acc = acc + _fold8(jnp.where(pred(key_scr[c, i * STRIP:(i + 1) * STRIP, :]), 1.0, 0.0), jnp.add)
            return acc
        acc = lax.fori_loop(0, qi + 1, body, jnp.zeros((8, tq), F32))
        return jnp.sum(acc, axis=0, keepdims=True)

    kf = jnp.float32(topk)
    n_ge0 = count(lambda kb: kb >= 0)
    n_gt0 = count(lambda kb: kb > 0)
    th0 = jnp.where(n_ge0 >= kf, 0, INT_MIN).astype(I32)
    settled0 = jnp.where(n_ge0 >= kf, jnp.where(n_gt0 < kf, 1.0, 0.0), 0.0)

    def unsettled(settled):
        return jnp.min(settled) == 0.0

    def search_cond(c):
        return jnp.logical_and(c[0] >= 0, c[3])

    def search_body(c):
        bit, th, settled, _ = c
        cand = th | lax.shift_left(jnp.int32(1), bit)
        n = count(lambda kb: kb >= cand)
        th = jnp.where(n >= kf, jnp.where(settled == 0.0, cand, th), th)
        settled = jnp.maximum(settled, jnp.where(n == kf, 1.0, 0.0))
        return bit - 1, th, settled, unsettled(settled)

    _, th, _, _ = lax.while_loop(search_cond, search_body, (jnp.int32(30), th0, settled0, unsettled(settled0)))
    need = kf - count(lambda kb: kb > th)

    qh = _head_halves(q0_ref[0] * 0.125) + _head_halves(q1_ref[0] * 0.125)
    k_refs = (k0_ref, k0_ref, k1_ref, k1_ref)
    m_scr[...] = jnp.full(m_scr.shape, NEG_BIG, F32)
    l_scr[...] = jnp.zeros(l_scr.shape, F32)
    acc_scr[...] = jnp.zeros(acc_scr.shape, F32)
    row = lax.broadcasted_iota(I32, (tq, tq), 0)
    col = lax.broadcasted_iota(I32, (tq, tq), 1)
    upto = jnp.where(col <= row, 1.0, 0.0).astype(BF16)

    def attend_chunk(c, diag, seen):
        ks = pl.multiple_of(c * tq, tq)
        for i in range(nstrip):
            r0 = i * STRIP
            eq_scr[r0:r0 + STRIP, :] = jnp.where(key_scr[c, r0:r0 + STRIP, :] == th, 1.0, 0.0).astype(BF16)
        rank_scr[...] = _dot(upto, eq_scr[...])
        for i in range(nstrip):
            r0 = i * STRIP
            kb = key_scr[c, r0:r0 + STRIP, :]
            tie = jnp.where(rank_scr[r0:r0 + STRIP, :] + seen <= need, 0.0, NEG_BIG)
            bias = jnp.where(kb > th, 0.0, jnp.where(kb == th, tie, NEG_BIG))
            if diag:
                bias = jnp.where(kk + r0 <= qq, bias, NEG_BIG)
            bias_scr[r0:r0 + STRIP, :] = bias
        seen = seen + rank_scr[tq - 1:tq, :]
        for h in range(B_HEADS):
            s_scr[h] = _nt(k_refs[h][0, pl.ds(ks, tq), :], qh[h])
        for h in range(B_HEADS):
            def pass1(i, mx):
                r0 = i * STRIP
                s = s_scr[h, r0:r0 + STRIP, :] + bias_scr[r0:r0 + STRIP, :]
                s_scr[h, r0:r0 + STRIP, :] = s
                return jnp.maximum(mx, _fold8(s, jnp.maximum))

            mx = _static_loop(nstrip, pass1, jnp.full((8, tq), NEG_BIG, F32))
            m_old = m_scr[h]
            m_new = jnp.maximum(m_old, jnp.max(mx, axis=0, keepdims=True))
            alpha = jnp.exp(m_old - m_new)
            m_scr[h] = m_new

            def pass2(i, ls):
                r0 = i * STRIP
                p = jnp.exp(s_scr[h, r0:r0 + STRIP, :] - m_new)
                p_scr[h, r0:r0 + STRIP, :] = p.astype(BF16)
                return ls + _fold8(p, jnp.add)

            ls = _static_loop(nstrip, pass2, jnp.zeros((8, tq), F32))
            l_scr[h] = alpha * l_scr[h] + jnp.sum(ls, axis=0, keepdims=True)
            lo = (h % 2) * HEAD_DIM
            acc_scr[h] = acc_scr[h] * alpha + _dot(vt_ref[0, h // 2, c, lo:lo + HEAD_DIM, :], p_scr[h])
        return seen

    seen = lax.fori_loop(0, qi, lambda c, seen: attend_chunk(c, False, seen), jnp.zeros((1, tq), F32))
    attend_chunk(qi, True, seen)
    for h in range(B_HEADS):
        lo = (h % 2) * HEAD_DIM
        o_ref[0, h // 2, lo:lo + HEAD_DIM, :] = (acc_scr[h] * (1.0 / l_scr[h])).astype(o_ref.dtype)


def _dsa(p16, p32, kcat, vt, tq):
    bsz, t, _ = p16.shape
    topk = min(DSA_TOPK, t // 4)
    npair = B_HEADS // 2
    nblk = t // tq
    qspec = lambda arr_blk: pl.BlockSpec((1, tq, LANE), lambda b, i: (b, i, arr_blk))
    once = pl.Buffered(1)
    kspec = lambda arr_blk: pl.BlockSpec((1, t, LANE), lambda b, i: (b, 0, arr_blk), pipeline_mode=once)
    sq = lambda n, dt: pltpu.VMEM((n, tq, tq), dt) if n else pltpu.VMEM((tq, tq), dt)
    return pl.pallas_call(
        functools.partial(_dsa_kernel, tq=tq, topk=topk),
        grid=(bsz, nblk),
        in_specs=[qspec(BLK_IQ), qspec(BLK_IQ + 1), qspec(BLK_IX),
                  pl.BlockSpec((1, t, 2 * LANE), lambda b, i: (b, 0, 0), pipeline_mode=once),
                  qspec(BLK_BQ), qspec(BLK_BQ + 1), kspec(BLK_BK), kspec(BLK_BK + 1),
                  pl.BlockSpec((1, npair, nblk, LANE, tq), lambda b, i: (b, 0, 0, 0, 0), pipeline_mode=once)],
        out_specs=pl.BlockSpec((1, npair, LANE, tq), lambda b, i: (b, 0, 0, i)),
        out_shape=jax.ShapeDtypeStruct((bsz, npair, LANE, t), BF16),
        scratch_shapes=[sq(nblk, I32), sq(B_HEADS, F32), sq(B_HEADS, BF16), sq(0, BF16), sq(0, F32), sq(0, F32),
                        pltpu.VMEM((B_HEADS, HEAD_DIM, tq), F32), pltpu.VMEM((B_HEADS, 1, tq), F32),
                        pltpu.VMEM((B_HEADS, 1, tq), F32)],
        compiler_params=_params(("parallel", "arbitrary")),
        name="dsa_topk_attention",
    )(p32, p32, p32, kcat, p16, p16, p16, p16, vt)


def _merge_kernel(x_ref, sh_ref, sc_ref, gt_ref, oa0_ref, oa1_ref, oa2_ref, la0_ref, la1_ref, la2_ref,
                  yb_ref, yc_ref, yd_ref, wg_ref, bg_ref, wb_ref, wo_ref, lg_ref, lb_ref, o_ref, *, alpha):
    x = x_ref[0]
    d = x.shape[1]
    u = (x * (1.0 + sc_ref[0]) + sh_ref[0]).astype(BF16)
    lses = [r[0] for r in (la0_ref, la1_ref, la2_ref)]
    top = jnp.maximum(jnp.maximum(lses[0], lses[1]), lses[2])
    es = [jnp.exp(l - top) for l in lses]
    den = es[0] + es[1] + es[2]
    ya = sum((e / den) * r[0].astype(F32) for e, r in zip(es, (oa0_ref, oa1_ref, oa2_ref)))
    branches = (ya.astype(BF16), yb_ref[0], yc_ref[0], yd_ref[0])
    merged = jnp.zeros(x.shape, F32)
    off = 0
    for i, br in enumerate(branches):
        gate = jax.nn.sigmoid(_dot(u, wg_ref[:, i * d:(i + 1) * d]) + bg_ref[:, i * d:(i + 1) * d])
        merged = merged + gate * _dot(br, wb_ref[off:off + BRANCH_WIDTHS[i], :])
        off += BRANCH_WIDTHS[i]
    h = _dot(merged.astype(BF16), wo_ref[...])
    o_ref[0] = _deepnorm_ln(x, h, gt_ref[0], lg_ref[...], lb_ref[...], alpha)


def _merge(x, shift, scale, gate, oas, las, yb, yc, yd, wg, bg, wb, wo, ln_g, ln_b, alpha):
    bsz, t, d = x.shape
    tm = _pick(t, (256, 128))
    row = lambda b, i: (b, 0, 0)
    tok = lambda w: pl.BlockSpec((1, tm, w), lambda b, i: (b, i, 0))
    wspec = lambda shape: pl.BlockSpec(shape, lambda b, i: (0, 0), pipeline_mode=pl.Buffered(1))
    return pl.pallas_call(
        functools.partial(_merge_kernel, alpha=alpha),
        grid=(bsz, t // tm),
        in_specs=[tok(d), pl.BlockSpec((1, 1, d), row), pl.BlockSpec((1, 1, d), row), pl.BlockSpec((1, 1, d), row),
                  tok(LANE), tok(LANE), tok(LANE), tok(LANE), tok(LANE), tok(LANE),
                  tok(2 * LANE), tok(2 * LANE), tok(2 * LANE),
                  wspec(wg.shape), wspec(bg.shape), wspec(wb.shape), wspec(wo.shape),
                  pl.BlockSpec((1, d), lambda b, i: (0, 0)), pl.BlockSpec((1, d), lambda b, i: (0, 0))],
        out_specs=tok(d),
        out_shape=jax.ShapeDtypeStruct((bsz, t, d), F32),
        compiler_params=_params(("parallel", "parallel")),
        name="gated_merge",
    )(x, shift, scale, gate, *oas, *las, yb, yc, yd, wg, bg, wb, wo, ln_g, ln_b)


def _rope_tables(t):
    half = ROT_DIM // 2
    inv_freq = ROPE_THETA ** (-(jnp.arange(half, dtype=F32) * (2.0 / ROT_DIM)))
    ang = jnp.arange(t, dtype=F32)[:, None] * inv_freq[None, :]
    cos, sin = jnp.cos(ang), jnp.sin(ang)
    ones = jnp.ones((t, HEAD_DIM - ROT_DIM), F32)
    zeros = jnp.zeros((t, HEAD_DIM - half), F32)
    c64 = jnp.concatenate([cos, cos, ones], axis=1)
    s1 = jnp.concatenate([-sin, zeros], axis=1)
    s2 = jnp.concatenate([jnp.zeros((t, half), F32), sin, zeros[:, half:]], axis=1)
    return tuple(jnp.concatenate([a, a], axis=1) for a in (c64, s1, s2))


def _mixer_weights(w_in):
    d = w_in.shape[0]

    def qkv(off, heads):
        w = heads * HEAD_DIM
        return w_in[:, off:off + w], w_in[:, off + w:off + 2 * w], w_in[:, off + 2 * w:off + 3 * w]

    aq, ak, av = qkv(OFF_A, A_HEADS)
    bq, bk, bv = qkv(OFF_B, B_HEADS)
    cq, ck, cv = qkv(OFF_C, C_HEADS)
    dq, dk, dv = qkv(OFF_D, D_HEADS)
    w16 = jnp.concatenate([aq, ak, bq, bk, av, bv, cq, ck, cv, dq, dk, dv, jnp.zeros((d, LANE), F32)], axis=1)
    ix = jnp.concatenate([w_in[:, OFF_IK:OFF_IK + IDX_DIM], jnp.zeros((d, LANE_IW - IDX_DIM), F32),
                          w_in[:, OFF_IW:OFF_IW + IDX_HEADS], w_in[:, OFF_FG:OFF_FG + D_HEADS],
                          jnp.zeros((d, LANE - LANE_FG - D_HEADS), F32)], axis=1)
    w32 = jnp.concatenate([w_in[:, OFF_IQ:OFF_IQ + IDX_HEADS * IDX_DIM], ix, jnp.zeros((d, LANE), F32)], axis=1)
    return w16.astype(BF16), w32.astype(BF16), w_in[:, OFF_GATE:].astype(BF16)


def _to_residues(a, dil):
    bsz, t, w = a.shape
    if dil == 1:
        return a
    return a.reshape(bsz, t // dil, dil, w).transpose(0, 2, 1, 3).reshape(bsz * dil, t // dil, w)


def _from_residues(a, dil, bsz):
    if dil == 1:
        return a
    _, ln, w = a.shape
    return a.reshape(bsz, dil, ln, w).transpose(0, 2, 1, 3).reshape(bsz, ln * dil, w)


def _mixer(x, shift, scale, gate, w_in, b_gate, b_forget, w_branch, w_out, ln_g, ln_b, tables, alpha):
    bsz, t, d = x.shape
    w16, w32, wg = _mixer_weights(w_in)
    p16 = _proj(x, shift, scale, w16, tables, N_ROPE_BLKS_16, BF16, "mixer_proj_bf16")
    p32 = _proj(x, shift, scale, w32, tables, N_BLKS_32, F32, "mixer_proj_f32")

    oas, las = [], []
    for g, (_, dil) in enumerate(DILATED_GROUPS):
        streams = [_to_residues(p16[:, :, (blk + g) * LANE:(blk + g + 1) * LANE], dil)
                   for blk in (BLK_AQ, BLK_AK, BLK_AV)]
        o, lse = _dilated(*streams)
        oas.append(_from_residues(o, dil, bsz))
        las.append(_from_residues(lse, dil, bsz))

    tq = _pick(t, (256, 128))
    bias_row = jnp.zeros((1, LANE), F32).at[0, LANE_FG:LANE_FG + D_HEADS].set(b_forget)
    fbias, kcat = _cum(p32, bias_row, tq)
    tqf = _pick(t, (512, 256, 128))
    yd = _token_major(_fox(p16, _transposed_values(p16, BLK_DV, D_HEADS // 2, tqf), fbias, tqf))
    yc = _token_major(_stick_breaking(p16, _transposed_values(p16, BLK_CV, C_HEADS // 2, tq), tq))
    yb = _token_major(_dsa(p16, p32, kcat, _transposed_values(p16, BLK_BV, B_HEADS // 2, tqf), tqf))
    return _merge(x, shift, scale, gate, oas, las, yb, yc, yd, wg, b_gate.reshape(1, -1),
                  w_branch.astype(BF16), w_out.astype(BF16), ln_g, ln_b, alpha)


def kernel(x, c, ada_w, ada_b, ln_g, ln_b, ffn_w_in, ffn_w_out, mix_w_in, mix_b_gate, mix_b_forget,
           mix_w_branch, mix_w_out):
    bsz, t, d = x.shape
    depth = ada_w.shape[0]
    f = ffn_w_out.shape[2]
    alpha = float((2 * depth) ** 0.25)
    assert t % (DILATED_GROUPS[-1][0]) == 0 and d % LANE == 0 and bsz <= 8

    c8 = jnp.zeros((8, d), F32).at[:bsz].set(c)
    mod = _ada(c8, ada_w, ada_b)[:, :bsz].reshape(depth, bsz, 3, 3, 1, d)
    tables = _rope_tables(t)

    for l in range(depth):
        m = lambda sub, kind: mod[l, :, sub, kind]
        lng = lambda sub: ln_g[l, sub].reshape(1, d)
        lnb = lambda sub: ln_b[l, sub].reshape(1, d)

        def ffn(x, sub, which):
            w_in = ffn_w_in[l, which].astype(BF16)
            return _ffn(x, m(sub, 0), m(sub, 1), m(sub, 2), w_in[:, :f], w_in[:, f:],
                        ffn_w_out[l, which].astype(BF16), lng(sub), lnb(sub), alpha)

        x = ffn(x, 0, 0)
        x = _mixer(x, m(1, 0), m(1, 1), m(1, 2), mix_w_in[l], mix_b_gate[l], mix_b_forget[l],
                   mix_w_branch[l], mix_w_out[l], lng(1), lnb(1), tables, alpha)
        x = ffn(x, 2, 1)
    return x
```

```python
import functools

import jax
import jax.numpy as jnp
from jax import lax
from jax.experimental import pallas as pl
from jax.experimental.pallas import tpu as pltpu

F32 = jnp.float32
BF16 = jnp.bfloat16
I32 = jnp.int32

LANE = 128
HEAD_DIM = 64
ROT_DIM = HEAD_DIM // 4
ROPE_THETA = 500000.0
DILATED_GROUPS = ((128, 1), (512, 4), (2048, 16))
SPAN = 128
A_HEADS, B_HEADS, C_HEADS, D_HEADS = 6, 4, 4, 4
IDX_HEADS, IDX_DIM = 4, 64
DSA_TOPK = 256
N_BRANCH = 4
LN_EPS = 1e-5
BRANCH_WIDTHS = (128, 256, 256, 256)

OFF_A = 0
OFF_B = OFF_A + 3 * A_HEADS * HEAD_DIM
OFF_IQ = OFF_B + 3 * B_HEADS * HEAD_DIM
OFF_IK = OFF_IQ + IDX_HEADS * IDX_DIM
OFF_IW = OFF_IK + IDX_DIM
OFF_C = OFF_IW + IDX_HEADS
OFF_D = OFF_C + 3 * C_HEADS * HEAD_DIM
OFF_FG = OFF_D + 3 * D_HEADS * HEAD_DIM
OFF_GATE = OFF_FG + D_HEADS

BLK_AQ, BLK_AK, BLK_BQ, BLK_BK = 0, 3, 6, 8
N_ROPE_BLKS_16 = 10
BLK_AV, BLK_BV = 10, 13
BLK_CQ, BLK_CK, BLK_CV = 15, 17, 19
BLK_DQ, BLK_DK, BLK_DV = 21, 23, 25
N_BLKS_16 = 28
BLK_IQ, BLK_IX = 0, 2
N_BLKS_32 = 4
LANE_IW = 80
LANE_FG = 84

INT_MIN = -2147483648
NEG_BIG = -1e30
LOG2E = 1.4426950408889634
EXP_DEAD = -104.0
STRIP = 32
VMEM_LIMIT = 56 * 1024 * 1024


def _nt(a, b):
    return lax.dot_general(a, b, (((1,), (1,)), ((), ())), preferred_element_type=F32)


def _dot(a, b):
    return jnp.dot(a, b, preferred_element_type=F32)


def _split3(x):
    hi = x.astype(BF16)
    r1 = x - hi.astype(F32)
    mid = r1.astype(BF16)
    lo = (r1 - mid.astype(F32)).astype(BF16)
    return hi, mid, lo


def _split2(x):
    hi = x.astype(BF16)
    return hi, (x - hi.astype(F32)).astype(BF16)


def _log_sigmoid(z):
    return -(jnp.maximum(-z, 0.0) + jnp.log(1.0 + jnp.exp(-jnp.abs(z))))


def _params(sem):
    return pltpu.CompilerParams(dimension_semantics=sem, vmem_limit_bytes=VMEM_LIMIT)


def _const_spec(shape):
    n = len(shape)
    return pl.BlockSpec(shape, lambda *_: (0,) * n)


def _pick(n, prefs):
    for p in prefs:
        if n % p == 0:
            return p
    return n


def _ada_kernel(c_ref, w_ref, b_ref, o_ref):
    c = c_ref[...]
    cond = c * jax.nn.sigmoid(c)
    ch, cm, _ = _split3(cond)
    wh, wm, _ = _split3(w_ref[0])
    o_ref[0] = _dot(ch, wh) + _dot(ch, wm) + _dot(cm, wh) + b_ref[0]


def _ada(c8, ada_w, ada_b):
    depth, d, n = ada_w.shape
    tn = _pick(n, (1152, 1024, 512, 256, 128))
    return pl.pallas_call(
        _ada_kernel,
        grid=(depth, n // tn),
        in_specs=[pl.BlockSpec((8, d), lambda l, j: (0, 0)),
                  pl.BlockSpec((1, d, tn), lambda l, j: (l, 0, j)),
                  pl.BlockSpec((1, 1, tn), lambda l, j: (l, 0, j))],
        out_specs=pl.BlockSpec((1, 8, tn), lambda l, j: (l, 0, j)),
        out_shape=jax.ShapeDtypeStruct((depth, 8, n), F32),
        compiler_params=_params(("parallel", "parallel")),
        name="ada_mod",
    )(c8, ada_w, ada_b.reshape(depth, 1, n))


def _deepnorm_ln(x, h, gate, g, b, alpha):
    y = alpha * x + (1.0 + gate) * h
    mu = jnp.mean(y, axis=-1, keepdims=True)
    yc = y - mu
    var = jnp.mean(yc * yc, axis=-1, keepdims=True)
    return yc * lax.rsqrt(var + LN_EPS) * g + b


def _ffn_kernel(x_ref, sh_ref, sc_ref, gt_ref, wg_ref, wu_ref, wo_ref, lg_ref, lb_ref, o_ref, *, tf, alpha):
    x = x_ref[0]
    u = (x * (1.0 + sc_ref[0]) + sh_ref[0]).astype(BF16)
    acc = jnp.zeros(x.shape, F32)
    for j in range(wg_ref.shape[1] // tf):
        g = _dot(u, wg_ref[:, j * tf:(j + 1) * tf])
        up = _dot(u, wu_ref[:, j * tf:(j + 1) * tf])
        h = (g * jax.nn.sigmoid(g) * up).astype(BF16)
        acc = acc + _dot(h, wo_ref[j * tf:(j + 1) * tf, :])
    o_ref[0] = _deepnorm_ln(x, 0.5 * acc, gt_ref[0], lg_ref[...], lb_ref[...], alpha)


def _ffn(x, shift, scale, gate, wg, wu, wo, ln_g, ln_b, alpha):
    bsz, t, d = x.shape
    f = wg.shape[1]
    tm = _pick(t, (512, 256, 128))
    tf = _pick(f, (256, 128))
    row = lambda b, i: (b, 0, 0)
    wspec = lambda shape: pl.BlockSpec(shape, lambda b, i: (0, 0), pipeline_mode=pl.Buffered(1))
    return pl.pallas_call(
        functools.partial(_ffn_kernel, tf=tf, alpha=alpha),
        grid=(bsz, t // tm),
        in_specs=[pl.BlockSpec((1, tm, d), lambda b, i: (b, i, 0)),
                  pl.BlockSpec((1, 1, d), row), pl.BlockSpec((1, 1, d), row), pl.BlockSpec((1, 1, d), row),
                  wspec((d, f)), wspec((d, f)), wspec((f, d)),
                  pl.BlockSpec((1, d), lambda b, i: (0, 0)), pl.BlockSpec((1, d), lambda b, i: (0, 0))],
        out_specs=pl.BlockSpec((1, tm, d), lambda b, i: (b, i, 0)),
        out_shape=jax.ShapeDtypeStruct((bsz, t, d), F32),
        compiler_params=_params(("parallel", "parallel")),
        name="ffn",
    )(x, shift, scale, gate, wg, wu, wo, ln_g, ln_b)


def _proj_kernel(x_ref, sh_ref, sc_ref, w_ref, cos_ref, s1_ref, s2_ref, o_ref, *, tn, n_rope_blks):
    u = (x_ref[0] * (1.0 + sc_ref[0]) + sh_ref[0]).astype(BF16)
    n = w_ref.shape[1]
    per = tn // LANE
    for j in range(n // tn):
        y = _dot(u, w_ref[:, j * tn:(j + 1) * tn])
        for i in range(per):
            blk = j * per + i
            yi = y[:, i * LANE:(i + 1) * LANE]
            if blk < n_rope_blks:
                yi = (yi * cos_ref[...] + pltpu.roll(yi, LANE - ROT_DIM // 2, 1) * s1_ref[...]
                      + pltpu.roll(yi, ROT_DIM // 2, 1) * s2_ref[...])
            o_ref[0, :, blk * LANE:(blk + 1) * LANE] = yi.astype(o_ref.dtype)


def _proj(x, shift, scale, w, tables, n_rope_blks, out_dtype, name):
    bsz, t, d = x.shape
    n = w.shape[1]
    tm = _pick(t, (512, 256, 128))
    tn = _pick(n, (256, 128))
    row = lambda b, i: (b, 0, 0)
    tab = pl.BlockSpec((tm, LANE), lambda b, i: (i, 0))
    return pl.pallas_call(
        functools.partial(_proj_kernel, tn=tn, n_rope_blks=n_rope_blks),
        grid=(bsz, t // tm),
        in_specs=[pl.BlockSpec((1, tm, d), lambda b, i: (b, i, 0)),
                  pl.BlockSpec((1, 1, d), row), pl.BlockSpec((1, 1, d), row),
                  pl.BlockSpec((d, n), lambda b, i: (0, 0), pipeline_mode=pl.Buffered(1)),
                  tab, tab, tab],
        out_specs=pl.BlockSpec((1, tm, n), lambda b, i: (b, i, 0)),
        out_shape=jax.ShapeDtypeStruct((bsz, t, n), out_dtype),
        compiler_params=_params(("parallel", "parallel")),
        name=name,
    )(x, shift, scale, w, *tables)


def _cum_kernel(p_ref, bias_ref, out_ref, kcat_ref, carry_ref):
    @pl.when(pl.program_id(1) == 0)
    def _():
        carry_ref[...] = jnp.zeros_like(carry_ref)

    tc = p_ref.shape[1]
    lane = lax.broadcasted_iota(I32, (tc, LANE), 1)
    lf = _log_sigmoid(p_ref[0] + bias_ref[...])
    lf = jnp.where(lane >= LANE_FG, jnp.where(lane < LANE_FG + D_HEADS, lf, 0.0), 0.0)
    ri = lax.broadcasted_iota(I32, (tc, tc), 0)
    ci = lax.broadcasted_iota(I32, (tc, tc), 1)
    tri = jnp.where(ri >= ci, 1.0, 0.0).astype(BF16)
    hi, mid, lo = _split3(lf)
    cum = _dot(tri, hi) + _dot(tri, mid) + _dot(tri, lo) + carry_ref[...]
    carry_ref[...] = cum[tc - 1:tc, :]
    er = lax.broadcasted_iota(I32, (LANE, LANE), 0)
    ch, cm, cl = _split3(-LOG2E * cum)
    for h in range(D_HEADS):
        onehot = jnp.where(er == LANE_FG + h, 1.0, 0.0).astype(BF16)
        out_ref[0, h] = _dot(ch, onehot) + _dot(cm, onehot) + _dot(cl, onehot)
    x = p_ref[0]
    hi = x.astype(BF16).astype(F32)
    kcat_ref[0, :, 0:LANE] = jnp.where(lane < IDX_DIM, hi, pltpu.roll(hi, IDX_DIM, 1)).astype(BF16)
    kcat_ref[0, :, LANE:2 * LANE] = jnp.where(lane < IDX_DIM, x - hi, 0.0).astype(BF16)


def _cum(p32, bias_row, tc):
    bsz, t, _ = p32.shape
    return pl.pallas_call(
        _cum_kernel,
        grid=(bsz, t // tc),
        in_specs=[pl.BlockSpec((1, tc, LANE), lambda b, i: (b, i, BLK_IX)),
                  pl.BlockSpec((1, LANE), lambda b, i: (0, 0))],
        out_specs=[pl.BlockSpec((1, D_HEADS, tc, LANE), lambda b, i: (b, 0, i, 0)),
                   pl.BlockSpec((1, tc, 2 * LANE), lambda b, i: (b, i, 0))],
        out_shape=[jax.ShapeDtypeStruct((bsz, D_HEADS, t, LANE), F32),
                   jax.ShapeDtypeStruct((bsz, t, 2 * LANE), BF16)],
        scratch_shapes=[pltpu.VMEM((1, LANE), F32)],
        compiler_params=_params(("parallel", "arbitrary")),
        name="forget_cumsum",
    )(p32, bias_row)


def _head_halves(q):
    lane = lax.broadcasted_iota(I32, q.shape, 1)
    zero = jnp.zeros_like(q)
    return jnp.where(lane < HEAD_DIM, q, zero), jnp.where(lane >= HEAD_DIM, q, zero)


def _static_loop(n, body, carry):
    for i in range(n):
        carry = body(i, carry)
    return carry


def _fold8(x, op):
    out = x[0:8]
    for r in range(8, x.shape[0], 8):
        out = op(out, x[r:r + 8])
    return out


def _transposed_values(p16, blk, npair, tq):
    bsz, t, _ = p16.shape
    v = p16[:, :, blk * LANE:(blk + npair) * LANE].reshape(bsz, t // tq, tq, npair, LANE)
    return v.transpose(0, 3, 1, 4, 2)


def _token_major(yt):
    bsz, npair, _, t = yt.shape
    return yt.transpose(0, 3, 1, 2).reshape(bsz, t, npair * LANE)


def _lane_pick(a, b):
    lane = lax.broadcasted_iota(I32, a.shape, 1)
    return jnp.where(lane < HEAD_DIM, a, b)


def _dil_kernel(q_ref, k_ref, v_ref, o_ref, lse_ref, *, kw):
    nb = q_ref.shape[1] // SPAN

    def body(i, carry):
        qs = pl.multiple_of(i * SPAN, SPAN)
        ks = pl.multiple_of(jnp.maximum(i * SPAN + SPAN - kw, 0), SPAN)
        q = q_ref[0, pl.ds(qs, SPAN), :] * 0.125
        k = k_ref[0, pl.ds(ks, kw), :]
        v = v_ref[0, pl.ds(ks, kw), :]
        qpos = qs + lax.broadcasted_iota(I32, (SPAN, kw), 0)
        kpos = ks + lax.broadcasted_iota(I32, (SPAN, kw), 1)
        dist = qpos - kpos
        outs, lses = [], []
        for qh in _head_halves(q):
            s = _nt(qh, k)
            s = jnp.where(dist >= 0, jnp.where(dist <= SPAN, s, NEG_BIG), NEG_BIG)
            m = jnp.max(s, axis=-1, keepdims=True)
            p = jnp.exp(s - m)
            l = jnp.sum(p, axis=-1, keepdims=True)
            outs.append(_dot(p.astype(BF16), v) / l)
            lses.append(jnp.broadcast_to(m + jnp.log(l), (SPAN, LANE)))
        o_ref[0, pl.ds(qs, SPAN), :] = _lane_pick(outs[0], outs[1]).astype(o_ref.dtype)
        lse_ref[0, pl.ds(qs, SPAN), :] = _lane_pick(lses[0], lses[1])
        return carry

    lax.fori_loop(0, nb, body, 0, unroll=min(nb, 4))


def _dilated(q, k, v):
    ns, ln, _ = q.shape
    kw = min(2 * SPAN, ln)
    spec = pl.BlockSpec((1, ln, LANE), lambda s: (s, 0, 0))
    return pl.pallas_call(
        functools.partial(_dil_kernel, kw=kw),
        grid=(ns,),
        in_specs=[spec, spec, spec],
        out_specs=[spec, spec],
        out_shape=[jax.ShapeDtypeStruct((ns, ln, LANE), BF16), jax.ShapeDtypeStruct((ns, ln, LANE), F32)],
        compiler_params=_params(("parallel",)),
        name="dilated_window",
    )(q, k, v)


def _fox_kernel(q_ref, k_ref, vt_ref, bias_ref, o_ref, s_scr, p_scr, al_scr, acc_scr, m_scr, l_scr, *, tq):
    qi = pl.program_id(2)
    rep = tq // LANE
    nstrip = tq // STRIP
    qh = _head_halves((q_ref[0].astype(F32) * (0.125 * LOG2E)).astype(BF16))
    m_scr[...] = jnp.full(m_scr.shape, NEG_BIG, F32)
    l_scr[...] = jnp.zeros(l_scr.shape, F32)
    acc_scr[...] = jnp.zeros(acc_scr.shape, F32)
    kk = lax.broadcasted_iota(I32, (STRIP, tq), 0)
    qq = lax.broadcasted_iota(I32, (STRIP, tq), 1)

    def issue_scores(j, s_buf):
        kblk = k_ref[0, pl.ds(pl.multiple_of(j * tq, tq), tq), :]
        for h in range(2):
            s_buf[h] = _nt(kblk, qh[h])

    def softmax(j, s_buf, p_buf, al_buf, diag):
        ks = pl.multiple_of(j * tq, tq)
        for h in range(2):
            def pass1(i, mx):
                r0 = i * STRIP
                s = s_buf[h, pl.ds(r0, STRIP), :] + jnp.tile(
                    bias_ref[0, h, pl.ds(pl.multiple_of(ks + r0, STRIP), STRIP), :], (1, rep))
                if diag:
                    s = jnp.where(kk + r0 <= qq, s, NEG_BIG)
                s_buf[h, pl.ds(r0, STRIP), :] = s
                return jnp.maximum(mx, _fold8(s, jnp.maximum))

            mx = _static_loop(nstrip, pass1, jnp.full((8, tq), NEG_BIG, F32))
            m_old = m_scr[h]
            m_new = jnp.maximum(m_old, jnp.max(mx, axis=0, keepdims=True))
            al_buf[h] = jnp.exp2(m_old - m_new)
            m_scr[h] = m_new

            def pass2(i, ls):
                r0 = i * STRIP
                p = jnp.exp2(s_buf[h, pl.ds(r0, STRIP), :] - m_new)
                p_buf[h, pl.ds(r0, STRIP), :] = p.astype(BF16)
                return ls + _fold8(p, jnp.add)

            ls = _static_loop(nstrip, pass2, jnp.zeros((8, tq), F32))
            l_scr[h] = al_buf[h] * l_scr[h] + jnp.sum(ls, axis=0, keepdims=True)

    def apply_values(j, p_buf, al_buf):
        for h in range(2):
            pv = _dot(vt_ref[0, 0, j, h * HEAD_DIM:(h + 1) * HEAD_DIM, :], p_buf[h])
            acc_scr[h] = acc_scr[h] * al_buf[h] + pv

    s_a, s_b = s_scr.at[0], s_scr.at[1]
    p_a, p_b = p_scr.at[0], p_scr.at[1]
    al_a, al_b = al_scr.at[0], al_scr.at[1]
    p_b[...] = jnp.zeros(p_b.shape, BF16)
    al_b[...] = jnp.ones(al_b.shape, F32)
    issue_scores(0, s_a)

    def pair(i, carry):
        a = 2 * i
        issue_scores(a + 1, s_b)
        softmax(a, s_a, p_a, al_a, False)
        apply_values(jnp.maximum(a - 1, 0), p_b, al_b)
        issue_scores(a + 2, s_a)
        softmax(a + 1, s_b, p_b, al_b, False)
        apply_values(a, p_a, al_a)
        return carry

    lax.fori_loop(0, qi // 2, pair, 0)
    last = 2 * (qi // 2)

    @pl.when(qi % 2 == 1)
    def _():
        issue_scores(qi, s_b)
        softmax(last, s_a, p_a, al_a, False)
        apply_values(jnp.maximum(last - 1, 0), p_b, al_b)
        softmax(qi, s_b, p_b, al_b, True)
        apply_values(last, p_a, al_a)
        apply_values(qi, p_b, al_b)

    @pl.when(qi % 2 == 0)
    def _():
        softmax(qi, s_a, p_a, al_a, True)
        apply_values(jnp.maximum(qi - 1, 0), p_b, al_b)
        apply_values(qi, p_a, al_a)

    for h in range(2):
        o_ref[0, 0, h * HEAD_DIM:(h + 1) * HEAD_DIM, :] = (acc_scr[h] * (1.0 / l_scr[h])).astype(o_ref.dtype)


def _fox(p16, vt, bias, tq):
    bsz, t, _ = p16.shape
    npair = D_HEADS // 2
    nblk = t // tq
    return pl.pallas_call(
        functools.partial(_fox_kernel, tq=tq),
        grid=(bsz, npair, nblk),
        in_specs=[pl.BlockSpec((1, tq, LANE), lambda b, h, i: (b, i, BLK_DQ + h)),
                  pl.BlockSpec((1, t, LANE), lambda b, h, i: (b, 0, BLK_DK + h)),
                  pl.BlockSpec((1, 1, nblk, LANE, tq), lambda b, h, i: (b, h, 0, 0, 0)),
                  pl.BlockSpec((1, 2, t, LANE), lambda b, h, i: (b, h, 0, 0))],
        out_specs=pl.BlockSpec((1, 1, LANE, tq), lambda b, h, i: (b, h, 0, i)),
        out_shape=jax.ShapeDtypeStruct((bsz, npair, LANE, t), BF16),
        scratch_shapes=[pltpu.VMEM((2, 2, tq, tq), F32), pltpu.VMEM((2, 2, tq, tq), BF16),
                        pltpu.VMEM((2, 2, 1, tq), F32),
                        pltpu.VMEM((2, HEAD_DIM, tq), F32), pltpu.VMEM((2, 1, tq), F32),
                        pltpu.VMEM((2, 1, tq), F32)],
        compiler_params=_params(("parallel", "parallel", "arbitrary")),
        name="forgetting_attention",
    )(p16, p16, vt, bias)


def _sb_kernel(q_ref, k_ref, vt_ref, o_ref, z_scr, sfx_scr, hi_scr, lo_scr, a_scr, acc_scr, r_scr, *, tq):
    qi = pl.program_id(2)
    nstrip = tq // STRIP
    qh = _head_halves(q_ref[0] * 0.125)
    acc_scr[...] = jnp.zeros(acc_scr.shape, F32)
    r_scr[...] = jnp.zeros(r_scr.shape, F32)
    kk = lax.broadcasted_iota(I32, (STRIP, tq), 0)
    qq = lax.broadcasted_iota(I32, (STRIP, tq), 1)
    row = lax.broadcasted_iota(I32, (tq, tq), 0)
    col = lax.broadcasted_iota(I32, (tq, tq), 1)
    later = jnp.where(col > row, 1.0, 0.0).astype(BF16)

    def block(j, diag):
        ks = pl.multiple_of(j * tq, tq)
        kblk = k_ref[0, pl.ds(ks, tq), :]
        for h in range(2):
            z_scr[h] = _nt(kblk, qh[h])
        for h in range(2):
            def pass1(i, rsum):
                r0 = i * STRIP
                z = z_scr[h, pl.ds(r0, STRIP), :]
                lk = _log_sigmoid(-z)
                z_scr[h, pl.ds(r0, STRIP), :] = lk + z
                if diag:
                    lk = jnp.where(kk + r0 < qq, lk, 0.0)
                hi = lk.astype(BF16)
                hi_scr[h, pl.ds(r0, STRIP), :] = hi
                lo_scr[h, pl.ds(r0, STRIP), :] = (lk - hi.astype(F32)).astype(BF16)
                return rsum + _fold8(lk, jnp.add)

            rsum = _static_loop(nstrip, pass1, jnp.zeros((8, tq), F32))
            sfx_scr[h] = _dot(later, hi_scr[h]) + _dot(later, lo_scr[h])
            r_old = r_scr[h]

            def pass2(i, carry):
                r0 = i * STRIP
                a = jnp.exp(z_scr[h, pl.ds(r0, STRIP), :] + sfx_scr[h, pl.ds(r0, STRIP), :] + r_old)
                if diag:
                    a = jnp.where(kk + r0 < qq, a, 0.0)
                a_scr[h, pl.ds(r0, STRIP), :] = a.astype(BF16)
                return carry

            _static_loop(nstrip, pass2, 0)
            acc_scr[h] = acc_scr[h] + _dot(vt_ref[0, 0, j, h * HEAD_DIM:(h + 1) * HEAD_DIM, :], a_scr[h])
            r_scr[h] = r_old + jnp.sum(rsum, axis=0, keepdims=True)

    block(qi, True)

    def live():
        return jnp.max(jnp.maximum(r_scr[0], r_scr[1])) > EXP_DEAD

    def cond(c):
        return jnp.logical_and(c[0] >= 0, c[1])

    def body(c):
        block(c[0], False)
        return c[0] - 1, live()

    lax.while_loop(cond, body, (qi - 1, live()))
    for h in range(2):
        o_ref[0, 0, h * HEAD_DIM:(h + 1) * HEAD_DIM, :] = acc_scr[h].astype(o_ref.dtype)


def _stick_breaking(p16, vt, tq):
    bsz, t, _ = p16.shape
    npair = C_HEADS // 2
    nblk = t // tq
    return pl.pallas_call(
        functools.partial(_sb_kernel, tq=tq),
        grid=(bsz, npair, nblk),
        in_specs=[pl.BlockSpec((1, tq, LANE), lambda b, h, i: (b, i, BLK_CQ + h)),
                  pl.BlockSpec((1, t, LANE), lambda b, h, i: (b, 0, BLK_CK + h)),
                  pl.BlockSpec((1, 1, nblk, LANE, tq), lambda b, h, i: (b, h, 0, 0, 0))],
        out_specs=pl.BlockSpec((1, 1, LANE, tq), lambda b, h, i: (b, h, 0, i)),
        out_shape=jax.ShapeDtypeStruct((bsz, npair, LANE, t), BF16),
        scratch_shapes=[pltpu.VMEM((2, tq, tq), F32), pltpu.VMEM((2, tq, tq), F32),
                        pltpu.VMEM((2, tq, tq), BF16), pltpu.VMEM((2, tq, tq), BF16),
                        pltpu.VMEM((2, tq, tq), BF16), pltpu.VMEM((2, HEAD_DIM, tq), F32),
                        pltpu.VMEM((2, 1, tq), F32)],
        compiler_params=_params(("parallel", "parallel", "arbitrary")),
        name="stick_breaking",
    )(p16, p16, vt)


def _dsa_kernel(iq0_ref, iq1_ref, ixq_ref, kcat_ref, q0_ref, q1_ref, k0_ref, k1_ref, vt_ref, o_ref,
                key_scr, s_scr, p_scr, eq_scr, rank_scr, bias_scr, al_scr, acc_scr, m_scr, l_scr, *, tq, topk):
    qi = pl.program_id(1)
    nstrip = tq // STRIP
    lane = lax.broadcasted_iota(I32, (tq, LANE), 1)
    kk = lax.broadcasted_iota(I32, (STRIP, tq), 0)
    qq = lax.broadcasted_iota(I32, (STRIP, tq), 1)

    qcat = []
    for ref in (iq0_ref, iq1_ref):
        for half in range(2):
            x = ref[0] if half == 0 else pltpu.roll(ref[0], IDX_DIM, 1)
            hi = x.astype(BF16).astype(F32)
            first = jnp.where(lane < IDX_DIM, hi, pltpu.roll(x - hi, IDX_DIM, 1))
            second = jnp.where(lane < IDX_DIM, hi, 0.0)
            qcat.append(jnp.concatenate([first, second], axis=1).astype(BF16))
    sr = lax.broadcasted_iota(I32, (8, LANE), 0)
    sl = lax.broadcasted_iota(I32, (8, LANE), 1)
    onehot = jnp.where(sl == sr + LANE_IW, 1.0, 0.0).astype(BF16)
    wh, wm, wl = _split3(ixq_ref[0])
    wt = _nt(onehot, wh) + _nt(onehot, wm) + _nt(onehot, wl)

    def score_chunk(c, diag):
        ks = pl.multiple_of(c * tq, tq)
        kc = kcat_ref[0, pl.ds(ks, tq), :]
        for h in range(IDX_HEADS):
            s_scr[0, h] = _nt(kc, qcat[h])
        for i in range(nstrip):
            r0 = i * STRIP
            sc = jnp.zeros((STRIP, tq), F32)
            for h in range(IDX_HEADS):
                sc = sc + wt[h:h + 1, :] * jnp.maximum(s_scr[0, h, r0:r0 + STRIP, :], 0.0)
            sc = jnp.where(sc == 0.0, 0.0, sc)
            bits = lax.bitcast_convert_type(sc, I32)
            key = jnp.where(bits < 0, bits ^ jnp.int32(0x7FFFFFFF), bits)
            if diag:
                key = jnp.where(kk + r0 <= qq, key, INT_MIN)
            key_scr[c, r0:r0 + STRIP, :] = key

    def score_body(c, carry):
        score_chunk(c, False)
        return carry

    lax.fori_loop(0, qi, score_body, 0)
    score_chunk(qi, True)

    def count(pred):
        def body(c, acc):
            for i in range(nstrip):
                acc = acc + _fold8(jnp.where(pred(key_scr[c, i * STRIP:(i + 1) * STRIP, :]), 1.0, 0.0), jnp.add)
            return acc
        acc = lax.fori_loop(0, qi + 1, body, jnp.zeros((8, tq), F32))
        return jnp.sum(acc, axis=0, keepdims=True)

    kf = jnp.float32(topk)
    n_ge0 = count(lambda kb: kb >= 0)
    n_gt0 = count(lambda kb: kb > 0)
    th0 = jnp.where(n_ge0 >= kf, 0, INT_MIN).astype(I32)
    settled0 = jnp.where(n_ge0 >= kf, jnp.where(n_gt0 < kf, 1.0, 0.0), 0.0)

    def unsettled(settled):
        return jnp.min(settled) == 0.0

    def search_cond(c):
        return jnp.logical_and(c[0] >= 0, c[3])

    def search_body(c):
        bit, th, settled, _ = c
        cand = th | lax.shift_left(jnp.int32(1), bit)
        n = count(lambda kb: kb >= cand)
        th = jnp.where(n >= kf, jnp.where(settled == 0.0, cand, th), th)
        settled = jnp.maximum(settled, jnp.where(n == kf, 1.0, 0.0))
        return bit - 1, th, settled, unsettled(settled)

    _, th, _, _ = lax.while_loop(search_cond, search_body, (jnp.int32(30), th0, settled0, unsettled(settled0)))
    need = kf - count(lambda kb: kb > th)

    qh = sum((_head_halves((r[0].astype(F32) * (0.125 * LOG2E)).astype(BF16)) for r in (q0_ref, q1_ref)), ())
    k_refs = (k0_ref, k0_ref, k1_ref, k1_ref)
    m_scr[...] = jnp.full(m_scr.shape, NEG_BIG, F32)
    l_scr[...] = jnp.zeros(l_scr.shape, F32)
    acc_scr[...] = jnp.zeros(acc_scr.shape, F32)
    row = lax.broadcasted_iota(I32, (tq, tq), 0)
    col = lax.broadcasted_iota(I32, (tq, tq), 1)
    upto = jnp.where(col <= row, 1.0, 0.0).astype(BF16)

    def issue(c, s_buf, eq_buf, rank_buf):
        ks = pl.multiple_of(c * tq, tq)
        for i in range(nstrip):
            r0 = i * STRIP
            eq_buf[r0:r0 + STRIP, :] = jnp.where(key_scr[c, r0:r0 + STRIP, :] == th, 1.0, 0.0).astype(BF16)
        rank_buf[...] = _dot(upto, eq_buf[...])
        for h in range(B_HEADS):
            s_buf[h] = _nt(k_refs[h][0, pl.ds(ks, tq), :], qh[h])

    def softmax(c, s_buf, rank_buf, p_buf, al_buf, diag, seen):
        for i in range(nstrip):
            r0 = i * STRIP
            kb = key_scr[c, r0:r0 + STRIP, :]
            tie = jnp.where(rank_buf[r0:r0 + STRIP, :] + seen <= need, 0.0, NEG_BIG)
            bias = jnp.where(kb > th, 0.0, jnp.where(kb == th, tie, NEG_BIG))
            if diag:
                bias = jnp.where(kk + r0 <= qq, bias, NEG_BIG)
            bias_scr[r0:r0 + STRIP, :] = bias
        for h in range(B_HEADS):
            def pass1(i, mx):
                r0 = i * STRIP
                s = s_buf[h, r0:r0 + STRIP, :] + bias_scr[r0:r0 + STRIP, :]
                s_buf[h, r0:r0 + STRIP, :] = s
                return jnp.maximum(mx, _fold8(s, jnp.maximum))

            mx = _static_loop(nstrip, pass1, jnp.full((8, tq), NEG_BIG, F32))
            m_old = m_scr[h]
            m_new = jnp.maximum(m_old, jnp.max(mx, axis=0, keepdims=True))
            al_buf[h] = jnp.exp2(m_old - m_new)
            m_scr[h] = m_new

            def pass2(i, ls):
                r0 = i * STRIP
                p = jnp.exp2(s_buf[h, r0:r0 + STRIP, :] - m_new)
                p_buf[h, r0:r0 + STRIP, :] = p.astype(BF16)
                return ls + _fold8(p, jnp.add)

            ls = _static_loop(nstrip, pass2, jnp.zeros((8, tq), F32))
            l_scr[h] = al_buf[h] * l_scr[h] + jnp.sum(ls, axis=0, keepdims=True)
        return seen + rank_buf[tq - 1:tq, :]

    def apply_values(c, p_buf, al_buf):
        for h in range(B_HEADS):
            lo = (h % 2) * HEAD_DIM
            acc_scr[h] = acc_scr[h] * al_buf[h] + _dot(vt_ref[0, h // 2, c, lo:lo + HEAD_DIM, :], p_buf[h])

    s_a, s_b = s_scr.at[0], s_scr.at[1]
    p_a, p_b = p_scr.at[0], p_scr.at[1]
    eq_a, eq_b = eq_scr.at[0], eq_scr.at[1]
    rank_a, rank_b = rank_scr.at[0], rank_scr.at[1]
    al_a, al_b = al_scr.at[0], al_scr.at[1]
    p_b[...] = jnp.zeros(p_b.shape, BF16)
    al_b[...] = jnp.ones(al_b.shape, F32)
    issue(0, s_a, eq_a, rank_a)

    def pair(i, seen):
        a = 2 * i
        issue(a + 1, s_b, eq_b, rank_b)
        seen = softmax(a, s_a, rank_a, p_a, al_a, False, seen)
        apply_values(jnp.maximum(a - 1, 0), p_b, al_b)
        issue(a + 2, s_a, eq_a, rank_a)
        seen = softmax(a + 1, s_b, rank_b, p_b, al_b, False, seen)
        apply_values(a, p_a, al_a)
        return seen

    seen = lax.fori_loop(0, qi // 2, pair, jnp.zeros((1, tq), F32))
    last = 2 * (qi // 2)

    @pl.when(qi % 2 == 1)
    def _():
        issue(qi, s_b, eq_b, rank_b)
        seen_d = softmax(last, s_a, rank_a, p_a, al_a, False, seen)
        apply_values(jnp.maximum(last - 1, 0), p_b, al_b)
        softmax(qi, s_b, rank_b, p_b, al_b, True, seen_d)
        apply_values(last, p_a, al_a)
        apply_values(qi, p_b, al_b)

    @pl.when(qi % 2 == 0)
    def _():
        softmax(qi, s_a, rank_a, p_a, al_a, True, seen)
        apply_values(jnp.maximum(qi - 1, 0), p_b, al_b)
        apply_values(qi, p_a, al_a)

    for h in range(B_HEADS):
        lo = (h % 2) * HEAD_DIM
        o_ref[0, h // 2, lo:lo + HEAD_DIM, :] = (acc_scr[h] * (1.0 / l_scr[h])).astype(o_ref.dtype)


def _dsa(p16, p32, kcat, vt, tq):
    bsz, t, _ = p16.shape
    topk = min(DSA_TOPK, t // 4)
    npair = B_HEADS // 2
    nblk = t // tq
    qspec = lambda arr_blk: pl.BlockSpec((1, tq, LANE), lambda b, i: (b, i, arr_blk))
    once = pl.Buffered(1)
    kspec = lambda arr_blk: pl.BlockSpec((1, t, LANE), lambda b, i: (b, 0, arr_blk), pipeline_mode=once)
    sq = lambda n, dt: pltpu.VMEM((n, tq, tq), dt) if n else pltpu.VMEM((tq, tq), dt)
    return pl.pallas_call(
        functools.partial(_dsa_kernel, tq=tq, topk=topk),
        grid=(bsz, nblk),
        in_specs=[qspec(BLK_IQ), qspec(BLK_IQ + 1), qspec(BLK_IX),
                  pl.BlockSpec((1, t, 2 * LANE), lambda b, i: (b, 0, 0), pipeline_mode=once),
                  qspec(BLK_BQ), qspec(BLK_BQ + 1), kspec(BLK_BK), kspec(BLK_BK + 1),
                  pl.BlockSpec((1, npair, nblk, LANE, tq), lambda b, i: (b, 0, 0, 0, 0), pipeline_mode=once)],
        out_specs=pl.BlockSpec((1, npair, LANE, tq), lambda b, i: (b, 0, 0, i)),
        out_shape=jax.ShapeDtypeStruct((bsz, npair, LANE, t), BF16),
        scratch_shapes=[sq(nblk, I32), pltpu.VMEM((2, B_HEADS, tq, tq), F32), pltpu.VMEM((2, B_HEADS, tq, tq), BF16),
                        sq(2, BF16), sq(2, F32), sq(0, F32), pltpu.VMEM((2, B_HEADS, 1, tq), F32),
                        pltpu.VMEM((B_HEADS, HEAD_DIM, tq), F32), pltpu.VMEM((B_HEADS, 1, tq), F32),
                        pltpu.VMEM((B_HEADS, 1, tq), F32)],
        compiler_params=_params(("parallel", "arbitrary")),
        name="dsa_topk_attention",
    )(p32, p32, p32, kcat, p16, p16, p16, p16, vt)


def _merge_kernel(x_ref, sh_ref, sc_ref, gt_ref, oa0_ref, oa1_ref, oa2_ref, la0_ref, la1_ref, la2_ref,
                  yb_ref, yc_ref, yd_ref, wg_ref, bg_ref, wb_ref, wo_ref, lg_ref, lb_ref, o_ref, *, alpha):
    x = x_ref[0]
    d = x.shape[1]
    u = (x * (1.0 + sc_ref[0]) + sh_ref[0]).astype(BF16)
    lses = [r[0] for r in (la0_ref, la1_ref, la2_ref)]
    top = jnp.maximum(jnp.maximum(lses[0], lses[1]), lses[2])
    es = [jnp.exp(l - top) for l in lses]
    den = es[0] + es[1] + es[2]
    ya = sum((e / den) * r[0].astype(F32) for e, r in zip(es, (oa0_ref, oa1_ref, oa2_ref)))
    branches = (ya.astype(BF16), yb_ref[0], yc_ref[0], yd_ref[0])
    merged = jnp.zeros(x.shape, F32)
    off = 0
    for i, br in enumerate(branches):
        gate = jax.nn.sigmoid(_dot(u, wg_ref[:, i * d:(i + 1) * d]) + bg_ref[:, i * d:(i + 1) * d])
        merged = merged + gate * _dot(br, wb_ref[off:off + BRANCH_WIDTHS[i], :])
        off += BRANCH_WIDTHS[i]
    h = _dot(merged.astype(BF16), wo_ref[...])
    o_ref[0] = _deepnorm_ln(x, h, gt_ref[0], lg_ref[...], lb_ref[...], alpha)


def _merge(x, shift, scale, gate, oas, las, yb, yc, yd, wg, bg, wb, wo, ln_g, ln_b, alpha):
    bsz, t, d = x.shape
    tm = _pick(t, (256, 128))
    row = lambda b, i: (b, 0, 0)
    tok = lambda w: pl.BlockSpec((1, tm, w), lambda b, i: (b, i, 0))
    wspec = lambda shape: pl.BlockSpec(shape, lambda b, i: (0, 0), pipeline_mode=pl.Buffered(1))
    return pl.pallas_call(
        functools.partial(_merge_kernel, alpha=alpha),
        grid=(bsz, t // tm),
        in_specs=[tok(d), pl.BlockSpec((1, 1, d), row), pl.BlockSpec((1, 1, d), row), pl.BlockSpec((1, 1, d), row),
                  tok(LANE), tok(LANE), tok(LANE), tok(LANE), tok(LANE), tok(LANE),
                  tok(2 * LANE), tok(2 * LANE), tok(2 * LANE),
                  wspec(wg.shape), wspec(bg.shape), wspec(wb.shape), wspec(wo.shape),
                  pl.BlockSpec((1, d), lambda b, i: (0, 0)), pl.BlockSpec((1, d), lambda b, i: (0, 0))],
        out_specs=tok(d),
        out_shape=jax.ShapeDtypeStruct((bsz, t, d), F32),
        compiler_params=_params(("parallel", "parallel")),
        name="gated_merge",
    )(x, shift, scale, gate, *oas, *las, yb, yc, yd, wg, bg, wb, wo, ln_g, ln_b)


def _rope_tables(t):
    half = ROT_DIM // 2
    inv_freq = ROPE_THETA ** (-(jnp.arange(half, dtype=F32) * (2.0 / ROT_DIM)))
    ang = jnp.arange(t, dtype=F32)[:, None] * inv_freq[None, :]
    cos, sin = jnp.cos(ang), jnp.sin(ang)
    ones = jnp.ones((t, HEAD_DIM - ROT_DIM), F32)
    zeros = jnp.zeros((t, HEAD_DIM - half), F32)
    c64 = jnp.concatenate([cos, cos, ones], axis=1)
    s1 = jnp.concatenate([-sin, zeros], axis=1)
    s2 = jnp.concatenate([jnp.zeros((t, half), F32), sin, zeros[:, half:]], axis=1)
    return tuple(jnp.concatenate([a, a], axis=1) for a in (c64, s1, s2))


def _mixer_weights(w_in):
    d = w_in.shape[0]

    def qkv(off, heads):
        w = heads * HEAD_DIM
        return w_in[:, off:off + w], w_in[:, off + w:off + 2 * w], w_in[:, off + 2 * w:off + 3 * w]

    aq, ak, av = qkv(OFF_A, A_HEADS)
    bq, bk, bv = qkv(OFF_B, B_HEADS)
    cq, ck, cv = qkv(OFF_C, C_HEADS)
    dq, dk, dv = qkv(OFF_D, D_HEADS)
    w16 = jnp.concatenate([aq, ak, bq, bk, av, bv, cq, ck, cv, dq, dk, dv, jnp.zeros((d, LANE), F32)], axis=1)
    ix = jnp.concatenate([w_in[:, OFF_IK:OFF_IK + IDX_DIM], jnp.zeros((d, LANE_IW - IDX_DIM), F32),
                          w_in[:, OFF_IW:OFF_IW + IDX_HEADS], w_in[:, OFF_FG:OFF_FG + D_HEADS],
                          jnp.zeros((d, LANE - LANE_FG - D_HEADS), F32)], axis=1)
    w32 = jnp.concatenate([w_in[:, OFF_IQ:OFF_IQ + IDX_HEADS * IDX_DIM], ix, jnp.zeros((d, LANE), F32)], axis=1)
    return w16.astype(BF16), w32.astype(BF16), w_in[:, OFF_GATE:].astype(BF16)


def _to_residues(a, dil):
    bsz, t, w = a.shape
    if dil == 1:
        return a
    return a.reshape(bsz, t // dil, dil, w).transpose(0, 2, 1, 3).reshape(bsz * dil, t // dil, w)


def _from_residues(a, dil, bsz):
    if dil == 1:
        return a
    _, ln, w = a.shape
    return a.reshape(bsz, dil, ln, w).transpose(0, 2, 1, 3).reshape(bsz, ln * dil, w)


def _mixer(x, shift, scale, gate, w_in, b_gate, b_forget, w_branch, w_out, ln_g, ln_b, tables, alpha):
    bsz, t, d = x.shape
    w16, w32, wg = _mixer_weights(w_in)
    p16 = _proj(x, shift, scale, w16, tables, N_ROPE_BLKS_16, BF16, "mixer_proj_bf16")
    p32 = _proj(x, shift, scale, w32, tables, N_BLKS_32, F32, "mixer_proj_f32")

    oas, las = [], []
    for g, (_, dil) in enumerate(DILATED_GROUPS):
        streams = [_to_residues(p16[:, :, (blk + g) * LANE:(blk + g + 1) * LANE], dil)
                   for blk in (BLK_AQ, BLK_AK, BLK_AV)]
        o, lse = _dilated(*streams)
        oas.append(_from_residues(o, dil, bsz))
        las.append(_from_residues(lse, dil, bsz))

    tq = _pick(t, (256, 128))
    bias_row = jnp.zeros((1, LANE), F32).at[0, LANE_FG:LANE_FG + D_HEADS].set(b_forget)
    fbias, kcat = _cum(p32, bias_row, tq)
    tqf = _pick(t, (512, 256, 128))
    yd = _token_major(_fox(p16, _transposed_values(p16, BLK_DV, D_HEADS // 2, tqf), fbias, tqf))
    yc = _token_major(_stick_breaking(p16, _transposed_values(p16, BLK_CV, C_HEADS // 2, tq), tq))
    yb = _token_major(_dsa(p16, p32, kcat, _transposed_values(p16, BLK_BV, B_HEADS // 2, tqf), tqf))
    return _merge(x, shift, scale, gate, oas, las, yb, yc, yd, wg, b_gate.reshape(1, -1),
                  w_branch.astype(BF16), w_out.astype(BF16), ln_g, ln_b, alpha)


def kernel(x, c, ada_w, ada_b, ln_g, ln_b, ffn_w_in, ffn_w_out, mix_w_in, mix_b_gate, mix_b_forget,
           mix_w_branch, mix_w_out):
    bsz, t, d = x.shape
    depth = ada_w.shape[0]
    f = ffn_w_out.shape[2]
    alpha = float((2 * depth) ** 0.25)
    assert t % (DILATED_GROUPS[-1][0]) == 0 and d % LANE == 0 and bsz <= 8

    c8 = jnp.zeros((8, d), F32).at[:bsz].set(c)
    mod = _ada(c8, ada_w, ada_b)[:, :bsz].reshape(depth, bsz, 3, 3, 1, d)
    tables = _rope_tables(t)

    for l in range(depth):
        m = lambda sub, kind: mod[l, :, sub, kind]
        lng = lambda sub: ln_g[l, sub].reshape(1, d)
        lnb = lambda sub: ln_b[l, sub].reshape(1, d)

        def ffn(x, sub, which):
            w_in = ffn_w_in[l, which].astype(BF16)
            return _ffn(x, m(sub, 0), m(sub, 1), m(sub, 2), w_in[:, :f], w_in[:, f:],
                        ffn_w_out[l, which].astype(BF16), lng(sub), lnb(sub), alpha)

        x = ffn(x, 0, 0)
        x = _mixer(x, m(1, 0), m(1, 1), m(1, 2), mix_w_in[l], mix_b_gate[l], mix_b_forget[l],
                   mix_w_branch[l], mix_w_out[l], lng(1), lnb(1), tables, alpha)
        x = ffn(x, 2, 1)
    return x
```

```python
import functools

import jax
import jax.numpy as jnp
from jax import lax
from jax.experimental import pallas as pl
from jax.experimental.pallas import tpu as pltpu

F32 = jnp.float32
BF16 = jnp.bfloat16
I32 = jnp.int32

LANE = 128
HEAD_DIM = 64
ROT_DIM = HEAD_DIM // 4
ROPE_THETA = 500000.0
DILATED_GROUPS = ((128, 1), (512, 4), (2048, 16))
SPAN = 128
A_HEADS, B_HEADS, C_HEADS, D_HEADS = 6, 4, 4, 4
IDX_HEADS, IDX_DIM = 4, 64
DSA_TOPK = 256
N_BRANCH = 4
LN_EPS = 1e-5
BRANCH_WIDTHS = (128, 256, 256, 256)

OFF_A = 0
OFF_B = OFF_A + 3 * A_HEADS * HEAD_DIM
OFF_IQ = OFF_B + 3 * B_HEADS * HEAD_DIM
OFF_IK = OFF_IQ + IDX_HEADS * IDX_DIM
OFF_IW = OFF_IK + IDX_DIM
OFF_C = OFF_IW + IDX_HEADS
OFF_D = OFF_C + 3 * C_HEADS * HEAD_DIM
OFF_FG = OFF_D + 3 * D_HEADS * HEAD_DIM
OFF_GATE = OFF_FG + D_HEADS

BLK_AQ, BLK_AK, BLK_BQ, BLK_BK = 0, 3, 6, 8
N_ROPE_BLKS_16 = 10
BLK_AV, BLK_BV = 10, 13
BLK_CQ, BLK_CK, BLK_CV = 15, 17, 19
BLK_DQ, BLK_DK, BLK_DV = 21, 23, 25
N_BLKS_16 = 28
BLK_IQ, BLK_IX = 0, 2
N_BLKS_32 = 4
LANE_IW = 80
LANE_FG = 84

INT_MIN = -2147483648
NEG_BIG = -1e30
LOG2E = 1.4426950408889634
EXP_DEAD = -104.0
VROWS = HEAD_DIM + 16
STRIP = 32
VMEM_LIMIT = 56 * 1024 * 1024


def _nt(a, b):
    return lax.dot_general(a, b, (((1,), (1,)), ((), ())), preferred_element_type=F32)


def _dot(a, b):
    return jnp.dot(a, b, preferred_element_type=F32)


def _split3(x):
    hi = x.astype(BF16)
    r1 = x - hi.astype(F32)
    mid = r1.astype(BF16)
    lo = (r1 - mid.astype(F32)).astype(BF16)
    return hi, mid, lo


def _split2(x):
    hi = x.astype(BF16)
    return hi, (x - hi.astype(F32)).astype(BF16)


def _log_sigmoid(z):
    return -(jnp.maximum(-z, 0.0) + jnp.log(1.0 + jnp.exp(-jnp.abs(z))))


def _params(sem):
    return pltpu.CompilerParams(dimension_semantics=sem, vmem_limit_bytes=VMEM_LIMIT)


def _const_spec(shape):
    n = len(shape)
    return pl.BlockSpec(shape, lambda *_: (0,) * n)


def _pick(n, prefs):
    for p in prefs:
        if n % p == 0:
            return p
    return n


def _ada_kernel(c_ref, w_ref, b_ref, o_ref):
    c = c_ref[...]
    cond = c * jax.nn.sigmoid(c)
    ch, cm, _ = _split3(cond)
    wh, wm, _ = _split3(w_ref[0])
    o_ref[0] = _dot(ch, wh) + _dot(ch, wm) + _dot(cm, wh) + b_ref[0]


def _ada(c8, ada_w, ada_b):
    depth, d, n = ada_w.shape
    tn = _pick(n, (1152, 1024, 512, 256, 128))
    return pl.pallas_call(
        _ada_kernel,
        grid=(depth, n // tn),
        in_specs=[pl.BlockSpec((8, d), lambda l, j: (0, 0)),
                  pl.BlockSpec((1, d, tn), lambda l, j: (l, 0, j)),
                  pl.BlockSpec((1, 1, tn), lambda l, j: (l, 0, j))],
        out_specs=pl.BlockSpec((1, 8, tn), lambda l, j: (l, 0, j)),
        out_shape=jax.ShapeDtypeStruct((depth, 8, n), F32),
        compiler_params=_params(("parallel", "parallel")),
        name="ada_mod",
    )(c8, ada_w, ada_b.reshape(depth, 1, n))


def _deepnorm_ln(x, h, gate, g, b, alpha):
    y = alpha * x + (1.0 + gate) * h
    mu = jnp.mean(y, axis=-1, keepdims=True)
    yc = y - mu
    var = jnp.mean(yc * yc, axis=-1, keepdims=True)
    return yc * lax.rsqrt(var + LN_EPS) * g + b


def _ffn_kernel(x_ref, sh_ref, sc_ref, gt_ref, wg_ref, wu_ref, wo_ref, lg_ref, lb_ref, o_ref, *, tf, alpha):
    x = x_ref[0]
    u = (x * (1.0 + sc_ref[0]) + sh_ref[0]).astype(BF16)
    acc = jnp.zeros(x.shape, F32)
    for j in range(wg_ref.shape[1] // tf):
        g = _dot(u, wg_ref[:, j * tf:(j + 1) * tf])
        up = _dot(u, wu_ref[:, j * tf:(j + 1) * tf])
        h = (g * jax.nn.sigmoid(g) * up).astype(BF16)
        acc = acc + _dot(h, wo_ref[j * tf:(j + 1) * tf, :])
    o_ref[0] = _deepnorm_ln(x, 0.5 * acc, gt_ref[0], lg_ref[...], lb_ref[...], alpha)


def _ffn(x, shift, scale, gate, wg, wu, wo, ln_g, ln_b, alpha):
    bsz, t, d = x.shape
    f = wg.shape[1]
    tm = _pick(t, (512, 256, 128))
    tf = _pick(f, (256, 128))
    row = lambda b, i: (b, 0, 0)
    wspec = lambda shape: pl.BlockSpec(shape, lambda b, i: (0, 0), pipeline_mode=pl.Buffered(1))
    return pl.pallas_call(
        functools.partial(_ffn_kernel, tf=tf, alpha=alpha),
        grid=(bsz, t // tm),
        in_specs=[pl.BlockSpec((1, tm, d), lambda b, i: (b, i, 0)),
                  pl.BlockSpec((1, 1, d), row), pl.BlockSpec((1, 1, d), row), pl.BlockSpec((1, 1, d), row),
                  wspec((d, f)), wspec((d, f)), wspec((f, d)),
                  pl.BlockSpec((1, d), lambda b, i: (0, 0)), pl.BlockSpec((1, d), lambda b, i: (0, 0))],
        out_specs=pl.BlockSpec((1, tm, d), lambda b, i: (b, i, 0)),
        out_shape=jax.ShapeDtypeStruct((bsz, t, d), F32),
        compiler_params=_params(("parallel", "parallel")),
        name="ffn",
    )(x, shift, scale, gate, wg, wu, wo, ln_g, ln_b)


def _proj_kernel(x_ref, sh_ref, sc_ref, w_ref, cos_ref, s1_ref, s2_ref, o_ref, *, tn, n_rope_blks):
    u = (x_ref[0] * (1.0 + sc_ref[0]) + sh_ref[0]).astype(BF16)
    n = w_ref.shape[1]
    per = tn // LANE
    for j in range(n // tn):
        y = _dot(u, w_ref[:, j * tn:(j + 1) * tn])
        for i in range(per):
            blk = j * per + i
            yi = y[:, i * LANE:(i + 1) * LANE]
            if blk < n_rope_blks:
                yi = (yi * cos_ref[...] + pltpu.roll(yi, LANE - ROT_DIM // 2, 1) * s1_ref[...]
                      + pltpu.roll(yi, ROT_DIM // 2, 1) * s2_ref[...])
            o_ref[0, :, blk * LANE:(blk + 1) * LANE] = yi.astype(o_ref.dtype)


def _proj(x, shift, scale, w, tables, n_rope_blks, out_dtype, name):
    bsz, t, d = x.shape
    n = w.shape[1]
    tm = _pick(t, (512, 256, 128))
    tn = _pick(n, (256, 128))
    row = lambda b, i: (b, 0, 0)
    tab = pl.BlockSpec((tm, LANE), lambda b, i: (i, 0))
    return pl.pallas_call(
        functools.partial(_proj_kernel, tn=tn, n_rope_blks=n_rope_blks),
        grid=(bsz, t // tm),
        in_specs=[pl.BlockSpec((1, tm, d), lambda b, i: (b, i, 0)),
                  pl.BlockSpec((1, 1, d), row), pl.BlockSpec((1, 1, d), row),
                  pl.BlockSpec((d, n), lambda b, i: (0, 0), pipeline_mode=pl.Buffered(1)),
                  tab, tab, tab],
        out_specs=pl.BlockSpec((1, tm, n), lambda b, i: (b, i, 0)),
        out_shape=jax.ShapeDtypeStruct((bsz, t, n), out_dtype),
        compiler_params=_params(("parallel", "parallel")),
        name=name,
    )(x, shift, scale, w, *tables)


def _cum_kernel(p_ref, bias_ref, out_ref, kcat_ref, carry_ref):
    @pl.when(pl.program_id(1) == 0)
    def _():
        carry_ref[...] = jnp.zeros_like(carry_ref)

    tc = p_ref.shape[1]
    lane = lax.broadcasted_iota(I32, (tc, LANE), 1)
    lf = _log_sigmoid(p_ref[0] + bias_ref[...])
    lf = jnp.where(lane >= LANE_FG, jnp.where(lane < LANE_FG + D_HEADS, lf, 0.0), 0.0)
    ri = lax.broadcasted_iota(I32, (tc, tc), 0)
    ci = lax.broadcasted_iota(I32, (tc, tc), 1)
    tri = jnp.where(ri >= ci, 1.0, 0.0).astype(BF16)
    hi, mid, lo = _split3(lf)
    cum = _dot(tri, hi) + _dot(tri, mid) + _dot(tri, lo) + carry_ref[...]
    carry_ref[...] = cum[tc - 1:tc, :]
    er = lax.broadcasted_iota(I32, (LANE, LANE), 0)
    ch, cm, cl = _split3(-LOG2E * cum)
    for h in range(D_HEADS):
        onehot = jnp.where(er == LANE_FG + h, 1.0, 0.0).astype(BF16)
        out_ref[0, h] = _dot(ch, onehot) + _dot(cm, onehot) + _dot(cl, onehot)
    x = p_ref[0]
    hi = x.astype(BF16).astype(F32)
    kcat_ref[0, :, 0:LANE] = jnp.where(lane < IDX_DIM, hi, pltpu.roll(hi, IDX_DIM, 1)).astype(BF16)
    kcat_ref[0, :, LANE:2 * LANE] = jnp.where(lane < IDX_DIM, x - hi, 0.0).astype(BF16)


def _cum(p32, bias_row, tc):
    bsz, t, _ = p32.shape
    return pl.pallas_call(
        _cum_kernel,
        grid=(bsz, t // tc),
        in_specs=[pl.BlockSpec((1, tc, LANE), lambda b, i: (b, i, BLK_IX)),
                  pl.BlockSpec((1, LANE), lambda b, i: (0, 0))],
        out_specs=[pl.BlockSpec((1, D_HEADS, tc, LANE), lambda b, i: (b, 0, i, 0)),
                   pl.BlockSpec((1, tc, 2 * LANE), lambda b, i: (b, i, 0))],
        out_shape=[jax.ShapeDtypeStruct((bsz, D_HEADS, t, LANE), F32),
                   jax.ShapeDtypeStruct((bsz, t, 2 * LANE), BF16)],
        scratch_shapes=[pltpu.VMEM((1, LANE), F32)],
        compiler_params=_params(("parallel", "arbitrary")),
        name="forget_cumsum",
    )(p32, bias_row)


def _head_halves(q):
    lane = lax.broadcasted_iota(I32, q.shape, 1)
    zero = jnp.zeros_like(q)
    return jnp.where(lane < HEAD_DIM, q, zero), jnp.where(lane >= HEAD_DIM, q, zero)


def _static_loop(n, body, carry):
    for i in range(n):
        carry = body(i, carry)
    return carry


def _fold8(x, op):
    out = x[0:8]
    for r in range(8, x.shape[0], 8):
        out = op(out, x[r:r + 8])
    return out


def _transposed_values(p16, blk, npair, tq, with_ones=False):
    bsz, t, _ = p16.shape
    v = p16[:, :, blk * LANE:(blk + npair) * LANE]
    rows = LANE
    if with_ones:
        v = v.reshape(bsz, t, 2 * npair, HEAD_DIM)
        v = jnp.concatenate([v, jnp.ones((bsz, t, 2 * npair, VROWS - HEAD_DIM), v.dtype)], axis=-1)
        rows = 2 * VROWS
    return v.reshape(bsz, t // tq, tq, npair, rows).transpose(0, 3, 1, 4, 2)


def _token_major(yt):
    bsz, npair, _, t = yt.shape
    return yt.transpose(0, 3, 1, 2).reshape(bsz, t, npair * LANE)


def _lane_pick(a, b):
    lane = lax.broadcasted_iota(I32, a.shape, 1)
    return jnp.where(lane < HEAD_DIM, a, b)


def _dil_kernel(q_ref, k_ref, v_ref, o_ref, lse_ref, *, kw):
    nb = q_ref.shape[1] // SPAN

    def body(i, carry):
        qs = pl.multiple_of(i * SPAN, SPAN)
        ks = pl.multiple_of(jnp.maximum(i * SPAN + SPAN - kw, 0), SPAN)
        q = q_ref[0, pl.ds(qs, SPAN), :] * 0.125
        k = k_ref[0, pl.ds(ks, kw), :]
        v = v_ref[0, pl.ds(ks, kw), :]
        qpos = qs + lax.broadcasted_iota(I32, (SPAN, kw), 0)
        kpos = ks + lax.broadcasted_iota(I32, (SPAN, kw), 1)
        dist = qpos - kpos
        outs, lses = [], []
        for qh in _head_halves(q):
            s = _nt(qh, k)
            s = jnp.where(dist >= 0, jnp.where(dist <= SPAN, s, NEG_BIG), NEG_BIG)
            m = jnp.max(s, axis=-1, keepdims=True)
            p = jnp.exp(s - m)
            l = jnp.sum(p, axis=-1, keepdims=True)
            outs.append(_dot(p.astype(BF16), v) / l)
            lses.append(jnp.broadcast_to(m + jnp.log(l), (SPAN, LANE)))
        o_ref[0, pl.ds(qs, SPAN), :] = _lane_pick(outs[0], outs[1]).astype(o_ref.dtype)
        lse_ref[0, pl.ds(qs, SPAN), :] = _lane_pick(lses[0], lses[1])
        return carry

    lax.fori_loop(0, nb, body, 0, unroll=min(nb, 4))


def _dilated(q, k, v, cols=(0, 0, 0)):
    ns, ln, _ = q.shape
    kw = min(2 * SPAN, ln)
    spec = pl.BlockSpec((1, ln, LANE), lambda s: (s, 0, 0))
    in_spec = lambda col: pl.BlockSpec((1, ln, LANE), lambda s: (s, 0, col))
    return pl.pallas_call(
        functools.partial(_dil_kernel, kw=kw),
        grid=(ns,),
        in_specs=[in_spec(c) for c in cols],
        out_specs=[spec, spec],
        out_shape=[jax.ShapeDtypeStruct((ns, ln, LANE), BF16), jax.ShapeDtypeStruct((ns, ln, LANE), F32)],
        compiler_params=_params(("parallel",)),
        name="dilated_window",
    )(q, k, v)


def _fox_kernel(q_ref, k_ref, vt_ref, bias_ref, o_ref, s_scr, p_scr, al_scr, acc_scr, m_scr, *, tq):
    qi = pl.program_id(2)
    rep = tq // LANE
    nstrip = tq // STRIP
    qh = _head_halves((q_ref[0].astype(F32) * (0.125 * LOG2E)).astype(BF16))
    m_scr[...] = jnp.full(m_scr.shape, NEG_BIG, F32)
    acc_scr[...] = jnp.zeros(acc_scr.shape, F32)
    kk = lax.broadcasted_iota(I32, (STRIP, tq), 0)
    qq = lax.broadcasted_iota(I32, (STRIP, tq), 1)

    def issue_scores(j, s_buf):
        kblk = k_ref[0, pl.ds(pl.multiple_of(j * tq, tq), tq), :]
        for h in range(2):
            s_buf[h] = _nt(kblk, qh[h])

    def softmax(j, s_buf, p_buf, al_buf, diag):
        ks = pl.multiple_of(j * tq, tq)
        for h in range(2):
            def pass1(i, mx):
                r0 = i * STRIP
                s = s_buf[h, pl.ds(r0, STRIP), :] + jnp.tile(
                    bias_ref[0, h, pl.ds(pl.multiple_of(ks + r0, STRIP), STRIP), :], (1, rep))
                if diag:
                    s = jnp.where(kk + r0 <= qq, s, NEG_BIG)
                s_buf[h, pl.ds(r0, STRIP), :] = s
                return jnp.maximum(mx, _fold8(s, jnp.maximum))

            mx = _static_loop(nstrip, pass1, jnp.full((8, tq), NEG_BIG, F32))
            m_old = m_scr[h]
            m_new = jnp.maximum(m_old, jnp.max(mx, axis=0, keepdims=True))
            al_buf[h] = jnp.exp2(m_old - m_new)
            m_scr[h] = m_new

            for i in range(nstrip):
                r0 = i * STRIP
                p_buf[h, pl.ds(r0, STRIP), :] = jnp.exp2((s_buf[h, pl.ds(r0, STRIP), :] - m_new).astype(BF16))

    def apply_values(j, p_buf, al_buf):
        for h in range(2):
            pv = _dot(vt_ref[0, 0, j, h * VROWS:(h + 1) * VROWS, :], p_buf[h])
            acc_scr[h] = acc_scr[h] * al_buf[h] + pv

    s_a, s_b = s_scr.at[0], s_scr.at[1]
    p_a, p_b = p_scr.at[0], p_scr.at[1]
    al_a, al_b = al_scr.at[0], al_scr.at[1]
    p_b[...] = jnp.zeros(p_b.shape, BF16)
    al_b[...] = jnp.ones(al_b.shape, F32)
    issue_scores(0, s_a)

    def pair(i, carry):
        a = 2 * i
        issue_scores(a + 1, s_b)
        softmax(a, s_a, p_a, al_a, False)
        apply_values(jnp.maximum(a - 1, 0), p_b, al_b)
        issue_scores(a + 2, s_a)
        softmax(a + 1, s_b, p_b, al_b, False)
        apply_values(a, p_a, al_a)
        return carry

    lax.fori_loop(0, qi // 2, pair, 0)
    last = 2 * (qi // 2)

    @pl.when(qi % 2 == 1)
    def _():
        issue_scores(qi, s_b)
        softmax(last, s_a, p_a, al_a, False)
        apply_values(jnp.maximum(last - 1, 0), p_b, al_b)
        softmax(qi, s_b, p_b, al_b, True)
        apply_values(last, p_a, al_a)
        apply_values(qi, p_b, al_b)

    @pl.when(qi % 2 == 0)
    def _():
        softmax(qi, s_a, p_a, al_a, True)
        apply_values(jnp.maximum(qi - 1, 0), p_b, al_b)
        apply_values(qi, p_a, al_a)

    for h in range(2):
        denom = acc_scr[h, HEAD_DIM:HEAD_DIM + 1, :]
        o_ref[0, 0, h * HEAD_DIM:(h + 1) * HEAD_DIM, :] = (acc_scr[h, 0:HEAD_DIM, :] * (1.0 / denom)).astype(o_ref.dtype)


def _fox(p16, vt, bias, tq):
    bsz, t, _ = p16.shape
    npair = D_HEADS // 2
    nblk = t // tq
    return pl.pallas_call(
        functools.partial(_fox_kernel, tq=tq),
        grid=(bsz, npair, nblk),
        in_specs=[pl.BlockSpec((1, tq, LANE), lambda b, h, i: (b, i, BLK_DQ + h)),
                  pl.BlockSpec((1, t, LANE), lambda b, h, i: (b, 0, BLK_DK + h), pipeline_mode=pl.Buffered(1)),
                  pl.BlockSpec((1, 1, nblk, 2 * VROWS, tq), lambda b, h, i: (b, h, 0, 0, 0),
                               pipeline_mode=pl.Buffered(1)),
                  pl.BlockSpec((1, 2, t, LANE), lambda b, h, i: (b, h, 0, 0), pipeline_mode=pl.Buffered(1))],
        out_specs=pl.BlockSpec((1, 1, LANE, tq), lambda b, h, i: (b, h, 0, i)),
        out_shape=jax.ShapeDtypeStruct((bsz, npair, LANE, t), BF16),
        scratch_shapes=[pltpu.VMEM((2, 2, tq, tq), F32), pltpu.VMEM((2, 2, tq, tq), BF16),
                        pltpu.VMEM((2, 2, 1, tq), F32),
                        pltpu.VMEM((2, VROWS, tq), F32), pltpu.VMEM((2, 1, tq), F32)],
        compiler_params=_params(("parallel", "parallel", "arbitrary")),
        name="forgetting_attention",
    )(p16, p16, vt, bias)


def _sb_kernel(q_ref, k_ref, vt_ref, o_ref, z_scr, sfx_scr, hi_scr, lo_scr, a_scr, acc_scr, r_scr, *, tq):
    qi = pl.program_id(2)
    nstrip = tq // STRIP
    qh = _head_halves(q_ref[0] * 0.125)
    acc_scr[...] = jnp.zeros(acc_scr.shape, F32)
    r_scr[...] = jnp.zeros(r_scr.shape, F32)
    kk = lax.broadcasted_iota(I32, (STRIP, tq), 0)
    qq = lax.broadcasted_iota(I32, (STRIP, tq), 1)
    row = lax.broadcasted_iota(I32, (tq, tq), 0)
    col = lax.broadcasted_iota(I32, (tq, tq), 1)
    later = jnp.where(col > row, 1.0, 0.0).astype(BF16)

    def block(j, diag):
        ks = pl.multiple_of(j * tq, tq)
        kblk = k_ref[0, pl.ds(ks, tq), :]
        for h in range(2):
            z_scr[h] = _nt(kblk, qh[h])
        for h in range(2):
            def pass1(i, rsum):
                r0 = i * STRIP
                z = z_scr[h, pl.ds(r0, STRIP), :]
                lk = _log_sigmoid(-z)
                z_scr[h, pl.ds(r0, STRIP), :] = lk + z
                if diag:
                    lk = jnp.where(kk + r0 < qq, lk, 0.0)
                hi = lk.astype(BF16)
                hi_scr[h, pl.ds(r0, STRIP), :] = hi
                lo_scr[h, pl.ds(r0, STRIP), :] = (lk - hi.astype(F32)).astype(BF16)
                return rsum + _fold8(lk, jnp.add)

            rsum = _static_loop(nstrip, pass1, jnp.zeros((8, tq), F32))
            sfx_scr[h] = _dot(later, hi_scr[h]) + _dot(later, lo_scr[h])
            r_old = r_scr[h]

            def pass2(i, carry):
                r0 = i * STRIP
                a = jnp.exp(z_scr[h, pl.ds(r0, STRIP), :] + sfx_scr[h, pl.ds(r0, STRIP), :] + r_old)
                if diag:
                    a = jnp.where(kk + r0 < qq, a, 0.0)
                a_scr[h, pl.ds(r0, STRIP), :] = a.astype(BF16)
                return carry

            _static_loop(nstrip, pass2, 0)
            acc_scr[h] = acc_scr[h] + _dot(vt_ref[0, 0, j, h * HEAD_DIM:(h + 1) * HEAD_DIM, :], a_scr[h])
            r_scr[h] = r_old + jnp.sum(rsum, axis=0, keepdims=True)

    block(qi, True)

    def live():
        return jnp.max(jnp.maximum(r_scr[0], r_scr[1])) > EXP_DEAD

    def cond(c):
        return jnp.logical_and(c[0] >= 0, c[1])

    def body(c):
        block(c[0], False)
        return c[0] - 1, live()

    lax.while_loop(cond, body, (qi - 1, live()))
    for h in range(2):
        o_ref[0, 0, h * HEAD_DIM:(h + 1) * HEAD_DIM, :] = acc_scr[h].astype(o_ref.dtype)


def _stick_breaking(p16, vt, tq):
    bsz, t, _ = p16.shape
    npair = C_HEADS // 2
    nblk = t // tq
    return pl.pallas_call(
        functools.partial(_sb_kernel, tq=tq),
        grid=(bsz, npair, nblk),
        in_specs=[pl.BlockSpec((1, tq, LANE), lambda b, h, i: (b, i, BLK_CQ + h)),
                  pl.BlockSpec((1, t, LANE), lambda b, h, i: (b, 0, BLK_CK + h)),
                  pl.BlockSpec((1, 1, nblk, LANE, tq), lambda b, h, i: (b, h, 0, 0, 0))],
        out_specs=pl.BlockSpec((1, 1, LANE, tq), lambda b, h, i: (b, h, 0, i)),
        out_shape=jax.ShapeDtypeStruct((bsz, npair, LANE, t), BF16),
        scratch_shapes=[pltpu.VMEM((2, tq, tq), F32), pltpu.VMEM((2, tq, tq), F32),
                        pltpu.VMEM((2, tq, tq), BF16), pltpu.VMEM((2, tq, tq), BF16),
                        pltpu.VMEM((2, tq, tq), BF16), pltpu.VMEM((2, HEAD_DIM, tq), F32),
                        pltpu.VMEM((2, 1, tq), F32)],
        compiler_params=_params(("parallel", "parallel", "arbitrary")),
        name="stick_breaking",
    )(p16, p16, vt)


def _dsa_kernel(iq0_ref, iq1_ref, ixq_ref, kcat_ref, q0_ref, q1_ref, k0_ref, k1_ref, vt_ref, o_ref,
                key_scr, s_scr, p_scr, eq_scr, rank_scr, bias_scr, al_scr, acc_scr, m_scr, *, tq, topk):
    qi = pl.program_id(1)
    nstrip = tq // STRIP
    lane = lax.broadcasted_iota(I32, (tq, LANE), 1)
    kk = lax.broadcasted_iota(I32, (STRIP, tq), 0)
    qq = lax.broadcasted_iota(I32, (STRIP, tq), 1)

    qcat = []
    for ref in (iq0_ref, iq1_ref):
        for half in range(2):
            x = ref[0] if half == 0 else pltpu.roll(ref[0], IDX_DIM, 1)
            hi = x.astype(BF16).astype(F32)
            first = jnp.where(lane < IDX_DIM, hi, pltpu.roll(x - hi, IDX_DIM, 1))
            second = jnp.where(lane < IDX_DIM, hi, 0.0)
            qcat.append(jnp.concatenate([first, second], axis=1).astype(BF16))
    sr = lax.broadcasted_iota(I32, (8, LANE), 0)
    sl = lax.broadcasted_iota(I32, (8, LANE), 1)
    onehot = jnp.where(sl == sr + LANE_IW, 1.0, 0.0).astype(BF16)
    wh, wm, wl = _split3(ixq_ref[0])
    wt = _nt(onehot, wh) + _nt(onehot, wm) + _nt(onehot, wl)

    def score_chunk(c, diag, counts):
        ge0, gt0 = counts
        ks = pl.multiple_of(c * tq, tq)
        kc = kcat_ref[0, pl.ds(ks, tq), :]
        for h in range(IDX_HEADS):
            s_scr[0, h] = _nt(kc, qcat[h])
        for i in range(nstrip):
            r0 = i * STRIP
            sc = jnp.zeros((STRIP, tq), F32)
            for h in range(IDX_HEADS):
                sc = sc + wt[h:h + 1, :] * jnp.maximum(s_scr[0, h, r0:r0 + STRIP, :], 0.0)
            sc = jnp.where(sc == 0.0, 0.0, sc)
            bits = lax.bitcast_convert_type(sc, I32)
            key = jnp.where(bits < 0, bits ^ jnp.int32(0x7FFFFFFF), bits)
            if diag:
                key = jnp.where(kk + r0 <= qq, key, INT_MIN)
            key_scr[c, r0:r0 + STRIP, :] = key
            ge0 = ge0 + _fold8(jnp.where(key >= 0, 1.0, 0.0), jnp.add)
            gt0 = gt0 + _fold8(jnp.where(key > 0, 1.0, 0.0), jnp.add)
        return ge0, gt0

    zeros = jnp.zeros((8, tq), F32)
    counts = lax.fori_loop(0, qi, lambda c, counts: score_chunk(c, False, counts), (zeros, zeros))
    ge0, gt0 = score_chunk(qi, True, counts)
    n_ge0 = jnp.sum(ge0, axis=0, keepdims=True)
    n_gt0 = jnp.sum(gt0, axis=0, keepdims=True)

    def count(pred):
        def body(c, acc):
            for i in range(nstrip):
                acc = jnp.where(pred(key_scr[c, i * STRIP:(i + 1) * STRIP, :]), acc + 1.0, acc)
            return acc
        acc = lax.fori_loop(0, qi + 1, body, jnp.zeros((STRIP, tq), F32))
        return jnp.sum(acc, axis=0, keepdims=True)

    kf = jnp.float32(topk)
    th0 = jnp.where(n_ge0 >= kf, 0, INT_MIN).astype(I32)
    settled0 = jnp.where(n_ge0 >= kf, jnp.where(n_gt0 < kf, 1.0, 0.0), 0.0)
    n_up0 = jnp.where(n_ge0 >= kf, jnp.where(n_gt0 < kf, n_gt0, 0.0), n_ge0)

    def unsettled(settled):
        return jnp.min(settled) == 0.0

    def search_cond(c):
        return jnp.logical_and(c[0] >= 0, c[4])

    def search_body(c):
        bit, th, n_up, settled, _ = c
        cand = th | lax.shift_left(jnp.int32(1), bit)
        n = count(lambda kb: kb >= cand)
        open_ = settled == 0.0
        th = jnp.where(n >= kf, jnp.where(open_, cand, th), th)
        n_up = jnp.where(n < kf, jnp.where(open_, n, n_up), n_up)
        settled = jnp.maximum(settled, jnp.where(n == kf, 1.0, 0.0))
        return bit - 1, th, n_up, settled, unsettled(settled)

    _, th, n_up, _, _ = lax.while_loop(search_cond, search_body,
                                       (jnp.int32(30), th0, n_up0, settled0, unsettled(settled0)))
    need = kf - n_up

    qh = sum((_head_halves((r[0].astype(F32) * (0.125 * LOG2E)).astype(BF16)) for r in (q0_ref, q1_ref)), ())
    k_refs = (k0_ref, k0_ref, k1_ref, k1_ref)
    m_scr[...] = jnp.full(m_scr.shape, NEG_BIG, F32)
    acc_scr[...] = jnp.zeros(acc_scr.shape, F32)
    row = lax.broadcasted_iota(I32, (tq, tq), 0)
    col = lax.broadcasted_iota(I32, (tq, tq), 1)
    upto = jnp.where(col <= row, 1.0, 0.0).astype(BF16)

    def issue(c, s_buf, eq_buf, rank_buf):
        ks = pl.multiple_of(c * tq, tq)
        for i in range(nstrip):
            r0 = i * STRIP
            eq_buf[r0:r0 + STRIP, :] = jnp.where(key_scr[c, r0:r0 + STRIP, :] == th, 1.0, 0.0).astype(BF16)
        rank_buf[...] = _dot(upto, eq_buf[...])
        for h in range(B_HEADS):
            s_buf[h] = _nt(k_refs[h][0, pl.ds(ks, tq), :], qh[h])

    def softmax(c, s_buf, rank_buf, p_buf, al_buf, diag, seen):
        for i in range(nstrip):
            r0 = i * STRIP
            kb = key_scr[c, r0:r0 + STRIP, :]
            tie = jnp.where(rank_buf[r0:r0 + STRIP, :] + seen <= need, 0.0, NEG_BIG)
            bias = jnp.where(kb > th, 0.0, jnp.where(kb == th, tie, NEG_BIG))
            if diag:
                bias = jnp.where(kk + r0 <= qq, bias, NEG_BIG)
            bias_scr[r0:r0 + STRIP, :] = bias
        for h in range(B_HEADS):
            def pass1(i, mx):
                r0 = i * STRIP
                s = s_buf[h, r0:r0 + STRIP, :] + bias_scr[r0:r0 + STRIP, :]
                s_buf[h, r0:r0 + STRIP, :] = s
                return jnp.maximum(mx, _fold8(s, jnp.maximum))

            mx = _static_loop(nstrip, pass1, jnp.full((8, tq), NEG_BIG, F32))
            m_old = m_scr[h]
            m_new = jnp.maximum(m_old, jnp.max(mx, axis=0, keepdims=True))
            al_buf[h] = jnp.exp2(m_old - m_new)
            m_scr[h] = m_new

            for i in range(nstrip):
                r0 = i * STRIP
                p_buf[h, r0:r0 + STRIP, :] = jnp.exp2((s_buf[h, r0:r0 + STRIP, :] - m_new).astype(BF16))
        return seen + rank_buf[tq - 1:tq, :]

    def apply_values(c, p_buf, al_buf):
        for h in range(B_HEADS):
            lo = (h % 2) * VROWS
            acc_scr[h] = acc_scr[h] * al_buf[h] + _dot(vt_ref[0, h // 2, c, lo:lo + VROWS, :], p_buf[h])

    s_a, s_b = s_scr.at[0], s_scr.at[1]
    p_a, p_b = p_scr.at[0], p_scr.at[1]
    eq_a, eq_b = eq_scr.at[0], eq_scr.at[1]
    rank_a, rank_b = rank_scr.at[0], rank_scr.at[1]
    al_a, al_b = al_scr.at[0], al_scr.at[1]
    p_b[...] = jnp.zeros(p_b.shape, BF16)
    al_b[...] = jnp.ones(al_b.shape, F32)
    issue(0, s_a, eq_a, rank_a)

    def pair(i, seen):
        a = 2 * i
        issue(a + 1, s_b, eq_b, rank_b)
        seen = softmax(a, s_a, rank_a, p_a, al_a, False, seen)
        apply_values(jnp.maximum(a - 1, 0), p_b, al_b)
        issue(a + 2, s_a, eq_a, rank_a)
        seen = softmax(a + 1, s_b, rank_b, p_b, al_b, False, seen)
        apply_values(a, p_a, al_a)
        return seen

    seen = lax.fori_loop(0, qi // 2, pair, jnp.zeros((1, tq), F32))
    last = 2 * (qi // 2)

    @pl.when(qi % 2 == 1)
    def _():
        issue(qi, s_b, eq_b, rank_b)
        seen_d = softmax(last, s_a, rank_a, p_a, al_a, False, seen)
        apply_values(jnp.maximum(last - 1, 0), p_b, al_b)
        softmax(qi, s_b, rank_b, p_b, al_b, True, seen_d)
        apply_values(last, p_a, al_a)
        apply_values(qi, p_b, al_b)

    @pl.when(qi % 2 == 0)
    def _():
        softmax(qi, s_a, rank_a, p_a, al_a, True, seen)
        apply_values(jnp.maximum(qi - 1, 0), p_b, al_b)
        apply_values(qi, p_a, al_a)

    for h in range(B_HEADS):
        lo = (h % 2) * HEAD_DIM
        denom = acc_scr[h, HEAD_DIM:HEAD_DIM + 1, :]
        o_ref[0, h // 2, lo:lo + HEAD_DIM, :] = (acc_scr[h, 0:HEAD_DIM, :] * (1.0 / denom)).astype(o_ref.dtype)


def _dsa(p16, p32, kcat, vt, tq):
    bsz, t, _ = p16.shape
    topk = min(DSA_TOPK, t // 4)
    npair = B_HEADS // 2
    nblk = t // tq
    qspec = lambda arr_blk: pl.BlockSpec((1, tq, LANE), lambda b, i: (b, i, arr_blk))
    once = pl.Buffered(1)
    kspec = lambda arr_blk: pl.BlockSpec((1, t, LANE), lambda b, i: (b, 0, arr_blk), pipeline_mode=once)
    sq = lambda n, dt: pltpu.VMEM((n, tq, tq), dt) if n else pltpu.VMEM((tq, tq), dt)
    return pl.pallas_call(
        functools.partial(_dsa_kernel, tq=tq, topk=topk),
        grid=(bsz, nblk),
        in_specs=[qspec(BLK_IQ), qspec(BLK_IQ + 1), qspec(BLK_IX),
                  pl.BlockSpec((1, t, 2 * LANE), lambda b, i: (b, 0, 0), pipeline_mode=once),
                  qspec(BLK_BQ), qspec(BLK_BQ + 1), kspec(BLK_BK), kspec(BLK_BK + 1),
                  pl.BlockSpec((1, npair, nblk, 2 * VROWS, tq), lambda b, i: (b, 0, 0, 0, 0), pipeline_mode=once)],
        out_specs=pl.BlockSpec((1, npair, LANE, tq), lambda b, i: (b, 0, 0, i)),
        out_shape=jax.ShapeDtypeStruct((bsz, npair, LANE, t), BF16),
        scratch_shapes=[sq(nblk, I32), pltpu.VMEM((2, B_HEADS, tq, tq), F32), pltpu.VMEM((2, B_HEADS, tq, tq), BF16),
                        sq(2, BF16), sq(2, F32), sq(0, F32), pltpu.VMEM((2, B_HEADS, 1, tq), F32),
                        pltpu.VMEM((B_HEADS, VROWS, tq), F32), pltpu.VMEM((B_HEADS, 1, tq), F32)],
        compiler_params=_params(("parallel", "arbitrary")),
        name="dsa_topk_attention",
    )(p32, p32, p32, kcat, p16, p16, p16, p16, vt)


def _merge_kernel(x_ref, sh_ref, sc_ref, gt_ref, oa0_ref, oa1_ref, oa2_ref, la0_ref, la1_ref, la2_ref,
                  yb_ref, yc_ref, yd_ref, wg_ref, bg_ref, wb_ref, wo_ref, lg_ref, lb_ref, o_ref, *, alpha):
    x = x_ref[0]
    d = x.shape[1]
    u = (x * (1.0 + sc_ref[0]) + sh_ref[0]).astype(BF16)
    lses = [r[0] for r in (la0_ref, la1_ref, la2_ref)]
    top = jnp.maximum(jnp.maximum(lses[0], lses[1]), lses[2])
    es = [jnp.exp(l - top) for l in lses]
    den = es[0] + es[1] + es[2]
    ya = sum((e / den) * r[0].astype(F32) for e, r in zip(es, (oa0_ref, oa1_ref, oa2_ref)))
    branches = (ya.astype(BF16), yb_ref[0], yc_ref[0], yd_ref[0])
    merged = jnp.zeros(x.shape, F32)
    off = 0
    for i, br in enumerate(branches):
        gate = jax.nn.sigmoid(_dot(u, wg_ref[:, i * d:(i + 1) * d]) + bg_ref[:, i * d:(i + 1) * d])
        merged = merged + gate * _dot(br, wb_ref[off:off + BRANCH_WIDTHS[i], :])
        off += BRANCH_WIDTHS[i]
    h = _dot(merged.astype(BF16), wo_ref[...])
    o_ref[0] = _deepnorm_ln(x, h, gt_ref[0], lg_ref[...], lb_ref[...], alpha)


def _merge(x, shift, scale, gate, oas, las, yb, yc, yd, wg, bg, wb, wo, ln_g, ln_b, alpha):
    bsz, t, d = x.shape
    tm = _pick(t, (256, 128))
    row = lambda b, i: (b, 0, 0)
    tok = lambda w: pl.BlockSpec((1, tm, w), lambda b, i: (b, i, 0))
    wspec = lambda shape: pl.BlockSpec(shape, lambda b, i: (0, 0), pipeline_mode=pl.Buffered(1))
    return pl.pallas_call(
        functools.partial(_merge_kernel, alpha=alpha),
        grid=(bsz, t // tm),
        in_specs=[tok(d), pl.BlockSpec((1, 1, d), row), pl.BlockSpec((1, 1, d), row), pl.BlockSpec((1, 1, d), row),
                  tok(LANE), tok(LANE), tok(LANE), tok(LANE), tok(LANE), tok(LANE),
                  tok(2 * LANE), tok(2 * LANE), tok(2 * LANE),
                  wspec(wg.shape), wspec(bg.shape), wspec(wb.shape), wspec(wo.shape),
                  pl.BlockSpec((1, d), lambda b, i: (0, 0)), pl.BlockSpec((1, d), lambda b, i: (0, 0))],
        out_specs=tok(d),
        out_shape=jax.ShapeDtypeStruct((bsz, t, d), F32),
        compiler_params=_params(("parallel", "parallel")),
        name="gated_merge",
    )(x, shift, scale, gate, *oas, *las, yb, yc, yd, wg, bg, wb, wo, ln_g, ln_b)


def _rope_tables(t):
    half = ROT_DIM // 2
    inv_freq = ROPE_THETA ** (-(jnp.arange(half, dtype=F32) * (2.0 / ROT_DIM)))
    ang = jnp.arange(t, dtype=F32)[:, None] * inv_freq[None, :]
    cos, sin = jnp.cos(ang), jnp.sin(ang)
    ones = jnp.ones((t, HEAD_DIM - ROT_DIM), F32)
    zeros = jnp.zeros((t, HEAD_DIM - half), F32)
    c64 = jnp.concatenate([cos, cos, ones], axis=1)
    s1 = jnp.concatenate([-sin, zeros], axis=1)
    s2 = jnp.concatenate([jnp.zeros((t, half), F32), sin, zeros[:, half:]], axis=1)
    return tuple(jnp.concatenate([a, a], axis=1) for a in (c64, s1, s2))


def _mixer_weights(w_in):
    d = w_in.shape[0]

    def qkv(off, heads):
        w = heads * HEAD_DIM
        return w_in[:, off:off + w], w_in[:, off + w:off + 2 * w], w_in[:, off + 2 * w:off + 3 * w]

    aq, ak, av = qkv(OFF_A, A_HEADS)
    bq, bk, bv = qkv(OFF_B, B_HEADS)
    cq, ck, cv = qkv(OFF_C, C_HEADS)
    dq, dk, dv = qkv(OFF_D, D_HEADS)
    w16 = jnp.concatenate([aq, ak, bq, bk, av, bv, cq, ck, cv, dq, dk, dv, jnp.zeros((d, LANE), F32)], axis=1)
    ix = jnp.concatenate([w_in[:, OFF_IK:OFF_IK + IDX_DIM], jnp.zeros((d, LANE_IW - IDX_DIM), F32),
                          w_in[:, OFF_IW:OFF_IW + IDX_HEADS], w_in[:, OFF_FG:OFF_FG + D_HEADS],
                          jnp.zeros((d, LANE - LANE_FG - D_HEADS), F32)], axis=1)
    w32 = jnp.concatenate([w_in[:, OFF_IQ:OFF_IQ + IDX_HEADS * IDX_DIM], ix, jnp.zeros((d, LANE), F32)], axis=1)
    return w16.astype(BF16), w32.astype(BF16), w_in[:, OFF_GATE:].astype(BF16)


def _to_residues(a, dil):
    bsz, t, w = a.shape
    if dil == 1:
        return a
    return a.reshape(bsz, t // dil, dil, w).transpose(0, 2, 1, 3).reshape(bsz * dil, t // dil, w)


def _from_residues(a, dil, bsz):
    if dil == 1:
        return a
    _, ln, w = a.shape
    return a.reshape(bsz, dil, ln, w).transpose(0, 2, 1, 3).reshape(bsz, ln * dil, w)


def _mixer(x, shift, scale, gate, w_in, b_gate, b_forget, w_branch, w_out, ln_g, ln_b, tables, alpha):
    bsz, t, d = x.shape
    w16, w32, wg = _mixer_weights(w_in)
    p16 = _proj(x, shift, scale, w16, tables, N_ROPE_BLKS_16, BF16, "mixer_proj_bf16")
    p32 = _proj(x, shift, scale, w32, tables, N_BLKS_32, F32, "mixer_proj_f32")

    oas, las = [], []
    for g, (_, dil) in enumerate(DILATED_GROUPS):
        if dil == 1:
            o, lse = _dilated(p16, p16, p16, cols=(BLK_AQ + g, BLK_AK + g, BLK_AV + g))
        else:
            streams = [_to_residues(p16[:, :, (blk + g) * LANE:(blk + g + 1) * LANE], dil)
                       for blk in (BLK_AQ, BLK_AK, BLK_AV)]
            o, lse = _dilated(*streams)
        oas.append(_from_residues(o, dil, bsz))
        las.append(_from_residues(lse, dil, bsz))

    tq = _pick(t, (256, 128))
    bias_row = jnp.zeros((1, LANE), F32).at[0, LANE_FG:LANE_FG + D_HEADS].set(b_forget)
    fbias, kcat = _cum(p32, bias_row, tq)
    tqf = _pick(t, (512, 256, 128))
    yd = _token_major(_fox(p16, _transposed_values(p16, BLK_DV, D_HEADS // 2, tqf, True), fbias, tqf))
    yc = _token_major(_stick_breaking(p16, _transposed_values(p16, BLK_CV, C_HEADS // 2, tq), tq))
    yb = _token_major(_dsa(p16, p32, kcat, _transposed_values(p16, BLK_BV, B_HEADS // 2, tqf, True), tqf))
    return _merge(x, shift, scale, gate, oas, las, yb, yc, yd, wg, b_gate.reshape(1, -1),
                  w_branch.astype(BF16), w_out.astype(BF16), ln_g, ln_b, alpha)


def kernel(x, c, ada_w, ada_b, ln_g, ln_b, ffn_w_in, ffn_w_out, mix_w_in, mix_b_gate, mix_b_forget,
           mix_w_branch, mix_w_out):
    bsz, t, d = x.shape
    depth = ada_w.shape[0]
    f = ffn_w_out.shape[2]
    alpha = float((2 * depth) ** 0.25)
    assert t % (DILATED_GROUPS[-1][0]) == 0 and d % LANE == 0 and bsz <= 8

    c8 = jnp.zeros((8, d), F32).at[:bsz].set(c)
    mod = _ada(c8, ada_w, ada_b)[:, :bsz].reshape(depth, bsz, 3, 3, 1, d)
    tables = _rope_tables(t)

    for l in range(depth):
        m = lambda sub, kind: mod[l, :, sub, kind]
        lng = lambda sub: ln_g[l, sub].reshape(1, d)
        lnb = lambda sub: ln_b[l, sub].reshape(1, d)

        def ffn(x, sub, which):
            w_in = ffn_w_in[l, which].astype(BF16)
            return _ffn(x, m(sub, 0), m(sub, 1), m(sub, 2), w_in[:, :f], w_in[:, f:],
                        ffn_w_out[l, which].astype(BF16), lng(sub), lnb(sub), alpha)

        x = ffn(x, 0, 0)
        x = _mixer(x, m(1, 0), m(1, 1), m(1, 2), mix_w_in[l], mix_b_gate[l], mix_b_forget[l],
                   mix_w_branch[l], mix_w_out[l], lng(1), lnb(1), tables, alpha)
        x = ffn(x, 2, 1)
    return x
```

```python
import functools

import jax
import jax.numpy as jnp
from jax import lax
from jax.experimental import pallas as pl
from jax.experimental.pallas import tpu as pltpu

F32 = jnp.float32
BF16 = jnp.bfloat16
I32 = jnp.int32

LANE = 128
HEAD_DIM = 64
ROT_DIM = HEAD_DIM // 4
ROPE_THETA = 500000.0
DILATED_GROUPS = ((128, 1), (512, 4), (2048, 16))
SPAN = 128
A_HEADS, B_HEADS, C_HEADS, D_HEADS = 6, 4, 4, 4
IDX_HEADS, IDX_DIM = 4, 64
DSA_TOPK = 256
N_BRANCH = 4
LN_EPS = 1e-5
BRANCH_WIDTHS = (128, 256, 256, 256)

OFF_A = 0
OFF_B = OFF_A + 3 * A_HEADS * HEAD_DIM
OFF_IQ = OFF_B + 3 * B_HEADS * HEAD_DIM
OFF_IK = OFF_IQ + IDX_HEADS * IDX_DIM
OFF_IW = OFF_IK + IDX_DIM
OFF_C = OFF_IW + IDX_HEADS
OFF_D = OFF_C + 3 * C_HEADS * HEAD_DIM
OFF_FG = OFF_D + 3 * D_HEADS * HEAD_DIM
OFF_GATE = OFF_FG + D_HEADS

BLK_AQ, BLK_AK, BLK_BQ, BLK_BK = 0, 3, 6, 8
N_ROPE_BLKS_16 = 10
BLK_AV, BLK_BV = 10, 13
BLK_CQ, BLK_CK, BLK_CV = 15, 17, 19
BLK_DQ, BLK_DK, BLK_DV = 21, 23, 25
N_BLKS_16 = 28
BLK_IQ, BLK_IX = 0, 2
N_BLKS_32 = 4
LANE_IW = 80
LANE_FG = 84

INT_MIN = -2147483648
NEG_BIG = -1e30
LOG2E = 1.4426950408889634
EXP_DEAD = -104.0
STRIP = 32
VMEM_LIMIT = 56 * 1024 * 1024


def _nt(a, b):
    return lax.dot_general(a, b, (((1,), (1,)), ((), ())), preferred_element_type=F32)


def _dot(a, b):
    return jnp.dot(a, b, preferred_element_type=F32)


def _split3(x):
    hi = x.astype(BF16)
    r1 = x - hi.astype(F32)
    mid = r1.astype(BF16)
    lo = (r1 - mid.astype(F32)).astype(BF16)
    return hi, mid, lo


def _split2(x):
    hi = x.astype(BF16)
    return hi, (x - hi.astype(F32)).astype(BF16)


def _log_sigmoid(z):
    return -(jnp.maximum(-z, 0.0) + jnp.log(1.0 + jnp.exp(-jnp.abs(z))))


def _params(sem):
    return pltpu.CompilerParams(dimension_semantics=sem, vmem_limit_bytes=VMEM_LIMIT)


def _const_spec(shape):
    n = len(shape)
    return pl.BlockSpec(shape, lambda *_: (0,) * n)


def _pick(n, prefs):
    for p in prefs:
        if n % p == 0:
            return p
    return n


def _ada_kernel(c_ref, w_ref, b_ref, o_ref):
    c = c_ref[...]
    cond = c * jax.nn.sigmoid(c)
    ch, cm, _ = _split3(cond)
    wh, wm, _ = _split3(w_ref[0])
    o_ref[0] = _dot(ch, wh) + _dot(ch, wm) + _dot(cm, wh) + b_ref[0]


def _ada(c8, ada_w, ada_b):
    depth, d, n = ada_w.shape
    tn = _pick(n, (1152, 1024, 512, 256, 128))
    return pl.pallas_call(
        _ada_kernel,
        grid=(depth, n // tn),
        in_specs=[pl.BlockSpec((8, d), lambda l, j: (0, 0)),
                  pl.BlockSpec((1, d, tn), lambda l, j: (l, 0, j)),
                  pl.BlockSpec((1, 1, tn), lambda l, j: (l, 0, j))],
        out_specs=pl.BlockSpec((1, 8, tn), lambda l, j: (l, 0, j)),
        out_shape=jax.ShapeDtypeStruct((depth, 8, n), F32),
        compiler_params=_params(("parallel", "parallel")),
        name="ada_mod",
    )(c8, ada_w, ada_b.reshape(depth, 1, n))


def _deepnorm_ln(x, h, gate, g, b, alpha):
    y = alpha * x + (1.0 + gate) * h
    mu = jnp.mean(y, axis=-1, keepdims=True)
    yc = y - mu
    var = jnp.mean(yc * yc, axis=-1, keepdims=True)
    return yc * lax.rsqrt(var + LN_EPS) * g + b


def _ffn_kernel(x_ref, sh_ref, sc_ref, gt_ref, wg_ref, wu_ref, wo_ref, lg_ref, lb_ref, o_ref, *, tf, alpha):
    x = x_ref[0]
    u = (x * (1.0 + sc_ref[0]) + sh_ref[0]).astype(BF16)
    acc = jnp.zeros(x.shape, F32)
    for j in range(wg_ref.shape[1] // tf):
        g = _dot(u, wg_ref[:, j * tf:(j + 1) * tf])
        up = _dot(u, wu_ref[:, j * tf:(j + 1) * tf])
        h = (g * jax.nn.sigmoid(g) * up).astype(BF16)
        acc = acc + _dot(h, wo_ref[j * tf:(j + 1) * tf, :])
    o_ref[0] = _deepnorm_ln(x, 0.5 * acc, gt_ref[0], lg_ref[...], lb_ref[...], alpha)


def _ffn(x, shift, scale, gate, wg, wu, wo, ln_g, ln_b, alpha):
    bsz, t, d = x.shape
    f = wg.shape[1]
    tm = _pick(t, (512, 256, 128))
    tf = _pick(f, (256, 128))
    row = lambda b, i: (b, 0, 0)
    wspec = lambda shape: pl.BlockSpec(shape, lambda b, i: (0, 0), pipeline_mode=pl.Buffered(1))
    return pl.pallas_call(
        functools.partial(_ffn_kernel, tf=tf, alpha=alpha),
        grid=(bsz, t // tm),
        in_specs=[pl.BlockSpec((1, tm, d), lambda b, i: (b, i, 0)),
                  pl.BlockSpec((1, 1, d), row), pl.BlockSpec((1, 1, d), row), pl.BlockSpec((1, 1, d), row),
                  wspec((d, f)), wspec((d, f)), wspec((f, d)),
                  pl.BlockSpec((1, d), lambda b, i: (0, 0)), pl.BlockSpec((1, d), lambda b, i: (0, 0))],
        out_specs=pl.BlockSpec((1, tm, d), lambda b, i: (b, i, 0)),
        out_shape=jax.ShapeDtypeStruct((bsz, t, d), F32),
        compiler_params=_params(("parallel", "parallel")),
        name="ffn",
    )(x, shift, scale, gate, wg, wu, wo, ln_g, ln_b)


def _proj_kernel(x_ref, sh_ref, sc_ref, w_ref, cos_ref, s1_ref, s2_ref, o_ref, *, tn, n_rope_blks):
    u = (x_ref[0] * (1.0 + sc_ref[0]) + sh_ref[0]).astype(BF16)
    n = w_ref.shape[1]
    per = tn // LANE
    for j in range(n // tn):
        y = _dot(u, w_ref[:, j * tn:(j + 1) * tn])
        for i in range(per):
            blk = j * per + i
            yi = y[:, i * LANE:(i + 1) * LANE]
            if blk < n_rope_blks:
                yi = (yi * cos_ref[...] + pltpu.roll(yi, LANE - ROT_DIM // 2, 1) * s1_ref[...]
                      + pltpu.roll(yi, ROT_DIM // 2, 1) * s2_ref[...])
            o_ref[0, :, blk * LANE:(blk + 1) * LANE] = yi.astype(o_ref.dtype)


def _proj(x, shift, scale, w, tables, n_rope_blks, out_dtype, name):
    bsz, t, d = x.shape
    n = w.shape[1]
    tm = _pick(t, (512, 256, 128))
    tn = _pick(n, (256, 128))
    row = lambda b, i: (b, 0, 0)
    tab = pl.BlockSpec((tm, LANE), lambda b, i: (i, 0))
    return pl.pallas_call(
        functools.partial(_proj_kernel, tn=tn, n_rope_blks=n_rope_blks),
        grid=(bsz, t // tm),
        in_specs=[pl.BlockSpec((1, tm, d), lambda b, i: (b, i, 0)),
                  pl.BlockSpec((1, 1, d), row), pl.BlockSpec((1, 1, d), row),
                  pl.BlockSpec((d, n), lambda b, i: (0, 0), pipeline_mode=pl.Buffered(1)),
                  tab, tab, tab],
        out_specs=pl.BlockSpec((1, tm, n), lambda b, i: (b, i, 0)),
        out_shape=jax.ShapeDtypeStruct((bsz, t, n), out_dtype),
        compiler_params=_params(("parallel", "parallel")),
        name=name,
    )(x, shift, scale, w, *tables)


def _cum_kernel(p_ref, bias_ref, out_ref, kcat_ref, carry_ref):
    @pl.when(pl.program_id(1) == 0)
    def _():
        carry_ref[...] = jnp.zeros_like(carry_ref)

    tc = p_ref.shape[1]
    lane = lax.broadcasted_iota(I32, (tc, LANE), 1)
    lf = _log_sigmoid(p_ref[0] + bias_ref[...])
    lf = jnp.where(lane >= LANE_FG, jnp.where(lane < LANE_FG + D_HEADS, lf, 0.0), 0.0)
    ri = lax.broadcasted_iota(I32, (tc, tc), 0)
    ci = lax.broadcasted_iota(I32, (tc, tc), 1)
    tri = jnp.where(ri >= ci, 1.0, 0.0).astype(BF16)
    hi, mid, lo = _split3(lf)
    cum = _dot(tri, hi) + _dot(tri, mid) + _dot(tri, lo) + carry_ref[...]
    carry_ref[...] = cum[tc - 1:tc, :]
    er = lax.broadcasted_iota(I32, (LANE, LANE), 0)
    ch, cm, cl = _split3(-LOG2E * cum)
    for h in range(D_HEADS):
        onehot = jnp.where(er == LANE_FG + h, 1.0, 0.0).astype(BF16)
        out_ref[0, h] = _dot(ch, onehot) + _dot(cm, onehot) + _dot(cl, onehot)
    x = p_ref[0]
    hi = x.astype(BF16).astype(F32)
    kcat_ref[0, :, 0:LANE] = jnp.where(lane < IDX_DIM, hi, pltpu.roll(hi, IDX_DIM, 1)).astype(BF16)
    kcat_ref[0, :, LANE:2 * LANE] = jnp.where(lane < IDX_DIM, x - hi, 0.0).astype(BF16)


def _cum(p32, bias_row, tc):
    bsz, t, _ = p32.shape
    return pl.pallas_call(
        _cum_kernel,
        grid=(bsz, t // tc),
        in_specs=[pl.BlockSpec((1, tc, LANE), lambda b, i: (b, i, BLK_IX)),
                  pl.BlockSpec((1, LANE), lambda b, i: (0, 0))],
        out_specs=[pl.BlockSpec((1, D_HEADS, tc, LANE), lambda b, i: (b, 0, i, 0)),
                   pl.BlockSpec((1, tc, 2 * LANE), lambda b, i: (b, i, 0))],
        out_shape=[jax.ShapeDtypeStruct((bsz, D_HEADS, t, LANE), F32),
                   jax.ShapeDtypeStruct((bsz, t, 2 * LANE), BF16)],
        scratch_shapes=[pltpu.VMEM((1, LANE), F32)],
        compiler_params=_params(("parallel", "arbitrary")),
        name="forget_cumsum",
    )(p32, bias_row)


def _head_halves(q):
    lane = lax.broadcasted_iota(I32, q.shape, 1)
    zero = jnp.zeros_like(q)
    return jnp.where(lane < HEAD_DIM, q, zero), jnp.where(lane >= HEAD_DIM, q, zero)


def _static_loop(n, body, carry):
    for i in range(n):
        carry = body(i, carry)
    return carry


def _fold8(x, op):
    out = x[0:8]
    for r in range(8, x.shape[0], 8):
        out = op(out, x[r:r + 8])
    return out


def _transposed_values(p16, blk, npair, tq):
    bsz, t, _ = p16.shape
    v = p16[:, :, blk * LANE:(blk + npair) * LANE].reshape(bsz, t // tq, tq, npair, LANE)
    return v.transpose(0, 3, 1, 4, 2)


def _token_major(yt):
    bsz, npair, _, t = yt.shape
    return yt.transpose(0, 3, 1, 2).reshape(bsz, t, npair * LANE)


def _lane_pick(a, b):
    lane = lax.broadcasted_iota(I32, a.shape, 1)
    return jnp.where(lane < HEAD_DIM, a, b)


def _dil_kernel(q_ref, k_ref, v_ref, o_ref, lse_ref, *, kw):
    nb = q_ref.shape[1] // SPAN

    def body(i, carry):
        qs = pl.multiple_of(i * SPAN, SPAN)
        ks = pl.multiple_of(jnp.maximum(i * SPAN + SPAN - kw, 0), SPAN)
        q = q_ref[0, pl.ds(qs, SPAN), :] * 0.125
        k = k_ref[0, pl.ds(ks, kw), :]
        v = v_ref[0, pl.ds(ks, kw), :]
        qpos = qs + lax.broadcasted_iota(I32, (SPAN, kw), 0)
        kpos = ks + lax.broadcasted_iota(I32, (SPAN, kw), 1)
        dist = qpos - kpos
        outs, lses = [], []
        for qh in _head_halves(q):
            s = _nt(qh, k)
            s = jnp.where(dist >= 0, jnp.where(dist <= SPAN, s, NEG_BIG), NEG_BIG)
            m = jnp.max(s, axis=-1, keepdims=True)
            p = jnp.exp(s - m)
            l = jnp.sum(p, axis=-1, keepdims=True)
            outs.append(_dot(p.astype(BF16), v) / l)
            lses.append(jnp.broadcast_to(m + jnp.log(l), (SPAN, LANE)))
        o_ref[0, pl.ds(qs, SPAN), :] = _lane_pick(outs[0], outs[1]).astype(o_ref.dtype)
        lse_ref[0, pl.ds(qs, SPAN), :] = _lane_pick(lses[0], lses[1])
        return carry

    lax.fori_loop(0, nb, body, 0, unroll=min(nb, 4))


def _dilated(q, k, v, cols=(0, 0, 0)):
    ns, ln, _ = q.shape
    kw = min(2 * SPAN, ln)
    spec = pl.BlockSpec((1, ln, LANE), lambda s: (s, 0, 0))
    in_spec = lambda col: pl.BlockSpec((1, ln, LANE), lambda s: (s, 0, col))
    return pl.pallas_call(
        functools.partial(_dil_kernel, kw=kw),
        grid=(ns,),
        in_specs=[in_spec(c) for c in cols],
        out_specs=[spec, spec],
        out_shape=[jax.ShapeDtypeStruct((ns, ln, LANE), BF16), jax.ShapeDtypeStruct((ns, ln, LANE), F32)],
        compiler_params=_params(("parallel",)),
        name="dilated_window",
    )(q, k, v)


def _fox_kernel(q_ref, k_ref, vt_ref, bias_ref, o_ref, s_scr, p_scr, al_scr, acc_scr, m_scr, l_scr, *, tq):
    qi = pl.program_id(2)
    rep = tq // LANE
    nstrip = tq // STRIP
    qh = _head_halves((q_ref[0].astype(F32) * (0.125 * LOG2E)).astype(BF16))
    m_scr[...] = jnp.full(m_scr.shape, NEG_BIG, F32)
    l_scr[...] = jnp.zeros(l_scr.shape, F32)
    acc_scr[...] = jnp.zeros(acc_scr.shape, F32)
    kk = lax.broadcasted_iota(I32, (STRIP, tq), 0)
    qq = lax.broadcasted_iota(I32, (STRIP, tq), 1)

    def issue_scores(j, s_buf):
        kblk = k_ref[0, pl.ds(pl.multiple_of(j * tq, tq), tq), :]
        for h in range(2):
            s_buf[h] = _nt(kblk, qh[h])

    def softmax(j, s_buf, p_buf, al_buf, diag):
        ks = pl.multiple_of(j * tq, tq)
        for h in range(2):
            def pass1(i, mx):
                r0 = i * STRIP
                s = s_buf[h, pl.ds(r0, STRIP), :] + jnp.tile(
                    bias_ref[0, h, pl.ds(pl.multiple_of(ks + r0, STRIP), STRIP), :], (1, rep))
                if diag:
                    s = jnp.where(kk + r0 <= qq, s, NEG_BIG)
                s_buf[h, pl.ds(r0, STRIP), :] = s
                return jnp.maximum(mx, _fold8(s, jnp.maximum))

            mx = _static_loop(nstrip, pass1, jnp.full((8, tq), NEG_BIG, F32))
            m_old = m_scr[h]
            m_new = jnp.maximum(m_old, jnp.max(mx, axis=0, keepdims=True))
            al_buf[h] = jnp.exp2(m_old - m_new)
            m_scr[h] = m_new

            def pass2(i, ls):
                r0 = i * STRIP
                p = jnp.exp2(s_buf[h, pl.ds(r0, STRIP), :] - m_new)
                p_buf[h, pl.ds(r0, STRIP), :] = p.astype(BF16)
                return ls + _fold8(p, jnp.add)

            ls = _static_loop(nstrip, pass2, jnp.zeros((8, tq), F32))
            l_scr[h] = al_buf[h] * l_scr[h] + jnp.sum(ls, axis=0, keepdims=True)

    def apply_values(j, p_buf, al_buf):
        for h in range(2):
            pv = _dot(vt_ref[0, 0, j, h * HEAD_DIM:(h + 1) * HEAD_DIM, :], p_buf[h])
            acc_scr[h] = acc_scr[h] * al_buf[h] + pv

    s_a, s_b = s_scr.at[0], s_scr.at[1]
    p_a, p_b = p_scr.at[0], p_scr.at[1]
    al_a, al_b = al_scr.at[0], al_scr.at[1]
    p_b[...] = jnp.zeros(p_b.shape, BF16)
    al_b[...] = jnp.ones(al_b.shape, F32)
    issue_scores(0, s_a)

    def pair(i, carry):
        a = 2 * i
        issue_scores(a + 1, s_b)
        softmax(a, s_a, p_a, al_a, False)
        apply_values(jnp.maximum(a - 1, 0), p_b, al_b)
        issue_scores(a + 2, s_a)
        softmax(a + 1, s_b, p_b, al_b, False)
        apply_values(a, p_a, al_a)
        return carry

    lax.fori_loop(0, qi // 2, pair, 0)
    last = 2 * (qi // 2)

    @pl.when(qi % 2 == 1)
    def _():
        issue_scores(qi, s_b)
        softmax(last, s_a, p_a, al_a, False)
        apply_values(jnp.maximum(last - 1, 0), p_b, al_b)
        softmax(qi, s_b, p_b, al_b, True)
        apply_values(last, p_a, al_a)
        apply_values(qi, p_b, al_b)

    @pl.when(qi % 2 == 0)
    def _():
        softmax(qi, s_a, p_a, al_a, True)
        apply_values(jnp.maximum(qi - 1, 0), p_b, al_b)
        apply_values(qi, p_a, al_a)

    for h in range(2):
        o_ref[0, 0, h * HEAD_DIM:(h + 1) * HEAD_DIM, :] = (acc_scr[h] * (1.0 / l_scr[h])).astype(o_ref.dtype)


def _fox(p16, vt, bias, tq):
    bsz, t, _ = p16.shape
    npair = D_HEADS // 2
    nblk = t // tq
    return pl.pallas_call(
        functools.partial(_fox_kernel, tq=tq),
        grid=(bsz, npair, nblk),
        in_specs=[pl.BlockSpec((1, tq, LANE), lambda b, h, i: (b, i, BLK_DQ + h)),
                  pl.BlockSpec((1, t, LANE), lambda b, h, i: (b, 0, BLK_DK + h)),
                  pl.BlockSpec((1, 1, nblk, LANE, tq), lambda b, h, i: (b, h, 0, 0, 0)),
                  pl.BlockSpec((1, 2, t, LANE), lambda b, h, i: (b, h, 0, 0))],
        out_specs=pl.BlockSpec((1, 1, LANE, tq), lambda b, h, i: (b, h, 0, i)),
        out_shape=jax.ShapeDtypeStruct((bsz, npair, LANE, t), BF16),
        scratch_shapes=[pltpu.VMEM((2, 2, tq, tq), F32), pltpu.VMEM((2, 2, tq, tq), BF16),
                        pltpu.VMEM((2, 2, 1, tq), F32),
                        pltpu.VMEM((2, HEAD_DIM, tq), F32), pltpu.VMEM((2, 1, tq), F32),
                        pltpu.VMEM((2, 1, tq), F32)],
        compiler_params=_params(("parallel", "parallel", "arbitrary")),
        name="forgetting_attention",
    )(p16, p16, vt, bias)


def _sb_kernel(q_ref, k_ref, vt_ref, o_ref, z_scr, sfx_scr, hi_scr, lo_scr, a_scr, acc_scr, r_scr, *, tq):
    qi = pl.program_id(2)
    nstrip = tq // STRIP
    qh = _head_halves(q_ref[0] * 0.125)
    acc_scr[...] = jnp.zeros(acc_scr.shape, F32)
    r_scr[...] = jnp.zeros(r_scr.shape, F32)
    kk = lax.broadcasted_iota(I32, (STRIP, tq), 0)
    qq = lax.broadcasted_iota(I32, (STRIP, tq), 1)
    row = lax.broadcasted_iota(I32, (tq, tq), 0)
    col = lax.broadcasted_iota(I32, (tq, tq), 1)
    later = jnp.where(col > row, 1.0, 0.0).astype(BF16)

    def block(j, diag):
        ks = pl.multiple_of(j * tq, tq)
        kblk = k_ref[0, pl.ds(ks, tq), :]
        for h in range(2):
            z_scr[h] = _nt(kblk, qh[h])
        for h in range(2):
            def pass1(i, rsum):
                r0 = i * STRIP
                z = z_scr[h, pl.ds(r0, STRIP), :]
                lk = _log_sigmoid(-z)
                z_scr[h, pl.ds(r0, STRIP), :] = lk + z
                if diag:
                    lk = jnp.where(kk + r0 < qq, lk, 0.0)
                hi = lk.astype(BF16)
                hi_scr[h, pl.ds(r0, STRIP), :] = hi
                lo_scr[h, pl.ds(r0, STRIP), :] = (lk - hi.astype(F32)).astype(BF16)
                return rsum + _fold8(lk, jnp.add)

            rsum = _static_loop(nstrip, pass1, jnp.zeros((8, tq), F32))
            sfx_scr[h] = _dot(later, hi_scr[h]) + _dot(later, lo_scr[h])
            r_old = r_scr[h]

            def pass2(i, carry):
                r0 = i * STRIP
                a = jnp.exp(z_scr[h, pl.ds(r0, STRIP), :] + sfx_scr[h, pl.ds(r0, STRIP), :] + r_old)
                if diag:
                    a = jnp.where(kk + r0 < qq, a, 0.0)
                a_scr[h, pl.ds(r0, STRIP), :] = a.astype(BF16)
                return carry

            _static_loop(nstrip, pass2, 0)
            acc_scr[h] = acc_scr[h] + _dot(vt_ref[0, 0, j, h * HEAD_DIM:(h + 1) * HEAD_DIM, :], a_scr[h])
            r_scr[h] = r_old + jnp.sum(rsum, axis=0, keepdims=True)

    block(qi, True)

    def live():
        return jnp.max(jnp.maximum(r_scr[0], r_scr[1])) > EXP_DEAD

    def cond(c):
        return jnp.logical_and(c[0] >= 0, c[1])

    def body(c):
        block(c[0], False)
        return c[0] - 1, live()

    lax.while_loop(cond, body, (qi - 1, live()))
    for h in range(2):
        o_ref[0, 0, h * HEAD_DIM:(h + 1) * HEAD_DIM, :] = acc_scr[h].astype(o_ref.dtype)


def _stick_breaking(p16, vt, tq):
    bsz, t, _ = p16.shape
    npair = C_HEADS // 2
    nblk = t // tq
    return pl.pallas_call(
        functools.partial(_sb_kernel, tq=tq),
        grid=(bsz, npair, nblk),
        in_specs=[pl.BlockSpec((1, tq, LANE), lambda b, h, i: (b, i, BLK_CQ + h)),
                  pl.BlockSpec((1, t, LANE), lambda b, h, i: (b, 0, BLK_CK + h)),
                  pl.BlockSpec((1, 1, nblk, LANE, tq), lambda b, h, i: (b, h, 0, 0, 0))],
        out_specs=pl.BlockSpec((1, 1, LANE, tq), lambda b, h, i: (b, h, 0, i)),
        out_shape=jax.ShapeDtypeStruct((bsz, npair, LANE, t), BF16),
        scratch_shapes=[pltpu.VMEM((2, tq, tq), F32), pltpu.VMEM((2, tq, tq), F32),
                        pltpu.VMEM((2, tq, tq), BF16), pltpu.VMEM((2, tq, tq), BF16),
                        pltpu.VMEM((2, tq, tq), BF16), pltpu.VMEM((2, HEAD_DIM, tq), F32),
                        pltpu.VMEM((2, 1, tq), F32)],
        compiler_params=_params(("parallel", "parallel", "arbitrary")),
        name="stick_breaking",
    )(p16, p16, vt)


def _dsa_kernel(iq0_ref, iq1_ref, ixq_ref, kcat_ref, q0_ref, q1_ref, k0_ref, k1_ref, vt_ref, o_ref,
                key_scr, s_scr, p_scr, eq_scr, rank_scr, bias_scr, al_scr, acc_scr, m_scr, l_scr, *, tq, topk):
    qi = pl.program_id(1)
    nstrip = tq // STRIP
    lane = lax.broadcasted_iota(I32, (tq, LANE), 1)
    kk = lax.broadcasted_iota(I32, (STRIP, tq), 0)
    qq = lax.broadcasted_iota(I32, (STRIP, tq), 1)

    qcat = []
    for ref in (iq0_ref, iq1_ref):
        for half in range(2):
            x = ref[0] if half == 0 else pltpu.roll(ref[0], IDX_DIM, 1)
            hi = x.astype(BF16).astype(F32)
            first = jnp.where(lane < IDX_DIM, hi, pltpu.roll(x - hi, IDX_DIM, 1))
            second = jnp.where(lane < IDX_DIM, hi, 0.0)
            qcat.append(jnp.concatenate([first, second], axis=1).astype(BF16))
    sr = lax.broadcasted_iota(I32, (8, LANE), 0)
    sl = lax.broadcasted_iota(I32, (8, LANE), 1)
    onehot = jnp.where(sl == sr + LANE_IW, 1.0, 0.0).astype(BF16)
    wh, wm, wl = _split3(ixq_ref[0])
    wt = _nt(onehot, wh) + _nt(onehot, wm) + _nt(onehot, wl)

    def score_chunk(c, diag, counts):
        ge0, gt0 = counts
        ks = pl.multiple_of(c * tq, tq)
        kc = kcat_ref[0, pl.ds(ks, tq), :]
        for h in range(IDX_HEADS):
            s_scr[0, h] = _nt(kc, qcat[h])
        for i in range(nstrip):
            r0 = i * STRIP
            sc = jnp.zeros((STRIP, tq), F32)
            for h in range(IDX_HEADS):
                sc = sc + wt[h:h + 1, :] * jnp.maximum(s_scr[0, h, r0:r0 + STRIP, :], 0.0)
            sc = jnp.where(sc == 0.0, 0.0, sc)
            bits = lax.bitcast_convert_type(sc, I32)
            key = jnp.where(bits < 0, bits ^ jnp.int32(0x7FFFFFFF), bits)
            if diag:
                key = jnp.where(kk + r0 <= qq, key, INT_MIN)
            key_scr[c, r0:r0 + STRIP, :] = key
            ge0 = ge0 + _fold8(jnp.where(key >= 0, 1.0, 0.0), jnp.add)
            gt0 = gt0 + _fold8(jnp.where(key > 0, 1.0, 0.0), jnp.add)
        return ge0, gt0

    zeros = jnp.zeros((8, tq), F32)
    counts = lax.fori_loop(0, qi, lambda c, counts: score_chunk(c, False, counts), (zeros, zeros))
    ge0, gt0 = score_chunk(qi, True, counts)
    n_ge0 = jnp.sum(ge0, axis=0, keepdims=True)
    n_gt0 = jnp.sum(gt0, axis=0, keepdims=True)

    def count(pred):
        def body(c, acc):
            for i in range(nstrip):
                acc = jnp.where(pred(key_scr[c, i * STRIP:(i + 1) * STRIP, :]), acc + 1.0, acc)
            return acc
        acc = lax.fori_loop(0, qi + 1, body, jnp.zeros((STRIP, tq), F32))
        return jnp.sum(acc, axis=0, keepdims=True)

    kf = jnp.float32(topk)
    th0 = jnp.where(n_ge0 >= kf, 0, INT_MIN).astype(I32)
    settled0 = jnp.where(n_ge0 >= kf, jnp.where(n_gt0 < kf, 1.0, 0.0), 0.0)
    n_up0 = jnp.where(n_ge0 >= kf, jnp.where(n_gt0 < kf, n_gt0, 0.0), n_ge0)

    def unsettled(settled):
        return jnp.min(settled) == 0.0

    def search_cond(c):
        return jnp.logical_and(c[0] >= 0, c[4])

    def search_body(c):
        bit, th, n_up, settled, _ = c
        cand = th | lax.shift_left(jnp.int32(1), bit)
        n = count(lambda kb: kb >= cand)
        open_ = settled == 0.0
        th = jnp.where(n >= kf, jnp.where(open_, cand, th), th)
        n_up = jnp.where(n < kf, jnp.where(open_, n, n_up), n_up)
        settled = jnp.maximum(settled, jnp.where(n == kf, 1.0, 0.0))
        return bit - 1, th, n_up, settled, unsettled(settled)

    _, th, n_up, _, _ = lax.while_loop(search_cond, search_body,
                                       (jnp.int32(30), th0, n_up0, settled0, unsettled(settled0)))
    need = kf - n_up

    qh = sum((_head_halves((r[0].astype(F32) * (0.125 * LOG2E)).astype(BF16)) for r in (q0_ref, q1_ref)), ())
    k_refs = (k0_ref, k0_ref, k1_ref, k1_ref)
    m_scr[...] = jnp.full(m_scr.shape, NEG_BIG, F32)
    l_scr[...] = jnp.zeros(l_scr.shape, F32)
    acc_scr[...] = jnp.zeros(acc_scr.shape, F32)
    row = lax.broadcasted_iota(I32, (tq, tq), 0)
    col = lax.broadcasted_iota(I32, (tq, tq), 1)
    upto = jnp.where(col <= row, 1.0, 0.0).astype(BF16)

    def issue(c, s_buf, eq_buf, rank_buf):
        ks = pl.multiple_of(c * tq, tq)
        for i in range(nstrip):
            r0 = i * STRIP
            eq_buf[r0:r0 + STRIP, :] = jnp.where(key_scr[c, r0:r0 + STRIP, :] == th, 1.0, 0.0).astype(BF16)
        rank_buf[...] = _dot(upto, eq_buf[...])
        for h in range(B_HEADS):
            s_buf[h] = _nt(k_refs[h][0, pl.ds(ks, tq), :], qh[h])

    def softmax(c, s_buf, rank_buf, p_buf, al_buf, diag, seen):
        for i in range(nstrip):
            r0 = i * STRIP
            kb = key_scr[c, r0:r0 + STRIP, :]
            tie = jnp.where(rank_buf[r0:r0 + STRIP, :] + seen <= need, 0.0, NEG_BIG)
            bias = jnp.where(kb > th, 0.0, jnp.where(kb == th, tie, NEG_BIG))
            if diag:
                bias = jnp.where(kk + r0 <= qq, bias, NEG_BIG)
            bias_scr[r0:r0 + STRIP, :] = bias
        for h in range(B_HEADS):
            def pass1(i, mx):
                r0 = i * STRIP
                s = s_buf[h, r0:r0 + STRIP, :] + bias_scr[r0:r0 + STRIP, :]
                s_buf[h, r0:r0 + STRIP, :] = s
                return jnp.maximum(mx, _fold8(s, jnp.maximum))

            mx = _static_loop(nstrip, pass1, jnp.full((8, tq), NEG_BIG, F32))
            m_old = m_scr[h]
            m_new = jnp.maximum(m_old, jnp.max(mx, axis=0, keepdims=True))
            al_buf[h] = jnp.exp2(m_old - m_new)
            m_scr[h] = m_new

            def pass2(i, ls):
                r0 = i * STRIP
                p = jnp.exp2(s_buf[h, r0:r0 + STRIP, :] - m_new)
                p_buf[h, r0:r0 + STRIP, :] = p.astype(BF16)
                return ls + _fold8(p, jnp.add)

            ls = _static_loop(nstrip, pass2, jnp.zeros((8, tq), F32))
            l_scr[h] = al_buf[h] * l_scr[h] + jnp.sum(ls, axis=0, keepdims=True)
        return seen + rank_buf[tq - 1:tq, :]

    def apply_values(c, p_buf, al_buf):
        for h in range(B_HEADS):
            lo = (h % 2) * HEAD_DIM
            acc_scr[h] = acc_scr[h] * al_buf[h] + _dot(vt_ref[0, h // 2, c, lo:lo + HEAD_DIM, :], p_buf[h])

    s_a, s_b = s_scr.at[0], s_scr.at[1]
    p_a, p_b = p_scr.at[0], p_scr.at[1]
    eq_a, eq_b = eq_scr.at[0], eq_scr.at[1]
    rank_a, rank_b = rank_scr.at[0], rank_scr.at[1]
    al_a, al_b = al_scr.at[0], al_scr.at[1]
    p_b[...] = jnp.zeros(p_b.shape, BF16)
    al_b[...] = jnp.ones(al_b.shape, F32)
    issue(0, s_a, eq_a, rank_a)

    def pair(i, seen):
        a = 2 * i
        issue(a + 1, s_b, eq_b, rank_b)
        seen = softmax(a, s_a, rank_a, p_a, al_a, False, seen)
        apply_values(jnp.maximum(a - 1, 0), p_b, al_b)
        issue(a + 2, s_a, eq_a, rank_a)
        seen = softmax(a + 1, s_b, rank_b, p_b, al_b, False, seen)
        apply_values(a, p_a, al_a)
        return seen

    seen = lax.fori_loop(0, qi // 2, pair, jnp.zeros((1, tq), F32))
    last = 2 * (qi // 2)

    @pl.when(qi % 2 == 1)
    def _():
        issue(qi, s_b, eq_b, rank_b)
        seen_d = softmax(last, s_a, rank_a, p_a, al_a, False, seen)
        apply_values(jnp.maximum(last - 1, 0), p_b, al_b)
        softmax(qi, s_b, rank_b, p_b, al_b, True, seen_d)
        apply_values(last, p_a, al_a)
        apply_values(qi, p_b, al_b)

    @pl.when(qi % 2 == 0)
    def _():
        softmax(qi, s_a, rank_a, p_a, al_a, True, seen)
        apply_values(jnp.maximum(qi - 1, 0), p_b, al_b)
        apply_values(qi, p_a, al_a)

    for h in range(B_HEADS):
        lo = (h % 2) * HEAD_DIM
        o_ref[0, h // 2, lo:lo + HEAD_DIM, :] = (acc_scr[h] * (1.0 / l_scr[h])).astype(o_ref.dtype)


def _dsa(p16, p32, kcat, vt, tq):
    bsz, t, _ = p16.shape
    topk = min(DSA_TOPK, t // 4)
    npair = B_HEADS // 2
    nblk = t // tq
    qspec = lambda arr_blk: pl.BlockSpec((1, tq, LANE), lambda b, i: (b, i, arr_blk))
    once = pl.Buffered(1)
    kspec = lambda arr_blk: pl.BlockSpec((1, t, LANE), lambda b, i: (b, 0, arr_blk), pipeline_mode=once)
    sq = lambda n, dt: pltpu.VMEM((n, tq, tq), dt) if n else pltpu.VMEM((tq, tq), dt)
    return pl.pallas_call(
        functools.partial(_dsa_kernel, tq=tq, topk=topk),
        grid=(bsz, nblk),
        in_specs=[qspec(BLK_IQ), qspec(BLK_IQ + 1), qspec(BLK_IX),
                  pl.BlockSpec((1, t, 2 * LANE), lambda b, i: (b, 0, 0), pipeline_mode=once),
                  qspec(BLK_BQ), qspec(BLK_BQ + 1), kspec(BLK_BK), kspec(BLK_BK + 1),
                  pl.BlockSpec((1, npair, nblk, LANE, tq), lambda b, i: (b, 0, 0, 0, 0), pipeline_mode=once)],
        out_specs=pl.BlockSpec((1, npair, LANE, tq), lambda b, i: (b, 0, 0, i)),
        out_shape=jax.ShapeDtypeStruct((bsz, npair, LANE, t), BF16),
        scratch_shapes=[sq(nblk, I32), pltpu.VMEM((2, B_HEADS, tq, tq), F32), pltpu.VMEM((2, B_HEADS, tq, tq), BF16),
                        sq(2, BF16), sq(2, F32), sq(0, F32), pltpu.VMEM((2, B_HEADS, 1, tq), F32),
                        pltpu.VMEM((B_HEADS, HEAD_DIM, tq), F32), pltpu.VMEM((B_HEADS, 1, tq), F32),
                        pltpu.VMEM((B_HEADS, 1, tq), F32)],
        compiler_params=_params(("parallel", "arbitrary")),
        name="dsa_topk_attention",
    )(p32, p32, p32, kcat, p16, p16, p16, p16, vt)


def _merge_kernel(x_ref, sh_ref, sc_ref, gt_ref, oa0_ref, oa1_ref, oa2_ref, la0_ref, la1_ref, la2_ref,
                  yb_ref, yc_ref, yd_ref, wg_ref, bg_ref, wb_ref, wo_ref, lg_ref, lb_ref, o_ref, *, alpha):
    x = x_ref[0]
    d = x.shape[1]
    u = (x * (1.0 + sc_ref[0]) + sh_ref[0]).astype(BF16)
    lses = [r[0] for r in (la0_ref, la1_ref, la2_ref)]
    top = jnp.maximum(jnp.maximum(lses[0], lses[1]), lses[2])
    es = [jnp.exp(l - top) for l in lses]
    den = es[0] + es[1] + es[2]
    ya = sum((e / den) * r[0].astype(F32) for e, r in zip(es, (oa0_ref, oa1_ref, oa2_ref)))
    branches = (ya.astype(BF16), yb_ref[0], yc_ref[0], yd_ref[0])
    merged = jnp.zeros(x.shape, F32)
    off = 0
    for i, br in enumerate(branches):
        gate = jax.nn.sigmoid(_dot(u, wg_ref[:, i * d:(i + 1) * d]) + bg_ref[:, i * d:(i + 1) * d])
        merged = merged + gate * _dot(br, wb_ref[off:off + BRANCH_WIDTHS[i], :])
        off += BRANCH_WIDTHS[i]
    h = _dot(merged.astype(BF16), wo_ref[...])
    o_ref[0] = _deepnorm_ln(x, h, gt_ref[0], lg_ref[...], lb_ref[...], alpha)


def _merge(x, shift, scale, gate, oas, las, yb, yc, yd, wg, bg, wb, wo, ln_g, ln_b, alpha):
    bsz, t, d = x.shape
    tm = _pick(t, (256, 128))
    row = lambda b, i: (b, 0, 0)
    tok = lambda w: pl.BlockSpec((1, tm, w), lambda b, i: (b, i, 0))
    wspec = lambda shape: pl.BlockSpec(shape, lambda b, i: (0, 0), pipeline_mode=pl.Buffered(1))
    return pl.pallas_call(
        functools.partial(_merge_kernel, alpha=alpha),
        grid=(bsz, t // tm),
        in_specs=[tok(d), pl.BlockSpec((1, 1, d), row), pl.BlockSpec((1, 1, d), row), pl.BlockSpec((1, 1, d), row),
                  tok(LANE), tok(LANE), tok(LANE), tok(LANE), tok(LANE), tok(LANE),
                  tok(2 * LANE), tok(2 * LANE), tok(2 * LANE),
                  wspec(wg.shape), wspec(bg.shape), wspec(wb.shape), wspec(wo.shape),
                  pl.BlockSpec((1, d), lambda b, i: (0, 0)), pl.BlockSpec((1, d), lambda b, i: (0, 0))],
        out_specs=tok(d),
        out_shape=jax.ShapeDtypeStruct((bsz, t, d), F32),
        compiler_params=_params(("parallel", "parallel")),
        name="gated_merge",
    )(x, shift, scale, gate, *oas, *las, yb, yc, yd, wg, bg, wb, wo, ln_g, ln_b)


def _rope_tables(t):
    half = ROT_DIM // 2
    inv_freq = ROPE_THETA ** (-(jnp.arange(half, dtype=F32) * (2.0 / ROT_DIM)))
    ang = jnp.arange(t, dtype=F32)[:, None] * inv_freq[None, :]
    cos, sin = jnp.cos(ang), jnp.sin(ang)
    ones = jnp.ones((t, HEAD_DIM - ROT_DIM), F32)
    zeros = jnp.zeros((t, HEAD_DIM - half), F32)
    c64 = jnp.concatenate([cos, cos, ones], axis=1)
    s1 = jnp.concatenate([-sin, zeros], axis=1)
    s2 = jnp.concatenate([jnp.zeros((t, half), F32), sin, zeros[:, half:]], axis=1)
    return tuple(jnp.concatenate([a, a], axis=1) for a in (c64, s1, s2))


def _mixer_weights(w_in):
    d = w_in.shape[0]

    def qkv(off, heads):
        w = heads * HEAD_DIM
        return w_in[:, off:off + w], w_in[:, off + w:off + 2 * w], w_in[:, off + 2 * w:off + 3 * w]

    aq, ak, av = qkv(OFF_A, A_HEADS)
    bq, bk, bv = qkv(OFF_B, B_HEADS)
    cq, ck, cv = qkv(OFF_C, C_HEADS)
    dq, dk, dv = qkv(OFF_D, D_HEADS)
    w16 = jnp.concatenate([aq, ak, bq, bk, av, bv, cq, ck, cv, dq, dk, dv, jnp.zeros((d, LANE), F32)], axis=1)
    ix = jnp.concatenate([w_in[:, OFF_IK:OFF_IK + IDX_DIM], jnp.zeros((d, LANE_IW - IDX_DIM), F32),
                          w_in[:, OFF_IW:OFF_IW + IDX_HEADS], w_in[:, OFF_FG:OFF_FG + D_HEADS],
                          jnp.zeros((d, LANE - LANE_FG - D_HEADS), F32)], axis=1)
    w32 = jnp.concatenate([w_in[:, OFF_IQ:OFF_IQ + IDX_HEADS * IDX_DIM], ix, jnp.zeros((d, LANE), F32)], axis=1)
    return w16.astype(BF16), w32.astype(BF16), w_in[:, OFF_GATE:].astype(BF16)


def _to_residues(a, dil):
    bsz, t, w = a.shape
    if dil == 1:
        return a
    return a.reshape(bsz, t // dil, dil, w).transpose(0, 2, 1, 3).reshape(bsz * dil, t // dil, w)


def _from_residues(a, dil, bsz):
    if dil == 1:
        return a
    _, ln, w = a.shape
    return a.reshape(bsz, dil, ln, w).transpose(0, 2, 1, 3).reshape(bsz, ln * dil, w)


def _mixer(x, shift, scale, gate, w_in, b_gate, b_forget, w_branch, w_out, ln_g, ln_b, tables, alpha):
    bsz, t, d = x.shape
    w16, w32, wg = _mixer_weights(w_in)
    p16 = _proj(x, shift, scale, w16, tables, N_ROPE_BLKS_16, BF16, "mixer_proj_bf16")
    p32 = _proj(x, shift, scale, w32, tables, N_BLKS_32, F32, "mixer_proj_f32")

    oas, las = [], []
    for g, (_, dil) in enumerate(DILATED_GROUPS):
        if dil == 1:
            o, lse = _dilated(p16, p16, p16, cols=(BLK_AQ + g, BLK_AK + g, BLK_AV + g))
        else:
            streams = [_to_residues(p16[:, :, (blk + g) * LANE:(blk + g + 1) * LANE], dil)
                       for blk in (BLK_AQ, BLK_AK, BLK_AV)]
            o, lse = _dilated(*streams)
        oas.append(_from_residues(o, dil, bsz))
        las.append(_from_residues(lse, dil, bsz))

    tq = _pick(t, (256, 128))
    bias_row = jnp.zeros((1, LANE), F32).at[0, LANE_FG:LANE_FG + D_HEADS].set(b_forget)
    fbias, kcat = _cum(p32, bias_row, tq)
    tqf = _pick(t, (512, 256, 128))
    yd = _token_major(_fox(p16, _transposed_values(p16, BLK_DV, D_HEADS // 2, tqf), fbias, tqf))
    yc = _token_major(_stick_breaking(p16, _transposed_values(p16, BLK_CV, C_HEADS // 2, tq), tq))
    yb = _token_major(_dsa(p16, p32, kcat, _transposed_values(p16, BLK_BV, B_HEADS // 2, tqf), tqf))
    return _merge(x, shift, scale, gate, oas, las, yb, yc, yd, wg, b_gate.reshape(1, -1),
                  w_branch.astype(BF16), w_out.astype(BF16), ln_g, ln_b, alpha)


def kernel(x, c, ada_w, ada_b, ln_g, ln_b, ffn_w_in, ffn_w_out, mix_w_in, mix_b_gate, mix_b_forget,
           mix_w_branch, mix_w_out):
    bsz, t, d = x.shape
    depth = ada_w.shape[0]
    f = ffn_w_out.shape[2]
    alpha = float((2 * depth) ** 0.25)
    assert t % (DILATED_GROUPS[-1][0]) == 0 and d % LANE == 0 and bsz <= 8

    c8 = jnp.zeros((8, d), F32).at[:bsz].set(c)
    mod = _ada(c8, ada_w, ada_b)[:, :bsz].reshape(depth, bsz, 3, 3, 1, d)
    tables = _rope_tables(t)

    for l in range(depth):
        m = lambda sub, kind: mod[l, :, sub, kind]
        lng = lambda sub: ln_g[l, sub].reshape(1, d)
        lnb = lambda sub: ln_b[l, sub].reshape(1, d)

        def ffn(x, sub, which):
            w_in = ffn_w_in[l, which].astype(BF16)
            return _ffn(x, m(sub, 0), m(sub, 1), m(sub, 2), w_in[:, :f], w_in[:, f:],
                        ffn_w_out[l, which].astype(BF16), lng(sub), lnb(sub), alpha)

        x = ffn(x, 0, 0)
        x = _mixer(x, m(1, 0), m(1, 1), m(1, 2), mix_w_in[l], mix_b_gate[l], mix_b_forget[l],
                   mix_w_branch[l], mix_w_out[l], lng(1), lnb(1), tables, alpha)
        x = ffn(x, 2, 1)
    return x
```

```python
import functools

import jax
import jax.numpy as jnp
from jax import lax
from jax.experimental import pallas as pl
from jax.experimental.pallas import tpu as pltpu

F32 = jnp.float32
BF16 = jnp.bfloat16
I32 = jnp.int32

LANE = 128
HEAD_DIM = 64
ROT_DIM = HEAD_DIM // 4
ROPE_THETA = 500000.0
DILATED_GROUPS = ((128, 1), (512, 4), (2048, 16))
SPAN = 128
A_HEADS, B_HEADS, C_HEADS, D_HEADS = 6, 4, 4, 4
IDX_HEADS, IDX_DIM = 4, 64
DSA_TOPK = 256
N_BRANCH = 4
LN_EPS = 1e-5
BRANCH_WIDTHS = (128, 256, 256, 256)

OFF_A = 0
OFF_B = OFF_A + 3 * A_HEADS * HEAD_DIM
OFF_IQ = OFF_B + 3 * B_HEADS * HEAD_DIM
OFF_IK = OFF_IQ + IDX_HEADS * IDX_DIM
OFF_IW = OFF_IK + IDX_DIM
OFF_C = OFF_IW + IDX_HEADS
OFF_D = OFF_C + 3 * C_HEADS * HEAD_DIM
OFF_FG = OFF_D + 3 * D_HEADS * HEAD_DIM
OFF_GATE = OFF_FG + D_HEADS

BLK_AQ, BLK_AK, BLK_BQ, BLK_BK = 0, 3, 6, 8
N_ROPE_BLKS_16 = 10
BLK_AV, BLK_BV = 10, 13
BLK_CQ, BLK_CK, BLK_CV = 15, 17, 19
BLK_DQ, BLK_DK, BLK_DV = 21, 23, 25
N_BLKS_16 = 28
BLK_IQ, BLK_IX = 0, 2
N_BLKS_32 = 4
LANE_IW = 80
LANE_FG = 84

INT_MIN = -2147483648
NEG_BIG = -1e30
LOG2E = 1.4426950408889634
EXP_DEAD = -104.0
STRIP = 32
VMEM_LIMIT = 56 * 1024 * 1024


def _nt(a, b):
    return lax.dot_general(a, b, (((1,), (1,)), ((), ())), preferred_element_type=F32)


def _dot(a, b):
    return jnp.dot(a, b, preferred_element_type=F32)


def _split3(x):
    hi = x.astype(BF16)
    r1 = x - hi.astype(F32)
    mid = r1.astype(BF16)
    lo = (r1 - mid.astype(F32)).astype(BF16)
    return hi, mid, lo


def _split2(x):
    hi = x.astype(BF16)
    return hi, (x - hi.astype(F32)).astype(BF16)


def _log_sigmoid(z):
    return -(jnp.maximum(-z, 0.0) + jnp.log(1.0 + jnp.exp(-jnp.abs(z))))


def _params(sem):
    return pltpu.CompilerParams(dimension_semantics=sem, vmem_limit_bytes=VMEM_LIMIT)


def _const_spec(shape):
    n = len(shape)
    return pl.BlockSpec(shape, lambda *_: (0,) * n)


def _pick(n, prefs):
    for p in prefs:
        if n % p == 0:
            return p
    return n


def _ada_kernel(c_ref, w_ref, b_ref, o_ref):
    c = c_ref[...]
    cond = c * jax.nn.sigmoid(c)
    ch, cm, _ = _split3(cond)
    wh, wm, _ = _split3(w_ref[0])
    o_ref[0] = _dot(ch, wh) + _dot(ch, wm) + _dot(cm, wh) + b_ref[0]


def _ada(c8, ada_w, ada_b):
    depth, d, n = ada_w.shape
    tn = _pick(n, (1152, 1024, 512, 256, 128))
    return pl.pallas_call(
        _ada_kernel,
        grid=(depth, n // tn),
        in_specs=[pl.BlockSpec((8, d), lambda l, j: (0, 0)),
                  pl.BlockSpec((1, d, tn), lambda l, j: (l, 0, j)),
                  pl.BlockSpec((1, 1, tn), lambda l, j: (l, 0, j))],
        out_specs=pl.BlockSpec((1, 8, tn), lambda l, j: (l, 0, j)),
        out_shape=jax.ShapeDtypeStruct((depth, 8, n), F32),
        compiler_params=_params(("parallel", "parallel")),
        name="ada_mod",
    )(c8, ada_w, ada_b.reshape(depth, 1, n))


def _deepnorm_ln(x, h, gate, g, b, alpha):
    y = alpha * x + (1.0 + gate) * h
    mu = jnp.mean(y, axis=-1, keepdims=True)
    yc = y - mu
    var = jnp.mean(yc * yc, axis=-1, keepdims=True)
    return yc * lax.rsqrt(var + LN_EPS) * g + b


def _ffn_kernel(x_ref, sh_ref, sc_ref, gt_ref, wg_ref, wu_ref, wo_ref, lg_ref, lb_ref, o_ref, *, tf, alpha):
    x = x_ref[0]
    u = (x * (1.0 + sc_ref[0]) + sh_ref[0]).astype(BF16)
    acc = jnp.zeros(x.shape, F32)
    for j in range(wg_ref.shape[1] // tf):
        g = _dot(u, wg_ref[:, j * tf:(j + 1) * tf])
        up = _dot(u, wu_ref[:, j * tf:(j + 1) * tf])
        h = (g * jax.nn.sigmoid(g) * up).astype(BF16)
        acc = acc + _dot(h, wo_ref[j * tf:(j + 1) * tf, :])
    o_ref[0] = _deepnorm_ln(x, 0.5 * acc, gt_ref[0], lg_ref[...], lb_ref[...], alpha)


def _ffn(x, shift, scale, gate, wg, wu, wo, ln_g, ln_b, alpha):
    bsz, t, d = x.shape
    f = wg.shape[1]
    tm = _pick(t, (512, 256, 128))
    tf = _pick(f, (256, 128))
    row = lambda b, i: (b, 0, 0)
    wspec = lambda shape: pl.BlockSpec(shape, lambda b, i: (0, 0), pipeline_mode=pl.Buffered(1))
    return pl.pallas_call(
        functools.partial(_ffn_kernel, tf=tf, alpha=alpha),
        grid=(bsz, t // tm),
        in_specs=[pl.BlockSpec((1, tm, d), lambda b, i: (b, i, 0)),
                  pl.BlockSpec((1, 1, d), row), pl.BlockSpec((1, 1, d), row), pl.BlockSpec((1, 1, d), row),
                  wspec((d, f)), wspec((d, f)), wspec((f, d)),
                  pl.BlockSpec((1, d), lambda b, i: (0, 0)), pl.BlockSpec((1, d), lambda b, i: (0, 0))],
        out_specs=pl.BlockSpec((1, tm, d), lambda b, i: (b, i, 0)),
        out_shape=jax.ShapeDtypeStruct((bsz, t, d), F32),
        compiler_params=_params(("parallel", "parallel")),
        name="ffn",
    )(x, shift, scale, gate, wg, wu, wo, ln_g, ln_b)


def _proj_kernel(x_ref, sh_ref, sc_ref, w16_ref, w32_ref, cos_ref, s1_ref, s2_ref, o16_ref, o32_ref, *, tn):
    u = (x_ref[0] * (1.0 + sc_ref[0]) + sh_ref[0]).astype(BF16)
    per = tn // LANE
    for w_ref, o_ref, n_rope_blks in ((w16_ref, o16_ref, N_ROPE_BLKS_16), (w32_ref, o32_ref, N_BLKS_32)):
        for j in range(w_ref.shape[1] // tn):
            y = _dot(u, w_ref[:, j * tn:(j + 1) * tn])
            for i in range(per):
                blk = j * per + i
                yi = y[:, i * LANE:(i + 1) * LANE]
                if blk < n_rope_blks:
                    yi = (yi * cos_ref[...] + pltpu.roll(yi, LANE - ROT_DIM // 2, 1) * s1_ref[...]
                          + pltpu.roll(yi, ROT_DIM // 2, 1) * s2_ref[...])
                o_ref[0, :, blk * LANE:(blk + 1) * LANE] = yi.astype(o_ref.dtype)


def _proj(x, shift, scale, w16, w32, tables):
    bsz, t, d = x.shape
    n16, n32 = w16.shape[1], w32.shape[1]
    tm = _pick(t, (512, 256, 128))
    row = lambda b, i: (b, 0, 0)
    tab = pl.BlockSpec((tm, LANE), lambda b, i: (i, 0))
    wspec = lambda n: pl.BlockSpec((d, n), lambda b, i: (0, 0), pipeline_mode=pl.Buffered(1))
    return pl.pallas_call(
        functools.partial(_proj_kernel, tn=2 * LANE),
        grid=(bsz, t // tm),
        in_specs=[pl.BlockSpec((1, tm, d), lambda b, i: (b, i, 0)),
                  pl.BlockSpec((1, 1, d), row), pl.BlockSpec((1, 1, d), row),
                  wspec(n16), wspec(n32), tab, tab, tab],
        out_specs=[pl.BlockSpec((1, tm, n16), lambda b, i: (b, i, 0)),
                   pl.BlockSpec((1, tm, n32), lambda b, i: (b, i, 0))],
        out_shape=[jax.ShapeDtypeStruct((bsz, t, n16), BF16), jax.ShapeDtypeStruct((bsz, t, n32), F32)],
        compiler_params=_params(("parallel", "parallel")),
        name="mixer_proj",
    )(x, shift, scale, w16, w32, *tables)


def _cum_kernel(p_ref, bias_ref, out_ref, kcat_ref, carry_ref):
    @pl.when(pl.program_id(1) == 0)
    def _():
        carry_ref[...] = jnp.zeros_like(carry_ref)

    tc = p_ref.shape[1]
    lane = lax.broadcasted_iota(I32, (tc, LANE), 1)
    lf = _log_sigmoid(p_ref[0] + bias_ref[...])
    lf = jnp.where(lane >= LANE_FG, jnp.where(lane < LANE_FG + D_HEADS, lf, 0.0), 0.0)
    ri = lax.broadcasted_iota(I32, (tc, tc), 0)
    ci = lax.broadcasted_iota(I32, (tc, tc), 1)
    tri = jnp.where(ri >= ci, 1.0, 0.0).astype(BF16)
    hi, mid, lo = _split3(lf)
    cum = _dot(tri, hi) + _dot(tri, mid) + _dot(tri, lo) + carry_ref[...]
    carry_ref[...] = cum[tc - 1:tc, :]
    er = lax.broadcasted_iota(I32, (LANE, LANE), 0)
    ch, cm, cl = _split3(-LOG2E * cum)
    for h in range(D_HEADS):
        onehot = jnp.where(er == LANE_FG + h, 1.0, 0.0).astype(BF16)
        out_ref[0, h] = _dot(ch, onehot) + _dot(cm, onehot) + _dot(cl, onehot)
    x = p_ref[0]
    hi = x.astype(BF16).astype(F32)
    kcat_ref[0, :, 0:LANE] = jnp.where(lane < IDX_DIM, hi, pltpu.roll(hi, IDX_DIM, 1)).astype(BF16)
    kcat_ref[0, :, LANE:2 * LANE] = jnp.where(lane < IDX_DIM, x - hi, 0.0).astype(BF16)


def _cum(p32, bias_row, tc):
    bsz, t, _ = p32.shape
    return pl.pallas_call(
        _cum_kernel,
        grid=(bsz, t // tc),
        in_specs=[pl.BlockSpec((1, tc, LANE), lambda b, i: (b, i, BLK_IX)),
                  pl.BlockSpec((1, LANE), lambda b, i: (0, 0))],
        out_specs=[pl.BlockSpec((1, D_HEADS, tc, LANE), lambda b, i: (b, 0, i, 0)),
                   pl.BlockSpec((1, tc, 2 * LANE), lambda b, i: (b, i, 0))],
        out_shape=[jax.ShapeDtypeStruct((bsz, D_HEADS, t, LANE), F32),
                   jax.ShapeDtypeStruct((bsz, t, 2 * LANE), BF16)],
        scratch_shapes=[pltpu.VMEM((1, LANE), F32)],
        compiler_params=_params(("parallel", "arbitrary")),
        name="forget_cumsum",
    )(p32, bias_row)


def _head_halves(q):
    lane = lax.broadcasted_iota(I32, q.shape, 1)
    zero = jnp.zeros_like(q)
    return jnp.where(lane < HEAD_DIM, q, zero), jnp.where(lane >= HEAD_DIM, q, zero)


def _static_loop(n, body, carry):
    for i in range(n):
        carry = body(i, carry)
    return carry


def _fold8(x, op):
    out = x[0:8]
    for r in range(8, x.shape[0], 8):
        out = op(out, x[r:r + 8])
    return out


def _transposed_values(p16, blk, npair, tq):
    bsz, t, _ = p16.shape
    v = p16[:, :, blk * LANE:(blk + npair) * LANE].reshape(bsz, t // tq, tq, npair, LANE)
    return v.transpose(0, 3, 1, 4, 2)


def _lane_pick(a, b):
    lane = lax.broadcasted_iota(I32, a.shape, 1)
    return jnp.where(lane < HEAD_DIM, a, b)


def _dil_kernel(q_ref, k_ref, v_ref, o_ref, lse_ref, *, kw):
    nb = q_ref.shape[1] // SPAN

    def body(i, carry):
        qs = pl.multiple_of(i * SPAN, SPAN)
        ks = pl.multiple_of(jnp.maximum(i * SPAN + SPAN - kw, 0), SPAN)
        q = q_ref[0, pl.ds(qs, SPAN), :] * 0.125
        k = k_ref[0, pl.ds(ks, kw), :]
        v = v_ref[0, pl.ds(ks, kw), :]
        qpos = qs + lax.broadcasted_iota(I32, (SPAN, kw), 0)
        kpos = ks + lax.broadcasted_iota(I32, (SPAN, kw), 1)
        dist = qpos - kpos
        outs, lses = [], []
        for qh in _head_halves(q):
            s = _nt(qh, k)
            s = jnp.where(dist >= 0, jnp.where(dist <= SPAN, s, NEG_BIG), NEG_BIG)
            m = jnp.max(s, axis=-1, keepdims=True)
            p = jnp.exp(s - m)
            l = jnp.sum(p, axis=-1, keepdims=True)
            outs.append(_dot(p.astype(BF16), v) / l)
            lses.append(jnp.broadcast_to(m + jnp.log(l), (SPAN, LANE)))
        o_ref[0, pl.ds(qs, SPAN), :] = _lane_pick(outs[0], outs[1]).astype(o_ref.dtype)
        lse_ref[0, pl.ds(qs, SPAN), :] = _lane_pick(lses[0], lses[1])
        return carry

    lax.fori_loop(0, nb, body, 0, unroll=min(nb, 4))


def _dilated(q, k, v, cols=(0, 0, 0)):
    ns, ln, _ = q.shape
    kw = min(2 * SPAN, ln)
    spec = pl.BlockSpec((1, ln, LANE), lambda s: (s, 0, 0))
    in_spec = lambda col: pl.BlockSpec((1, ln, LANE), lambda s: (s, 0, col))
    return pl.pallas_call(
        functools.partial(_dil_kernel, kw=kw),
        grid=(ns,),
        in_specs=[in_spec(c) for c in cols],
        out_specs=[spec, spec],
        out_shape=[jax.ShapeDtypeStruct((ns, ln, LANE), BF16), jax.ShapeDtypeStruct((ns, ln, LANE), F32)],
        compiler_params=_params(("parallel",)),
        name="dilated_window",
    )(q, k, v)


def _fox_kernel(q_ref, k_ref, vt_ref, bias_ref, o_ref, s_scr, p_scr, al_scr, acc_scr, m_scr, l_scr, *, tq):
    qi = pl.program_id(2)
    rep = tq // LANE
    nstrip = tq // STRIP
    qh = _head_halves((q_ref[0].astype(F32) * (0.125 * LOG2E)).astype(BF16))
    m_scr[...] = jnp.full(m_scr.shape, NEG_BIG, F32)
    l_scr[...] = jnp.zeros(l_scr.shape, F32)
    acc_scr[...] = jnp.zeros(acc_scr.shape, F32)
    kk = lax.broadcasted_iota(I32, (STRIP, tq), 0)
    qq = lax.broadcasted_iota(I32, (STRIP, tq), 1)

    def issue_scores(j, s_buf):
        kblk = k_ref[0, pl.ds(pl.multiple_of(j * tq, tq), tq), :]
        for h in range(2):
            s_buf[h] = _nt(kblk, qh[h])

    def softmax(j, s_buf, p_buf, al_buf, diag):
        ks = pl.multiple_of(j * tq, tq)
        for h in range(2):
            def pass1(i, mx):
                r0 = i * STRIP
                s = s_buf[h, pl.ds(r0, STRIP), :] + jnp.tile(
                    bias_ref[0, h, pl.ds(pl.multiple_of(ks + r0, STRIP), STRIP), :], (1, rep))
                if diag:
                    s = jnp.where(kk + r0 <= qq, s, NEG_BIG)
                s_buf[h, pl.ds(r0, STRIP), :] = s
                return jnp.maximum(mx, _fold8(s, jnp.maximum))

            mx = _static_loop(nstrip, pass1, jnp.full((8, tq), NEG_BIG, F32))
            m_old = m_scr[h]
            m_new = jnp.maximum(m_old, jnp.max(mx, axis=0, keepdims=True))
            al_buf[h] = jnp.exp2(m_old - m_new)
            m_scr[h] = m_new

            def pass2(i, ls):
                r0 = i * STRIP
                p = jnp.exp2(s_buf[h, pl.ds(r0, STRIP), :] - m_new)
                p_buf[h, pl.ds(r0, STRIP), :] = p.astype(BF16)
                return ls + _fold8(p, jnp.add)

            ls = _static_loop(nstrip, pass2, jnp.zeros((8, tq), F32))
            l_scr[h] = al_buf[h] * l_scr[h] + jnp.sum(ls, axis=0, keepdims=True)

    def apply_values(j, p_buf, al_buf):
        for h in range(2):
            pv = _dot(vt_ref[0, 0, j, h * HEAD_DIM:(h + 1) * HEAD_DIM, :], p_buf[h])
            acc_scr[h] = acc_scr[h] * al_buf[h] + pv

    s_a, s_b = s_scr.at[0], s_scr.at[1]
    p_a, p_b = p_scr.at[0], p_scr.at[1]
    al_a, al_b = al_scr.at[0], al_scr.at[1]
    p_b[...] = jnp.zeros(p_b.shape, BF16)
    al_b[...] = jnp.ones(al_b.shape, F32)
    issue_scores(0, s_a)

    def pair(i, carry):
        a = 2 * i
        issue_scores(a + 1, s_b)
        softmax(a, s_a, p_a, al_a, False)
        apply_values(jnp.maximum(a - 1, 0), p_b, al_b)
        issue_scores(a + 2, s_a)
        softmax(a + 1, s_b, p_b, al_b, False)
        apply_values(a, p_a, al_a)
        return carry

    lax.fori_loop(0, qi // 2, pair, 0)
    last = 2 * (qi // 2)

    @pl.when(qi % 2 == 1)
    def _():
        issue_scores(qi, s_b)
        softmax(last, s_a, p_a, al_a, False)
        apply_values(jnp.maximum(last - 1, 0), p_b, al_b)
        softmax(qi, s_b, p_b, al_b, True)
        apply_values(last, p_a, al_a)
        apply_values(qi, p_b, al_b)

    @pl.when(qi % 2 == 0)
    def _():
        softmax(qi, s_a, p_a, al_a, True)
        apply_values(jnp.maximum(qi - 1, 0), p_b, al_b)
        apply_values(qi, p_a, al_a)

    out_t = jnp.concatenate([acc_scr[h] * (1.0 / l_scr[h]) for h in range(2)], axis=0)
    o_ref[0] = out_t.T.astype(o_ref.dtype)


def _fox(p16, vt, bias, tq):
    bsz, t, _ = p16.shape
    npair = D_HEADS // 2
    nblk = t // tq
    return pl.pallas_call(
        functools.partial(_fox_kernel, tq=tq),
        grid=(bsz, npair, nblk),
        in_specs=[pl.BlockSpec((1, tq, LANE), lambda b, h, i: (b, i, BLK_DQ + h)),
                  pl.BlockSpec((1, t, LANE), lambda b, h, i: (b, 0, BLK_DK + h)),
                  pl.BlockSpec((1, 1, nblk, LANE, tq), lambda b, h, i: (b, h, 0, 0, 0)),
                  pl.BlockSpec((1, 2, t, LANE), lambda b, h, i: (b, h, 0, 0))],
        out_specs=pl.BlockSpec((1, tq, LANE), lambda b, h, i: (b, i, h)),
        out_shape=jax.ShapeDtypeStruct((bsz, t, npair * LANE), BF16),
        scratch_shapes=[pltpu.VMEM((2, 2, tq, tq), F32), pltpu.VMEM((2, 2, tq, tq), BF16),
                        pltpu.VMEM((2, 2, 1, tq), F32),
                        pltpu.VMEM((2, HEAD_DIM, tq), F32), pltpu.VMEM((2, 1, tq), F32),
                        pltpu.VMEM((2, 1, tq), F32)],
        compiler_params=_params(("parallel", "parallel", "arbitrary")),
        name="forgetting_attention",
    )(p16, p16, vt, bias)


def _sb_kernel(q_ref, k_ref, vt_ref, o_ref, z_scr, sfx_scr, hi_scr, lo_scr, a_scr, acc_scr, r_scr, *, tq):
    qi = pl.program_id(2)
    nstrip = tq // STRIP
    qh = _head_halves(q_ref[0] * 0.125)
    acc_scr[...] = jnp.zeros(acc_scr.shape, F32)
    r_scr[...] = jnp.zeros(r_scr.shape, F32)
    kk = lax.broadcasted_iota(I32, (STRIP, tq), 0)
    qq = lax.broadcasted_iota(I32, (STRIP, tq), 1)
    row = lax.broadcasted_iota(I32, (tq, tq), 0)
    col = lax.broadcasted_iota(I32, (tq, tq), 1)
    later = jnp.where(col > row, 1.0, 0.0).astype(BF16)

    def block(j, diag):
        ks = pl.multiple_of(j * tq, tq)
        kblk = k_ref[0, pl.ds(ks, tq), :]
        for h in range(2):
            z_scr[h] = _nt(kblk, qh[h])
        for h in range(2):
            def pass1(i, rsum):
                r0 = i * STRIP
                z = z_scr[h, pl.ds(r0, STRIP), :]
                lk = _log_sigmoid(-z)
                z_scr[h, pl.ds(r0, STRIP), :] = lk + z
                if diag:
                    lk = jnp.where(kk + r0 < qq, lk, 0.0)
                hi = lk.astype(BF16)
                hi_scr[h, pl.ds(r0, STRIP), :] = hi
                lo_scr[h, pl.ds(r0, STRIP), :] = (lk - hi.astype(F32)).astype(BF16)
                return rsum + _fold8(lk, jnp.add)

            rsum = _static_loop(nstrip, pass1, jnp.zeros((8, tq), F32))
            sfx_scr[h] = _dot(later, hi_scr[h]) + _dot(later, lo_scr[h])
            r_old = r_scr[h]

            def pass2(i, carry):
                r0 = i * STRIP
                a = jnp.exp(z_scr[h, pl.ds(r0, STRIP), :] + sfx_scr[h, pl.ds(r0, STRIP), :] + r_old)
                if diag:
                    a = jnp.where(kk + r0 < qq, a, 0.0)
                a_scr[h, pl.ds(r0, STRIP), :] = a.astype(BF16)
                return carry

            _static_loop(nstrip, pass2, 0)
            acc_scr[h] = acc_scr[h] + _dot(vt_ref[0, 0, j, h * HEAD_DIM:(h + 1) * HEAD_DIM, :], a_scr[h])
            r_scr[h] = r_old + jnp.sum(rsum, axis=0, keepdims=True)

    block(qi, True)

    def live():
        return jnp.max(jnp.maximum(r_scr[0], r_scr[1])) > EXP_DEAD

    def cond(c):
        return jnp.logical_and(c[0] >= 0, c[1])

    def body(c):
        block(c[0], False)
        return c[0] - 1, live()

    lax.while_loop(cond, body, (qi - 1, live()))
    o_ref[0] = jnp.concatenate([acc_scr[0], acc_scr[1]], axis=0).T.astype(o_ref.dtype)


def _stick_breaking(p16, vt, tq):
    bsz, t, _ = p16.shape
    npair = C_HEADS // 2
    nblk = t // tq
    return pl.pallas_call(
        functools.partial(_sb_kernel, tq=tq),
        grid=(bsz, npair, nblk),
        in_specs=[pl.BlockSpec((1, tq, LANE), lambda b, h, i: (b, i, BLK_CQ + h)),
                  pl.BlockSpec((1, t, LANE), lambda b, h, i: (b, 0, BLK_CK + h)),
                  pl.BlockSpec((1, 1, nblk, LANE, tq), lambda b, h, i: (b, h, 0, 0, 0))],
        out_specs=pl.BlockSpec((1, tq, LANE), lambda b, h, i: (b, i, h)),
        out_shape=jax.ShapeDtypeStruct((bsz, t, npair * LANE), BF16),
        scratch_shapes=[pltpu.VMEM((2, tq, tq), F32), pltpu.VMEM((2, tq, tq), F32),
                        pltpu.VMEM((2, tq, tq), BF16), pltpu.VMEM((2, tq, tq), BF16),
                        pltpu.VMEM((2, tq, tq), BF16), pltpu.VMEM((2, HEAD_DIM, tq), F32),
                        pltpu.VMEM((2, 1, tq), F32)],
        compiler_params=_params(("parallel", "parallel", "arbitrary")),
        name="stick_breaking",
    )(p16, p16, vt)


def _dsa_kernel(iq0_ref, iq1_ref, ixq_ref, kcat_ref, q0_ref, q1_ref, k0_ref, k1_ref, vt_ref, o_ref,
                key_scr, s_scr, p_scr, eq_scr, rank_scr, bias_scr, al_scr, acc_scr, m_scr, l_scr, *, tq, topk):
    qi = pl.program_id(1)
    nstrip = tq // STRIP
    lane = lax.broadcasted_iota(I32, (tq, LANE), 1)
    kk = lax.broadcasted_iota(I32, (STRIP, tq), 0)
    qq = lax.broadcasted_iota(I32, (STRIP, tq), 1)

    qcat = []
    for ref in (iq0_ref, iq1_ref):
        for half in range(2):
            x = ref[0] if half == 0 else pltpu.roll(ref[0], IDX_DIM, 1)
            hi = x.astype(BF16).astype(F32)
            first = jnp.where(lane < IDX_DIM, hi, pltpu.roll(x - hi, IDX_DIM, 1))
            second = jnp.where(lane < IDX_DIM, hi, 0.0)
            qcat.append(jnp.concatenate([first, second], axis=1).astype(BF16))
    sr = lax.broadcasted_iota(I32, (8, LANE), 0)
    sl = lax.broadcasted_iota(I32, (8, LANE), 1)
    onehot = jnp.where(sl == sr + LANE_IW, 1.0, 0.0).astype(BF16)
    wh, wm, wl = _split3(ixq_ref[0])
    wt = _nt(onehot, wh) + _nt(onehot, wm) + _nt(onehot, wl)

    def score_chunk(c, diag, counts):
        ge0, gt0 = counts
        ks = pl.multiple_of(c * tq, tq)
        kc = kcat_ref[0, pl.ds(ks, tq), :]
        for h in range(IDX_HEADS):
            s_scr[0, h] = _nt(kc, qcat[h])
        for i in range(nstrip):
            r0 = i * STRIP
            sc = jnp.zeros((STRIP, tq), F32)
            for h in range(IDX_HEADS):
                sc = sc + wt[h:h + 1, :] * jnp.maximum(s_scr[0, h, r0:r0 + STRIP, :], 0.0)
            sc = jnp.where(sc == 0.0, 0.0, sc)
            bits = lax.bitcast_convert_type(sc, I32)
            key = jnp.where(bits < 0, bits ^ jnp.int32(0x7FFFFFFF), bits)
            if diag:
                key = jnp.where(kk + r0 <= qq, key, INT_MIN)
            key_scr[c, r0:r0 + STRIP, :] = key
            ge0 = ge0 + _fold8(jnp.where(key >= 0, 1.0, 0.0), jnp.add)
            gt0 = gt0 + _fold8(jnp.where(key > 0, 1.0, 0.0), jnp.add)
        return ge0, gt0

    zeros = jnp.zeros((8, tq), F32)
    counts = lax.fori_loop(0, qi, lambda c, counts: score_chunk(c, False, counts), (zeros, zeros))
    ge0, gt0 = score_chunk(qi, True, counts)
    n_ge0 = jnp.sum(ge0, axis=0, keepdims=True)
    n_gt0 = jnp.sum(gt0, axis=0, keepdims=True)

    def count(pred):
        def body(c, acc):
            for i in range(nstrip):
                acc = jnp.where(pred(key_scr[c, i * STRIP:(i + 1) * STRIP, :]), acc + 1.0, acc)
            return acc
        acc = lax.fori_loop(0, qi + 1, body, jnp.zeros((STRIP, tq), F32))
        return jnp.sum(acc, axis=0, keepdims=True)

    kf = jnp.float32(topk)
    th0 = jnp.where(n_ge0 >= kf, 0, INT_MIN).astype(I32)
    settled0 = jnp.where(n_ge0 >= kf, jnp.where(n_gt0 < kf, 1.0, 0.0), 0.0)
    n_up0 = jnp.where(n_ge0 >= kf, jnp.where(n_gt0 < kf, n_gt0, 0.0), n_ge0)

    def unsettled(settled):
        return jnp.min(settled) == 0.0

    def search_cond(c):
        return jnp.logical_and(c[0] >= 0, c[4])

    def search_body(c):
        bit, th, n_up, settled, _ = c
        cand = th | lax.shift_left(jnp.int32(1), bit)
        n = count(lambda kb: kb >= cand)
        open_ = settled == 0.0
        th = jnp.where(n >= kf, jnp.where(open_, cand, th), th)
        n_up = jnp.where(n < kf, jnp.where(open_, n, n_up), n_up)
        settled = jnp.maximum(settled, jnp.where(n == kf, 1.0, 0.0))
        return bit - 1, th, n_up, settled, unsettled(settled)

    _, th, n_up, _, _ = lax.while_loop(search_cond, search_body,
                                       (jnp.int32(30), th0, n_up0, settled0, unsettled(settled0)))
    need = kf - n_up

    qh = sum((_head_halves((r[0].astype(F32) * (0.125 * LOG2E)).astype(BF16)) for r in (q0_ref, q1_ref)), ())
    k_refs = (k0_ref, k0_ref, k1_ref, k1_ref)
    m_scr[...] = jnp.full(m_scr.shape, NEG_BIG, F32)
    l_scr[...] = jnp.zeros(l_scr.shape, F32)
    acc_scr[...] = jnp.zeros(acc_scr.shape, F32)
    row = lax.broadcasted_iota(I32, (tq, tq), 0)
    col = lax.broadcasted_iota(I32, (tq, tq), 1)
    upto = jnp.where(col <= row, 1.0, 0.0).astype(BF16)

    def issue(c, s_buf, eq_buf, rank_buf):
        ks = pl.multiple_of(c * tq, tq)
        for i in range(nstrip):
            r0 = i * STRIP
            eq_buf[r0:r0 + STRIP, :] = jnp.where(key_scr[c, r0:r0 + STRIP, :] == th, 1.0, 0.0).astype(BF16)
        rank_buf[...] = _dot(upto, eq_buf[...])
        for h in range(B_HEADS):
            s_buf[h] = _nt(k_refs[h][0, pl.ds(ks, tq), :], qh[h])

    def softmax(c, s_buf, rank_buf, p_buf, al_buf, diag, seen):
        for i in range(nstrip):
            r0 = i * STRIP
            kb = key_scr[c, r0:r0 + STRIP, :]
            tie = jnp.where(rank_buf[r0:r0 + STRIP, :] + seen <= need, 0.0, NEG_BIG)
            bias = jnp.where(kb > th, 0.0, jnp.where(kb == th, tie, NEG_BIG))
            if diag:
                bias = jnp.where(kk + r0 <= qq, bias, NEG_BIG)
            bias_scr[r0:r0 + STRIP, :] = bias
        for h in range(B_HEADS):
            def pass1(i, mx):
                r0 = i * STRIP
                s = s_buf[h, r0:r0 + STRIP, :] + bias_scr[r0:r0 + STRIP, :]
                s_buf[h, r0:r0 + STRIP, :] = s
                return jnp.maximum(mx, _fold8(s, jnp.maximum))

            mx = _static_loop(nstrip, pass1, jnp.full((8, tq), NEG_BIG, F32))
            m_old = m_scr[h]
            m_new = jnp.maximum(m_old, jnp.max(mx, axis=0, keepdims=True))
            al_buf[h] = jnp.exp2(m_old - m_new)
            m_scr[h] = m_new

            def pass2(i, ls):
                r0 = i * STRIP
                p = jnp.exp2(s_buf[h, r0:r0 + STRIP, :] - m_new)
                p_buf[h, r0:r0 + STRIP, :] = p.astype(BF16)
                return ls + _fold8(p, jnp.add)

            ls = _static_loop(nstrip, pass2, jnp.zeros((8, tq), F32))
            l_scr[h] = al_buf[h] * l_scr[h] + jnp.sum(ls, axis=0, keepdims=True)
        return seen + rank_buf[tq - 1:tq, :]

    def apply_values(c, p_buf, al_buf):
        for h in range(B_HEADS):
            lo = (h % 2) * HEAD_DIM
            acc_scr[h] = acc_scr[h] * al_buf[h] + _dot(vt_ref[0, h // 2, c, lo:lo + HEAD_DIM, :], p_buf[h])

    s_a, s_b = s_scr.at[0], s_scr.at[1]
    p_a, p_b = p_scr.at[0], p_scr.at[1]
    eq_a, eq_b = eq_scr.at[0], eq_scr.at[1]
    rank_a, rank_b = rank_scr.at[0], rank_scr.at[1]
    al_a, al_b = al_scr.at[0], al_scr.at[1]
    p_b[...] = jnp.zeros(p_b.shape, BF16)
    al_b[...] = jnp.ones(al_b.shape, F32)
    issue(0, s_a, eq_a, rank_a)

    def pair(i, seen):
        a = 2 * i
        issue(a + 1, s_b, eq_b, rank_b)
        seen = softmax(a, s_a, rank_a, p_a, al_a, False, seen)
        apply_values(jnp.maximum(a - 1, 0), p_b, al_b)
        issue(a + 2, s_a, eq_a, rank_a)
        seen = softmax(a + 1, s_b, rank_b, p_b, al_b, False, seen)
        apply_values(a, p_a, al_a)
        return seen

    seen = lax.fori_loop(0, qi // 2, pair, jnp.zeros((1, tq), F32))
    last = 2 * (qi // 2)

    @pl.when(qi % 2 == 1)
    def _():
        issue(qi, s_b, eq_b, rank_b)
        seen_d = softmax(last, s_a, rank_a, p_a, al_a, False, seen)
        apply_values(jnp.maximum(last - 1, 0), p_b, al_b)
        softmax(qi, s_b, rank_b, p_b, al_b, True, seen_d)
        apply_values(last, p_a, al_a)
        apply_values(qi, p_b, al_b)

    @pl.when(qi % 2 == 0)
    def _():
        softmax(qi, s_a, rank_a, p_a, al_a, True, seen)
        apply_values(jnp.maximum(qi - 1, 0), p_b, al_b)
        apply_values(qi, p_a, al_a)

    for g in range(B_HEADS // 2):
        out_t = jnp.concatenate([acc_scr[h] * (1.0 / l_scr[h]) for h in (2 * g, 2 * g + 1)], axis=0)
        o_ref[0, :, g * LANE:(g + 1) * LANE] = out_t.T.astype(o_ref.dtype)


def _dsa(p16, p32, kcat, vt, tq):
    bsz, t, _ = p16.shape
    topk = min(DSA_TOPK, t // 4)
    npair = B_HEADS // 2
    nblk = t // tq
    qspec = lambda arr_blk: pl.BlockSpec((1, tq, LANE), lambda b, i: (b, i, arr_blk))
    once = pl.Buffered(1)
    kspec = lambda arr_blk: pl.BlockSpec((1, t, LANE), lambda b, i: (b, 0, arr_blk), pipeline_mode=once)
    sq = lambda n, dt: pltpu.VMEM((n, tq, tq), dt) if n else pltpu.VMEM((tq, tq), dt)
    return pl.pallas_call(
        functools.partial(_dsa_kernel, tq=tq, topk=topk),
        grid=(bsz, nblk),
        in_specs=[qspec(BLK_IQ), qspec(BLK_IQ + 1), qspec(BLK_IX),
                  pl.BlockSpec((1, t, 2 * LANE), lambda b, i: (b, 0, 0), pipeline_mode=once),
                  qspec(BLK_BQ), qspec(BLK_BQ + 1), kspec(BLK_BK), kspec(BLK_BK + 1),
                  pl.BlockSpec((1, npair, nblk, LANE, tq), lambda b, i: (b, 0, 0, 0, 0), pipeline_mode=once)],
        out_specs=pl.BlockSpec((1, tq, npair * LANE), lambda b, i: (b, i, 0)),
        out_shape=jax.ShapeDtypeStruct((bsz, t, npair * LANE), BF16),
        scratch_shapes=[sq(nblk, I32), pltpu.VMEM((2, B_HEADS, tq, tq), F32), pltpu.VMEM((2, B_HEADS, tq, tq), BF16),
                        sq(2, BF16), sq(2, F32), sq(0, F32), pltpu.VMEM((2, B_HEADS, 1, tq), F32),
                        pltpu.VMEM((B_HEADS, HEAD_DIM, tq), F32), pltpu.VMEM((B_HEADS, 1, tq), F32),
                        pltpu.VMEM((B_HEADS, 1, tq), F32)],
        compiler_params=_params(("parallel", "arbitrary")),
        name="dsa_topk_attention",
    )(p32, p32, p32, kcat, p16, p16, p16, p16, vt)


def _merge_kernel(x_ref, sh_ref, sc_ref, gt_ref, oa0_ref, oa1_ref, oa2_ref, la0_ref, la1_ref, la2_ref,
                  yb_ref, yc_ref, yd_ref, wg_ref, bg_ref, wb_ref, wo_ref, lg_ref, lb_ref, o_ref, *, alpha):
    x = x_ref[0]
    d = x.shape[1]
    u = (x * (1.0 + sc_ref[0]) + sh_ref[0]).astype(BF16)
    lses = [r[0] for r in (la0_ref, la1_ref, la2_ref)]
    top = jnp.maximum(jnp.maximum(lses[0], lses[1]), lses[2])
    es = [jnp.exp(l - top) for l in lses]
    den = es[0] + es[1] + es[2]
    ya = sum((e / den) * r[0].astype(F32) for e, r in zip(es, (oa0_ref, oa1_ref, oa2_ref)))
    branches = (ya.astype(BF16), yb_ref[0], yc_ref[0], yd_ref[0])
    merged = jnp.zeros(x.shape, F32)
    off = 0
    for i, br in enumerate(branches):
        gate = jax.nn.sigmoid(_dot(u, wg_ref[:, i * d:(i + 1) * d]) + bg_ref[:, i * d:(i + 1) * d])
        merged = merged + gate * _dot(br, wb_ref[off:off + BRANCH_WIDTHS[i], :])
        off += BRANCH_WIDTHS[i]
    h = _dot(merged.astype(BF16), wo_ref[...])
    o_ref[0] = _deepnorm_ln(x, h, gt_ref[0], lg_ref[...], lb_ref[...], alpha)


def _merge(x, shift, scale, gate, oas, las, yb, yc, yd, wg, bg, wb, wo, ln_g, ln_b, alpha):
    bsz, t, d = x.shape
    tm = _pick(t, (256, 128))
    row = lambda b, i: (b, 0, 0)
    tok = lambda w: pl.BlockSpec((1, tm, w), lambda b, i: (b, i, 0))
    wspec = lambda shape: pl.BlockSpec(shape, lambda b, i: (0, 0), pipeline_mode=pl.Buffered(1))
    return pl.pallas_call(
        functools.partial(_merge_kernel, alpha=alpha),
        grid=(bsz, t // tm),
        in_specs=[tok(d), pl.BlockSpec((1, 1, d), row), pl.BlockSpec((1, 1, d), row), pl.BlockSpec((1, 1, d), row),
                  tok(LANE), tok(LANE), tok(LANE), tok(LANE), tok(LANE), tok(LANE),
                  tok(2 * LANE), tok(2 * LANE), tok(2 * LANE),
                  wspec(wg.shape), wspec(bg.shape), wspec(wb.shape), wspec(wo.shape),
                  pl.BlockSpec((1, d), lambda b, i: (0, 0)), pl.BlockSpec((1, d), lambda b, i: (0, 0))],
        out_specs=tok(d),
        out_shape=jax.ShapeDtypeStruct((bsz, t, d), F32),
        compiler_params=_params(("parallel", "parallel")),
        name="gated_merge",
    )(x, shift, scale, gate, *oas, *las, yb, yc, yd, wg, bg, wb, wo, ln_g, ln_b)


def _rope_tables(t):
    half = ROT_DIM // 2
    inv_freq = ROPE_THETA ** (-(jnp.arange(half, dtype=F32) * (2.0 / ROT_DIM)))
    ang = jnp.arange(t, dtype=F32)[:, None] * inv_freq[None, :]
    cos, sin = jnp.cos(ang), jnp.sin(ang)
    ones = jnp.ones((t, HEAD_DIM - ROT_DIM), F32)
    zeros = jnp.zeros((t, HEAD_DIM - half), F32)
    c64 = jnp.concatenate([cos, cos, ones], axis=1)
    s1 = jnp.concatenate([-sin, zeros], axis=1)
    s2 = jnp.concatenate([jnp.zeros((t, half), F32), sin, zeros[:, half:]], axis=1)
    return tuple(jnp.concatenate([a, a], axis=1) for a in (c64, s1, s2))


def _mixer_weights(w_in):
    d = w_in.shape[0]

    def qkv(off, heads):
        w = heads * HEAD_DIM
        return w_in[:, off:off + w], w_in[:, off + w:off + 2 * w], w_in[:, off + 2 * w:off + 3 * w]

    aq, ak, av = qkv(OFF_A, A_HEADS)
    bq, bk, bv = qkv(OFF_B, B_HEADS)
    cq, ck, cv = qkv(OFF_C, C_HEADS)
    dq, dk, dv = qkv(OFF_D, D_HEADS)
    w16 = jnp.concatenate([aq, ak, bq, bk, av, bv, cq, ck, cv, dq, dk, dv, jnp.zeros((d, LANE), F32)], axis=1)
    ix = jnp.concatenate([w_in[:, OFF_IK:OFF_IK + IDX_DIM], jnp.zeros((d, LANE_IW - IDX_DIM), F32),
                          w_in[:, OFF_IW:OFF_IW + IDX_HEADS], w_in[:, OFF_FG:OFF_FG + D_HEADS],
                          jnp.zeros((d, LANE - LANE_FG - D_HEADS), F32)], axis=1)
    w32 = jnp.concatenate([w_in[:, OFF_IQ:OFF_IQ + IDX_HEADS * IDX_DIM], ix, jnp.zeros((d, LANE), F32)], axis=1)
    return w16.astype(BF16), w32.astype(BF16), w_in[:, OFF_GATE:].astype(BF16)


def _to_residues(a, dil):
    bsz, t, w = a.shape
    if dil == 1:
        return a
    return a.reshape(bsz, t // dil, dil, w).transpose(0, 2, 1, 3).reshape(bsz * dil, t // dil, w)


def _from_residues(a, dil, bsz):
    if dil == 1:
        return a
    _, ln, w = a.shape
    return a.reshape(bsz, dil, ln, w).transpose(0, 2, 1, 3).reshape(bsz, ln * dil, w)


def _mixer(x, shift, scale, gate, w_in, b_gate, b_forget, w_branch, w_out, ln_g, ln_b, tables, alpha):
    bsz, t, d = x.shape
    w16, w32, wg = _mixer_weights(w_in)
    p16, p32 = _proj(x, shift, scale, w16, w32, tables)

    oas, las = [], []
    for g, (_, dil) in enumerate(DILATED_GROUPS):
        if dil == 1:
            o, lse = _dilated(p16, p16, p16, cols=(BLK_AQ + g, BLK_AK + g, BLK_AV + g))
        else:
            streams = [_to_residues(p16[:, :, (blk + g) * LANE:(blk + g + 1) * LANE], dil)
                       for blk in (BLK_AQ, BLK_AK, BLK_AV)]
            o, lse = _dilated(*streams)
        oas.append(_from_residues(o, dil, bsz))
        las.append(_from_residues(lse, dil, bsz))

    tq = _pick(t, (256, 128))
    bias_row = jnp.zeros((1, LANE), F32).at[0, LANE_FG:LANE_FG + D_HEADS].set(b_forget)
    fbias, kcat = _cum(p32, bias_row, tq)
    tqf = _pick(t, (512, 256, 128))
    yd = _fox(p16, _transposed_values(p16, BLK_DV, D_HEADS // 2, tqf), fbias, tqf)
    yc = _stick_breaking(p16, _transposed_values(p16, BLK_CV, C_HEADS // 2, tq), tq)
    yb = _dsa(p16, p32, kcat, _transposed_values(p16, BLK_BV, B_HEADS // 2, tqf), tqf)
    return _merge(x, shift, scale, gate, oas, las, yb, yc, yd, wg, b_gate.reshape(1, -1),
                  w_branch.astype(BF16), w_out.astype(BF16), ln_g, ln_b, alpha)


def kernel(x, c, ada_w, ada_b, ln_g, ln_b, ffn_w_in, ffn_w_out, mix_w_in, mix_b_gate, mix_b_forget,
           mix_w_branch, mix_w_out):
    bsz, t, d = x.shape
    depth = ada_w.shape[0]
    f = ffn_w_out.shape[2]
    alpha = float((2 * depth) ** 0.25)
    assert t % (DILATED_GROUPS[-1][0]) == 0 and d % LANE == 0 and bsz <= 8

    c8 = jnp.zeros((8, d), F32).at[:bsz].set(c)
    mod = _ada(c8, ada_w, ada_b)[:, :bsz].reshape(depth, bsz, 3, 3, 1, d)
    tables = _rope_tables(t)

    for l in range(depth):
        m = lambda sub, kind: mod[l, :, sub, kind]
        lng = lambda sub: ln_g[l, sub].reshape(1, d)
        lnb = lambda sub: ln_b[l, sub].reshape(1, d)

        def ffn(x, sub, which):
            w_in = ffn_w_in[l, which].astype(BF16)
            return _ffn(x, m(sub, 0), m(sub, 1), m(sub, 2), w_in[:, :f], w_in[:, f:],
                        ffn_w_out[l, which].astype(BF16), lng(sub), lnb(sub), alpha)

        x = ffn(x, 0, 0)
        x = _mixer(x, m(1, 0), m(1, 1), m(1, 2), mix_w_in[l], mix_b_gate[l], mix_b_forget[l],
                   mix_w_branch[l], mix_w_out[l], lng(1), lnb(1), tables, alpha)
        x = ffn(x, 2, 1)
    return x
```

```python
import functools

import jax
import jax.numpy as jnp
from jax import lax
from jax.experimental import pallas as pl
from jax.experimental.pallas import tpu as pltpu

F32 = jnp.float32
BF16 = jnp.bfloat16
I32 = jnp.int32

LANE = 128
HEAD_DIM = 64
ROT_DIM = HEAD_DIM // 4
ROPE_THETA = 500000.0
DILATED_GROUPS = ((128, 1), (512, 4), (2048, 16))
SPAN = 128
A_HEADS, B_HEADS, C_HEADS, D_HEADS = 6, 4, 4, 4
IDX_HEADS, IDX_DIM = 4, 64
DSA_TOPK = 256
N_BRANCH = 4
LN_EPS = 1e-5
BRANCH_WIDTHS = (128, 256, 256, 256)

OFF_A = 0
OFF_B = OFF_A + 3 * A_HEADS * HEAD_DIM
OFF_IQ = OFF_B + 3 * B_HEADS * HEAD_DIM
OFF_IK = OFF_IQ + IDX_HEADS * IDX_DIM
OFF_IW = OFF_IK + IDX_DIM
OFF_C = OFF_IW + IDX_HEADS
OFF_D = OFF_C + 3 * C_HEADS * HEAD_DIM
OFF_FG = OFF_D + 3 * D_HEADS * HEAD_DIM
OFF_GATE = OFF_FG + D_HEADS

BLK_AQ, BLK_AK, BLK_BQ, BLK_BK = 0, 3, 6, 8
N_ROPE_BLKS_16 = 10
BLK_AV, BLK_BV = 10, 13
BLK_CQ, BLK_CK, BLK_CV = 15, 17, 19
BLK_DQ, BLK_DK, BLK_DV = 21, 23, 25
N_BLKS_16 = 28
BLK_IQ, BLK_IX = 0, 2
N_BLKS_32 = 4
LANE_IW = 80
LANE_FG = 84

INT_MIN = -2147483648
NEG_BIG = -1e30
LOG2E = 1.4426950408889634
EXP_DEAD = -104.0
STRIP = 32
VMEM_LIMIT = 56 * 1024 * 1024


def _nt(a, b):
    return lax.dot_general(a, b, (((1,), (1,)), ((), ())), preferred_element_type=F32)


def _dot(a, b):
    return jnp.dot(a, b, preferred_element_type=F32)


def _split3(x):
    hi = x.astype(BF16)
    r1 = x - hi.astype(F32)
    mid = r1.astype(BF16)
    lo = (r1 - mid.astype(F32)).astype(BF16)
    return hi, mid, lo


def _split2(x):
    hi = x.astype(BF16)
    return hi, (x - hi.astype(F32)).astype(BF16)


def _log_sigmoid(z):
    return -(jnp.maximum(-z, 0.0) + jnp.log(1.0 + jnp.exp(-jnp.abs(z))))


def _params(sem):
    return pltpu.CompilerParams(dimension_semantics=sem, vmem_limit_bytes=VMEM_LIMIT)


def _const_spec(shape):
    n = len(shape)
    return pl.BlockSpec(shape, lambda *_: (0,) * n)


def _pick(n, prefs):
    for p in prefs:
        if n % p == 0:
            return p
    return n


def _ada_kernel(c_ref, w_ref, b_ref, o_ref):
    c = c_ref[...]
    cond = c * jax.nn.sigmoid(c)
    ch, cm, _ = _split3(cond)
    wh, wm, _ = _split3(w_ref[0])
    o_ref[0] = _dot(ch, wh) + _dot(ch, wm) + _dot(cm, wh) + b_ref[0]


def _ada(c8, ada_w, ada_b):
    depth, d, n = ada_w.shape
    tn = _pick(n, (1152, 1024, 512, 256, 128))
    return pl.pallas_call(
        _ada_kernel,
        grid=(depth, n // tn),
        in_specs=[pl.BlockSpec((8, d), lambda l, j: (0, 0)),
                  pl.BlockSpec((1, d, tn), lambda l, j: (l, 0, j)),
                  pl.BlockSpec((1, 1, tn), lambda l, j: (l, 0, j))],
        out_specs=pl.BlockSpec((1, 8, tn), lambda l, j: (l, 0, j)),
        out_shape=jax.ShapeDtypeStruct((depth, 8, n), F32),
        compiler_params=_params(("parallel", "parallel")),
        name="ada_mod",
    )(c8, ada_w, ada_b.reshape(depth, 1, n))


def _deepnorm_ln(x, h, gate, g, b, alpha):
    y = alpha * x + (1.0 + gate) * h
    mu = jnp.mean(y, axis=-1, keepdims=True)
    yc = y - mu
    var = jnp.mean(yc * yc, axis=-1, keepdims=True)
    return yc * lax.rsqrt(var + LN_EPS) * g + b


def _ffn_kernel(x_ref, sh_ref, sc_ref, gt_ref, wg_ref, wu_ref, wo_ref, lg_ref, lb_ref, o_ref, *, tf, alpha):
    x = x_ref[0]
    u = (x * (1.0 + sc_ref[0]) + sh_ref[0]).astype(BF16)
    acc = jnp.zeros(x.shape, F32)
    for j in range(wg_ref.shape[1] // tf):
        g = _dot(u, wg_ref[:, j * tf:(j + 1) * tf])
        up = _dot(u, wu_ref[:, j * tf:(j + 1) * tf])
        h = (g * jax.nn.sigmoid(g) * up).astype(BF16)
        acc = acc + _dot(h, wo_ref[j * tf:(j + 1) * tf, :])
    o_ref[0] = _deepnorm_ln(x, 0.5 * acc, gt_ref[0], lg_ref[...], lb_ref[...], alpha)


def _ffn(x, shift, scale, gate, wg, wu, wo, ln_g, ln_b, alpha):
    bsz, t, d = x.shape
    f = wg.shape[1]
    tm = _pick(t, (512, 256, 128))
    tf = _pick(f, (256, 128))
    row = lambda b, i: (b, 0, 0)
    wspec = lambda shape: pl.BlockSpec(shape, lambda b, i: (0, 0), pipeline_mode=pl.Buffered(1))
    return pl.pallas_call(
        functools.partial(_ffn_kernel, tf=tf, alpha=alpha),
        grid=(bsz, t // tm),
        in_specs=[pl.BlockSpec((1, tm, d), lambda b, i: (b, i, 0)),
                  pl.BlockSpec((1, 1, d), row), pl.BlockSpec((1, 1, d), row), pl.BlockSpec((1, 1, d), row),
                  wspec((d, f)), wspec((d, f)), wspec((f, d)),
                  pl.BlockSpec((1, d), lambda b, i: (0, 0)), pl.BlockSpec((1, d), lambda b, i: (0, 0))],
        out_specs=pl.BlockSpec((1, tm, d), lambda b, i: (b, i, 0)),
        out_shape=jax.ShapeDtypeStruct((bsz, t, d), F32),
        compiler_params=_params(("parallel", "parallel")),
        name="ffn",
    )(x, shift, scale, gate, wg, wu, wo, ln_g, ln_b)


def _proj_kernel(x_ref, sh_ref, sc_ref, w16_ref, w32_ref, cos_ref, s1_ref, s2_ref, o16_ref, o32_ref, *vt_refs,
                 tn, vt_blks):
    u = (x_ref[0] * (1.0 + sc_ref[0]) + sh_ref[0]).astype(BF16)
    per = tn // LANE
    for w_ref, o_ref, n_rope_blks in ((w16_ref, o16_ref, N_ROPE_BLKS_16), (w32_ref, o32_ref, N_BLKS_32)):
        for j in range(w_ref.shape[1] // tn):
            y = _dot(u, w_ref[:, j * tn:(j + 1) * tn])
            for i in range(per):
                blk = j * per + i
                yi = y[:, i * LANE:(i + 1) * LANE]
                if blk < n_rope_blks:
                    yi = (yi * cos_ref[...] + pltpu.roll(yi, LANE - ROT_DIM // 2, 1) * s1_ref[...]
                          + pltpu.roll(yi, ROT_DIM // 2, 1) * s2_ref[...])
                o_ref[0, :, blk * LANE:(blk + 1) * LANE] = yi.astype(o_ref.dtype)
    for vt_ref, blk0 in zip(vt_refs, vt_blks):
        _, npair, nparts, _, tqc = vt_ref.shape
        for pair in range(npair):
            vt = o16_ref[0, :, (blk0 + pair) * LANE:(blk0 + pair + 1) * LANE].astype(F32).T
            for part in range(nparts):
                vt_ref[0, pair, part] = vt[:, part * tqc:(part + 1) * tqc].astype(BF16)


def _proj(x, shift, scale, w16, w32, tables, vt_tiles):
    bsz, t, d = x.shape
    n16, n32 = w16.shape[1], w32.shape[1]
    tm = _pick(t, (512, 256, 128))
    row = lambda b, i: (b, 0, 0)
    tab = pl.BlockSpec((tm, LANE), lambda b, i: (i, 0))
    wspec = lambda n: pl.BlockSpec((d, n), lambda b, i: (0, 0), pipeline_mode=pl.Buffered(1))
    vt_specs = [pl.BlockSpec((1, 2, tm // tq, LANE, tq), lambda b, i: (b, 0, i, 0, 0)) for _, tq in vt_tiles]
    vt_shapes = [jax.ShapeDtypeStruct((bsz, 2, t // tq, LANE, tq), BF16) for _, tq in vt_tiles]
    return pl.pallas_call(
        functools.partial(_proj_kernel, tn=2 * LANE, vt_blks=tuple(blk for blk, _ in vt_tiles)),
        grid=(bsz, t // tm),
        in_specs=[pl.BlockSpec((1, tm, d), lambda b, i: (b, i, 0)),
                  pl.BlockSpec((1, 1, d), row), pl.BlockSpec((1, 1, d), row),
                  wspec(n16), wspec(n32), tab, tab, tab],
        out_specs=[pl.BlockSpec((1, tm, n16), lambda b, i: (b, i, 0)),
                   pl.BlockSpec((1, tm, n32), lambda b, i: (b, i, 0))] + vt_specs,
        out_shape=[jax.ShapeDtypeStruct((bsz, t, n16), BF16), jax.ShapeDtypeStruct((bsz, t, n32), F32)] + vt_shapes,
        compiler_params=_params(("parallel", "parallel")),
        name="mixer_proj",
    )(x, shift, scale, w16, w32, *tables)


def _cum_kernel(p_ref, bias_ref, out_ref, kcat_ref, carry_ref):
    @pl.when(pl.program_id(1) == 0)
    def _():
        carry_ref[...] = jnp.zeros_like(carry_ref)

    tc = p_ref.shape[1]
    lane = lax.broadcasted_iota(I32, (tc, LANE), 1)
    lf = _log_sigmoid(p_ref[0] + bias_ref[...])
    lf = jnp.where(lane >= LANE_FG, jnp.where(lane < LANE_FG + D_HEADS, lf, 0.0), 0.0)
    ri = lax.broadcasted_iota(I32, (tc, tc), 0)
    ci = lax.broadcasted_iota(I32, (tc, tc), 1)
    tri = jnp.where(ri >= ci, 1.0, 0.0).astype(BF16)
    hi, mid, lo = _split3(lf)
    cum = _dot(tri, hi) + _dot(tri, mid) + _dot(tri, lo) + carry_ref[...]
    carry_ref[...] = cum[tc - 1:tc, :]
    er = lax.broadcasted_iota(I32, (LANE, LANE), 0)
    ch, cm, cl = _split3(-LOG2E * cum)
    for h in range(D_HEADS):
        onehot = jnp.where(er == LANE_FG + h, 1.0, 0.0).astype(BF16)
        out_ref[0, h] = _dot(ch, onehot) + _dot(cm, onehot) + _dot(cl, onehot)
    x = p_ref[0]
    hi = x.astype(BF16).astype(F32)
    kcat_ref[0, :, 0:LANE] = jnp.where(lane < IDX_DIM, hi, pltpu.roll(hi, IDX_DIM, 1)).astype(BF16)
    kcat_ref[0, :, LANE:2 * LANE] = jnp.where(lane < IDX_DIM, x - hi, 0.0).astype(BF16)


def _cum(p32, bias_row, tc):
    bsz, t, _ = p32.shape
    return pl.pallas_call(
        _cum_kernel,
        grid=(bsz, t // tc),
        in_specs=[pl.BlockSpec((1, tc, LANE), lambda b, i: (b, i, BLK_IX)),
                  pl.BlockSpec((1, LANE), lambda b, i: (0, 0))],
        out_specs=[pl.BlockSpec((1, D_HEADS, tc, LANE), lambda b, i: (b, 0, i, 0)),
                   pl.BlockSpec((1, tc, 2 * LANE), lambda b, i: (b, i, 0))],
        out_shape=[jax.ShapeDtypeStruct((bsz, D_HEADS, t, LANE), F32),
                   jax.ShapeDtypeStruct((bsz, t, 2 * LANE), BF16)],
        scratch_shapes=[pltpu.VMEM((1, LANE), F32)],
        compiler_params=_params(("parallel", "arbitrary")),
        name="forget_cumsum",
    )(p32, bias_row)


def _head_halves(q):
    lane = lax.broadcasted_iota(I32, q.shape, 1)
    zero = jnp.zeros_like(q)
    return jnp.where(lane < HEAD_DIM, q, zero), jnp.where(lane >= HEAD_DIM, q, zero)


def _static_loop(n, body, carry):
    for i in range(n):
        carry = body(i, carry)
    return carry


def _fold8(x, op):
    out = x[0:8]
    for r in range(8, x.shape[0], 8):
        out = op(out, x[r:r + 8])
    return out


def _lane_pick(a, b):
    lane = lax.broadcasted_iota(I32, a.shape, 1)
    return jnp.where(lane < HEAD_DIM, a, b)


def _dil_kernel(q_ref, k_ref, v_ref, o_ref, lse_ref, *, kw):
    nb = q_ref.shape[1] // SPAN

    def body(i, carry):
        qs = pl.multiple_of(i * SPAN, SPAN)
        ks = pl.multiple_of(jnp.maximum(i * SPAN + SPAN - kw, 0), SPAN)
        q = q_ref[0, pl.ds(qs, SPAN), :] * 0.125
        k = k_ref[0, pl.ds(ks, kw), :]
        v = v_ref[0, pl.ds(ks, kw), :]
        qpos = qs + lax.broadcasted_iota(I32, (SPAN, kw), 0)
        kpos = ks + lax.broadcasted_iota(I32, (SPAN, kw), 1)
        dist = qpos - kpos
        outs, lses = [], []
        for qh in _head_halves(q):
            s = _nt(qh, k)
            s = jnp.where(dist >= 0, jnp.where(dist <= SPAN, s, NEG_BIG), NEG_BIG)
            m = jnp.max(s, axis=-1, keepdims=True)
            p = jnp.exp(s - m)
            l = jnp.sum(p, axis=-1, keepdims=True)
            outs.append(_dot(p.astype(BF16), v) / l)
            lses.append(jnp.broadcast_to(m + jnp.log(l), (SPAN, LANE)))
        o_ref[0, pl.ds(qs, SPAN), :] = _lane_pick(outs[0], outs[1]).astype(o_ref.dtype)
        lse_ref[0, pl.ds(qs, SPAN), :] = _lane_pick(lses[0], lses[1])
        return carry

    lax.fori_loop(0, nb, body, 0, unroll=min(nb, 4))


def _dilated(q, k, v, cols=(0, 0, 0)):
    ns, ln, _ = q.shape
    kw = min(2 * SPAN, ln)
    spec = pl.BlockSpec((1, ln, LANE), lambda s: (s, 0, 0))
    in_spec = lambda col: pl.BlockSpec((1, ln, LANE), lambda s: (s, 0, col))
    return pl.pallas_call(
        functools.partial(_dil_kernel, kw=kw),
        grid=(ns,),
        in_specs=[in_spec(c) for c in cols],
        out_specs=[spec, spec],
        out_shape=[jax.ShapeDtypeStruct((ns, ln, LANE), BF16), jax.ShapeDtypeStruct((ns, ln, LANE), F32)],
        compiler_params=_params(("parallel",)),
        name="dilated_window",
    )(q, k, v)


def _fox_kernel(q_ref, k_ref, vt_ref, bias_ref, o_ref, s_scr, p_scr, al_scr, acc_scr, m_scr, l_scr, *, tq):
    qi = pl.program_id(2)
    rep = tq // LANE
    nstrip = tq // STRIP
    qh = _head_halves((q_ref[0].astype(F32) * (0.125 * LOG2E)).astype(BF16))
    m_scr[...] = jnp.full(m_scr.shape, NEG_BIG, F32)
    l_scr[...] = jnp.zeros(l_scr.shape, F32)
    acc_scr[...] = jnp.zeros(acc_scr.shape, F32)
    kk = lax.broadcasted_iota(I32, (STRIP, tq), 0)
    qq = lax.broadcasted_iota(I32, (STRIP, tq), 1)

    def issue_scores(j, s_buf):
        kblk = k_ref[0, pl.ds(pl.multiple_of(j * tq, tq), tq), :]
        for h in range(2):
            s_buf[h] = _nt(kblk, qh[h])

    def softmax(j, s_buf, p_buf, al_buf, diag):
        ks = pl.multiple_of(j * tq, tq)
        for h in range(2):
            def pass1(i, mx):
                r0 = i * STRIP
                s = s_buf[h, pl.ds(r0, STRIP), :] + jnp.tile(
                    bias_ref[0, h, pl.ds(pl.multiple_of(ks + r0, STRIP), STRIP), :], (1, rep))
                if diag:
                    s = jnp.where(kk + r0 <= qq, s, NEG_BIG)
                s_buf[h, pl.ds(r0, STRIP), :] = s
                return jnp.maximum(mx, _fold8(s, jnp.maximum))

            mx = _static_loop(nstrip, pass1, jnp.full((8, tq), NEG_BIG, F32))
            m_old = m_scr[h]
            m_new = jnp.maximum(m_old, jnp.max(mx, axis=0, keepdims=True))
            al_buf[h] = jnp.exp2(m_old - m_new)
            m_scr[h] = m_new

            def pass2(i, ls):
                r0 = i * STRIP
                p = jnp.exp2(s_buf[h, pl.ds(r0, STRIP), :] - m_new)
                p_buf[h, pl.ds(r0, STRIP), :] = p.astype(BF16)
                return ls + _fold8(p, jnp.add)

            ls = _static_loop(nstrip, pass2, jnp.zeros((8, tq), F32))
            l_scr[h] = al_buf[h] * l_scr[h] + jnp.sum(ls, axis=0, keepdims=True)

    def apply_values(j, p_buf, al_buf):
        for h in range(2):
            pv = _dot(vt_ref[0, 0, j, h * HEAD_DIM:(h + 1) * HEAD_DIM, :], p_buf[h])
            acc_scr[h] = acc_scr[h] * al_buf[h] + pv

    s_a, s_b = s_scr.at[0], s_scr.at[1]
    p_a, p_b = p_scr.at[0], p_scr.at[1]
    al_a, al_b = al_scr.at[0], al_scr.at[1]
    p_b[...] = jnp.zeros(p_b.shape, BF16)
    al_b[...] = jnp.ones(al_b.shape, F32)
    issue_scores(0, s_a)

    def pair(i, carry):
        a = 2 * i
        issue_scores(a + 1, s_b)
        softmax(a, s_a, p_a, al_a, False)
        apply_values(jnp.maximum(a - 1, 0), p_b, al_b)
        issue_scores(a + 2, s_a)
        softmax(a + 1, s_b, p_b, al_b, False)
        apply_values(a, p_a, al_a)
        return carry

    lax.fori_loop(0, qi // 2, pair, 0)
    last = 2 * (qi // 2)

    @pl.when(qi % 2 == 1)
    def _():
        issue_scores(qi, s_b)
        softmax(last, s_a, p_a, al_a, False)
        apply_values(jnp.maximum(last - 1, 0), p_b, al_b)
        softmax(qi, s_b, p_b, al_b, True)
        apply_values(last, p_a, al_a)
        apply_values(qi, p_b, al_b)

    @pl.when(qi % 2 == 0)
    def _():
        softmax(qi, s_a, p_a, al_a, True)
        apply_values(jnp.maximum(qi - 1, 0), p_b, al_b)
        apply_values(qi, p_a, al_a)

    out_t = jnp.concatenate([acc_scr[h] * (1.0 / l_scr[h]) for h in range(2)], axis=0)
    o_ref[0] = out_t.T.astype(o_ref.dtype)


def _fox(p16, vt, bias, tq):
    bsz, t, _ = p16.shape
    npair = D_HEADS // 2
    nblk = t // tq
    return pl.pallas_call(
        functools.partial(_fox_kernel, tq=tq),
        grid=(bsz, npair, nblk),
        in_specs=[pl.BlockSpec((1, tq, LANE), lambda b, h, i: (b, i, BLK_DQ + h)),
                  pl.BlockSpec((1, t, LANE), lambda b, h, i: (b, 0, BLK_DK + h)),
                  pl.BlockSpec((1, 1, nblk, LANE, tq), lambda b, h, i: (b, h, 0, 0, 0)),
                  pl.BlockSpec((1, 2, t, LANE), lambda b, h, i: (b, h, 0, 0))],
        out_specs=pl.BlockSpec((1, tq, LANE), lambda b, h, i: (b, i, h)),
        out_shape=jax.ShapeDtypeStruct((bsz, t, npair * LANE), BF16),
        scratch_shapes=[pltpu.VMEM((2, 2, tq, tq), F32), pltpu.VMEM((2, 2, tq, tq), BF16),
                        pltpu.VMEM((2, 2, 1, tq), F32),
                        pltpu.VMEM((2, HEAD_DIM, tq), F32), pltpu.VMEM((2, 1, tq), F32),
                        pltpu.VMEM((2, 1, tq), F32)],
        compiler_params=_params(("parallel", "parallel", "arbitrary")),
        name="forgetting_attention",
    )(p16, p16, vt, bias)


def _sb_kernel(q_ref, k_ref, vt_ref, o_ref, z_scr, sfx_scr, hi_scr, lo_scr, a_scr, acc_scr, r_scr, *, tq):
    qi = pl.program_id(2)
    nstrip = tq // STRIP
    qh = _head_halves(q_ref[0] * 0.125)
    acc_scr[...] = jnp.zeros(acc_scr.shape, F32)
    r_scr[...] = jnp.zeros(r_scr.shape, F32)
    kk = lax.broadcasted_iota(I32, (STRIP, tq), 0)
    qq = lax.broadcasted_iota(I32, (STRIP, tq), 1)
    row = lax.broadcasted_iota(I32, (tq, tq), 0)
    col = lax.broadcasted_iota(I32, (tq, tq), 1)
    later = jnp.where(col > row, 1.0, 0.0).astype(BF16)

    def block(j, diag):
        ks = pl.multiple_of(j * tq, tq)
        kblk = k_ref[0, pl.ds(ks, tq), :]
        for h in range(2):
            z_scr[h] = _nt(kblk, qh[h])
        for h in range(2):
            def pass1(i, rsum):
                r0 = i * STRIP
                z = z_scr[h, pl.ds(r0, STRIP), :]
                lk = _log_sigmoid(-z)
                z_scr[h, pl.ds(r0, STRIP), :] = lk + z
                if diag:
                    lk = jnp.where(kk + r0 < qq, lk, 0.0)
                hi = lk.astype(BF16)
                hi_scr[h, pl.ds(r0, STRIP), :] = hi
                lo_scr[h, pl.ds(r0, STRIP), :] = (lk - hi.astype(F32)).astype(BF16)
                return rsum + _fold8(lk, jnp.add)

            rsum = _static_loop(nstrip, pass1, jnp.zeros((8, tq), F32))
            sfx_scr[h] = _dot(later, hi_scr[h]) + _dot(later, lo_scr[h])
            r_old = r_scr[h]

            def pass2(i, carry):
                r0 = i * STRIP
                a = jnp.exp(z_scr[h, pl.ds(r0, STRIP), :] + sfx_scr[h, pl.ds(r0, STRIP), :] + r_old)
                if diag:
                    a = jnp.where(kk + r0 < qq, a, 0.0)
                a_scr[h, pl.ds(r0, STRIP), :] = a.astype(BF16)
                return carry

            _static_loop(nstrip, pass2, 0)
            acc_scr[h] = acc_scr[h] + _dot(vt_ref[0, 0, j, h * HEAD_DIM:(h + 1) * HEAD_DIM, :], a_scr[h])
            r_scr[h] = r_old + jnp.sum(rsum, axis=0, keepdims=True)

    block(qi, True)

    def live():
        return jnp.max(jnp.maximum(r_scr[0], r_scr[1])) > EXP_DEAD

    def cond(c):
        return jnp.logical_and(c[0] >= 0, c[1])

    def body(c):
        block(c[0], False)
        return c[0] - 1, live()

    lax.while_loop(cond, body, (qi - 1, live()))
    o_ref[0] = jnp.concatenate([acc_scr[0], acc_scr[1]], axis=0).T.astype(o_ref.dtype)


def _stick_breaking(p16, vt, tq):
    bsz, t, _ = p16.shape
    npair = C_HEADS // 2
    nblk = t // tq
    return pl.pallas_call(
        functools.partial(_sb_kernel, tq=tq),
        grid=(bsz, npair, nblk),
        in_specs=[pl.BlockSpec((1, tq, LANE), lambda b, h, i: (b, i, BLK_CQ + h)),
                  pl.BlockSpec((1, t, LANE), lambda b, h, i: (b, 0, BLK_CK + h)),
                  pl.BlockSpec((1, 1, nblk, LANE, tq), lambda b, h, i: (b, h, 0, 0, 0))],
        out_specs=pl.BlockSpec((1, tq, LANE), lambda b, h, i: (b, i, h)),
        out_shape=jax.ShapeDtypeStruct((bsz, t, npair * LANE), BF16),
        scratch_shapes=[pltpu.VMEM((2, tq, tq), F32), pltpu.VMEM((2, tq, tq), F32),
                        pltpu.VMEM((2, tq, tq), BF16), pltpu.VMEM((2, tq, tq), BF16),
                        pltpu.VMEM((2, tq, tq), BF16), pltpu.VMEM((2, HEAD_DIM, tq), F32),
                        pltpu.VMEM((2, 1, tq), F32)],
        compiler_params=_params(("parallel", "parallel", "arbitrary")),
        name="stick_breaking",
    )(p16, p16, vt)


def _dsa_kernel(iq0_ref, iq1_ref, ixq_ref, kcat_ref, q0_ref, q1_ref, k0_ref, k1_ref, vt_ref, o_ref,
                key_scr, s_scr, p_scr, eq_scr, rank_scr, bias_scr, al_scr, acc_scr, m_scr, l_scr, *, tq, topk):
    qi = pl.program_id(1)
    nstrip = tq // STRIP
    lane = lax.broadcasted_iota(I32, (tq, LANE), 1)
    kk = lax.broadcasted_iota(I32, (STRIP, tq), 0)
    qq = lax.broadcasted_iota(I32, (STRIP, tq), 1)

    qcat = []
    for ref in (iq0_ref, iq1_ref):
        for half in range(2):
            x = ref[0] if half == 0 else pltpu.roll(ref[0], IDX_DIM, 1)
            hi = x.astype(BF16).astype(F32)
            first = jnp.where(lane < IDX_DIM, hi, pltpu.roll(x - hi, IDX_DIM, 1))
            second = jnp.where(lane < IDX_DIM, hi, 0.0)
            qcat.append(jnp.concatenate([first, second], axis=1).astype(BF16))
    sr = lax.broadcasted_iota(I32, (8, LANE), 0)
    sl = lax.broadcasted_iota(I32, (8, LANE), 1)
    onehot = jnp.where(sl == sr + LANE_IW, 1.0, 0.0).astype(BF16)
    wh, wm, wl = _split3(ixq_ref[0])
    wt = _nt(onehot, wh) + _nt(onehot, wm) + _nt(onehot, wl)

    def score_chunk(c, diag, counts):
        ge0, gt0 = counts
        ks = pl.multiple_of(c * tq, tq)
        kc = kcat_ref[0, pl.ds(ks, tq), :]
        for h in range(IDX_HEADS):
            s_scr[0, h] = _nt(kc, qcat[h])
        for i in range(nstrip):
            r0 = i * STRIP
            sc = jnp.zeros((STRIP, tq), F32)
            for h in range(IDX_HEADS):
                sc = sc + wt[h:h + 1, :] * jnp.maximum(s_scr[0, h, r0:r0 + STRIP, :], 0.0)
            sc = jnp.where(sc == 0.0, 0.0, sc)
            bits = lax.bitcast_convert_type(sc, I32)
            key = jnp.where(bits < 0, bits ^ jnp.int32(0x7FFFFFFF), bits)
            if diag:
                key = jnp.where(kk + r0 <= qq, key, INT_MIN)
            key_scr[c, r0:r0 + STRIP, :] = key
            ge0 = ge0 + _fold8(jnp.where(key >= 0, 1.0, 0.0), jnp.add)
            gt0 = gt0 + _fold8(jnp.where(key > 0, 1.0, 0.0), jnp.add)
        return ge0, gt0

    zeros = jnp.zeros((8, tq), F32)
    counts = lax.fori_loop(0, qi, lambda c, counts: score_chunk(c, False, counts), (zeros, zeros))
    ge0, gt0 = score_chunk(qi, True, counts)
    n_ge0 = jnp.sum(ge0, axis=0, keepdims=True)
    n_gt0 = jnp.sum(gt0, axis=0, keepdims=True)

    def count(pred):
        def body(c, acc):
            for i in range(nstrip):
                acc = jnp.where(pred(key_scr[c, i * STRIP:(i + 1) * STRIP, :]), acc + 1.0, acc)
            return acc
        acc = lax.fori_loop(0, qi + 1, body, jnp.zeros((STRIP, tq), F32))
        return jnp.sum(acc, axis=0, keepdims=True)

    kf = jnp.float32(topk)
    th0 = jnp.where(n_ge0 >= kf, 0, INT_MIN).astype(I32)
    settled0 = jnp.where(n_ge0 >= kf, jnp.where(n_gt0 < kf, 1.0, 0.0), 0.0)
    n_up0 = jnp.where(n_ge0 >= kf, jnp.where(n_gt0 < kf, n_gt0, 0.0), n_ge0)

    def unsettled(settled):
        return jnp.min(settled) == 0.0

    def search_cond(c):
        return jnp.logical_and(c[0] >= 0, c[4])

    def search_body(c):
        bit, th, n_up, settled, _ = c
        cand = th | lax.shift_left(jnp.int32(1), bit)
        n = count(lambda kb: kb >= cand)
        open_ = settled == 0.0
        th = jnp.where(n >= kf, jnp.where(open_, cand, th), th)
        n_up = jnp.where(n < kf, jnp.where(open_, n, n_up), n_up)
        settled = jnp.maximum(settled, jnp.where(n == kf, 1.0, 0.0))
        return bit - 1, th, n_up, settled, unsettled(settled)

    _, th, n_up, _, _ = lax.while_loop(search_cond, search_body,
                                       (jnp.int32(30), th0, n_up0, settled0, unsettled(settled0)))
    need = kf - n_up

    qh = sum((_head_halves((r[0].astype(F32) * (0.125 * LOG2E)).astype(BF16)) for r in (q0_ref, q1_ref)), ())
    k_refs = (k0_ref, k0_ref, k1_ref, k1_ref)
    m_scr[...] = jnp.full(m_scr.shape, NEG_BIG, F32)
    l_scr[...] = jnp.zeros(l_scr.shape, F32)
    acc_scr[...] = jnp.zeros(acc_scr.shape, F32)
    row = lax.broadcasted_iota(I32, (tq, tq), 0)
    col = lax.broadcasted_iota(I32, (tq, tq), 1)
    upto = jnp.where(col <= row, 1.0, 0.0).astype(BF16)

    def issue(c, s_buf, eq_buf, rank_buf):
        ks = pl.multiple_of(c * tq, tq)
        for i in range(nstrip):
            r0 = i * STRIP
            eq_buf[r0:r0 + STRIP, :] = jnp.where(key_scr[c, r0:r0 + STRIP, :] == th, 1.0, 0.0).astype(BF16)
        rank_buf[...] = _dot(upto, eq_buf[...])
        for h in range(B_HEADS):
            s_buf[h] = _nt(k_refs[h][0, pl.ds(ks, tq), :], qh[h])

    def softmax(c, s_buf, rank_buf, p_buf, al_buf, diag, seen):
        for i in range(nstrip):
            r0 = i * STRIP
            kb = key_scr[c, r0:r0 + STRIP, :]
            tie = jnp.where(rank_buf[r0:r0 + STRIP, :] + seen <= need, 0.0, NEG_BIG)
            bias = jnp.where(kb > th, 0.0, jnp.where(kb == th, tie, NEG_BIG))
            if diag:
                bias = jnp.where(kk + r0 <= qq, bias, NEG_BIG)
            bias_scr[r0:r0 + STRIP, :] = bias
        for h in range(B_HEADS):
            def pass1(i, mx):
                r0 = i * STRIP
                s = s_buf[h, r0:r0 + STRIP, :] + bias_scr[r0:r0 + STRIP, :]
                s_buf[h, r0:r0 + STRIP, :] = s
                return jnp.maximum(mx, _fold8(s, jnp.maximum))

            mx = _static_loop(nstrip, pass1, jnp.full((8, tq), NEG_BIG, F32))
            m_old = m_scr[h]
            m_new = jnp.maximum(m_old, jnp.max(mx, axis=0, keepdims=True))
            al_buf[h] = jnp.exp2(m_old - m_new)
            m_scr[h] = m_new

            def pass2(i, ls):
                r0 = i * STRIP
                p = jnp.exp2(s_buf[h, r0:r0 + STRIP, :] - m_new)
                p_buf[h, r0:r0 + STRIP, :] = p.astype(BF16)
                return ls + _fold8(p, jnp.add)

            ls = _static_loop(nstrip, pass2, jnp.zeros((8, tq), F32))
            l_scr[h] = al_buf[h] * l_scr[h] + jnp.sum(ls, axis=0, keepdims=True)
        return seen + rank_buf[tq - 1:tq, :]

    def apply_values(c, p_buf, al_buf):
        for h in range(B_HEADS):
            lo = (h % 2) * HEAD_DIM
            acc_scr[h] = acc_scr[h] * al_buf[h] + _dot(vt_ref[0, h // 2, c, lo:lo + HEAD_DIM, :], p_buf[h])

    s_a, s_b = s_scr.at[0], s_scr.at[1]
    p_a, p_b = p_scr.at[0], p_scr.at[1]
    eq_a, eq_b = eq_scr.at[0], eq_scr.at[1]
    rank_a, rank_b = rank_scr.at[0], rank_scr.at[1]
    al_a, al_b = al_scr.at[0], al_scr.at[1]
    p_b[...] = jnp.zeros(p_b.shape, BF16)
    al_b[...] = jnp.ones(al_b.shape, F32)
    issue(0, s_a, eq_a, rank_a)

    def pair(i, seen):
        a = 2 * i
        issue(a + 1, s_b, eq_b, rank_b)
        seen = softmax(a, s_a, rank_a, p_a, al_a, False, seen)
        apply_values(jnp.maximum(a - 1, 0), p_b, al_b)
        issue(a + 2, s_a, eq_a, rank_a)
        seen = softmax(a + 1, s_b, rank_b, p_b, al_b, False, seen)
        apply_values(a, p_a, al_a)
        return seen

    seen = lax.fori_loop(0, qi // 2, pair, jnp.zeros((1, tq), F32))
    last = 2 * (qi // 2)

    @pl.when(qi % 2 == 1)
    def _():
        issue(qi, s_b, eq_b, rank_b)
        seen_d = softmax(last, s_a, rank_a, p_a, al_a, False, seen)
        apply_values(jnp.maximum(last - 1, 0), p_b, al_b)
        softmax(qi, s_b, rank_b, p_b, al_b, True, seen_d)
        apply_values(last, p_a, al_a)
        apply_values(qi, p_b, al_b)

    @pl.when(qi % 2 == 0)
    def _():
        softmax(qi, s_a, rank_a, p_a, al_a, True, seen)
        apply_values(jnp.maximum(qi - 1, 0), p_b, al_b)
        apply_values(qi, p_a, al_a)

    for g in range(B_HEADS // 2):
        out_t = jnp.concatenate([acc_scr[h] * (1.0 / l_scr[h]) for h in (2 * g, 2 * g + 1)], axis=0)
        o_ref[0, :, g * LANE:(g + 1) * LANE] = out_t.T.astype(o_ref.dtype)


def _dsa(p16, p32, kcat, vt, tq):
    bsz, t, _ = p16.shape
    topk = min(DSA_TOPK, t // 4)
    npair = B_HEADS // 2
    nblk = t // tq
    qspec = lambda arr_blk: pl.BlockSpec((1, tq, LANE), lambda b, i: (b, i, arr_blk))
    once = pl.Buffered(1)
    kspec = lambda arr_blk: pl.BlockSpec((1, t, LANE), lambda b, i: (b, 0, arr_blk), pipeline_mode=once)
    sq = lambda n, dt: pltpu.VMEM((n, tq, tq), dt) if n else pltpu.VMEM((tq, tq), dt)
    return pl.pallas_call(
        functools.partial(_dsa_kernel, tq=tq, topk=topk),
        grid=(bsz, nblk),
        in_specs=[qspec(BLK_IQ), qspec(BLK_IQ + 1), qspec(BLK_IX),
                  pl.BlockSpec((1, t, 2 * LANE), lambda b, i: (b, 0, 0), pipeline_mode=once),
                  qspec(BLK_BQ), qspec(BLK_BQ + 1), kspec(BLK_BK), kspec(BLK_BK + 1),
                  pl.BlockSpec((1, npair, nblk, LANE, tq), lambda b, i: (b, 0, 0, 0, 0), pipeline_mode=once)],
        out_specs=pl.BlockSpec((1, tq, npair * LANE), lambda b, i: (b, i, 0)),
        out_shape=jax.ShapeDtypeStruct((bsz, t, npair * LANE), BF16),
        scratch_shapes=[sq(nblk, I32), pltpu.VMEM((2, B_HEADS, tq, tq), F32), pltpu.VMEM((2, B_HEADS, tq, tq), BF16),
                        sq(2, BF16), sq(2, F32), sq(0, F32), pltpu.VMEM((2, B_HEADS, 1, tq), F32),
                        pltpu.VMEM((B_HEADS, HEAD_DIM, tq), F32), pltpu.VMEM((B_HEADS, 1, tq), F32),
                        pltpu.VMEM((B_HEADS, 1, tq), F32)],
        compiler_params=_params(("parallel", "arbitrary")),
        name="dsa_topk_attention",
    )(p32, p32, p32, kcat, p16, p16, p16, p16, vt)


def _merge_kernel(x_ref, sh_ref, sc_ref, gt_ref, oa0_ref, oa1_ref, oa2_ref, la0_ref, la1_ref, la2_ref,
                  yb_ref, yc_ref, yd_ref, wg_ref, bg_ref, wb_ref, wo_ref, lg_ref, lb_ref, o_ref, *, alpha):
    x = x_ref[0]
    d = x.shape[1]
    u = (x * (1.0 + sc_ref[0]) + sh_ref[0]).astype(BF16)
    lses = [r[0] for r in (la0_ref, la1_ref, la2_ref)]
    top = jnp.maximum(jnp.maximum(lses[0], lses[1]), lses[2])
    es = [jnp.exp(l - top) for l in lses]
    den = es[0] + es[1] + es[2]
    ya = sum((e / den) * r[0].astype(F32) for e, r in zip(es, (oa0_ref, oa1_ref, oa2_ref)))
    branches = (ya.astype(BF16), yb_ref[0], yc_ref[0], yd_ref[0])
    merged = jnp.zeros(x.shape, F32)
    off = 0
    for i, br in enumerate(branches):
        gate = jax.nn.sigmoid(_dot(u, wg_ref[:, i * d:(i + 1) * d]) + bg_ref[:, i * d:(i + 1) * d])
        merged = merged + gate * _dot(br, wb_ref[off:off + BRANCH_WIDTHS[i], :])
        off += BRANCH_WIDTHS[i]
    h = _dot(merged.astype(BF16), wo_ref[...])
    o_ref[0] = _deepnorm_ln(x, h, gt_ref[0], lg_ref[...], lb_ref[...], alpha)


def _merge(x, shift, scale, gate, oas, las, yb, yc, yd, wg, bg, wb, wo, ln_g, ln_b, alpha):
    bsz, t, d = x.shape
    tm = _pick(t, (256, 128))
    row = lambda b, i: (b, 0, 0)
    tok = lambda w: pl.BlockSpec((1, tm, w), lambda b, i: (b, i, 0))
    wspec = lambda shape: pl.BlockSpec(shape, lambda b, i: (0, 0), pipeline_mode=pl.Buffered(1))
    return pl.pallas_call(
        functools.partial(_merge_kernel, alpha=alpha),
        grid=(bsz, t // tm),
        in_specs=[tok(d), pl.BlockSpec((1, 1, d), row), pl.BlockSpec((1, 1, d), row), pl.BlockSpec((1, 1, d), row),
                  tok(LANE), tok(LANE), tok(LANE), tok(LANE), tok(LANE), tok(LANE),
                  tok(2 * LANE), tok(2 * LANE), tok(2 * LANE),
                  wspec(wg.shape), wspec(bg.shape), wspec(wb.shape), wspec(wo.shape),
                  pl.BlockSpec((1, d), lambda b, i: (0, 0)), pl.BlockSpec((1, d), lambda b, i: (0, 0))],
        out_specs=tok(d),
        out_shape=jax.ShapeDtypeStruct((bsz, t, d), F32),
        compiler_params=_params(("parallel", "parallel")),
        name="gated_merge",
    )(x, shift, scale, gate, *oas, *las, yb, yc, yd, wg, bg, wb, wo, ln_g, ln_b)


def _rope_tables(t):
    half = ROT_DIM // 2
    inv_freq = ROPE_THETA ** (-(jnp.arange(half, dtype=F32) * (2.0 / ROT_DIM)))
    ang = jnp.arange(t, dtype=F32)[:, None] * inv_freq[None, :]
    cos, sin = jnp.cos(ang), jnp.sin(ang)
    ones = jnp.ones((t, HEAD_DIM - ROT_DIM), F32)
    zeros = jnp.zeros((t, HEAD_DIM - half), F32)
    c64 = jnp.concatenate([cos, cos, ones], axis=1)
    s1 = jnp.concatenate([-sin, zeros], axis=1)
    s2 = jnp.concatenate([jnp.zeros((t, half), F32), sin, zeros[:, half:]], axis=1)
    return tuple(jnp.concatenate([a, a], axis=1) for a in (c64, s1, s2))


def _mixer_weights(w_in):
    d = w_in.shape[0]

    def qkv(off, heads):
        w = heads * HEAD_DIM
        return w_in[:, off:off + w], w_in[:, off + w:off + 2 * w], w_in[:, off + 2 * w:off + 3 * w]

    aq, ak, av = qkv(OFF_A, A_HEADS)
    bq, bk, bv = qkv(OFF_B, B_HEADS)
    cq, ck, cv = qkv(OFF_C, C_HEADS)
    dq, dk, dv = qkv(OFF_D, D_HEADS)
    w16 = jnp.concatenate([aq, ak, bq, bk, av, bv, cq, ck, cv, dq, dk, dv, jnp.zeros((d, LANE), F32)], axis=1)
    ix = jnp.concatenate([w_in[:, OFF_IK:OFF_IK + IDX_DIM], jnp.zeros((d, LANE_IW - IDX_DIM), F32),
                          w_in[:, OFF_IW:OFF_IW + IDX_HEADS], w_in[:, OFF_FG:OFF_FG + D_HEADS],
                          jnp.zeros((d, LANE - LANE_FG - D_HEADS), F32)], axis=1)
    w32 = jnp.concatenate([w_in[:, OFF_IQ:OFF_IQ + IDX_HEADS * IDX_DIM], ix, jnp.zeros((d, LANE), F32)], axis=1)
    return w16.astype(BF16), w32.astype(BF16), w_in[:, OFF_GATE:].astype(BF16)


def _to_residues(a, dil):
    bsz, t, w = a.shape
    if dil == 1:
        return a
    return a.reshape(bsz, t // dil, dil, w).transpose(0, 2, 1, 3).reshape(bsz * dil, t // dil, w)


def _from_residues(a, dil, bsz):
    if dil == 1:
        return a
    _, ln, w = a.shape
    return a.reshape(bsz, dil, ln, w).transpose(0, 2, 1, 3).reshape(bsz, ln * dil, w)


def _mixer(x, shift, scale, gate, w_in, b_gate, b_forget, w_branch, w_out, ln_g, ln_b, tables, alpha):
    bsz, t, d = x.shape
    w16, w32, wg = _mixer_weights(w_in)
    tq = _pick(t, (256, 128))
    tqf = _pick(t, (512, 256, 128))
    p16, p32, vtb, vtc, vtd = _proj(x, shift, scale, w16, w32, tables,
                                    ((BLK_BV, tqf), (BLK_CV, tq), (BLK_DV, tqf)))

    oas, las = [], []
    for g, (_, dil) in enumerate(DILATED_GROUPS):
        if dil == 1:
            o, lse = _dilated(p16, p16, p16, cols=(BLK_AQ + g, BLK_AK + g, BLK_AV + g))
        else:
            streams = [_to_residues(p16[:, :, (blk + g) * LANE:(blk + g + 1) * LANE], dil)
                       for blk in (BLK_AQ, BLK_AK, BLK_AV)]
            o, lse = _dilated(*streams)
        oas.append(_from_residues(o, dil, bsz))
        las.append(_from_residues(lse, dil, bsz))

    bias_row = jnp.zeros((1, LANE), F32).at[0, LANE_FG:LANE_FG + D_HEADS].set(b_forget)
    fbias, kcat = _cum(p32, bias_row, tq)
    yd = _fox(p16, vtd, fbias, tqf)
    yc = _stick_breaking(p16, vtc, tq)
    yb = _dsa(p16, p32, kcat, vtb, tqf)
    return _merge(x, shift, scale, gate, oas, las, yb, yc, yd, wg, b_gate.reshape(1, -1),
                  w_branch.astype(BF16), w_out.astype(BF16), ln_g, ln_b, alpha)


def kernel(x, c, ada_w, ada_b, ln_g, ln_b, ffn_w_in, ffn_w_out, mix_w_in, mix_b_gate, mix_b_forget,
           mix_w_branch, mix_w_out):
    bsz, t, d = x.shape
    depth = ada_w.shape[0]
    f = ffn_w_out.shape[2]
    alpha = float((2 * depth) ** 0.25)
    assert t % (DILATED_GROUPS[-1][0]) == 0 and d % LANE == 0 and bsz <= 8

    c8 = jnp.zeros((8, d), F32).at[:bsz].set(c)
    mod = _ada(c8, ada_w, ada_b)[:, :bsz].reshape(depth, bsz, 3, 3, 1, d)
    tables = _rope_tables(t)

    for l in range(depth):
        m = lambda sub, kind: mod[l, :, sub, kind]
        lng = lambda sub: ln_g[l, sub].reshape(1, d)
        lnb = lambda sub: ln_b[l, sub].reshape(1, d)

        def ffn(x, sub, which):
            w_in = ffn_w_in[l, which].astype(BF16)
            return _ffn(x, m(sub, 0), m(sub, 1), m(sub, 2), w_in[:, :f], w_in[:, f:],
                        ffn_w_out[l, which].astype(BF16), lng(sub), lnb(sub), alpha)

        x = ffn(x, 0, 0)
        x = _mixer(x, m(1, 0), m(1, 1), m(1, 2), mix_w_in[l], mix_b_gate[l], mix_b_forget[l],
                   mix_w_branch[l], mix_w_out[l], lng(1), lnb(1), tables, alpha)
        x = ffn(x, 2, 1)
    return x
```

```python
import functools

import jax
import jax.numpy as jnp
from jax import lax
from jax.experimental import pallas as pl
from jax.experimental.pallas import tpu as pltpu

F32 = jnp.float32
BF16 = jnp.bfloat16
I32 = jnp.int32

LANE = 128
HEAD_DIM = 64
ROT_DIM = HEAD_DIM // 4
ROPE_THETA = 500000.0
DILATED_GROUPS = ((128, 1), (512, 4), (2048, 16))
SPAN = 128
A_HEADS, B_HEADS, C_HEADS, D_HEADS = 6, 4, 4, 4
IDX_HEADS, IDX_DIM = 4, 64
DSA_TOPK = 256
N_BRANCH = 4
LN_EPS = 1e-5
BRANCH_WIDTHS = (128, 256, 256, 256)

OFF_A = 0
OFF_B = OFF_A + 3 * A_HEADS * HEAD_DIM
OFF_IQ = OFF_B + 3 * B_HEADS * HEAD_DIM
OFF_IK = OFF_IQ + IDX_HEADS * IDX_DIM
OFF_IW = OFF_IK + IDX_DIM
OFF_C = OFF_IW + IDX_HEADS
OFF_D = OFF_C + 3 * C_HEADS * HEAD_DIM
OFF_FG = OFF_D + 3 * D_HEADS * HEAD_DIM
OFF_GATE = OFF_FG + D_HEADS

BLK_AQ, BLK_AK, BLK_BQ, BLK_BK = 0, 3, 6, 8
N_ROPE_BLKS_16 = 10
BLK_AV, BLK_BV = 10, 13
BLK_CQ, BLK_CK, BLK_CV = 15, 17, 19
BLK_DQ, BLK_DK, BLK_DV = 21, 23, 25
N_BLKS_16 = 28
BLK_IQ, BLK_IX = 0, 2
N_BLKS_32 = 4
LANE_IW = 80
LANE_FG = 84

INT_MIN = -2147483648
NEG_BIG = -1e30
LOG2E = 1.4426950408889634
EXP_DEAD = -104.0
STRIP = 32
VMEM_LIMIT = 56 * 1024 * 1024


def _nt(a, b):
    return lax.dot_general(a, b, (((1,), (1,)), ((), ())), preferred_element_type=F32)


def _dot(a, b):
    return jnp.dot(a, b, preferred_element_type=F32)


def _split3(x):
    hi = x.astype(BF16)
    r1 = x - hi.astype(F32)
    mid = r1.astype(BF16)
    lo = (r1 - mid.astype(F32)).astype(BF16)
    return hi, mid, lo


def _split2(x):
    hi = x.astype(BF16)
    return hi, (x - hi.astype(F32)).astype(BF16)


def _log_sigmoid(z):
    return -(jnp.maximum(-z, 0.0) + jnp.log(1.0 + jnp.exp(-jnp.abs(z))))


def _params(sem):
    return pltpu.CompilerParams(dimension_semantics=sem, vmem_limit_bytes=VMEM_LIMIT)


def _const_spec(shape):
    n = len(shape)
    return pl.BlockSpec(shape, lambda *_: (0,) * n)


def _pick(n, prefs):
    for p in prefs:
        if n % p == 0:
            return p
    return n


def _ada_kernel(c_ref, w_ref, b_ref, o_ref):
    c = c_ref[...]
    cond = c * jax.nn.sigmoid(c)
    ch, cm, _ = _split3(cond)
    wh, wm, _ = _split3(w_ref[0])
    o_ref[0] = _dot(ch, wh) + _dot(ch, wm) + _dot(cm, wh) + b_ref[0]


def _ada(c8, ada_w, ada_b):
    depth, d, n = ada_w.shape
    tn = _pick(n, (1152, 1024, 512, 256, 128))
    return pl.pallas_call(
        _ada_kernel,
        grid=(depth, n // tn),
        in_specs=[pl.BlockSpec((8, d), lambda l, j: (0, 0)),
                  pl.BlockSpec((1, d, tn), lambda l, j: (l, 0, j)),
                  pl.BlockSpec((1, 1, tn), lambda l, j: (l, 0, j))],
        out_specs=pl.BlockSpec((1, 8, tn), lambda l, j: (l, 0, j)),
        out_shape=jax.ShapeDtypeStruct((depth, 8, n), F32),
        compiler_params=_params(("parallel", "parallel")),
        name="ada_mod",
    )(c8, ada_w, ada_b.reshape(depth, 1, n))


def _deepnorm_ln(x, h, gate, g, b, alpha):
    y = alpha * x + (1.0 + gate) * h
    mu = jnp.mean(y, axis=-1, keepdims=True)
    yc = y - mu
    var = jnp.mean(yc * yc, axis=-1, keepdims=True)
    return yc * lax.rsqrt(var + LN_EPS) * g + b


def _ffn_kernel(x_ref, sh_ref, sc_ref, gt_ref, wg_ref, wu_ref, wo_ref, lg_ref, lb_ref, o_ref, *, tf, alpha):
    x = x_ref[0]
    u = (x * (1.0 + sc_ref[0]) + sh_ref[0]).astype(BF16)
    acc = jnp.zeros(x.shape, F32)
    for j in range(wg_ref.shape[1] // tf):
        g = _dot(u, wg_ref[:, j * tf:(j + 1) * tf])
        up = _dot(u, wu_ref[:, j * tf:(j + 1) * tf])
        h = (g * jax.nn.sigmoid(g) * up).astype(BF16)
        acc = acc + _dot(h, wo_ref[j * tf:(j + 1) * tf, :])
    o_ref[0] = _deepnorm_ln(x, 0.5 * acc, gt_ref[0], lg_ref[...], lb_ref[...], alpha)


def _ffn(x, shift, scale, gate, wg, wu, wo, ln_g, ln_b, alpha):
    bsz, t, d = x.shape
    f = wg.shape[1]
    tm = _pick(t, (512, 256, 128))
    tf = _pick(f, (256, 128))
    row = lambda b, i: (b, 0, 0)
    wspec = lambda shape: pl.BlockSpec(shape, lambda b, i: (0, 0), pipeline_mode=pl.Buffered(1))
    return pl.pallas_call(
        functools.partial(_ffn_kernel, tf=tf, alpha=alpha),
        grid=(bsz, t // tm),
        in_specs=[pl.BlockSpec((1, tm, d), lambda b, i: (b, i, 0)),
                  pl.BlockSpec((1, 1, d), row), pl.BlockSpec((1, 1, d), row), pl.BlockSpec((1, 1, d), row),
                  wspec((d, f)), wspec((d, f)), wspec((f, d)),
                  pl.BlockSpec((1, d), lambda b, i: (0, 0)), pl.BlockSpec((1, d), lambda b, i: (0, 0))],
        out_specs=pl.BlockSpec((1, tm, d), lambda b, i: (b, i, 0)),
        out_shape=jax.ShapeDtypeStruct((bsz, t, d), F32),
        compiler_params=_params(("parallel", "parallel")),
        name="ffn",
    )(x, shift, scale, gate, wg, wu, wo, ln_g, ln_b)


def _proj_kernel(x_ref, sh_ref, sc_ref, w16_ref, w32_ref, cos_ref, s1_ref, s2_ref, o16_ref, o32_ref, *vt_refs,
                 tn, vt_blks):
    u = (x_ref[0] * (1.0 + sc_ref[0]) + sh_ref[0]).astype(BF16)
    per = tn // LANE
    for w_ref, o_ref, n_rope_blks in ((w16_ref, o16_ref, N_ROPE_BLKS_16), (w32_ref, o32_ref, N_BLKS_32)):
        for j in range(w_ref.shape[1] // tn):
            y = _dot(u, w_ref[:, j * tn:(j + 1) * tn])
            for i in range(per):
                blk = j * per + i
                yi = y[:, i * LANE:(i + 1) * LANE]
                if blk < n_rope_blks:
                    yi = (yi * cos_ref[...] + pltpu.roll(yi, LANE - ROT_DIM // 2, 1) * s1_ref[...]
                          + pltpu.roll(yi, ROT_DIM // 2, 1) * s2_ref[...])
                o_ref[0, :, blk * LANE:(blk + 1) * LANE] = yi.astype(o_ref.dtype)
    for vt_ref, blk0 in zip(vt_refs, vt_blks):
        _, npair, nparts, _, tqc = vt_ref.shape
        for pair in range(npair):
            vt = o16_ref[0, :, (blk0 + pair) * LANE:(blk0 + pair + 1) * LANE].astype(F32).T
            for part in range(nparts):
                vt_ref[0, pair, part] = vt[:, part * tqc:(part + 1) * tqc].astype(BF16)


def _proj(x, shift, scale, w16, w32, tables, vt_tiles):
    bsz, t, d = x.shape
    n16, n32 = w16.shape[1], w32.shape[1]
    tm = _pick(t, (512, 256, 128))
    row = lambda b, i: (b, 0, 0)
    tab = pl.BlockSpec((tm, LANE), lambda b, i: (i, 0))
    wspec = lambda n: pl.BlockSpec((d, n), lambda b, i: (0, 0), pipeline_mode=pl.Buffered(1))
    vt_specs = [pl.BlockSpec((1, 2, tm // tq, LANE, tq), lambda b, i: (b, 0, i, 0, 0)) for _, tq in vt_tiles]
    vt_shapes = [jax.ShapeDtypeStruct((bsz, 2, t // tq, LANE, tq), BF16) for _, tq in vt_tiles]
    return pl.pallas_call(
        functools.partial(_proj_kernel, tn=2 * LANE, vt_blks=tuple(blk for blk, _ in vt_tiles)),
        grid=(bsz, t // tm),
        in_specs=[pl.BlockSpec((1, tm, d), lambda b, i: (b, i, 0)),
                  pl.BlockSpec((1, 1, d), row), pl.BlockSpec((1, 1, d), row),
                  wspec(n16), wspec(n32), tab, tab, tab],
        out_specs=[pl.BlockSpec((1, tm, n16), lambda b, i: (b, i, 0)),
                   pl.BlockSpec((1, tm, n32), lambda b, i: (b, i, 0))] + vt_specs,
        out_shape=[jax.ShapeDtypeStruct((bsz, t, n16), BF16), jax.ShapeDtypeStruct((bsz, t, n32), F32)] + vt_shapes,
        compiler_params=_params(("parallel", "parallel")),
        name="mixer_proj",
    )(x, shift, scale, w16, w32, *tables)


def _cum_kernel(p_ref, bias_ref, out_ref, kcat_ref, carry_ref):
    @pl.when(pl.program_id(1) == 0)
    def _():
        carry_ref[...] = jnp.zeros_like(carry_ref)

    tc = p_ref.shape[1]
    lane = lax.broadcasted_iota(I32, (tc, LANE), 1)
    lf = _log_sigmoid(p_ref[0] + bias_ref[...])
    lf = jnp.where(lane >= LANE_FG, jnp.where(lane < LANE_FG + D_HEADS, lf, 0.0), 0.0)
    ri = lax.broadcasted_iota(I32, (tc, tc), 0)
    ci = lax.broadcasted_iota(I32, (tc, tc), 1)
    tri = jnp.where(ri >= ci, 1.0, 0.0).astype(BF16)
    hi, mid, lo = _split3(lf)
    cum = _dot(tri, hi) + _dot(tri, mid) + _dot(tri, lo) + carry_ref[...]
    carry_ref[...] = cum[tc - 1:tc, :]
    er = lax.broadcasted_iota(I32, (LANE, LANE), 0)
    ch, cm, cl = _split3(-LOG2E * cum)
    for h in range(D_HEADS):
        onehot = jnp.where(er == LANE_FG + h, 1.0, 0.0).astype(BF16)
        out_ref[0, h] = _dot(ch, onehot) + _dot(cm, onehot) + _dot(cl, onehot)
    x = p_ref[0]
    hi = x.astype(BF16).astype(F32)
    kcat_ref[0, :, 0:LANE] = jnp.where(lane < IDX_DIM, hi, pltpu.roll(hi, IDX_DIM, 1)).astype(BF16)
    kcat_ref[0, :, LANE:2 * LANE] = jnp.where(lane < IDX_DIM, x - hi, 0.0).astype(BF16)


def _cum(p32, bias_row, tc):
    bsz, t, _ = p32.shape
    return pl.pallas_call(
        _cum_kernel,
        grid=(bsz, t // tc),
        in_specs=[pl.BlockSpec((1, tc, LANE), lambda b, i: (b, i, BLK_IX)),
                  pl.BlockSpec((1, LANE), lambda b, i: (0, 0))],
        out_specs=[pl.BlockSpec((1, D_HEADS, tc, LANE), lambda b, i: (b, 0, i, 0)),
                   pl.BlockSpec((1, tc, 2 * LANE), lambda b, i: (b, i, 0))],
        out_shape=[jax.ShapeDtypeStruct((bsz, D_HEADS, t, LANE), F32),
                   jax.ShapeDtypeStruct((bsz, t, 2 * LANE), BF16)],
        scratch_shapes=[pltpu.VMEM((1, LANE), F32)],
        compiler_params=_params(("parallel", "arbitrary")),
        name="forget_cumsum",
    )(p32, bias_row)


def _head_halves(q):
    lane = lax.broadcasted_iota(I32, q.shape, 1)
    zero = jnp.zeros_like(q)
    return jnp.where(lane < HEAD_DIM, q, zero), jnp.where(lane >= HEAD_DIM, q, zero)


def _static_loop(n, body, carry):
    for i in range(n):
        carry = body(i, carry)
    return carry


def _fold8(x, op):
    out = x[0:8]
    for r in range(8, x.shape[0], 8):
        out = op(out, x[r:r + 8])
    return out


def _lane_pick(a, b):
    lane = lax.broadcasted_iota(I32, a.shape, 1)
    return jnp.where(lane < HEAD_DIM, a, b)


def _dil_kernel(q_ref, k_ref, v_ref, o_ref, lse_ref, *, kw):
    nb = q_ref.shape[1] // SPAN

    def body(i, carry):
        qs = pl.multiple_of(i * SPAN, SPAN)
        ks = pl.multiple_of(jnp.maximum(i * SPAN + SPAN - kw, 0), SPAN)
        q = q_ref[0, pl.ds(qs, SPAN), :] * 0.125
        k = k_ref[0, pl.ds(ks, kw), :]
        v = v_ref[0, pl.ds(ks, kw), :]
        qpos = qs + lax.broadcasted_iota(I32, (SPAN, kw), 0)
        kpos = ks + lax.broadcasted_iota(I32, (SPAN, kw), 1)
        dist = qpos - kpos
        outs, lses = [], []
        for qh in _head_halves(q):
            s = _nt(qh, k)
            s = jnp.where(dist >= 0, jnp.where(dist <= SPAN, s, NEG_BIG), NEG_BIG)
            m = jnp.max(s, axis=-1, keepdims=True)
            p = jnp.exp(s - m)
            l = jnp.sum(p, axis=-1, keepdims=True)
            outs.append(_dot(p.astype(BF16), v) / l)
            lses.append(jnp.broadcast_to(m + jnp.log(l), (SPAN, LANE)))
        o_ref[0, pl.ds(qs, SPAN), :] = _lane_pick(outs[0], outs[1]).astype(o_ref.dtype)
        lse_ref[0, pl.ds(qs, SPAN), :] = _lane_pick(lses[0], lses[1])
        return carry

    lax.fori_loop(0, nb, body, 0, unroll=min(nb, 4))


def _dilated(p16, cols, dil):
    bsz, t, n = p16.shape
    ln, nblk = t // dil, n // LANE
    kw = min(2 * SPAN, ln)
    view = p16.reshape(bsz, ln, dil * n)
    in_spec = lambda col: pl.BlockSpec((1, ln, LANE), lambda b, r: (b, 0, r * nblk + col))
    out_spec = pl.BlockSpec((1, ln, LANE), lambda b, r: (b, 0, r))
    o, lse = pl.pallas_call(
        functools.partial(_dil_kernel, kw=kw),
        grid=(bsz, dil),
        in_specs=[in_spec(c) for c in cols],
        out_specs=[out_spec, out_spec],
        out_shape=[jax.ShapeDtypeStruct((bsz, ln, dil * LANE), BF16),
                   jax.ShapeDtypeStruct((bsz, ln, dil * LANE), F32)],
        compiler_params=_params(("parallel", "parallel")),
        name="dilated_window",
    )(view, view, view)
    return o.reshape(bsz, t, LANE), lse.reshape(bsz, t, LANE)


def _fox_kernel(q_ref, k_ref, vt_ref, bias_ref, o_ref, s_scr, p_scr, al_scr, acc_scr, m_scr, l_scr, *, tq):
    qi = pl.program_id(2)
    rep = tq // LANE
    nstrip = tq // STRIP
    qh = _head_halves((q_ref[0].astype(F32) * (0.125 * LOG2E)).astype(BF16))
    m_scr[...] = jnp.full(m_scr.shape, NEG_BIG, F32)
    l_scr[...] = jnp.zeros(l_scr.shape, F32)
    acc_scr[...] = jnp.zeros(acc_scr.shape, F32)
    kk = lax.broadcasted_iota(I32, (STRIP, tq), 0)
    qq = lax.broadcasted_iota(I32, (STRIP, tq), 1)

    def issue_scores(j, s_buf):
        kblk = k_ref[0, pl.ds(pl.multiple_of(j * tq, tq), tq), :]
        for h in range(2):
            s_buf[h] = _nt(kblk, qh[h])

    def softmax(j, s_buf, p_buf, al_buf, diag):
        ks = pl.multiple_of(j * tq, tq)
        for h in range(2):
            def pass1(i, mx):
                r0 = i * STRIP
                s = s_buf[h, pl.ds(r0, STRIP), :] + jnp.tile(
                    bias_ref[0, h, pl.ds(pl.multiple_of(ks + r0, STRIP), STRIP), :], (1, rep))
                if diag:
                    s = jnp.where(kk + r0 <= qq, s, NEG_BIG)
                s_buf[h, pl.ds(r0, STRIP), :] = s
                return jnp.maximum(mx, _fold8(s, jnp.maximum))

            mx = _static_loop(nstrip, pass1, jnp.full((8, tq), NEG_BIG, F32))
            m_old = m_scr[h]
            m_new = jnp.maximum(m_old, jnp.max(mx, axis=0, keepdims=True))
            al_buf[h] = jnp.exp2(m_old - m_new)
            m_scr[h] = m_new

            def pass2(i, ls):
                r0 = i * STRIP
                p = jnp.exp2(s_buf[h, pl.ds(r0, STRIP), :] - m_new)
                p_buf[h, pl.ds(r0, STRIP), :] = p.astype(BF16)
                return ls + _fold8(p, jnp.add)

            ls = _static_loop(nstrip, pass2, jnp.zeros((8, tq), F32))
            l_scr[h] = al_buf[h] * l_scr[h] + jnp.sum(ls, axis=0, keepdims=True)

    def apply_values(j, p_buf, al_buf):
        for h in range(2):
            pv = _dot(vt_ref[0, 0, j, h * HEAD_DIM:(h + 1) * HEAD_DIM, :], p_buf[h])
            acc_scr[h] = acc_scr[h] * al_buf[h] + pv

    s_a, s_b = s_scr.at[0], s_scr.at[1]
    p_a, p_b = p_scr.at[0], p_scr.at[1]
    al_a, al_b = al_scr.at[0], al_scr.at[1]
    p_b[...] = jnp.zeros(p_b.shape, BF16)
    al_b[...] = jnp.ones(al_b.shape, F32)
    issue_scores(0, s_a)

    def pair(i, carry):
        a = 2 * i
        issue_scores(a + 1, s_b)
        softmax(a, s_a, p_a, al_a, False)
        apply_values(jnp.maximum(a - 1, 0), p_b, al_b)
        issue_scores(a + 2, s_a)
        softmax(a + 1, s_b, p_b, al_b, False)
        apply_values(a, p_a, al_a)
        return carry

    lax.fori_loop(0, qi // 2, pair, 0)
    last = 2 * (qi // 2)

    @pl.when(qi % 2 == 1)
    def _():
        issue_scores(qi, s_b)
        softmax(last, s_a, p_a, al_a, False)
        apply_values(jnp.maximum(last - 1, 0), p_b, al_b)
        softmax(qi, s_b, p_b, al_b, True)
        apply_values(last, p_a, al_a)
        apply_values(qi, p_b, al_b)

    @pl.when(qi % 2 == 0)
    def _():
        softmax(qi, s_a, p_a, al_a, True)
        apply_values(jnp.maximum(qi - 1, 0), p_b, al_b)
        apply_values(qi, p_a, al_a)

    out_t = jnp.concatenate([acc_scr[h] * (1.0 / l_scr[h]) for h in range(2)], axis=0)
    o_ref[0] = out_t.T.astype(o_ref.dtype)


def _fox(p16, vt, bias, tq):
    bsz, t, _ = p16.shape
    npair = D_HEADS // 2
    nblk = t // tq
    return pl.pallas_call(
        functools.partial(_fox_kernel, tq=tq),
        grid=(bsz, npair, nblk),
        in_specs=[pl.BlockSpec((1, tq, LANE), lambda b, h, i: (b, i, BLK_DQ + h)),
                  pl.BlockSpec((1, t, LANE), lambda b, h, i: (b, 0, BLK_DK + h)),
                  pl.BlockSpec((1, 1, nblk, LANE, tq), lambda b, h, i: (b, h, 0, 0, 0)),
                  pl.BlockSpec((1, 2, t, LANE), lambda b, h, i: (b, h, 0, 0))],
        out_specs=pl.BlockSpec((1, tq, LANE), lambda b, h, i: (b, i, h)),
        out_shape=jax.ShapeDtypeStruct((bsz, t, npair * LANE), BF16),
        scratch_shapes=[pltpu.VMEM((2, 2, tq, tq), F32), pltpu.VMEM((2, 2, tq, tq), BF16),
                        pltpu.VMEM((2, 2, 1, tq), F32),
                        pltpu.VMEM((2, HEAD_DIM, tq), F32), pltpu.VMEM((2, 1, tq), F32),
                        pltpu.VMEM((2, 1, tq), F32)],
        compiler_params=_params(("parallel", "parallel", "arbitrary")),
        name="forgetting_attention",
    )(p16, p16, vt, bias)


def _sb_kernel(q_ref, k_ref, vt_ref, o_ref, z_scr, sfx_scr, hi_scr, lo_scr, a_scr, acc_scr, r_scr, *, tq):
    qi = pl.program_id(2)
    nstrip = tq // STRIP
    qh = _head_halves(q_ref[0] * 0.125)
    acc_scr[...] = jnp.zeros(acc_scr.shape, F32)
    r_scr[...] = jnp.zeros(r_scr.shape, F32)
    kk = lax.broadcasted_iota(I32, (STRIP, tq), 0)
    qq = lax.broadcasted_iota(I32, (STRIP, tq), 1)
    row = lax.broadcasted_iota(I32, (tq, tq), 0)
    col = lax.broadcasted_iota(I32, (tq, tq), 1)
    later = jnp.where(col > row, 1.0, 0.0).astype(BF16)

    def block(j, diag):
        ks = pl.multiple_of(j * tq, tq)
        kblk = k_ref[0, pl.ds(ks, tq), :]
        for h in range(2):
            z_scr[h] = _nt(kblk, qh[h])
        for h in range(2):
            def pass1(i, rsum):
                r0 = i * STRIP
                z = z_scr[h, pl.ds(r0, STRIP), :]
                lk = _log_sigmoid(-z)
                z_scr[h, pl.ds(r0, STRIP), :] = lk + z
                if diag:
                    lk = jnp.where(kk + r0 < qq, lk, 0.0)
                hi = lk.astype(BF16)
                hi_scr[h, pl.ds(r0, STRIP), :] = hi
                lo_scr[h, pl.ds(r0, STRIP), :] = (lk - hi.astype(F32)).astype(BF16)
                return rsum + _fold8(lk, jnp.add)

            rsum = _static_loop(nstrip, pass1, jnp.zeros((8, tq), F32))
            sfx_scr[h] = _dot(later, hi_scr[h]) + _dot(later, lo_scr[h])
            r_old = r_scr[h]

            def pass2(i, carry):
                r0 = i * STRIP
                a = jnp.exp(z_scr[h, pl.ds(r0, STRIP), :] + sfx_scr[h, pl.ds(r0, STRIP), :] + r_old)
                if diag:
                    a = jnp.where(kk + r0 < qq, a, 0.0)
                a_scr[h, pl.ds(r0, STRIP), :] = a.astype(BF16)
                return carry

            _static_loop(nstrip, pass2, 0)
            acc_scr[h] = acc_scr[h] + _dot(vt_ref[0, 0, j, h * HEAD_DIM:(h + 1) * HEAD_DIM, :], a_scr[h])
            r_scr[h] = r_old + jnp.sum(rsum, axis=0, keepdims=True)

    block(qi, True)

    def live():
        return jnp.max(jnp.maximum(r_scr[0], r_scr[1])) > EXP_DEAD

    def cond(c):
        return jnp.logical_and(c[0] >= 0, c[1])

    def body(c):
        block(c[0], False)
        return c[0] - 1, live()

    lax.while_loop(cond, body, (qi - 1, live()))
    o_ref[0] = jnp.concatenate([acc_scr[0], acc_scr[1]], axis=0).T.astype(o_ref.dtype)


def _stick_breaking(p16, vt, tq):
    bsz, t, _ = p16.shape
    npair = C_HEADS // 2
    nblk = t // tq
    return pl.pallas_call(
        functools.partial(_sb_kernel, tq=tq),
        grid=(bsz, npair, nblk),
        in_specs=[pl.BlockSpec((1, tq, LANE), lambda b, h, i: (b, i, BLK_CQ + h)),
                  pl.BlockSpec((1, t, LANE), lambda b, h, i: (b, 0, BLK_CK + h)),
                  pl.BlockSpec((1, 1, nblk, LANE, tq), lambda b, h, i: (b, h, 0, 0, 0))],
        out_specs=pl.BlockSpec((1, tq, LANE), lambda b, h, i: (b, i, h)),
        out_shape=jax.ShapeDtypeStruct((bsz, t, npair * LANE), BF16),
        scratch_shapes=[pltpu.VMEM((2, tq, tq), F32), pltpu.VMEM((2, tq, tq), F32),
                        pltpu.VMEM((2, tq, tq), BF16), pltpu.VMEM((2, tq, tq), BF16),
                        pltpu.VMEM((2, tq, tq), BF16), pltpu.VMEM((2, HEAD_DIM, tq), F32),
                        pltpu.VMEM((2, 1, tq), F32)],
        compiler_params=_params(("parallel", "parallel", "arbitrary")),
        name="stick_breaking",
    )(p16, p16, vt)


def _dsa_kernel(iq0_ref, iq1_ref, ixq_ref, kcat_ref, q0_ref, q1_ref, k0_ref, k1_ref, vt_ref, o_ref,
                key_scr, s_scr, p_scr, eq_scr, rank_scr, bias_scr, al_scr, acc_scr, m_scr, l_scr, *, tq, topk):
    qi = pl.program_id(1)
    nstrip = tq // STRIP
    lane = lax.broadcasted_iota(I32, (tq, LANE), 1)
    kk = lax.broadcasted_iota(I32, (STRIP, tq), 0)
    qq = lax.broadcasted_iota(I32, (STRIP, tq), 1)

    qcat = []
    for ref in (iq0_ref, iq1_ref):
        for half in range(2):
            x = ref[0] if half == 0 else pltpu.roll(ref[0], IDX_DIM, 1)
            hi = x.astype(BF16).astype(F32)
            first = jnp.where(lane < IDX_DIM, hi, pltpu.roll(x - hi, IDX_DIM, 1))
            second = jnp.where(lane < IDX_DIM, hi, 0.0)
            qcat.append(jnp.concatenate([first, second], axis=1).astype(BF16))
    sr = lax.broadcasted_iota(I32, (8, LANE), 0)
    sl = lax.broadcasted_iota(I32, (8, LANE), 1)
    onehot = jnp.where(sl == sr + LANE_IW, 1.0, 0.0).astype(BF16)
    wh, wm, wl = _split3(ixq_ref[0])
    wt = _nt(onehot, wh) + _nt(onehot, wm) + _nt(onehot, wl)

    def score_chunk(c, diag, counts):
        ge0, gt0 = counts
        ks = pl.multiple_of(c * tq, tq)
        kc = kcat_ref[0, pl.ds(ks, tq), :]
        for h in range(IDX_HEADS):
            s_scr[0, h] = _nt(kc, qcat[h])
        for i in range(nstrip):
            r0 = i * STRIP
            sc = jnp.zeros((STRIP, tq), F32)
            for h in range(IDX_HEADS):
                sc = sc + wt[h:h + 1, :] * jnp.maximum(s_scr[0, h, r0:r0 + STRIP, :], 0.0)
            sc = jnp.where(sc == 0.0, 0.0, sc)
            bits = lax.bitcast_convert_type(sc, I32)
            key = jnp.where(bits < 0, bits ^ jnp.int32(0x7FFFFFFF), bits)
            if diag:
                key = jnp.where(kk + r0 <= qq, key, INT_MIN)
            key_scr[c, r0:r0 + STRIP, :] = key
            ge0 = ge0 + _fold8(jnp.where(key >= 0, 1.0, 0.0), jnp.add)
            gt0 = gt0 + _fold8(jnp.where(key > 0, 1.0, 0.0), jnp.add)
        return ge0, gt0

    zeros = jnp.zeros((8, tq), F32)
    counts = lax.fori_loop(0, qi, lambda c, counts: score_chunk(c, False, counts), (zeros, zeros))
    ge0, gt0 = score_chunk(qi, True, counts)
    n_ge0 = jnp.sum(ge0, axis=0, keepdims=True)
    n_gt0 = jnp.sum(gt0, axis=0, keepdims=True)

    def count(pred):
        def body(c, acc):
            for i in range(nstrip):
                acc = jnp.where(pred(key_scr[c, i * STRIP:(i + 1) * STRIP, :]), acc + 1.0, acc)
            return acc
        acc = lax.fori_loop(0, qi + 1, body, jnp.zeros((STRIP, tq), F32))
        return jnp.sum(acc, axis=0, keepdims=True)

    kf = jnp.float32(topk)
    th0 = jnp.where(n_ge0 >= kf, 0, INT_MIN).astype(I32)
    settled0 = jnp.where(n_ge0 >= kf, jnp.where(n_gt0 < kf, 1.0, 0.0), 0.0)
    n_up0 = jnp.where(n_ge0 >= kf, jnp.where(n_gt0 < kf, n_gt0, 0.0), n_ge0)

    def unsettled(settled):
        return jnp.min(settled) == 0.0

    def search_cond(c):
        return jnp.logical_and(c[0] >= 0, c[4])

    def search_body(c):
        bit, th, n_up, settled, _ = c
        cand = th | lax.shift_left(jnp.int32(1), bit)
        n = count(lambda kb: kb >= cand)
        open_ = settled == 0.0
        th = jnp.where(n >= kf, jnp.where(open_, cand, th), th)
        n_up = jnp.where(n < kf, jnp.where(open_, n, n_up), n_up)
        settled = jnp.maximum(settled, jnp.where(n == kf, 1.0, 0.0))
        return bit - 1, th, n_up, settled, unsettled(settled)

    _, th, n_up, _, _ = lax.while_loop(search_cond, search_body,
                                       (jnp.int32(30), th0, n_up0, settled0, unsettled(settled0)))
    need = kf - n_up

    qh = sum((_head_halves((r[0].astype(F32) * (0.125 * LOG2E)).astype(BF16)) for r in (q0_ref, q1_ref)), ())
    k_refs = (k0_ref, k0_ref, k1_ref, k1_ref)
    m_scr[...] = jnp.full(m_scr.shape, NEG_BIG, F32)
    l_scr[...] = jnp.zeros(l_scr.shape, F32)
    acc_scr[...] = jnp.zeros(acc_scr.shape, F32)
    row = lax.broadcasted_iota(I32, (tq, tq), 0)
    col = lax.broadcasted_iota(I32, (tq, tq), 1)
    upto = jnp.where(col <= row, 1.0, 0.0).astype(BF16)

    def issue(c, s_buf, eq_buf, rank_buf):
        ks = pl.multiple_of(c * tq, tq)
        for i in range(nstrip):
            r0 = i * STRIP
            eq_buf[r0:r0 + STRIP, :] = jnp.where(key_scr[c, r0:r0 + STRIP, :] == th, 1.0, 0.0).astype(BF16)
        rank_buf[...] = _dot(upto, eq_buf[...])
        for h in range(B_HEADS):
            s_buf[h] = _nt(k_refs[h][0, pl.ds(ks, tq), :], qh[h])

    def softmax(c, s_buf, rank_buf, p_buf, al_buf, diag, seen):
        for i in range(nstrip):
            r0 = i * STRIP
            kb = key_scr[c, r0:r0 + STRIP, :]
            tie = jnp.where(rank_buf[r0:r0 + STRIP, :] + seen <= need, 0.0, NEG_BIG)
            bias = jnp.where(kb > th, 0.0, jnp.where(kb == th, tie, NEG_BIG))
            if diag:
                bias = jnp.where(kk + r0 <= qq, bias, NEG_BIG)
            bias_scr[r0:r0 + STRIP, :] = bias
        for h in range(B_HEADS):
            def pass1(i, mx):
                r0 = i * STRIP
                s = s_buf[h, r0:r0 + STRIP, :] + bias_scr[r0:r0 + STRIP, :]
                s_buf[h, r0:r0 + STRIP, :] = s
                return jnp.maximum(mx, _fold8(s, jnp.maximum))

            mx = _static_loop(nstrip, pass1, jnp.full((8, tq), NEG_BIG, F32))
            m_old = m_scr[h]
            m_new = jnp.maximum(m_old, jnp.max(mx, axis=0, keepdims=True))
            al_buf[h] = jnp.exp2(m_old - m_new)
            m_scr[h] = m_new

            def pass2(i, ls):
                r0 = i * STRIP
                p = jnp.exp2(s_buf[h, r0:r0 + STRIP, :] - m_new)
                p_buf[h, r0:r0 + STRIP, :] = p.astype(BF16)
                return ls + _fold8(p, jnp.add)

            ls = _static_loop(nstrip, pass2, jnp.zeros((8, tq), F32))
            l_scr[h] = al_buf[h] * l_scr[h] + jnp.sum(ls, axis=0, keepdims=True)
        return seen + rank_buf[tq - 1:tq, :]

    def apply_values(c, p_buf, al_buf):
        for h in range(B_HEADS):
            lo = (h % 2) * HEAD_DIM
            acc_scr[h] = acc_scr[h] * al_buf[h] + _dot(vt_ref[0, h // 2, c, lo:lo + HEAD_DIM, :], p_buf[h])

    s_a, s_b = s_scr.at[0], s_scr.at[1]
    p_a, p_b = p_scr.at[0], p_scr.at[1]
    eq_a, eq_b = eq_scr.at[0], eq_scr.at[1]
    rank_a, rank_b = rank_scr.at[0], rank_scr.at[1]
    al_a, al_b = al_scr.at[0], al_scr.at[1]
    p_b[...] = jnp.zeros(p_b.shape, BF16)
    al_b[...] = jnp.ones(al_b.shape, F32)
    issue(0, s_a, eq_a, rank_a)

    def pair(i, seen):
        a = 2 * i
        issue(a + 1, s_b, eq_b, rank_b)
        seen = softmax(a, s_a, rank_a, p_a, al_a, False, seen)
        apply_values(jnp.maximum(a - 1, 0), p_b, al_b)
        issue(a + 2, s_a, eq_a, rank_a)
        seen = softmax(a + 1, s_b, rank_b, p_b, al_b, False, seen)
        apply_values(a, p_a, al_a)
        return seen

    seen = lax.fori_loop(0, qi // 2, pair, jnp.zeros((1, tq), F32))
    last = 2 * (qi // 2)

    @pl.when(qi % 2 == 1)
    def _():
        issue(qi, s_b, eq_b, rank_b)
        seen_d = softmax(last, s_a, rank_a, p_a, al_a, False, seen)
        apply_values(jnp.maximum(last - 1, 0), p_b, al_b)
        softmax(qi, s_b, rank_b, p_b, al_b, True, seen_d)
        apply_values(last, p_a, al_a)
        apply_values(qi, p_b, al_b)

    @pl.when(qi % 2 == 0)
    def _():
        softmax(qi, s_a, rank_a, p_a, al_a, True, seen)
        apply_values(jnp.maximum(qi - 1, 0), p_b, al_b)
        apply_values(qi, p_a, al_a)

    for g in range(B_HEADS // 2):
        out_t = jnp.concatenate([acc_scr[h] * (1.0 / l_scr[h]) for h in (2 * g, 2 * g + 1)], axis=0)
        o_ref[0, :, g * LANE:(g + 1) * LANE] = out_t.T.astype(o_ref.dtype)


def _dsa(p16, p32, kcat, vt, tq):
    bsz, t, _ = p16.shape
    topk = min(DSA_TOPK, t // 4)
    npair = B_HEADS // 2
    nblk = t // tq
    qspec = lambda arr_blk: pl.BlockSpec((1, tq, LANE), lambda b, i: (b, i, arr_blk))
    once = pl.Buffered(1)
    kspec = lambda arr_blk: pl.BlockSpec((1, t, LANE), lambda b, i: (b, 0, arr_blk), pipeline_mode=once)
    sq = lambda n, dt: pltpu.VMEM((n, tq, tq), dt) if n else pltpu.VMEM((tq, tq), dt)
    return pl.pallas_call(
        functools.partial(_dsa_kernel, tq=tq, topk=topk),
        grid=(bsz, nblk),
        in_specs=[qspec(BLK_IQ), qspec(BLK_IQ + 1), qspec(BLK_IX),
                  pl.BlockSpec((1, t, 2 * LANE), lambda b, i: (b, 0, 0), pipeline_mode=once),
                  qspec(BLK_BQ), qspec(BLK_BQ + 1), kspec(BLK_BK), kspec(BLK_BK + 1),
                  pl.BlockSpec((1, npair, nblk, LANE, tq), lambda b, i: (b, 0, 0, 0, 0), pipeline_mode=once)],
        out_specs=pl.BlockSpec((1, tq, npair * LANE), lambda b, i: (b, i, 0)),
        out_shape=jax.ShapeDtypeStruct((bsz, t, npair * LANE), BF16),
        scratch_shapes=[sq(nblk, I32), pltpu.VMEM((2, B_HEADS, tq, tq), F32), pltpu.VMEM((2, B_HEADS, tq, tq), BF16),
                        sq(2, BF16), sq(2, F32), sq(0, F32), pltpu.VMEM((2, B_HEADS, 1, tq), F32),
                        pltpu.VMEM((B_HEADS, HEAD_DIM, tq), F32), pltpu.VMEM((B_HEADS, 1, tq), F32),
                        pltpu.VMEM((B_HEADS, 1, tq), F32)],
        compiler_params=_params(("parallel", "arbitrary")),
        name="dsa_topk_attention",
    )(p32, p32, p32, kcat, p16, p16, p16, p16, vt)


def _merge_kernel(x_ref, sh_ref, sc_ref, gt_ref, oa0_ref, oa1_ref, oa2_ref, la0_ref, la1_ref, la2_ref,
                  yb_ref, yc_ref, yd_ref, wg_ref, bg_ref, wb_ref, wo_ref, lg_ref, lb_ref, o_ref, *, alpha):
    x = x_ref[0]
    d = x.shape[1]
    u = (x * (1.0 + sc_ref[0]) + sh_ref[0]).astype(BF16)
    lses = [r[0] for r in (la0_ref, la1_ref, la2_ref)]
    top = jnp.maximum(jnp.maximum(lses[0], lses[1]), lses[2])
    es = [jnp.exp(l - top) for l in lses]
    den = es[0] + es[1] + es[2]
    ya = sum((e / den) * r[0].astype(F32) for e, r in zip(es, (oa0_ref, oa1_ref, oa2_ref)))
    branches = (ya.astype(BF16), yb_ref[0], yc_ref[0], yd_ref[0])
    merged = jnp.zeros(x.shape, F32)
    off = 0
    for i, br in enumerate(branches):
        gate = jax.nn.sigmoid(_dot(u, wg_ref[:, i * d:(i + 1) * d]) + bg_ref[:, i * d:(i + 1) * d])
        merged = merged + gate * _dot(br, wb_ref[off:off + BRANCH_WIDTHS[i], :])
        off += BRANCH_WIDTHS[i]
    h = _dot(merged.astype(BF16), wo_ref[...])
    o_ref[0] = _deepnorm_ln(x, h, gt_ref[0], lg_ref[...], lb_ref[...], alpha)


def _merge(x, shift, scale, gate, oas, las, yb, yc, yd, wg, bg, wb, wo, ln_g, ln_b, alpha):
    bsz, t, d = x.shape
    tm = _pick(t, (256, 128))
    row = lambda b, i: (b, 0, 0)
    tok = lambda w: pl.BlockSpec((1, tm, w), lambda b, i: (b, i, 0))
    wspec = lambda shape: pl.BlockSpec(shape, lambda b, i: (0, 0), pipeline_mode=pl.Buffered(1))
    return pl.pallas_call(
        functools.partial(_merge_kernel, alpha=alpha),
        grid=(bsz, t // tm),
        in_specs=[tok(d), pl.BlockSpec((1, 1, d), row), pl.BlockSpec((1, 1, d), row), pl.BlockSpec((1, 1, d), row),
                  tok(LANE), tok(LANE), tok(LANE), tok(LANE), tok(LANE), tok(LANE),
                  tok(2 * LANE), tok(2 * LANE), tok(2 * LANE),
                  wspec(wg.shape), wspec(bg.shape), wspec(wb.shape), wspec(wo.shape),
                  pl.BlockSpec((1, d), lambda b, i: (0, 0)), pl.BlockSpec((1, d), lambda b, i: (0, 0))],
        out_specs=tok(d),
        out_shape=jax.ShapeDtypeStruct((bsz, t, d), F32),
        compiler_params=_params(("parallel", "parallel")),
        name="gated_merge",
    )(x, shift, scale, gate, *oas, *las, yb, yc, yd, wg, bg, wb, wo, ln_g, ln_b)


def _rope_tables(t):
    half = ROT_DIM // 2
    inv_freq = ROPE_THETA ** (-(jnp.arange(half, dtype=F32) * (2.0 / ROT_DIM)))
    ang = jnp.arange(t, dtype=F32)[:, None] * inv_freq[None, :]
    cos, sin = jnp.cos(ang), jnp.sin(ang)
    ones = jnp.ones((t, HEAD_DIM - ROT_DIM), F32)
    zeros = jnp.zeros((t, HEAD_DIM - half), F32)
    c64 = jnp.concatenate([cos, cos, ones], axis=1)
    s1 = jnp.concatenate([-sin, zeros], axis=1)
    s2 = jnp.concatenate([jnp.zeros((t, half), F32), sin, zeros[:, half:]], axis=1)
    return tuple(jnp.concatenate([a, a], axis=1) for a in (c64, s1, s2))


def _mixer_weights(w_in):
    d = w_in.shape[0]

    def qkv(off, heads):
        w = heads * HEAD_DIM
        return w_in[:, off:off + w], w_in[:, off + w:off + 2 * w], w_in[:, off + 2 * w:off + 3 * w]

    aq, ak, av = qkv(OFF_A, A_HEADS)
    bq, bk, bv = qkv(OFF_B, B_HEADS)
    cq, ck, cv = qkv(OFF_C, C_HEADS)
    dq, dk, dv = qkv(OFF_D, D_HEADS)
    w16 = jnp.concatenate([aq, ak, bq, bk, av, bv, cq, ck, cv, dq, dk, dv, jnp.zeros((d, LANE), F32)], axis=1)
    ix = jnp.concatenate([w_in[:, OFF_IK:OFF_IK + IDX_DIM], jnp.zeros((d, LANE_IW - IDX_DIM), F32),
                          w_in[:, OFF_IW:OFF_IW + IDX_HEADS], w_in[:, OFF_FG:OFF_FG + D_HEADS],
                          jnp.zeros((d, LANE - LANE_FG - D_HEADS), F32)], axis=1)
    w32 = jnp.concatenate([w_in[:, OFF_IQ:OFF_IQ + IDX_HEADS * IDX_DIM], ix, jnp.zeros((d, LANE), F32)], axis=1)
    return w16.astype(BF16), w32.astype(BF16), w_in[:, OFF_GATE:].astype(BF16)


def _mixer(x, shift, scale, gate, w_in, b_gate, b_forget, w_branch, w_out, ln_g, ln_b, tables, alpha):
    bsz, t, d = x.shape
    w16, w32, wg = _mixer_weights(w_in)
    tq = _pick(t, (256, 128))
    tqf = _pick(t, (512, 256, 128))
    p16, p32, vtb, vtc, vtd = _proj(x, shift, scale, w16, w32, tables,
                                    ((BLK_BV, tqf), (BLK_CV, tq), (BLK_DV, tqf)))

    oas, las = [], []
    for g, (_, dil) in enumerate(DILATED_GROUPS):
        o, lse = _dilated(p16, (BLK_AQ + g, BLK_AK + g, BLK_AV + g), dil)
        oas.append(o)
        las.append(lse)

    bias_row = jnp.zeros((1, LANE), F32).at[0, LANE_FG:LANE_FG + D_HEADS].set(b_forget)
    fbias, kcat = _cum(p32, bias_row, tq)
    yd = _fox(p16, vtd, fbias, tqf)
    yc = _stick_breaking(p16, vtc, tq)
    yb = _dsa(p16, p32, kcat, vtb, tqf)
    return _merge(x, shift, scale, gate, oas, las, yb, yc, yd, wg, b_gate.reshape(1, -1),
                  w_branch.astype(BF16), w_out.astype(BF16), ln_g, ln_b, alpha)


def kernel(x, c, ada_w, ada_b, ln_g, ln_b, ffn_w_in, ffn_w_out, mix_w_in, mix_b_gate, mix_b_forget,
           mix_w_branch, mix_w_out):
    bsz, t, d = x.shape
    depth = ada_w.shape[0]
    f = ffn_w_out.shape[2]
    alpha = float((2 * depth) ** 0.25)
    assert t % (DILATED_GROUPS[-1][0]) == 0 and d % LANE == 0 and bsz <= 8

    c8 = jnp.zeros((8, d), F32).at[:bsz].set(c)
    mod = _ada(c8, ada_w, ada_b)[:, :bsz].reshape(depth, bsz, 3, 3, 1, d)
    tables = _rope_tables(t)

    for l in range(depth):
        m = lambda sub, kind: mod[l, :, sub, kind]
        lng = lambda sub: ln_g[l, sub].reshape(1, d)
        lnb = lambda sub: ln_b[l, sub].reshape(1, d)

        def ffn(x, sub, which):
            w_in = ffn_w_in[l, which].astype(BF16)
            return _ffn(x, m(sub, 0), m(sub, 1), m(sub, 2), w_in[:, :f], w_in[:, f:],
                        ffn_w_out[l, which].astype(BF16), lng(sub), lnb(sub), alpha)

        x = ffn(x, 0, 0)
        x = _mixer(x, m(1, 0), m(1, 1), m(1, 2), mix_w_in[l], mix_b_gate[l], mix_b_forget[l],
                   mix_w_branch[l], mix_w_out[l], lng(1), lnb(1), tables, alpha)
        x = ffn(x, 2, 1)
    return x
```

```python
import functools

import jax
import jax.numpy as jnp
from jax import lax
from jax.experimental import pallas as pl
from jax.experimental.pallas import tpu as pltpu

F32 = jnp.float32
BF16 = jnp.bfloat16
I32 = jnp.int32

LANE = 128
HEAD_DIM = 64
ROT_DIM = HEAD_DIM // 4
ROPE_THETA = 500000.0
DILATED_GROUPS = ((128, 1), (512, 4), (2048, 16))
SPAN = 128
A_HEADS, B_HEADS, C_HEADS, D_HEADS = 6, 4, 4, 4
IDX_HEADS, IDX_DIM = 4, 64
DSA_TOPK = 256
N_BRANCH = 4
LN_EPS = 1e-5
BRANCH_WIDTHS = (128, 256, 256, 256)

OFF_A = 0
OFF_B = OFF_A + 3 * A_HEADS * HEAD_DIM
OFF_IQ = OFF_B + 3 * B_HEADS * HEAD_DIM
OFF_IK = OFF_IQ + IDX_HEADS * IDX_DIM
OFF_IW = OFF_IK + IDX_DIM
OFF_C = OFF_IW + IDX_HEADS
OFF_D = OFF_C + 3 * C_HEADS * HEAD_DIM
OFF_FG = OFF_D + 3 * D_HEADS * HEAD_DIM
OFF_GATE = OFF_FG + D_HEADS

BLK_AQ, BLK_AK, BLK_BQ, BLK_BK = 0, 3, 6, 8
N_ROPE_BLKS_16 = 10
BLK_AV, BLK_BV = 10, 13
BLK_CQ, BLK_CK, BLK_CV = 15, 17, 19
BLK_DQ, BLK_DK, BLK_DV = 21, 23, 25
N_BLKS_16 = 28
BLK_IQ, BLK_IX = 0, 2
N_BLKS_32 = 4
LANE_IW = 80
LANE_FG = 84

INT_MIN = -2147483648
NEG_BIG = -1e30
LOG2E = 1.4426950408889634
EXP_DEAD = -104.0
STRIP = 32
VMEM_LIMIT = 56 * 1024 * 1024


def _nt(a, b):
    return lax.dot_general(a, b, (((1,), (1,)), ((), ())), preferred_element_type=F32)


def _dot(a, b):
    return jnp.dot(a, b, preferred_element_type=F32)


def _split3(x):
    hi = x.astype(BF16)
    r1 = x - hi.astype(F32)
    mid = r1.astype(BF16)
    lo = (r1 - mid.astype(F32)).astype(BF16)
    return hi, mid, lo


def _split2(x):
    hi = x.astype(BF16)
    return hi, (x - hi.astype(F32)).astype(BF16)


def _log_sigmoid(z):
    return -(jnp.maximum(-z, 0.0) + jnp.log(1.0 + jnp.exp(-jnp.abs(z))))


def _params(sem):
    return pltpu.CompilerParams(dimension_semantics=sem, vmem_limit_bytes=VMEM_LIMIT)


def _const_spec(shape):
    n = len(shape)
    return pl.BlockSpec(shape, lambda *_: (0,) * n)


def _pick(n, prefs):
    for p in prefs:
        if n % p == 0:
            return p
    return n


def _ada_kernel(c_ref, w_ref, b_ref, o_ref):
    c = c_ref[...]
    cond = c * jax.nn.sigmoid(c)
    ch, cm, _ = _split3(cond)
    wh, wm, _ = _split3(w_ref[0])
    o_ref[0] = _dot(ch, wh) + _dot(ch, wm) + _dot(cm, wh) + b_ref[0]


def _ada(c8, ada_w, ada_b):
    depth, d, n = ada_w.shape
    tn = _pick(n, (1152, 1024, 512, 256, 128))
    return pl.pallas_call(
        _ada_kernel,
        grid=(depth, n // tn),
        in_specs=[pl.BlockSpec((8, d), lambda l, j: (0, 0)),
                  pl.BlockSpec((1, d, tn), lambda l, j: (l, 0, j)),
                  pl.BlockSpec((1, 1, tn), lambda l, j: (l, 0, j))],
        out_specs=pl.BlockSpec((1, 8, tn), lambda l, j: (l, 0, j)),
        out_shape=jax.ShapeDtypeStruct((depth, 8, n), F32),
        compiler_params=_params(("parallel", "parallel")),
        name="ada_mod",
    )(c8, ada_w, ada_b.reshape(depth, 1, n))


def _deepnorm_ln(x, h, gate, g, b, alpha):
    y = alpha * x + (1.0 + gate) * h
    mu = jnp.mean(y, axis=-1, keepdims=True)
    yc = y - mu
    var = jnp.mean(yc * yc, axis=-1, keepdims=True)
    return yc * lax.rsqrt(var + LN_EPS) * g + b


def _ffn_kernel(x_ref, sh_ref, sc_ref, gt_ref, wg_ref, wu_ref, wo_ref, lg_ref, lb_ref, o_ref, *, tf, alpha):
    x = x_ref[0]
    u = (x * (1.0 + sc_ref[0]) + sh_ref[0]).astype(BF16)
    acc = jnp.zeros(x.shape, F32)
    for j in range(wg_ref.shape[1] // tf):
        g = _dot(u, wg_ref[:, j * tf:(j + 1) * tf])
        up = _dot(u, wu_ref[:, j * tf:(j + 1) * tf])
        h = (g * jax.nn.sigmoid(g) * up).astype(BF16)
        acc = acc + _dot(h, wo_ref[j * tf:(j + 1) * tf, :])
    o_ref[0] = _deepnorm_ln(x, 0.5 * acc, gt_ref[0], lg_ref[...], lb_ref[...], alpha)


def _ffn(x, shift, scale, gate, wg, wu, wo, ln_g, ln_b, alpha):
    bsz, t, d = x.shape
    f = wg.shape[1]
    tm = _pick(t, (1024, 512, 256, 128))
    tf = _pick(f, (256, 128))
    row = lambda b, i: (b, 0, 0)
    wspec = lambda shape: pl.BlockSpec(shape, lambda b, i: (0, 0), pipeline_mode=pl.Buffered(1))
    return pl.pallas_call(
        functools.partial(_ffn_kernel, tf=tf, alpha=alpha),
        grid=(bsz, t // tm),
        in_specs=[pl.BlockSpec((1, tm, d), lambda b, i: (b, i, 0)),
                  pl.BlockSpec((1, 1, d), row), pl.BlockSpec((1, 1, d), row), pl.BlockSpec((1, 1, d), row),
                  wspec((d, f)), wspec((d, f)), wspec((f, d)),
                  pl.BlockSpec((1, d), lambda b, i: (0, 0)), pl.BlockSpec((1, d), lambda b, i: (0, 0))],
        out_specs=pl.BlockSpec((1, tm, d), lambda b, i: (b, i, 0)),
        out_shape=jax.ShapeDtypeStruct((bsz, t, d), F32),
        compiler_params=_params(("parallel", "parallel")),
        name="ffn",
    )(x, shift, scale, gate, wg, wu, wo, ln_g, ln_b)


def _proj_kernel(x_ref, sh_ref, sc_ref, w16_ref, w32_ref, cos_ref, s1_ref, s2_ref, o16_ref, o32_ref, *vt_refs,
                 tn, vt_blks):
    u = (x_ref[0] * (1.0 + sc_ref[0]) + sh_ref[0]).astype(BF16)
    per = tn // LANE
    for w_ref, o_ref, n_rope_blks in ((w16_ref, o16_ref, N_ROPE_BLKS_16), (w32_ref, o32_ref, N_BLKS_32)):
        for j in range(w_ref.shape[1] // tn):
            y = _dot(u, w_ref[:, j * tn:(j + 1) * tn])
            for i in range(per):
                blk = j * per + i
                yi = y[:, i * LANE:(i + 1) * LANE]
                if blk < n_rope_blks:
                    yi = (yi * cos_ref[...] + pltpu.roll(yi, LANE - ROT_DIM // 2, 1) * s1_ref[...]
                          + pltpu.roll(yi, ROT_DIM // 2, 1) * s2_ref[...])
                o_ref[0, :, blk * LANE:(blk + 1) * LANE] = yi.astype(o_ref.dtype)
    for vt_ref, blk0 in zip(vt_refs, vt_blks):
        _, npair, nparts, _, tqc = vt_ref.shape
        for pair in range(npair):
            vt = o16_ref[0, :, (blk0 + pair) * LANE:(blk0 + pair + 1) * LANE].astype(F32).T
            for part in range(nparts):
                vt_ref[0, pair, part] = vt[:, part * tqc:(part + 1) * tqc].astype(BF16)


def _proj(x, shift, scale, w16, w32, tables, vt_tiles):
    bsz, t, d = x.shape
    n16, n32 = w16.shape[1], w32.shape[1]
    tm = _pick(t, (512, 256, 128))
    row = lambda b, i: (b, 0, 0)
    tab = pl.BlockSpec((tm, LANE), lambda b, i: (i, 0))
    wspec = lambda n: pl.BlockSpec((d, n), lambda b, i: (0, 0), pipeline_mode=pl.Buffered(1))
    vt_specs = [pl.BlockSpec((1, 2, tm // tq, LANE, tq), lambda b, i: (b, 0, i, 0, 0)) for _, tq in vt_tiles]
    vt_shapes = [jax.ShapeDtypeStruct((bsz, 2, t // tq, LANE, tq), BF16) for _, tq in vt_tiles]
    return pl.pallas_call(
        functools.partial(_proj_kernel, tn=2 * LANE, vt_blks=tuple(blk for blk, _ in vt_tiles)),
        grid=(bsz, t // tm),
        in_specs=[pl.BlockSpec((1, tm, d), lambda b, i: (b, i, 0)),
                  pl.BlockSpec((1, 1, d), row), pl.BlockSpec((1, 1, d), row),
                  wspec(n16), wspec(n32), tab, tab, tab],
        out_specs=[pl.BlockSpec((1, tm, n16), lambda b, i: (b, i, 0)),
                   pl.BlockSpec((1, tm, n32), lambda b, i: (b, i, 0))] + vt_specs,
        out_shape=[jax.ShapeDtypeStruct((bsz, t, n16), BF16), jax.ShapeDtypeStruct((bsz, t, n32), F32)] + vt_shapes,
        compiler_params=_params(("parallel", "parallel")),
        name="mixer_proj",
    )(x, shift, scale, w16, w32, *tables)


def _cum_kernel(p_ref, bias_ref, out_ref, kcat_ref, carry_ref):
    @pl.when(pl.program_id(1) == 0)
    def _():
        carry_ref[...] = jnp.zeros_like(carry_ref)

    tc = p_ref.shape[1]
    lane = lax.broadcasted_iota(I32, (tc, LANE), 1)
    lf = _log_sigmoid(p_ref[0] + bias_ref[...])
    lf = jnp.where(lane >= LANE_FG, jnp.where(lane < LANE_FG + D_HEADS, lf, 0.0), 0.0)
    ri = lax.broadcasted_iota(I32, (tc, tc), 0)
    ci = lax.broadcasted_iota(I32, (tc, tc), 1)
    tri = jnp.where(ri >= ci, 1.0, 0.0).astype(BF16)
    hi, mid, lo = _split3(lf)
    cum = _dot(tri, hi) + _dot(tri, mid) + _dot(tri, lo) + carry_ref[...]
    carry_ref[...] = cum[tc - 1:tc, :]
    er = lax.broadcasted_iota(I32, (LANE, LANE), 0)
    ch, cm, cl = _split3(-LOG2E * cum)
    for h in range(D_HEADS):
        onehot = jnp.where(er == LANE_FG + h, 1.0, 0.0).astype(BF16)
        out_ref[0, h] = _dot(ch, onehot) + _dot(cm, onehot) + _dot(cl, onehot)
    x = p_ref[0]
    hi = x.astype(BF16).astype(F32)
    kcat_ref[0, :, 0:LANE] = jnp.where(lane < IDX_DIM, hi, pltpu.roll(hi, IDX_DIM, 1)).astype(BF16)
    kcat_ref[0, :, LANE:2 * LANE] = jnp.where(lane < IDX_DIM, x - hi, 0.0).astype(BF16)


def _cum(p32, bias_row, tc):
    bsz, t, _ = p32.shape
    return pl.pallas_call(
        _cum_kernel,
        grid=(bsz, t // tc),
        in_specs=[pl.BlockSpec((1, tc, LANE), lambda b, i: (b, i, BLK_IX)),
                  pl.BlockSpec((1, LANE), lambda b, i: (0, 0))],
        out_specs=[pl.BlockSpec((1, D_HEADS, tc, LANE), lambda b, i: (b, 0, i, 0)),
                   pl.BlockSpec((1, tc, 2 * LANE), lambda b, i: (b, i, 0))],
        out_shape=[jax.ShapeDtypeStruct((bsz, D_HEADS, t, LANE), F32),
                   jax.ShapeDtypeStruct((bsz, t, 2 * LANE), BF16)],
        scratch_shapes=[pltpu.VMEM((1, LANE), F32)],
        compiler_params=_params(("parallel", "arbitrary")),
        name="forget_cumsum",
    )(p32, bias_row)


def _head_halves(q):
    lane = lax.broadcasted_iota(I32, q.shape, 1)
    zero = jnp.zeros_like(q)
    return jnp.where(lane < HEAD_DIM, q, zero), jnp.where(lane >= HEAD_DIM, q, zero)


def _static_loop(n, body, carry):
    for i in range(n):
        carry = body(i, carry)
    return carry


def _fold8(x, op):
    out = x[0:8]
    for r in range(8, x.shape[0], 8):
        out = op(out, x[r:r + 8])
    return out


def _lane_pick(a, b):
    lane = lax.broadcasted_iota(I32, a.shape, 1)
    return jnp.where(lane < HEAD_DIM, a, b)


def _dil_kernel(q_ref, k_ref, v_ref, o_ref, lse_ref, *, kw):
    nb = q_ref.shape[1] // SPAN

    def body(i, carry):
        qs = pl.multiple_of(i * SPAN, SPAN)
        ks = pl.multiple_of(jnp.maximum(i * SPAN + SPAN - kw, 0), SPAN)
        q = q_ref[0, pl.ds(qs, SPAN), :] * 0.125
        k = k_ref[0, pl.ds(ks, kw), :]
        v = v_ref[0, pl.ds(ks, kw), :]
        qpos = qs + lax.broadcasted_iota(I32, (SPAN, kw), 0)
        kpos = ks + lax.broadcasted_iota(I32, (SPAN, kw), 1)
        dist = qpos - kpos
        outs, lses = [], []
        for qh in _head_halves(q):
            s = _nt(qh, k)
            s = jnp.where(dist >= 0, jnp.where(dist <= SPAN, s, NEG_BIG), NEG_BIG)
            m = jnp.max(s, axis=-1, keepdims=True)
            p = jnp.exp(s - m)
            l = jnp.sum(p, axis=-1, keepdims=True)
            outs.append(_dot(p.astype(BF16), v) / l)
            lses.append(jnp.broadcast_to(m + jnp.log(l), (SPAN, LANE)))
        o_ref[0, pl.ds(qs, SPAN), :] = _lane_pick(outs[0], outs[1]).astype(o_ref.dtype)
        lse_ref[0, pl.ds(qs, SPAN), :] = _lane_pick(lses[0], lses[1])
        return carry

    lax.fori_loop(0, nb, body, 0, unroll=min(nb, 4))


def _dilated(q, k, v, cols=(0, 0, 0)):
    ns, ln, _ = q.shape
    kw = min(2 * SPAN, ln)
    spec = pl.BlockSpec((1, ln, LANE), lambda s: (s, 0, 0))
    in_spec = lambda col: pl.BlockSpec((1, ln, LANE), lambda s: (s, 0, col))
    return pl.pallas_call(
        functools.partial(_dil_kernel, kw=kw),
        grid=(ns,),
        in_specs=[in_spec(c) for c in cols],
        out_specs=[spec, spec],
        out_shape=[jax.ShapeDtypeStruct((ns, ln, LANE), BF16), jax.ShapeDtypeStruct((ns, ln, LANE), F32)],
        compiler_params=_params(("parallel",)),
        name="dilated_window",
    )(q, k, v)


def _fox_kernel(q_ref, k_ref, vt_ref, bias_ref, o_ref, s_scr, p_scr, al_scr, acc_scr, m_scr, l_scr, *, tq):
    qi = pl.program_id(2)
    rep = tq // LANE
    nstrip = tq // STRIP
    qh = _head_halves((q_ref[0].astype(F32) * (0.125 * LOG2E)).astype(BF16))
    m_scr[...] = jnp.full(m_scr.shape, NEG_BIG, F32)
    l_scr[...] = jnp.zeros(l_scr.shape, F32)
    acc_scr[...] = jnp.zeros(acc_scr.shape, F32)
    kk = lax.broadcasted_iota(I32, (STRIP, tq), 0)
    qq = lax.broadcasted_iota(I32, (STRIP, tq), 1)

    def issue_scores(j, s_buf):
        kblk = k_ref[0, pl.ds(pl.multiple_of(j * tq, tq), tq), :]
        for h in range(2):
            s_buf[h] = _nt(kblk, qh[h])

    def softmax(j, s_buf, p_buf, al_buf, diag):
        ks = pl.multiple_of(j * tq, tq)
        for h in range(2):
            def pass1(i, mx):
                r0 = i * STRIP
                s = s_buf[h, pl.ds(r0, STRIP), :] + jnp.tile(
                    bias_ref[0, h, pl.ds(pl.multiple_of(ks + r0, STRIP), STRIP), :], (1, rep))
                if diag:
                    s = jnp.where(kk + r0 <= qq, s, NEG_BIG)
                s_buf[h, pl.ds(r0, STRIP), :] = s
                return jnp.maximum(mx, _fold8(s, jnp.maximum))

            mx = _static_loop(nstrip, pass1, jnp.full((8, tq), NEG_BIG, F32))
            m_old = m_scr[h]
            m_new = jnp.maximum(m_old, jnp.max(mx, axis=0, keepdims=True))
            al_buf[h] = jnp.exp2(m_old - m_new)
            m_scr[h] = m_new

            def pass2(i, ls):
                r0 = i * STRIP
                p = jnp.exp2(s_buf[h, pl.ds(r0, STRIP), :] - m_new)
                p_buf[h, pl.ds(r0, STRIP), :] = p.astype(BF16)
                return ls + _fold8(p, jnp.add)

            ls = _static_loop(nstrip, pass2, jnp.zeros((8, tq), F32))
            l_scr[h] = al_buf[h] * l_scr[h] + jnp.sum(ls, axis=0, keepdims=True)

    def apply_values(j, p_buf, al_buf):
        for h in range(2):
            pv = _dot(vt_ref[0, 0, j, h * HEAD_DIM:(h + 1) * HEAD_DIM, :], p_buf[h])
            acc_scr[h] = acc_scr[h] * al_buf[h] + pv

    s_a, s_b = s_scr.at[0], s_scr.at[1]
    p_a, p_b = p_scr.at[0], p_scr.at[1]
    al_a, al_b = al_scr.at[0], al_scr.at[1]
    p_b[...] = jnp.zeros(p_b.shape, BF16)
    al_b[...] = jnp.ones(al_b.shape, F32)
    issue_scores(0, s_a)

    def pair(i, carry):
        a = 2 * i
        issue_scores(a + 1, s_b)
        softmax(a, s_a, p_a, al_a, False)
        apply_values(jnp.maximum(a - 1, 0), p_b, al_b)
        issue_scores(a + 2, s_a)
        softmax(a + 1, s_b, p_b, al_b, False)
        apply_values(a, p_a, al_a)
        return carry

    lax.fori_loop(0, qi // 2, pair, 0)
    last = 2 * (qi // 2)

    @pl.when(qi % 2 == 1)
    def _():
        issue_scores(qi, s_b)
        softmax(last, s_a, p_a, al_a, False)
        apply_values(jnp.maximum(last - 1, 0), p_b, al_b)
        softmax(qi, s_b, p_b, al_b, True)
        apply_values(last, p_a, al_a)
        apply_values(qi, p_b, al_b)

    @pl.when(qi % 2 == 0)
    def _():
        softmax(qi, s_a, p_a, al_a, True)
        apply_values(jnp.maximum(qi - 1, 0), p_b, al_b)
        apply_values(qi, p_a, al_a)

    out_t = jnp.concatenate([acc_scr[h] * (1.0 / l_scr[h]) for h in range(2)], axis=0)
    o_ref[0] = out_t.T.astype(o_ref.dtype)


def _fox(p16, vt, bias, tq):
    bsz, t, _ = p16.shape
    npair = D_HEADS // 2
    nblk = t // tq
    return pl.pallas_call(
        functools.partial(_fox_kernel, tq=tq),
        grid=(bsz, npair, nblk),
        in_specs=[pl.BlockSpec((1, tq, LANE), lambda b, h, i: (b, i, BLK_DQ + h)),
                  pl.BlockSpec((1, t, LANE), lambda b, h, i: (b, 0, BLK_DK + h)),
                  pl.BlockSpec((1, 1, nblk, LANE, tq), lambda b, h, i: (b, h, 0, 0, 0)),
                  pl.BlockSpec((1, 2, t, LANE), lambda b, h, i: (b, h, 0, 0))],
        out_specs=pl.BlockSpec((1, tq, LANE), lambda b, h, i: (b, i, h)),
        out_shape=jax.ShapeDtypeStruct((bsz, t, npair * LANE), BF16),
        scratch_shapes=[pltpu.VMEM((2, 2, tq, tq), F32), pltpu.VMEM((2, 2, tq, tq), BF16),
                        pltpu.VMEM((2, 2, 1, tq), F32),
                        pltpu.VMEM((2, HEAD_DIM, tq), F32), pltpu.VMEM((2, 1, tq), F32),
                        pltpu.VMEM((2, 1, tq), F32)],
        compiler_params=_params(("parallel", "parallel", "arbitrary")),
        name="forgetting_attention",
    )(p16, p16, vt, bias)


def _sb_kernel(q_ref, k_ref, vt_ref, o_ref, z_scr, sfx_scr, hi_scr, lo_scr, a_scr, acc_scr, r_scr, *, tq):
    qi = pl.program_id(2)
    nstrip = tq // STRIP
    qh = _head_halves(q_ref[0] * 0.125)
    acc_scr[...] = jnp.zeros(acc_scr.shape, F32)
    r_scr[...] = jnp.zeros(r_scr.shape, F32)
    kk = lax.broadcasted_iota(I32, (STRIP, tq), 0)
    qq = lax.broadcasted_iota(I32, (STRIP, tq), 1)
    row = lax.broadcasted_iota(I32, (tq, tq), 0)
    col = lax.broadcasted_iota(I32, (tq, tq), 1)
    later = jnp.where(col > row, 1.0, 0.0).astype(BF16)

    def block(j, diag):
        ks = pl.multiple_of(j * tq, tq)
        kblk = k_ref[0, pl.ds(ks, tq), :]
        for h in range(2):
            z_scr[h] = _nt(kblk, qh[h])
        for h in range(2):
            def pass1(i, rsum):
                r0 = i * STRIP
                z = z_scr[h, pl.ds(r0, STRIP), :]
                lk = _log_sigmoid(-z)
                z_scr[h, pl.ds(r0, STRIP), :] = lk + z
                if diag:
                    lk = jnp.where(kk + r0 < qq, lk, 0.0)
                hi = lk.astype(BF16)
                hi_scr[h, pl.ds(r0, STRIP), :] = hi
                lo_scr[h, pl.ds(r0, STRIP), :] = (lk - hi.astype(F32)).astype(BF16)
                return rsum + _fold8(lk, jnp.add)

            rsum = _static_loop(nstrip, pass1, jnp.zeros((8, tq), F32))
            sfx_scr[h] = _dot(later, hi_scr[h]) + _dot(later, lo_scr[h])
            r_old = r_scr[h]

            def pass2(i, carry):
                r0 = i * STRIP
                a = jnp.exp(z_scr[h, pl.ds(r0, STRIP), :] + sfx_scr[h, pl.ds(r0, STRIP), :] + r_old)
                if diag:
                    a = jnp.where(kk + r0 < qq, a, 0.0)
                a_scr[h, pl.ds(r0, STRIP), :] = a.astype(BF16)
                return carry

            _static_loop(nstrip, pass2, 0)
            acc_scr[h] = acc_scr[h] + _dot(vt_ref[0, 0, j, h * HEAD_DIM:(h + 1) * HEAD_DIM, :], a_scr[h])
            r_scr[h] = r_old + jnp.sum(rsum, axis=0, keepdims=True)

    block(qi, True)

    def live():
        return jnp.max(jnp.maximum(r_scr[0], r_scr[1])) > EXP_DEAD

    def cond(c):
        return jnp.logical_and(c[0] >= 0, c[1])

    def body(c):
        block(c[0], False)
        return c[0] - 1, live()

    lax.while_loop(cond, body, (qi - 1, live()))
    o_ref[0] = jnp.concatenate([acc_scr[0], acc_scr[1]], axis=0).T.astype(o_ref.dtype)


def _stick_breaking(p16, vt, tq):
    bsz, t, _ = p16.shape
    npair = C_HEADS // 2
    nblk = t // tq
    return pl.pallas_call(
        functools.partial(_sb_kernel, tq=tq),
        grid=(bsz, npair, nblk),
        in_specs=[pl.BlockSpec((1, tq, LANE), lambda b, h, i: (b, i, BLK_CQ + h)),
                  pl.BlockSpec((1, t, LANE), lambda b, h, i: (b, 0, BLK_CK + h)),
                  pl.BlockSpec((1, 1, nblk, LANE, tq), lambda b, h, i: (b, h, 0, 0, 0))],
        out_specs=pl.BlockSpec((1, tq, LANE), lambda b, h, i: (b, i, h)),
        out_shape=jax.ShapeDtypeStruct((bsz, t, npair * LANE), BF16),
        scratch_shapes=[pltpu.VMEM((2, tq, tq), F32), pltpu.VMEM((2, tq, tq), F32),
                        pltpu.VMEM((2, tq, tq), BF16), pltpu.VMEM((2, tq, tq), BF16),
                        pltpu.VMEM((2, tq, tq), BF16), pltpu.VMEM((2, HEAD_DIM, tq), F32),
                        pltpu.VMEM((2, 1, tq), F32)],
        compiler_params=_params(("parallel", "parallel", "arbitrary")),
        name="stick_breaking",
    )(p16, p16, vt)


def _dsa_kernel(iq0_ref, iq1_ref, ixq_ref, kcat_ref, q0_ref, q1_ref, k0_ref, k1_ref, vt_ref, o_ref,
                key_scr, s_scr, p_scr, eq_scr, rank_scr, bias_scr, al_scr, acc_scr, m_scr, l_scr, *, tq, topk):
    qi = pl.program_id(1)
    nstrip = tq // STRIP
    lane = lax.broadcasted_iota(I32, (tq, LANE), 1)
    kk = lax.broadcasted_iota(I32, (STRIP, tq), 0)
    qq = lax.broadcasted_iota(I32, (STRIP, tq), 1)

    qcat = []
    for ref in (iq0_ref, iq1_ref):
        for half in range(2):
            x = ref[0] if half == 0 else pltpu.roll(ref[0], IDX_DIM, 1)
            hi = x.astype(BF16).astype(F32)
            first = jnp.where(lane < IDX_DIM, hi, pltpu.roll(x - hi, IDX_DIM, 1))
            second = jnp.where(lane < IDX_DIM, hi, 0.0)
            qcat.append(jnp.concatenate([first, second], axis=1).astype(BF16))
    sr = lax.broadcasted_iota(I32, (8, LANE), 0)
    sl = lax.broadcasted_iota(I32, (8, LANE), 1)
    onehot = jnp.where(sl == sr + LANE_IW, 1.0, 0.0).astype(BF16)
    wh, wm, wl = _split3(ixq_ref[0])
    wt = _nt(onehot, wh) + _nt(onehot, wm) + _nt(onehot, wl)

    def score_chunk(c, diag, counts):
        ge0, gt0 = counts
        ks = pl.multiple_of(c * tq, tq)
        kc = kcat_ref[0, pl.ds(ks, tq), :]
        for h in range(IDX_HEADS):
            s_scr[0, h] = _nt(kc, qcat[h])
        for i in range(nstrip):
            r0 = i * STRIP
            sc = jnp.zeros((STRIP, tq), F32)
            for h in range(IDX_HEADS):
                sc = sc + wt[h:h + 1, :] * jnp.maximum(s_scr[0, h, r0:r0 + STRIP, :], 0.0)
            sc = jnp.where(sc == 0.0, 0.0, sc)
            bits = lax.bitcast_convert_type(sc, I32)
            key = jnp.where(bits < 0, bits ^ jnp.int32(0x7FFFFFFF), bits)
            if diag:
                key = jnp.where(kk + r0 <= qq, key, INT_MIN)
            key_scr[c, r0:r0 + STRIP, :] = key
            ge0 = ge0 + _fold8(jnp.where(key >= 0, 1.0, 0.0), jnp.add)
            gt0 = gt0 + _fold8(jnp.where(key > 0, 1.0, 0.0), jnp.add)
        return ge0, gt0

    zeros = jnp.zeros((8, tq), F32)
    counts = lax.fori_loop(0, qi, lambda c, counts: score_chunk(c, False, counts), (zeros, zeros))
    ge0, gt0 = score_chunk(qi, True, counts)
    n_ge0 = jnp.sum(ge0, axis=0, keepdims=True)
    n_gt0 = jnp.sum(gt0, axis=0, keepdims=True)

    def count(pred):
        def body(c, acc):
            for i in range(nstrip):
                acc = jnp.where(pred(key_scr[c, i * STRIP:(i + 1) * STRIP, :]), acc + 1.0, acc)
            return acc
        acc = lax.fori_loop(0, qi + 1, body, jnp.zeros((STRIP, tq), F32))
        return jnp.sum(acc, axis=0, keepdims=True)

    kf = jnp.float32(topk)
    th0 = jnp.where(n_ge0 >= kf, 0, INT_MIN).astype(I32)
    settled0 = jnp.where(n_ge0 >= kf, jnp.where(n_gt0 < kf, 1.0, 0.0), 0.0)
    n_up0 = jnp.where(n_ge0 >= kf, jnp.where(n_gt0 < kf, n_gt0, 0.0), n_ge0)

    def unsettled(settled):
        return jnp.min(settled) == 0.0

    def search_cond(c):
        return jnp.logical_and(c[0] >= 0, c[4])

    def search_body(c):
        bit, th, n_up, settled, _ = c
        cand = th | lax.shift_left(jnp.int32(1), bit)
        n = count(lambda kb: kb >= cand)
        open_ = settled == 0.0
        th = jnp.where(n >= kf, jnp.where(open_, cand, th), th)
        n_up = jnp.where(n < kf, jnp.where(open_, n, n_up), n_up)
        settled = jnp.maximum(settled, jnp.where(n == kf, 1.0, 0.0))
        return bit - 1, th, n_up, settled, unsettled(settled)

    _, th, n_up, _, _ = lax.while_loop(search_cond, search_body,
                                       (jnp.int32(30), th0, n_up0, settled0, unsettled(settled0)))
    need = kf - n_up

    qh = sum((_head_halves((r[0].astype(F32) * (0.125 * LOG2E)).astype(BF16)) for r in (q0_ref, q1_ref)), ())
    k_refs = (k0_ref, k0_ref, k1_ref, k1_ref)
    m_scr[...] = jnp.full(m_scr.shape, NEG_BIG, F32)
    l_scr[...] = jnp.zeros(l_scr.shape, F32)
    acc_scr[...] = jnp.zeros(acc_scr.shape, F32)
    row = lax.broadcasted_iota(I32, (tq, tq), 0)
    col = lax.broadcasted_iota(I32, (tq, tq), 1)
    upto = jnp.where(col <= row, 1.0, 0.0).astype(BF16)

    def issue(c, s_buf, eq_buf, rank_buf):
        ks = pl.multiple_of(c * tq, tq)
        for i in range(nstrip):
            r0 = i * STRIP
            eq_buf[r0:r0 + STRIP, :] = jnp.where(key_scr[c, r0:r0 + STRIP, :] == th, 1.0, 0.0).astype(BF16)
        rank_buf[...] = _dot(upto, eq_buf[...])
        for h in range(B_HEADS):
            s_buf[h] = _nt(k_refs[h][0, pl.ds(ks, tq), :], qh[h])

    def softmax(c, s_buf, rank_buf, p_buf, al_buf, diag, seen):
        for i in range(nstrip):
            r0 = i * STRIP
            kb = key_scr[c, r0:r0 + STRIP, :]
            tie = jnp.where(rank_buf[r0:r0 + STRIP, :] + seen <= need, 0.0, NEG_BIG)
            bias = jnp.where(kb > th, 0.0, jnp.where(kb == th, tie, NEG_BIG))
            if diag:
                bias = jnp.where(kk + r0 <= qq, bias, NEG_BIG)
            bias_scr[r0:r0 + STRIP, :] = bias
        for h in range(B_HEADS):
            def pass1(i, mx):
                r0 = i * STRIP
                s = s_buf[h, r0:r0 + STRIP, :] + bias_scr[r0:r0 + STRIP, :]
                s_buf[h, r0:r0 + STRIP, :] = s
                return jnp.maximum(mx, _fold8(s, jnp.maximum))

            mx = _static_loop(nstrip, pass1, jnp.full((8, tq), NEG_BIG, F32))
            m_old = m_scr[h]
            m_new = jnp.maximum(m_old, jnp.max(mx, axis=0, keepdims=True))
            al_buf[h] = jnp.exp2(m_old - m_new)
            m_scr[h] = m_new

            def pass2(i, ls):
                r0 = i * STRIP
                p = jnp.exp2(s_buf[h, r0:r0 + STRIP, :] - m_new)
                p_buf[h, r0:r0 + STRIP, :] = p.astype(BF16)
                return ls + _fold8(p, jnp.add)

            ls = _static_loop(nstrip, pass2, jnp.zeros((8, tq), F32))
            l_scr[h] = al_buf[h] * l_scr[h] + jnp.sum(ls, axis=0, keepdims=True)
        return seen + rank_buf[tq - 1:tq, :]

    def apply_values(c, p_buf, al_buf):
        for h in range(B_HEADS):
            lo = (h % 2) * HEAD_DIM
            acc_scr[h] = acc_scr[h] * al_buf[h] + _dot(vt_ref[0, h // 2, c, lo:lo + HEAD_DIM, :], p_buf[h])

    s_a, s_b = s_scr.at[0], s_scr.at[1]
    p_a, p_b = p_scr.at[0], p_scr.at[1]
    eq_a, eq_b = eq_scr.at[0], eq_scr.at[1]
    rank_a, rank_b = rank_scr.at[0], rank_scr.at[1]
    al_a, al_b = al_scr.at[0], al_scr.at[1]
    p_b[...] = jnp.zeros(p_b.shape, BF16)
    al_b[...] = jnp.ones(al_b.shape, F32)
    issue(0, s_a, eq_a, rank_a)

    def pair(i, seen):
        a = 2 * i
        issue(a + 1, s_b, eq_b, rank_b)
        seen = softmax(a, s_a, rank_a, p_a, al_a, False, seen)
        apply_values(jnp.maximum(a - 1, 0), p_b, al_b)
        issue(a + 2, s_a, eq_a, rank_a)
        seen = softmax(a + 1, s_b, rank_b, p_b, al_b, False, seen)
        apply_values(a, p_a, al_a)
        return seen

    seen = lax.fori_loop(0, qi // 2, pair, jnp.zeros((1, tq), F32))
    last = 2 * (qi // 2)

    @pl.when(qi % 2 == 1)
    def _():
        issue(qi, s_b, eq_b, rank_b)
        seen_d = softmax(last, s_a, rank_a, p_a, al_a, False, seen)
        apply_values(jnp.maximum(last - 1, 0), p_b, al_b)
        softmax(qi, s_b, rank_b, p_b, al_b, True, seen_d)
        apply_values(last, p_a, al_a)
        apply_values(qi, p_b, al_b)

    @pl.when(qi % 2 == 0)
    def _():
        softmax(qi, s_a, rank_a, p_a, al_a, True, seen)
        apply_values(jnp.maximum(qi - 1, 0), p_b, al_b)
        apply_values(qi, p_a, al_a)

    for g in range(B_HEADS // 2):
        out_t = jnp.concatenate([acc_scr[h] * (1.0 / l_scr[h]) for h in (2 * g, 2 * g + 1)], axis=0)
        o_ref[0, :, g * LANE:(g + 1) * LANE] = out_t.T.astype(o_ref.dtype)


def _dsa(p16, p32, kcat, vt, tq):
    bsz, t, _ = p16.shape
    topk = min(DSA_TOPK, t // 4)
    npair = B_HEADS // 2
    nblk = t // tq
    qspec = lambda arr_blk: pl.BlockSpec((1, tq, LANE), lambda b, i: (b, i, arr_blk))
    once = pl.Buffered(1)
    kspec = lambda arr_blk: pl.BlockSpec((1, t, LANE), lambda b, i: (b, 0, arr_blk), pipeline_mode=once)
    sq = lambda n, dt: pltpu.VMEM((n, tq, tq), dt) if n else pltpu.VMEM((tq, tq), dt)
    return pl.pallas_call(
        functools.partial(_dsa_kernel, tq=tq, topk=topk),
        grid=(bsz, nblk),
        in_specs=[qspec(BLK_IQ), qspec(BLK_IQ + 1), qspec(BLK_IX),
                  pl.BlockSpec((1, t, 2 * LANE), lambda b, i: (b, 0, 0), pipeline_mode=once),
                  qspec(BLK_BQ), qspec(BLK_BQ + 1), kspec(BLK_BK), kspec(BLK_BK + 1),
                  pl.BlockSpec((1, npair, nblk, LANE, tq), lambda b, i: (b, 0, 0, 0, 0), pipeline_mode=once)],
        out_specs=pl.BlockSpec((1, tq, npair * LANE), lambda b, i: (b, i, 0)),
        out_shape=jax.ShapeDtypeStruct((bsz, t, npair * LANE), BF16),
        scratch_shapes=[sq(nblk, I32), pltpu.VMEM((2, B_HEADS, tq, tq), F32), pltpu.VMEM((2, B_HEADS, tq, tq), BF16),
                        sq(2, BF16), sq(2, F32), sq(0, F32), pltpu.VMEM((2, B_HEADS, 1, tq), F32),
                        pltpu.VMEM((B_HEADS, HEAD_DIM, tq), F32), pltpu.VMEM((B_HEADS, 1, tq), F32),
                        pltpu.VMEM((B_HEADS, 1, tq), F32)],
        compiler_params=_params(("parallel", "arbitrary")),
        name="dsa_topk_attention",
    )(p32, p32, p32, kcat, p16, p16, p16, p16, vt)


def _merge_kernel(x_ref, sh_ref, sc_ref, gt_ref, oa0_ref, oa1_ref, oa2_ref, la0_ref, la1_ref, la2_ref,
                  yb_ref, yc_ref, yd_ref, wg_ref, bg_ref, wb_ref, wo_ref, lg_ref, lb_ref, o_ref, *, alpha):
    x = x_ref[0]
    d = x.shape[1]
    u = (x * (1.0 + sc_ref[0]) + sh_ref[0]).astype(BF16)
    lses = [r[0] for r in (la0_ref, la1_ref, la2_ref)]
    top = jnp.maximum(jnp.maximum(lses[0], lses[1]), lses[2])
    es = [jnp.exp(l - top) for l in lses]
    den = es[0] + es[1] + es[2]
    ya = sum((e / den) * r[0].astype(F32) for e, r in zip(es, (oa0_ref, oa1_ref, oa2_ref)))
    branches = (ya.astype(BF16), yb_ref[0], yc_ref[0], yd_ref[0])
    merged = jnp.zeros(x.shape, F32)
    off = 0
    for i, br in enumerate(branches):
        gate = jax.nn.sigmoid(_dot(u, wg_ref[:, i * d:(i + 1) * d]) + bg_ref[:, i * d:(i + 1) * d])
        merged = merged + gate * _dot(br, wb_ref[off:off + BRANCH_WIDTHS[i], :])
        off += BRANCH_WIDTHS[i]
    h = _dot(merged.astype(BF16), wo_ref[...])
    o_ref[0] = _deepnorm_ln(x, h, gt_ref[0], lg_ref[...], lb_ref[...], alpha)


def _merge(x, shift, scale, gate, oas, las, yb, yc, yd, wg, bg, wb, wo, ln_g, ln_b, alpha):
    bsz, t, d = x.shape
    tm = _pick(t, (512, 256, 128))
    row = lambda b, i: (b, 0, 0)
    tok = lambda w: pl.BlockSpec((1, tm, w), lambda b, i: (b, i, 0))
    wspec = lambda shape: pl.BlockSpec(shape, lambda b, i: (0, 0), pipeline_mode=pl.Buffered(1))
    return pl.pallas_call(
        functools.partial(_merge_kernel, alpha=alpha),
        grid=(bsz, t // tm),
        in_specs=[tok(d), pl.BlockSpec((1, 1, d), row), pl.BlockSpec((1, 1, d), row), pl.BlockSpec((1, 1, d), row),
                  tok(LANE), tok(LANE), tok(LANE), tok(LANE), tok(LANE), tok(LANE),
                  tok(2 * LANE), tok(2 * LANE), tok(2 * LANE),
                  wspec(wg.shape), wspec(bg.shape), wspec(wb.shape), wspec(wo.shape),
                  pl.BlockSpec((1, d), lambda b, i: (0, 0)), pl.BlockSpec((1, d), lambda b, i: (0, 0))],
        out_specs=tok(d),
        out_shape=jax.ShapeDtypeStruct((bsz, t, d), F32),
        compiler_params=_params(("parallel", "parallel")),
        name="gated_merge",
    )(x, shift, scale, gate, *oas, *las, yb, yc, yd, wg, bg, wb, wo, ln_g, ln_b)


def _rope_tables(t):
    half = ROT_DIM // 2
    inv_freq = ROPE_THETA ** (-(jnp.arange(half, dtype=F32) * (2.0 / ROT_DIM)))
    ang = jnp.arange(t, dtype=F32)[:, None] * inv_freq[None, :]
    cos, sin = jnp.cos(ang), jnp.sin(ang)
    ones = jnp.ones((t, HEAD_DIM - ROT_DIM), F32)
    zeros = jnp.zeros((t, HEAD_DIM - half), F32)
    c64 = jnp.concatenate([cos, cos, ones], axis=1)
    s1 = jnp.concatenate([-sin, zeros], axis=1)
    s2 = jnp.concatenate([jnp.zeros((t, half), F32), sin, zeros[:, half:]], axis=1)
    return tuple(jnp.concatenate([a, a], axis=1) for a in (c64, s1, s2))


def _mixer_weights(w_in):
    d = w_in.shape[0]

    def qkv(off, heads):
        w = heads * HEAD_DIM
        return w_in[:, off:off + w], w_in[:, off + w:off + 2 * w], w_in[:, off + 2 * w:off + 3 * w]

    aq, ak, av = qkv(OFF_A, A_HEADS)
    bq, bk, bv = qkv(OFF_B, B_HEADS)
    cq, ck, cv = qkv(OFF_C, C_HEADS)
    dq, dk, dv = qkv(OFF_D, D_HEADS)
    w16 = jnp.concatenate([aq, ak, bq, bk, av, bv, cq, ck, cv, dq, dk, dv, jnp.zeros((d, LANE), F32)], axis=1)
    ix = jnp.concatenate([w_in[:, OFF_IK:OFF_IK + IDX_DIM], jnp.zeros((d, LANE_IW - IDX_DIM), F32),
                          w_in[:, OFF_IW:OFF_IW + IDX_HEADS], w_in[:, OFF_FG:OFF_FG + D_HEADS],
                          jnp.zeros((d, LANE - LANE_FG - D_HEADS), F32)], axis=1)
    w32 = jnp.concatenate([w_in[:, OFF_IQ:OFF_IQ + IDX_HEADS * IDX_DIM], ix, jnp.zeros((d, LANE), F32)], axis=1)
    return w16.astype(BF16), w32.astype(BF16), w_in[:, OFF_GATE:].astype(BF16)


def _to_residues(a, dil):
    bsz, t, w = a.shape
    if dil == 1:
        return a
    return a.reshape(bsz, t // dil, dil, w).transpose(0, 2, 1, 3).reshape(bsz * dil, t // dil, w)


def _from_residues(a, dil, bsz):
    if dil == 1:
        return a
    _, ln, w = a.shape
    return a.reshape(bsz, dil, ln, w).transpose(0, 2, 1, 3).reshape(bsz, ln * dil, w)


def _mixer(x, shift, scale, gate, w_in, b_gate, b_forget, w_branch, w_out, ln_g, ln_b, tables, alpha):
    bsz, t, d = x.shape
    w16, w32, wg = _mixer_weights(w_in)
    tq = _pick(t, (256, 128))
    tqf = _pick(t, (512, 256, 128))
    p16, p32, vtb, vtc, vtd = _proj(x, shift, scale, w16, w32, tables,
                                    ((BLK_BV, tqf), (BLK_CV, tq), (BLK_DV, tqf)))

    oas, las = [], []
    for g, (_, dil) in enumerate(DILATED_GROUPS):
        if dil == 1:
            o, lse = _dilated(p16, p16, p16, cols=(BLK_AQ + g, BLK_AK + g, BLK_AV + g))
        else:
            streams = [_to_residues(p16[:, :, (blk + g) * LANE:(blk + g + 1) * LANE], dil)
                       for blk in (BLK_AQ, BLK_AK, BLK_AV)]
            o, lse = _dilated(*streams)
        oas.append(_from_residues(o, dil, bsz))
        las.append(_from_residues(lse, dil, bsz))

    bias_row = jnp.zeros((1, LANE), F32).at[0, LANE_FG:LANE_FG + D_HEADS].set(b_forget)
    fbias, kcat = _cum(p32, bias_row, tq)
    yd = _fox(p16, vtd, fbias, tqf)
    yc = _stick_breaking(p16, vtc, tq)
    yb = _dsa(p16, p32, kcat, vtb, tqf)
    return _merge(x, shift, scale, gate, oas, las, yb, yc, yd, wg, b_gate.reshape(1, -1),
                  w_branch.astype(BF16), w_out.astype(BF16), ln_g, ln_b, alpha)


def kernel(x, c, ada_w, ada_b, ln_g, ln_b, ffn_w_in, ffn_w_out, mix_w_in, mix_b_gate, mix_b_forget,
           mix_w_branch, mix_w_out):
    bsz, t, d = x.shape
    depth = ada_w.shape[0]
    f = ffn_w_out.shape[2]
    alpha = float((2 * depth) ** 0.25)
    assert t % (DILATED_GROUPS[-1][0]) == 0 and d % LANE == 0 and bsz <= 8

    c8 = jnp.zeros((8, d), F32).at[:bsz].set(c)
    mod = _ada(c8, ada_w, ada_b)[:, :bsz].reshape(depth, bsz, 3, 3, 1, d)
    tables = _rope_tables(t)

    for l in range(depth):
        m = lambda sub, kind: mod[l, :, sub, kind]
        lng = lambda sub: ln_g[l, sub].reshape(1, d)
        lnb = lambda sub: ln_b[l, sub].reshape(1, d)

        def ffn(x, sub, which):
            w_in = ffn_w_in[l, which].astype(BF16)
            return _ffn(x, m(sub, 0), m(sub, 1), m(sub, 2), w_in[:, :f], w_in[:, f:],
                        ffn_w_out[l, which].astype(BF16), lng(sub), lnb(sub), alpha)

        x = ffn(x, 0, 0)
        x = _mixer(x, m(1, 0), m(1, 1), m(1, 2), mix_w_in[l], mix_b_gate[l], mix_b_forget[l],
                   mix_w_branch[l], mix_w_out[l], lng(1), lnb(1), tables, alpha)
        x = ffn(x, 2, 1)
    return x
```

```python
import functools

import jax
import jax.numpy as jnp
from jax import lax
from jax.experimental import pallas as pl
from jax.experimental.pallas import tpu as pltpu

F32 = jnp.float32
BF16 = jnp.bfloat16
I32 = jnp.int32

LANE = 128
HEAD_DIM = 64
ROT_DIM = HEAD_DIM // 4
ROPE_THETA = 500000.0
DILATED_GROUPS = ((128, 1), (512, 4), (2048, 16))
SPAN = 128
A_HEADS, B_HEADS, C_HEADS, D_HEADS = 6, 4, 4, 4
IDX_HEADS, IDX_DIM = 4, 64
DSA_TOPK = 256
N_BRANCH = 4
LN_EPS = 1e-5
BRANCH_WIDTHS = (128, 256, 256, 256)

OFF_A = 0
OFF_B = OFF_A + 3 * A_HEADS * HEAD_DIM
OFF_IQ = OFF_B + 3 * B_HEADS * HEAD_DIM
OFF_IK = OFF_IQ + IDX_HEADS * IDX_DIM
OFF_IW = OFF_IK + IDX_DIM
OFF_C = OFF_IW + IDX_HEADS
OFF_D = OFF_C + 3 * C_HEADS * HEAD_DIM
OFF_FG = OFF_D + 3 * D_HEADS * HEAD_DIM
OFF_GATE = OFF_FG + D_HEADS

BLK_AQ, BLK_AK, BLK_BQ, BLK_BK = 0, 3, 6, 8
N_ROPE_BLKS_16 = 10
BLK_AV, BLK_BV = 10, 13
BLK_CQ, BLK_CK, BLK_CV = 15, 17, 19
BLK_DQ, BLK_DK, BLK_DV = 21, 23, 25
N_BLKS_16 = 28
BLK_IQ, BLK_IX = 0, 2
N_BLKS_32 = 4
LANE_IW = 80
LANE_FG = 84

INT_MIN = -2147483648
NEG_BIG = -1e30
LOG2E = 1.4426950408889634
EXP_DEAD = -104.0
STRIP = 32
VMEM_LIMIT = 56 * 1024 * 1024


def _nt(a, b):
    return lax.dot_general(a, b, (((1,), (1,)), ((), ())), preferred_element_type=F32)


def _dot(a, b):
    return jnp.dot(a, b, preferred_element_type=F32)


def _split3(x):
    hi = x.astype(BF16)
    r1 = x - hi.astype(F32)
    mid = r1.astype(BF16)
    lo = (r1 - mid.astype(F32)).astype(BF16)
    return hi, mid, lo


def _split2(x):
    hi = x.astype(BF16)
    return hi, (x - hi.astype(F32)).astype(BF16)


def _log_sigmoid(z):
    return -(jnp.maximum(-z, 0.0) + jnp.log(1.0 + jnp.exp(-jnp.abs(z))))


def _params(sem):
    return pltpu.CompilerParams(dimension_semantics=sem, vmem_limit_bytes=VMEM_LIMIT)


def _const_spec(shape):
    n = len(shape)
    return pl.BlockSpec(shape, lambda *_: (0,) * n)


def _pick(n, prefs):
    for p in prefs:
        if n % p == 0:
            return p
    return n


def _ada_kernel(c_ref, w_ref, b_ref, o_ref):
    c = c_ref[...]
    cond = c * jax.nn.sigmoid(c)
    ch, cm, _ = _split3(cond)
    wh, wm, _ = _split3(w_ref[0])
    o_ref[0] = _dot(ch, wh) + _dot(ch, wm) + _dot(cm, wh) + b_ref[0]


def _ada(c8, ada_w, ada_b):
    depth, d, n = ada_w.shape
    tn = _pick(n, (1152, 1024, 512, 256, 128))
    return pl.pallas_call(
        _ada_kernel,
        grid=(depth, n // tn),
        in_specs=[pl.BlockSpec((8, d), lambda l, j: (0, 0)),
                  pl.BlockSpec((1, d, tn), lambda l, j: (l, 0, j)),
                  pl.BlockSpec((1, 1, tn), lambda l, j: (l, 0, j))],
        out_specs=pl.BlockSpec((1, 8, tn), lambda l, j: (l, 0, j)),
        out_shape=jax.ShapeDtypeStruct((depth, 8, n), F32),
        compiler_params=_params(("parallel", "parallel")),
        name="ada_mod",
    )(c8, ada_w, ada_b.reshape(depth, 1, n))


def _deepnorm_ln(x, h, gate, g, b, alpha):
    y = alpha * x + (1.0 + gate) * h
    mu = jnp.mean(y, axis=-1, keepdims=True)
    yc = y - mu
    var = jnp.mean(yc * yc, axis=-1, keepdims=True)
    return yc * lax.rsqrt(var + LN_EPS) * g + b


def _ffn_kernel(x_ref, sh_ref, sc_ref, gt_ref, wg_ref, wu_ref, wo_ref, lg_ref, lb_ref, o_ref, *, tf, alpha):
    x = x_ref[0]
    u = (x * (1.0 + sc_ref[0]) + sh_ref[0]).astype(BF16)
    acc = jnp.zeros(x.shape, F32)
    for j in range(wg_ref.shape[1] // tf):
        g = _dot(u, wg_ref[:, j * tf:(j + 1) * tf])
        up = _dot(u, wu_ref[:, j * tf:(j + 1) * tf])
        h = (g * jax.nn.sigmoid(g) * up).astype(BF16)
        acc = acc + _dot(h, wo_ref[j * tf:(j + 1) * tf, :])
    o_ref[0] = _deepnorm_ln(x, 0.5 * acc, gt_ref[0], lg_ref[...], lb_ref[...], alpha)


def _ffn(x, shift, scale, gate, wg, wu, wo, ln_g, ln_b, alpha):
    bsz, t, d = x.shape
    f = wg.shape[1]
    tm = _pick(t, (512, 256, 128))
    tf = _pick(f, (256, 128))
    row = lambda b, i: (b, 0, 0)
    wspec = lambda shape: pl.BlockSpec(shape, lambda b, i: (0, 0), pipeline_mode=pl.Buffered(1))
    return pl.pallas_call(
        functools.partial(_ffn_kernel, tf=tf, alpha=alpha),
        grid=(bsz, t // tm),
        in_specs=[pl.BlockSpec((1, tm, d), lambda b, i: (b, i, 0)),
                  pl.BlockSpec((1, 1, d), row), pl.BlockSpec((1, 1, d), row), pl.BlockSpec((1, 1, d), row),
                  wspec((d, f)), wspec((d, f)), wspec((f, d)),
                  pl.BlockSpec((1, d), lambda b, i: (0, 0)), pl.BlockSpec((1, d), lambda b, i: (0, 0))],
        out_specs=pl.BlockSpec((1, tm, d), lambda b, i: (b, i, 0)),
        out_shape=jax.ShapeDtypeStruct((bsz, t, d), F32),
        compiler_params=_params(("parallel", "parallel")),
        name="ffn",
    )(x, shift, scale, gate, wg, wu, wo, ln_g, ln_b)


def _proj_kernel(x_ref, sh_ref, sc_ref, w16_ref, w32_ref, cos_ref, s1_ref, s2_ref, o16_ref, o32_ref, *vt_refs,
                 tn, vt_blks):
    u = (x_ref[0] * (1.0 + sc_ref[0]) + sh_ref[0]).astype(BF16)
    per = tn // LANE
    for w_ref, o_ref, n_rope_blks in ((w16_ref, o16_ref, N_ROPE_BLKS_16), (w32_ref, o32_ref, N_BLKS_32)):
        for j in range(w_ref.shape[1] // tn):
            y = _dot(u, w_ref[:, j * tn:(j + 1) * tn])
            for i in range(per):
                blk = j * per + i
                yi = y[:, i * LANE:(i + 1) * LANE]
                if blk < n_rope_blks:
                    yi = (yi * cos_ref[...] + pltpu.roll(yi, LANE - ROT_DIM // 2, 1) * s1_ref[...]
                          + pltpu.roll(yi, ROT_DIM // 2, 1) * s2_ref[...])
                o_ref[0, :, blk * LANE:(blk + 1) * LANE] = yi.astype(o_ref.dtype)
    for vt_ref, blk0 in zip(vt_refs, vt_blks):
        _, npair, nparts, _, tqc = vt_ref.shape
        for pair in range(npair):
            vt = o16_ref[0, :, (blk0 + pair) * LANE:(blk0 + pair + 1) * LANE].astype(F32).T
            for part in range(nparts):
                vt_ref[0, pair, part] = vt[:, part * tqc:(part + 1) * tqc].astype(BF16)


def _proj(x, shift, scale, w16, w32, tables, vt_tiles):
    bsz, t, d = x.shape
    n16, n32 = w16.shape[1], w32.shape[1]
    tm = _pick(t, (512, 256, 128))
    row = lambda b, i: (b, 0, 0)
    tab = pl.BlockSpec((tm, LANE), lambda b, i: (i, 0))
    wspec = lambda n: pl.BlockSpec((d, n), lambda b, i: (0, 0), pipeline_mode=pl.Buffered(1))
    vt_specs = [pl.BlockSpec((1, 2, tm // tq, LANE, tq), lambda b, i: (b, 0, i, 0, 0)) for _, tq in vt_tiles]
    vt_shapes = [jax.ShapeDtypeStruct((bsz, 2, t // tq, LANE, tq), BF16) for _, tq in vt_tiles]
    return pl.pallas_call(
        functools.partial(_proj_kernel, tn=2 * LANE, vt_blks=tuple(blk for blk, _ in vt_tiles)),
        grid=(bsz, t // tm),
        in_specs=[pl.BlockSpec((1, tm, d), lambda b, i: (b, i, 0)),
                  pl.BlockSpec((1, 1, d), row), pl.BlockSpec((1, 1, d), row),
                  wspec(n16), wspec(n32), tab, tab, tab],
        out_specs=[pl.BlockSpec((1, tm, n16), lambda b, i: (b, i, 0)),
                   pl.BlockSpec((1, tm, n32), lambda b, i: (b, i, 0))] + vt_specs,
        out_shape=[jax.ShapeDtypeStruct((bsz, t, n16), BF16), jax.ShapeDtypeStruct((bsz, t, n32), F32)] + vt_shapes,
        compiler_params=_params(("parallel", "parallel")),
        name="mixer_proj",
    )(x, shift, scale, w16, w32, *tables)


def _cum_kernel(p_ref, bias_ref, out_ref, kcat_ref, carry_ref):
    @pl.when(pl.program_id(1) == 0)
    def _():
        carry_ref[...] = jnp.zeros_like(carry_ref)

    tc = p_ref.shape[1]
    lane = lax.broadcasted_iota(I32, (tc, LANE), 1)
    lf = _log_sigmoid(p_ref[0] + bias_ref[...])
    lf = jnp.where(lane >= LANE_FG, jnp.where(lane < LANE_FG + D_HEADS, lf, 0.0), 0.0)
    ri = lax.broadcasted_iota(I32, (tc, tc), 0)
    ci = lax.broadcasted_iota(I32, (tc, tc), 1)
    tri = jnp.where(ri >= ci, 1.0, 0.0).astype(BF16)
    hi, mid, lo = _split3(lf)
    cum = _dot(tri, hi) + _dot(tri, mid) + _dot(tri, lo) + carry_ref[...]
    carry_ref[...] = cum[tc - 1:tc, :]
    er = lax.broadcasted_iota(I32, (LANE, LANE), 0)
    ch, cm, cl = _split3(-LOG2E * cum)
    for h in range(D_HEADS):
        onehot = jnp.where(er == LANE_FG + h, 1.0, 0.0).astype(BF16)
        out_ref[0, h] = _dot(ch, onehot) + _dot(cm, onehot) + _dot(cl, onehot)
    x = p_ref[0]
    hi = x.astype(BF16).astype(F32)
    kcat_ref[0, :, 0:LANE] = jnp.where(lane < IDX_DIM, hi, pltpu.roll(hi, IDX_DIM, 1)).astype(BF16)
    kcat_ref[0, :, LANE:2 * LANE] = jnp.where(lane < IDX_DIM, x - hi, 0.0).astype(BF16)


def _cum(p32, bias_row, tc):
    bsz, t, _ = p32.shape
    return pl.pallas_call(
        _cum_kernel,
        grid=(bsz, t // tc),
        in_specs=[pl.BlockSpec((1, tc, LANE), lambda b, i: (b, i, BLK_IX)),
                  pl.BlockSpec((1, LANE), lambda b, i: (0, 0))],
        out_specs=[pl.BlockSpec((1, D_HEADS, tc, LANE), lambda b, i: (b, 0, i, 0)),
                   pl.BlockSpec((1, tc, 2 * LANE), lambda b, i: (b, i, 0))],
        out_shape=[jax.ShapeDtypeStruct((bsz, D_HEADS, t, LANE), F32),
                   jax.ShapeDtypeStruct((bsz, t, 2 * LANE), BF16)],
        scratch_shapes=[pltpu.VMEM((1, LANE), F32)],
        compiler_params=_params(("parallel", "arbitrary")),
        name="forget_cumsum",
    )(p32, bias_row)


def _head_halves(q):
    lane = lax.broadcasted_iota(I32, q.shape, 1)
    zero = jnp.zeros_like(q)
    return jnp.where(lane < HEAD_DIM, q, zero), jnp.where(lane >= HEAD_DIM, q, zero)


def _static_loop(n, body, carry):
    for i in range(n):
        carry = body(i, carry)
    return carry


def _fold8(x, op):
    out = x[0:8]
    for r in range(8, x.shape[0], 8):
        out = op(out, x[r:r + 8])
    return out


def _lane_pick(a, b):
    lane = lax.broadcasted_iota(I32, a.shape, 1)
    return jnp.where(lane < HEAD_DIM, a, b)


def _dil_kernel(q_ref, k_ref, v_ref, o_ref, lse_ref, *, kw):
    nb = q_ref.shape[1] // SPAN

    def body(i, carry):
        qs = pl.multiple_of(i * SPAN, SPAN)
        ks = pl.multiple_of(jnp.maximum(i * SPAN + SPAN - kw, 0), SPAN)
        q = q_ref[0, pl.ds(qs, SPAN), :] * 0.125
        k = k_ref[0, pl.ds(ks, kw), :]
        v = v_ref[0, pl.ds(ks, kw), :]
        qpos = qs + lax.broadcasted_iota(I32, (SPAN, kw), 0)
        kpos = ks + lax.broadcasted_iota(I32, (SPAN, kw), 1)
        dist = qpos - kpos
        outs, lses = [], []
        for qh in _head_halves(q):
            s = _nt(qh, k)
            s = jnp.where(dist >= 0, jnp.where(dist <= SPAN, s, NEG_BIG), NEG_BIG)
            m = jnp.max(s, axis=-1, keepdims=True)
            p = jnp.exp(s - m)
            l = jnp.sum(p, axis=-1, keepdims=True)
            outs.append(_dot(p.astype(BF16), v) / l)
            lses.append(jnp.broadcast_to(m + jnp.log(l), (SPAN, LANE)))
        o_ref[0, pl.ds(qs, SPAN), :] = _lane_pick(outs[0], outs[1]).astype(o_ref.dtype)
        lse_ref[0, pl.ds(qs, SPAN), :] = _lane_pick(lses[0], lses[1])
        return carry

    lax.fori_loop(0, nb, body, 0, unroll=min(nb, 4))


def _dilated(q, k, v, cols=(0, 0, 0)):
    ns, ln, _ = q.shape
    kw = min(2 * SPAN, ln)
    spec = pl.BlockSpec((1, ln, LANE), lambda s: (s, 0, 0))
    in_spec = lambda col: pl.BlockSpec((1, ln, LANE), lambda s: (s, 0, col))
    return pl.pallas_call(
        functools.partial(_dil_kernel, kw=kw),
        grid=(ns,),
        in_specs=[in_spec(c) for c in cols],
        out_specs=[spec, spec],
        out_shape=[jax.ShapeDtypeStruct((ns, ln, LANE), BF16), jax.ShapeDtypeStruct((ns, ln, LANE), F32)],
        compiler_params=_params(("parallel",)),
        name="dilated_window",
    )(q, k, v)


def _fox_kernel(q_ref, k_ref, vt_ref, bias_ref, o_ref, s_scr, p_scr, al_scr, acc_scr, m_scr, l_scr, *, tq):
    qi = pl.program_id(2)
    rep = tq // LANE
    nstrip = tq // STRIP
    qh = _head_halves((q_ref[0].astype(F32) * (0.125 * LOG2E)).astype(BF16))
    m_scr[...] = jnp.full(m_scr.shape, NEG_BIG, F32)
    l_scr[...] = jnp.zeros(l_scr.shape, F32)
    acc_scr[...] = jnp.zeros(acc_scr.shape, F32)
    kk = lax.broadcasted_iota(I32, (STRIP, tq), 0)
    qq = lax.broadcasted_iota(I32, (STRIP, tq), 1)

    def issue_scores(j, s_buf):
        kblk = k_ref[0, pl.ds(pl.multiple_of(j * tq, tq), tq), :]
        for h in range(2):
            s_buf[h] = _nt(kblk, qh[h])

    def softmax(j, s_buf, p_buf, al_buf, diag):
        ks = pl.multiple_of(j * tq, tq)
        for h in range(2):
            def pass1(i, mx):
                r0 = i * STRIP
                s = s_buf[h, pl.ds(r0, STRIP), :] + jnp.tile(
                    bias_ref[0, h, pl.ds(pl.multiple_of(ks + r0, STRIP), STRIP), :], (1, rep))
                if diag:
                    s = jnp.where(kk + r0 <= qq, s, NEG_BIG)
                s_buf[h, pl.ds(r0, STRIP), :] = s
                return jnp.maximum(mx, _fold8(s, jnp.maximum))

            mx = _static_loop(nstrip, pass1, jnp.full((8, tq), NEG_BIG, F32))
            m_old = m_scr[h]
            m_new = jnp.maximum(m_old, jnp.max(mx, axis=0, keepdims=True))
            al_buf[h] = jnp.exp2(m_old - m_new)
            m_scr[h] = m_new

            def pass2(i, ls):
                r0 = i * STRIP
                p = jnp.exp2(s_buf[h, pl.ds(r0, STRIP), :] - m_new)
                p_buf[h, pl.ds(r0, STRIP), :] = p.astype(BF16)
                return ls + _fold8(p, jnp.add)

            ls = _static_loop(nstrip, pass2, jnp.zeros((8, tq), F32))
            l_scr[h] = al_buf[h] * l_scr[h] + jnp.sum(ls, axis=0, keepdims=True)

    def apply_values(j, p_buf, al_buf):
        for h in range(2):
            pv = _dot(vt_ref[0, 0, j, h * HEAD_DIM:(h + 1) * HEAD_DIM, :], p_buf[h])
            acc_scr[h] = acc_scr[h] * al_buf[h] + pv

    s_a, s_b = s_scr.at[0], s_scr.at[1]
    p_a, p_b = p_scr.at[0], p_scr.at[1]
    al_a, al_b = al_scr.at[0], al_scr.at[1]
    p_b[...] = jnp.zeros(p_b.shape, BF16)
    al_b[...] = jnp.ones(al_b.shape, F32)
    issue_scores(0, s_a)

    def pair(i, carry):
        a = 2 * i
        issue_scores(a + 1, s_b)
        softmax(a, s_a, p_a, al_a, False)
        apply_values(jnp.maximum(a - 1, 0), p_b, al_b)
        issue_scores(a + 2, s_a)
        softmax(a + 1, s_b, p_b, al_b, False)
        apply_values(a, p_a, al_a)
        return carry

    lax.fori_loop(0, qi // 2, pair, 0)
    last = 2 * (qi // 2)

    @pl.when(qi % 2 == 1)
    def _():
        issue_scores(qi, s_b)
        softmax(last, s_a, p_a, al_a, False)
        apply_values(jnp.maximum(last - 1, 0), p_b, al_b)
        softmax(qi, s_b, p_b, al_b, True)
        apply_values(last, p_a, al_a)
        apply_values(qi, p_b, al_b)

    @pl.when(qi % 2 == 0)
    def _():
        softmax(qi, s_a, p_a, al_a, True)
        apply_values(jnp.maximum(qi - 1, 0), p_b, al_b)
        apply_values(qi, p_a, al_a)

    out_t = jnp.concatenate([acc_scr[h] * (1.0 / l_scr[h]) for h in range(2)], axis=0)
    o_ref[0] = out_t.T.astype(o_ref.dtype)


def _fox(p16, vt, bias, tq):
    bsz, t, _ = p16.shape
    npair = D_HEADS // 2
    nblk = t // tq
    return pl.pallas_call(
        functools.partial(_fox_kernel, tq=tq),
        grid=(bsz, npair, nblk),
        in_specs=[pl.BlockSpec((1, tq, LANE), lambda b, h, i: (b, i, BLK_DQ + h)),
                  pl.BlockSpec((1, t, LANE), lambda b, h, i: (b, 0, BLK_DK + h)),
                  pl.BlockSpec((1, 1, nblk, LANE, tq), lambda b, h, i: (b, h, 0, 0, 0)),
                  pl.BlockSpec((1, 2, t, LANE), lambda b, h, i: (b, h, 0, 0))],
        out_specs=pl.BlockSpec((1, tq, LANE), lambda b, h, i: (b, i, h)),
        out_shape=jax.ShapeDtypeStruct((bsz, t, npair * LANE), BF16),
        scratch_shapes=[pltpu.VMEM((2, 2, tq, tq), F32), pltpu.VMEM((2, 2, tq, tq), BF16),
                        pltpu.VMEM((2, 2, 1, tq), F32),
                        pltpu.VMEM((2, HEAD_DIM, tq), F32), pltpu.VMEM((2, 1, tq), F32),
                        pltpu.VMEM((2, 1, tq), F32)],
        compiler_params=_params(("parallel", "parallel", "arbitrary")),
        name="forgetting_attention",
    )(p16, p16, vt, bias)


def _sb_kernel(q_ref, k_ref, vt_ref, o_ref, z_scr, sfx_scr, hi_scr, lo_scr, a_scr, acc_scr, r_scr, *, tq):
    qi = pl.program_id(2)
    nstrip = tq // STRIP
    qh = _head_halves(q_ref[0] * 0.125)
    acc_scr[...] = jnp.zeros(acc_scr.shape, F32)
    r_scr[...] = jnp.zeros(r_scr.shape, F32)
    kk = lax.broadcasted_iota(I32, (STRIP, tq), 0)
    qq = lax.broadcasted_iota(I32, (STRIP, tq), 1)
    row = lax.broadcasted_iota(I32, (tq, tq), 0)
    col = lax.broadcasted_iota(I32, (tq, tq), 1)
    later = jnp.where(col > row, 1.0, 0.0).astype(BF16)

    def block(j, diag):
        ks = pl.multiple_of(j * tq, tq)
        kblk = k_ref[0, pl.ds(ks, tq), :]
        for h in range(2):
            z_scr[h] = _nt(kblk, qh[h])
        for h in range(2):
            def pass1(i, rsum):
                r0 = i * STRIP
                z = z_scr[h, pl.ds(r0, STRIP), :]
                lk = _log_sigmoid(-z)
                z_scr[h, pl.ds(r0, STRIP), :] = lk + z
                if diag:
                    lk = jnp.where(kk + r0 < qq, lk, 0.0)
                hi = lk.astype(BF16)
                hi_scr[h, pl.ds(r0, STRIP), :] = hi
                lo_scr[h, pl.ds(r0, STRIP), :] = (lk - hi.astype(F32)).astype(BF16)
                return rsum + _fold8(lk, jnp.add)

            rsum = _static_loop(nstrip, pass1, jnp.zeros((8, tq), F32))
            sfx_scr[h] = _dot(later, hi_scr[h]) + _dot(later, lo_scr[h])
            r_old = r_scr[h]

            def pass2(i, carry):
                r0 = i * STRIP
                a = jnp.exp(z_scr[h, pl.ds(r0, STRIP), :] + sfx_scr[h, pl.ds(r0, STRIP), :] + r_old)
                if diag:
                    a = jnp.where(kk + r0 < qq, a, 0.0)
                a_scr[h, pl.ds(r0, STRIP), :] = a.astype(BF16)
                return carry

            _static_loop(nstrip, pass2, 0)
            acc_scr[h] = acc_scr[h] + _dot(vt_ref[0, 0, j, h * HEAD_DIM:(h + 1) * HEAD_DIM, :], a_scr[h])
            r_scr[h] = r_old + jnp.sum(rsum, axis=0, keepdims=True)

    block(qi, True)

    def live():
        return jnp.max(jnp.maximum(r_scr[0], r_scr[1])) > EXP_DEAD

    def cond(c):
        return jnp.logical_and(c[0] >= 0, c[1])

    def body(c):
        block(c[0], False)
        return c[0] - 1, live()

    lax.while_loop(cond, body, (qi - 1, live()))
    o_ref[0] = jnp.concatenate([acc_scr[0], acc_scr[1]], axis=0).T.astype(o_ref.dtype)


def _stick_breaking(p16, vt, tq):
    bsz, t, _ = p16.shape
    npair = C_HEADS // 2
    nblk = t // tq
    return pl.pallas_call(
        functools.partial(_sb_kernel, tq=tq),
        grid=(bsz, npair, nblk),
        in_specs=[pl.BlockSpec((1, tq, LANE), lambda b, h, i: (b, i, BLK_CQ + h)),
                  pl.BlockSpec((1, t, LANE), lambda b, h, i: (b, 0, BLK_CK + h)),
                  pl.BlockSpec((1, 1, nblk, LANE, tq), lambda b, h, i: (b, h, 0, 0, 0))],
        out_specs=pl.BlockSpec((1, tq, LANE), lambda b, h, i: (b, i, h)),
        out_shape=jax.ShapeDtypeStruct((bsz, t, npair * LANE), BF16),
        scratch_shapes=[pltpu.VMEM((2, tq, tq), F32), pltpu.VMEM((2, tq, tq), F32),
                        pltpu.VMEM((2, tq, tq), BF16), pltpu.VMEM((2, tq, tq), BF16),
                        pltpu.VMEM((2, tq, tq), BF16), pltpu.VMEM((2, HEAD_DIM, tq), F32),
                        pltpu.VMEM((2, 1, tq), F32)],
        compiler_params=_params(("parallel", "parallel", "arbitrary")),
        name="stick_breaking",
    )(p16, p16, vt)


def _dsa_kernel(iq0_ref, iq1_ref, ixq_ref, kcat_ref, q0_ref, q1_ref, k0_ref, k1_ref, vt_ref, o_ref,
                key_scr, s_scr, p_scr, eq_scr, rank_scr, bias_scr, al_scr, acc_scr, m_scr, l_scr, *, tq, topk):
    qi = pl.program_id(1)
    nstrip = tq // STRIP
    lane = lax.broadcasted_iota(I32, (tq, LANE), 1)
    kk = lax.broadcasted_iota(I32, (STRIP, tq), 0)
    qq = lax.broadcasted_iota(I32, (STRIP, tq), 1)

    qcat = []
    for ref in (iq0_ref, iq1_ref):
        for half in range(2):
            x = ref[0] if half == 0 else pltpu.roll(ref[0], IDX_DIM, 1)
            hi = x.astype(BF16).astype(F32)
            first = jnp.where(lane < IDX_DIM, hi, pltpu.roll(x - hi, IDX_DIM, 1))
            second = jnp.where(lane < IDX_DIM, hi, 0.0)
            qcat.append(jnp.concatenate([first, second], axis=1).astype(BF16))
    sr = lax.broadcasted_iota(I32, (8, LANE), 0)
    sl = lax.broadcasted_iota(I32, (8, LANE), 1)
    onehot = jnp.where(sl == sr + LANE_IW, 1.0, 0.0).astype(BF16)
    wh, wm, wl = _split3(ixq_ref[0])
    wt = _nt(onehot, wh) + _nt(onehot, wm) + _nt(onehot, wl)

    def score_chunk(c, diag, counts):
        ge0, gt0 = counts
        ks = pl.multiple_of(c * tq, tq)
        kc = kcat_ref[0, pl.ds(ks, tq), :]
        for h in range(IDX_HEADS):
            s_scr[0, h] = _nt(kc, qcat[h])
        for i in range(nstrip):
            r0 = i * STRIP
            sc = jnp.zeros((STRIP, tq), F32)
            for h in range(IDX_HEADS):
                sc = sc + wt[h:h + 1, :] * jnp.maximum(s_scr[0, h, r0:r0 + STRIP, :], 0.0)
            sc = jnp.where(sc == 0.0, 0.0, sc)
            bits = lax.bitcast_convert_type(sc, I32)
            key = jnp.where(bits < 0, bits ^ jnp.int32(0x7FFFFFFF), bits)
            if diag:
                key = jnp.where(kk + r0 <= qq, key, INT_MIN)
            key_scr[c, r0:r0 + STRIP, :] = key
            ge0 = ge0 + _fold8(jnp.where(key >= 0, 1.0, 0.0), jnp.add)
            gt0 = gt0 + _fold8(jnp.where(key > 0, 1.0, 0.0), jnp.add)
        return ge0, gt0

    zeros = jnp.zeros((8, tq), F32)
    counts = lax.fori_loop(0, qi, lambda c, counts: score_chunk(c, False, counts), (zeros, zeros))
    ge0, gt0 = score_chunk(qi, True, counts)
    n_ge0 = jnp.sum(ge0, axis=0, keepdims=True)
    n_gt0 = jnp.sum(gt0, axis=0, keepdims=True)

    def count(pred):
        def body(c, acc):
            for i in range(nstrip):
                acc = jnp.where(pred(key_scr[c, i * STRIP:(i + 1) * STRIP, :]), acc + 1.0, acc)
            return acc
        acc = lax.fori_loop(0, qi + 1, body, jnp.zeros((STRIP, tq), F32))
        return jnp.sum(acc, axis=0, keepdims=True)

    kf = jnp.float32(topk)
    th0 = jnp.where(n_ge0 >= kf, 0, INT_MIN).astype(I32)
    settled0 = jnp.where(n_ge0 >= kf, jnp.where(n_gt0 < kf, 1.0, 0.0), 0.0)
    n_up0 = jnp.where(n_ge0 >= kf, jnp.where(n_gt0 < kf, n_gt0, 0.0), n_ge0)

    def unsettled(settled):
        return jnp.min(settled) == 0.0

    def search_cond(c):
        return jnp.logical_and(c[0] >= 0, c[4])

    def search_body(c):
        bit, th, n_up, settled, _ = c
        cand = th | lax.shift_left(jnp.int32(1), bit)
        n = count(lambda kb: kb >= cand)
        open_ = settled == 0.0
        th = jnp.where(n >= kf, jnp.where(open_, cand, th), th)
        n_up = jnp.where(n < kf, jnp.where(open_, n, n_up), n_up)
        settled = jnp.maximum(settled, jnp.where(n == kf, 1.0, 0.0))
        return bit - 1, th, n_up, settled, unsettled(settled)

    _, th, n_up, _, _ = lax.while_loop(search_cond, search_body,
                                       (jnp.int32(30), th0, n_up0, settled0, unsettled(settled0)))
    need = kf - n_up

    qh = sum((_head_halves((r[0].astype(F32) * (0.125 * LOG2E)).astype(BF16)) for r in (q0_ref, q1_ref)), ())
    k_refs = (k0_ref, k0_ref, k1_ref, k1_ref)
    m_scr[...] = jnp.full(m_scr.shape, NEG_BIG, F32)
    l_scr[...] = jnp.zeros(l_scr.shape, F32)
    acc_scr[...] = jnp.zeros(acc_scr.shape, F32)
    row = lax.broadcasted_iota(I32, (tq, tq), 0)
    col = lax.broadcasted_iota(I32, (tq, tq), 1)
    upto = jnp.where(col <= row, 1.0, 0.0).astype(BF16)

    def issue(c, s_buf, eq_buf, rank_buf):
        ks = pl.multiple_of(c * tq, tq)
        for i in range(nstrip):
            r0 = i * STRIP
            eq_buf[r0:r0 + STRIP, :] = jnp.where(key_scr[c, r0:r0 + STRIP, :] == th, 1.0, 0.0).astype(BF16)
        rank_buf[...] = _dot(upto, eq_buf[...])
        for h in range(B_HEADS):
            s_buf[h] = _nt(k_refs[h][0, pl.ds(ks, tq), :], qh[h])

    def softmax(c, s_buf, rank_buf, p_buf, al_buf, diag, seen):
        for i in range(nstrip):
            r0 = i * STRIP
            kb = key_scr[c, r0:r0 + STRIP, :]
            tie = jnp.where(rank_buf[r0:r0 + STRIP, :] + seen <= need, 0.0, NEG_BIG)
            bias = jnp.where(kb > th, 0.0, jnp.where(kb == th, tie, NEG_BIG))
            if diag:
                bias = jnp.where(kk + r0 <= qq, bias, NEG_BIG)
            bias_scr[r0:r0 + STRIP, :] = bias
        for h in range(B_HEADS):
            def pass1(i, mx):
                r0 = i * STRIP
                s = s_buf[h, r0:r0 + STRIP, :] + bias_scr[r0:r0 + STRIP, :]
                s_buf[h, r0:r0 + STRIP, :] = s
                return jnp.maximum(mx, _fold8(s, jnp.maximum))

            mx = _static_loop(nstrip, pass1, jnp.full((8, tq), NEG_BIG, F32))
            m_old = m_scr[h]
            m_new = jnp.maximum(m_old, jnp.max(mx, axis=0, keepdims=True))
            al_buf[h] = jnp.exp2(m_old - m_new)
            m_scr[h] = m_new

            def pass2(i, ls):
                r0 = i * STRIP
                p = jnp.exp2(s_buf[h, r0:r0 + STRIP, :] - m_new)
                p_buf[h, r0:r0 + STRIP, :] = p.astype(BF16)
                return ls + _fold8(p, jnp.add)

            ls = _static_loop(nstrip, pass2, jnp.zeros((8, tq), F32))
            l_scr[h] = al_buf[h] * l_scr[h] + jnp.sum(ls, axis=0, keepdims=True)
        return seen + rank_buf[tq - 1:tq, :]

    def apply_values(c, p_buf, al_buf):
        for h in range(B_HEADS):
            lo = (h % 2) * HEAD_DIM
            acc_scr[h] = acc_scr[h] * al_buf[h] + _dot(vt_ref[0, h // 2, c, lo:lo + HEAD_DIM, :], p_buf[h])

    s_a, s_b = s_scr.at[0], s_scr.at[1]
    p_a, p_b = p_scr.at[0], p_scr.at[1]
    eq_a, eq_b = eq_scr.at[0], eq_scr.at[1]
    rank_a, rank_b = rank_scr.at[0], rank_scr.at[1]
    al_a, al_b = al_scr.at[0], al_scr.at[1]
    p_b[...] = jnp.zeros(p_b.shape, BF16)
    al_b[...] = jnp.ones(al_b.shape, F32)
    issue(0, s_a, eq_a, rank_a)

    def pair(i, seen):
        a = 2 * i
        issue(a + 1, s_b, eq_b, rank_b)
        seen = softmax(a, s_a, rank_a, p_a, al_a, False, seen)
        apply_values(jnp.maximum(a - 1, 0), p_b, al_b)
        issue(a + 2, s_a, eq_a, rank_a)
        seen = softmax(a + 1, s_b, rank_b, p_b, al_b, False, seen)
        apply_values(a, p_a, al_a)
        return seen

    seen = lax.fori_loop(0, qi // 2, pair, jnp.zeros((1, tq), F32))
    last = 2 * (qi // 2)

    @pl.when(qi % 2 == 1)
    def _():
        issue(qi, s_b, eq_b, rank_b)
        seen_d = softmax(last, s_a, rank_a, p_a, al_a, False, seen)
        apply_values(jnp.maximum(last - 1, 0), p_b, al_b)
        softmax(qi, s_b, rank_b, p_b, al_b, True, seen_d)
        apply_values(last, p_a, al_a)
        apply_values(qi, p_b, al_b)

    @pl.when(qi % 2 == 0)
    def _():
        softmax(qi, s_a, rank_a, p_a, al_a, True, seen)
        apply_values(jnp.maximum(qi - 1, 0), p_b, al_b)
        apply_values(qi, p_a, al_a)

    for g in range(B_HEADS // 2):
        out_t = jnp.concatenate([acc_scr[h] * (1.0 / l_scr[h]) for h in (2 * g, 2 * g + 1)], axis=0)
        o_ref[0, :, g * LANE:(g + 1) * LANE] = out_t.T.astype(o_ref.dtype)


def _dsa(p16, p32, kcat, vt, tq):
    bsz, t, _ = p16.shape
    topk = min(DSA_TOPK, t // 4)
    npair = B_HEADS // 2
    nblk = t // tq
    qspec = lambda arr_blk: pl.BlockSpec((1, tq, LANE), lambda b, i: (b, i, arr_blk))
    once = pl.Buffered(1)
    kspec = lambda arr_blk: pl.BlockSpec((1, t, LANE), lambda b, i: (b, 0, arr_blk), pipeline_mode=once)
    sq = lambda n, dt: pltpu.VMEM((n, tq, tq), dt) if n else pltpu.VMEM((tq, tq), dt)
    return pl.pallas_call(
        functools.partial(_dsa_kernel, tq=tq, topk=topk),
        grid=(bsz, nblk),
        in_specs=[qspec(BLK_IQ), qspec(BLK_IQ + 1), qspec(BLK_IX),
                  pl.BlockSpec((1, t, 2 * LANE), lambda b, i: (b, 0, 0), pipeline_mode=once),
                  qspec(BLK_BQ), qspec(BLK_BQ + 1), kspec(BLK_BK), kspec(BLK_BK + 1),
                  pl.BlockSpec((1, npair, nblk, LANE, tq), lambda b, i: (b, 0, 0, 0, 0), pipeline_mode=once)],
        out_specs=pl.BlockSpec((1, tq, npair * LANE), lambda b, i: (b, i, 0)),
        out_shape=jax.ShapeDtypeStruct((bsz, t, npair * LANE), BF16),
        scratch_shapes=[sq(nblk, I32), pltpu.VMEM((2, B_HEADS, tq, tq), F32), pltpu.VMEM((2, B_HEADS, tq, tq), BF16),
                        sq(2, BF16), sq(2, F32), sq(0, F32), pltpu.VMEM((2, B_HEADS, 1, tq), F32),
                        pltpu.VMEM((B_HEADS, HEAD_DIM, tq), F32), pltpu.VMEM((B_HEADS, 1, tq), F32),
                        pltpu.VMEM((B_HEADS, 1, tq), F32)],
        compiler_params=_params(("parallel", "arbitrary")),
        name="dsa_topk_attention",
    )(p32, p32, p32, kcat, p16, p16, p16, p16, vt)


def _merge_kernel(x_ref, sh_ref, sc_ref, gt_ref, oa0_ref, oa1_ref, oa2_ref, la0_ref, la1_ref, la2_ref,
                  yb_ref, yc_ref, yd_ref, wg_ref, bg_ref, wb_ref, wo_ref, lg_ref, lb_ref, o_ref, *, alpha):
    x = x_ref[0]
    d = x.shape[1]
    u = (x * (1.0 + sc_ref[0]) + sh_ref[0]).astype(BF16)
    lses = [r[0] for r in (la0_ref, la1_ref, la2_ref)]
    top = jnp.maximum(jnp.maximum(lses[0], lses[1]), lses[2])
    es = [jnp.exp(l - top) for l in lses]
    den = es[0] + es[1] + es[2]
    ya = sum((e / den) * r[0].astype(F32) for e, r in zip(es, (oa0_ref, oa1_ref, oa2_ref)))
    branches = (ya.astype(BF16), yb_ref[0], yc_ref[0], yd_ref[0])
    merged = jnp.zeros(x.shape, F32)
    off = 0
    for i, br in enumerate(branches):
        gate = jax.nn.sigmoid(_dot(u, wg_ref[:, i * d:(i + 1) * d]) + bg_ref[:, i * d:(i + 1) * d])
        merged = merged + gate * _dot(br, wb_ref[off:off + BRANCH_WIDTHS[i], :])
        off += BRANCH_WIDTHS[i]
    h = _dot(merged.astype(BF16), wo_ref[...])
    o_ref[0] = _deepnorm_ln(x, h, gt_ref[0], lg_ref[...], lb_ref[...], alpha)


def _merge(x, shift, scale, gate, oas, las, yb, yc, yd, wg, bg, wb, wo, ln_g, ln_b, alpha):
    bsz, t, d = x.shape
    tm = _pick(t, (256, 128))
    row = lambda b, i: (b, 0, 0)
    tok = lambda w: pl.BlockSpec((1, tm, w), lambda b, i: (b, i, 0))
    wspec = lambda shape: pl.BlockSpec(shape, lambda b, i: (0, 0), pipeline_mode=pl.Buffered(1))
    return pl.pallas_call(
        functools.partial(_merge_kernel, alpha=alpha),
        grid=(bsz, t // tm),
        in_specs=[tok(d), pl.BlockSpec((1, 1, d), row), pl.BlockSpec((1, 1, d), row), pl.BlockSpec((1, 1, d), row),
                  tok(LANE), tok(LANE), tok(LANE), tok(LANE), tok(LANE), tok(LANE),
                  tok(2 * LANE), tok(2 * LANE), tok(2 * LANE),
                  wspec(wg.shape), wspec(bg.shape), wspec(wb.shape), wspec(wo.shape),
                  pl.BlockSpec((1, d), lambda b, i: (0, 0)), pl.BlockSpec((1, d), lambda b, i: (0, 0))],
        out_specs=tok(d),
        out_shape=jax.ShapeDtypeStruct((bsz, t, d), F32),
        compiler_params=_params(("parallel", "parallel")),
        name="gated_merge",
    )(x, shift, scale, gate, *oas, *las, yb, yc, yd, wg, bg, wb, wo, ln_g, ln_b)


def _rope_tables(t):
    half = ROT_DIM // 2
    inv_freq = ROPE_THETA ** (-(jnp.arange(half, dtype=F32) * (2.0 / ROT_DIM)))
    ang = jnp.arange(t, dtype=F32)[:, None] * inv_freq[None, :]
    cos, sin = jnp.cos(ang), jnp.sin(ang)
    ones = jnp.ones((t, HEAD_DIM - ROT_DIM), F32)
    zeros = jnp.zeros((t, HEAD_DIM - half), F32)
    c64 = jnp.concatenate([cos, cos, ones], axis=1)
    s1 = jnp.concatenate([-sin, zeros], axis=1)
    s2 = jnp.concatenate([jnp.zeros((t, half), F32), sin, zeros[:, half:]], axis=1)
    return tuple(jnp.concatenate([a, a], axis=1) for a in (c64, s1, s2))


def _mixer_weights(w_in):
    d = w_in.shape[0]

    def qkv(off, heads):
        w = heads * HEAD_DIM
        return w_in[:, off:off + w], w_in[:, off + w:off + 2 * w], w_in[:, off + 2 * w:off + 3 * w]

    aq, ak, av = qkv(OFF_A, A_HEADS)
    bq, bk, bv = qkv(OFF_B, B_HEADS)
    cq, ck, cv = qkv(OFF_C, C_HEADS)
    dq, dk, dv = qkv(OFF_D, D_HEADS)
    w16 = jnp.concatenate([aq, ak, bq, bk, av, bv, cq, ck, cv, dq, dk, dv, jnp.zeros((d, LANE), F32)], axis=1)
    ix = jnp.concatenate([w_in[:, OFF_IK:OFF_IK + IDX_DIM], jnp.zeros((d, LANE_IW - IDX_DIM), F32),
                          w_in[:, OFF_IW:OFF_IW + IDX_HEADS], w_in[:, OFF_FG:OFF_FG + D_HEADS],
                          jnp.zeros((d, LANE - LANE_FG - D_HEADS), F32)], axis=1)
    w32 = jnp.concatenate([w_in[:, OFF_IQ:OFF_IQ + IDX_HEADS * IDX_DIM], ix, jnp.zeros((d, LANE), F32)], axis=1)
    return w16.astype(BF16), w32.astype(BF16), w_in[:, OFF_GATE:].astype(BF16)


def _to_residues(a, dil):
    bsz, t, w = a.shape
    if dil == 1:
        return a
    return a.reshape(bsz, t // dil, dil, w).transpose(0, 2, 1, 3).reshape(bsz * dil, t // dil, w)


def _from_residues(a, dil, bsz):
    if dil == 1:
        return a
    _, ln, w = a.shape
    return a.reshape(bsz, dil, ln, w).transpose(0, 2, 1, 3).reshape(bsz, ln * dil, w)


def _mixer(x, shift, scale, gate, w_in, b_gate, b_forget, w_branch, w_out, ln_g, ln_b, tables, alpha):
    bsz, t, d = x.shape
    w16, w32, wg = _mixer_weights(w_in)
    tq = _pick(t, (128,))
    tqf = _pick(t, (512, 256, 128))
    p16, p32, vtb, vtc, vtd = _proj(x, shift, scale, w16, w32, tables,
                                    ((BLK_BV, tqf), (BLK_CV, tq), (BLK_DV, tqf)))

    oas, las = [], []
    for g, (_, dil) in enumerate(DILATED_GROUPS):
        if dil == 1:
            o, lse = _dilated(p16, p16, p16, cols=(BLK_AQ + g, BLK_AK + g, BLK_AV + g))
        else:
            streams = [_to_residues(p16[:, :, (blk + g) * LANE:(blk + g + 1) * LANE], dil)
                       for blk in (BLK_AQ, BLK_AK, BLK_AV)]
            o, lse = _dilated(*streams)
        oas.append(_from_residues(o, dil, bsz))
        las.append(_from_residues(lse, dil, bsz))

    bias_row = jnp.zeros((1, LANE), F32).at[0, LANE_FG:LANE_FG + D_HEADS].set(b_forget)
    fbias, kcat = _cum(p32, bias_row, _pick(t, (256, 128)))
    yd = _fox(p16, vtd, fbias, tqf)
    yc = _stick_breaking(p16, vtc, tq)
    yb = _dsa(p16, p32, kcat, vtb, tqf)
    return _merge(x, shift, scale, gate, oas, las, yb, yc, yd, wg, b_gate.reshape(1, -1),
                  w_branch.astype(BF16), w_out.astype(BF16), ln_g, ln_b, alpha)


def kernel(x, c, ada_w, ada_b, ln_g, ln_b, ffn_w_in, ffn_w_out, mix_w_in, mix_b_gate, mix_b_forget,
           mix_w_branch, mix_w_out):
    bsz, t, d = x.shape
    depth = ada_w.shape[0]
    f = ffn_w_out.shape[2]
    alpha = float((2 * depth) ** 0.25)
    assert t % (DILATED_GROUPS[-1][0]) == 0 and d % LANE == 0 and bsz <= 8

    c8 = jnp.zeros((8, d), F32).at[:bsz].set(c)
    mod = _ada(c8, ada_w, ada_b)[:, :bsz].reshape(depth, bsz, 3, 3, 1, d)
    tables = _rope_tables(t)

    for l in range(depth):
        m = lambda sub, kind: mod[l, :, sub, kind]
        lng = lambda sub: ln_g[l, sub].reshape(1, d)
        lnb = lambda sub: ln_b[l, sub].reshape(1, d)

        def ffn(x, sub, which):
            w_in = ffn_w_in[l, which].astype(BF16)
            return _ffn(x, m(sub, 0), m(sub, 1), m(sub, 2), w_in[:, :f], w_in[:, f:],
                        ffn_w_out[l, which].astype(BF16), lng(sub), lnb(sub), alpha)

        x = ffn(x, 0, 0)
        x = _mixer(x, m(1, 0), m(1, 1), m(1, 2), mix_w_in[l], mix_b_gate[l], mix_b_forget[l],
                   mix_w_branch[l], mix_w_out[l], lng(1), lnb(1), tables, alpha)
        x = ffn(x, 2, 1)
    return x
```
